```python
import jax, jax.numpy as jnp
from jax import lax
import numpy as np

D_MODEL = 1024
BATCH = 4
SEQ = 4096
DEPTH = 4
DEC_BATCH = 128
DEC_SEQ = 4
PAST_LEN = 2048
PAGE_SIZE = 128

HEAD_DIM = 64
ROPE_THETA = 10000.0
A_HEADS = 8
A_KV_HEADS = 2
IDX_HEADS = 4
IDX_DIM = 64
A_TOPK_MAX = 256
A_WIDTH = A_HEADS * HEAD_DIM
B_WIDTH = D_MODEL // 2
CONV_W = 3
C_HEADS = 16
C_KV_HEADS = 4
C_WIDTH = C_HEADS * HEAD_DIM
CMP_BLOCK = 32
CMP_STRIDE = 16
SLC_BLOCK = 64
SLC_TOPN = 16
WINDOW = 512
D_FF = ((8 * D_MODEL + 3 * 256 - 1) // (3 * 256)) * 256
LN_EPS = 1e-5
ALPHA = (2 * DEPTH) ** 0.25
BETA = (8 * DEPTH) ** -0.25
Q_BLOCK = 128
N_AB = (DEPTH + 1) // 2
N_C = DEPTH // 2

AB_SIZES = (A_WIDTH, A_KV_HEADS * HEAD_DIM, A_KV_HEADS * HEAD_DIM, IDX_HEADS * IDX_DIM, IDX_DIM, IDX_HEADS,
            B_WIDTH, B_WIDTH, B_WIDTH)
AB_IN = sum(AB_SIZES)
AB_SPLITS = tuple(int(s) for s in np.cumsum(AB_SIZES)[:-1])
C_SIZES = (C_WIDTH,) + (C_KV_HEADS * HEAD_DIM,) * 6 + (3 * C_HEADS,)
C_IN = sum(C_SIZES)
C_SPLITS = tuple(int(s) for s in np.cumsum(C_SIZES)[:-1])

kernel_name = "hybrid_dsa_conv_nsa_deepnorm_step"

F32 = jnp.float32


def rope(x, pos):
    half = x.shape[-1] // 2
    inv = ROPE_THETA ** (-jnp.arange(half, dtype=F32) / half)
    ang = pos.astype(F32)[:, None] * inv[None, :]
    cos = jnp.cos(ang)[None, :, None, :]
    sin = jnp.sin(ang)[None, :, None, :]
    xf = x.astype(F32)
    x1, x2 = xf[..., :half], xf[..., half:]
    return jnp.concatenate([x1 * cos - x2 * sin, x2 * cos + x1 * sin], -1).astype(x.dtype)


def layernorm(x, g, b):
    xf = x.astype(F32)
    mu = jnp.mean(xf, -1, keepdims=True)
    var = jnp.mean(jnp.square(xf - mu), -1, keepdims=True)
    return ((xf - mu) * lax.rsqrt(var + LN_EPS) * g + b).astype(x.dtype)


def masked_softmax(s, mask):
    s = jnp.where(mask, s.astype(F32), -jnp.inf)
    m = jnp.max(s, -1, keepdims=True)
    m = jnp.where(jnp.isfinite(m), m, 0.0)
    p = jnp.exp(s - m)
    den = jnp.sum(p, -1, keepdims=True)
    return p / jnp.where(den > 0, den, 1.0)


def to_blocks(a, blk):
    return a.reshape(a.shape[0], a.shape[1] // blk, blk, *a.shape[2:])


def map_blocks(fn, args):
    nb = args[0].shape[1]
    if nb == 1:
        return fn(*[a[:, 0] for a in args])
    out = lax.map(lambda xs: fn(*xs), tuple(jnp.moveaxis(a, 1, 0) for a in args))
    out = jnp.moveaxis(out, 0, 1)
    return out.reshape(out.shape[0], -1, *out.shape[3:])


def gather_pages(pool, layer, page_table):
    g = pool[layer, page_table]
    return g.reshape(g.shape[0], g.shape[1] * g.shape[2], *g.shape[3:])


def dsa_block(q, iq, iw, pos, k_all, v_all, ik_all, topk):
    bsz, tb = q.shape[:2]
    L = k_all.shape[1]
    causal = jnp.arange(L)[None, None, :] <= pos[:, :, None]
    logits = jnp.einsum('bthd,bsd->bths', iq.astype(F32), ik_all.astype(F32)) * IDX_DIM ** -0.5
    score = jnp.einsum('bths,bth->bts', jax.nn.relu(logits), iw.astype(F32))
    score = jnp.where(causal, score, -jnp.inf)
    top_val, top_idx = lax.top_k(score, topk)
    take = jax.vmap(lambda kk, ii: kk[ii])
    k_sel = take(k_all, top_idx).astype(F32)
    v_sel = take(v_all, top_idx).astype(F32)
    qg = q.reshape(bsz, tb, A_KV_HEADS, A_HEADS // A_KV_HEADS, HEAD_DIM).astype(F32)
    s = jnp.einsum('btghd,btkgd->btghk', qg, k_sel) * HEAD_DIM ** -0.5
    p = masked_softmax(s, (top_val > -jnp.inf)[:, :, None, None, :])
    o = jnp.einsum('btghk,btkgd->btghd', p, v_sel)
    return o.reshape(bsz, tb, A_WIDTH).astype(q.dtype)


def ab_mixer(x, start, past, w_in, conv_w, w_out):
    bsz, T, _ = x.shape
    pos = start + jnp.arange(T)
    q, k, v, iq, ik, iw, gate_b, gate_c, h = jnp.split(x @ w_in, AB_SPLITS, axis=-1)
    q = rope(q.reshape(bsz, T, A_HEADS, HEAD_DIM), pos)
    k = rope(k.reshape(bsz, T, A_KV_HEADS, HEAD_DIM), pos)
    v = v.reshape(bsz, T, A_KV_HEADS, HEAD_DIM)
    iq = rope(iq.reshape(bsz, T, IDX_HEADS, IDX_DIM), pos)
    ik = rope(ik.reshape(bsz, T, 1, IDX_DIM), pos)[:, :, 0]
    iw = iw * IDX_HEADS ** -0.5
    u = gate_c * h
    if past is None:
        k_all, v_all, ik_all = k, v, ik
        prev = jnp.zeros((bsz, CONV_W - 1, B_WIDTH), u.dtype)
    else:
        k_past, v_past, ik_past, prev = past
        k_all = jnp.concatenate([k_past, k], 1)
        v_all = jnp.concatenate([v_past, v], 1)
        ik_all = jnp.concatenate([ik_past, ik], 1)
    topk = min(A_TOPK_MAX, k_all.shape[1] // 4)
    blk = min(Q_BLOCK, T)
    a_out = map_blocks(lambda qb, iqb, iwb, pb: dsa_block(qb, iqb, iwb, pb, k_all, v_all, ik_all, topk),
                       (to_blocks(q, blk), to_blocks(iq, blk), to_blocks(iw, blk), to_blocks(pos[None], blk)))
    full = jnp.concatenate([prev, u], 1)
    conv = full[:, 0:T] * conv_w[0]
    for j in range(1, CONV_W):
        conv = conv + full[:, j:j + T] * conv_w[j]
    b_out = gate_b * conv
    y = jnp.concatenate([a_out, b_out], -1) @ w_out
    return y, (k, v, ik, full[:, T:])


def compress(k_all, pe, w1, w2):
    L = k_all.shape[1]
    n_cmp = (L - CMP_BLOCK) // CMP_STRIDE + 1
    idx = jnp.arange(n_cmp)[:, None] * CMP_STRIDE + jnp.arange(CMP_BLOCK)[None, :]
    blocks = k_all[:, idx] + pe[None, None, :, None, :]
    hid = jax.nn.gelu(jnp.einsum('bnjgd,jde->bnge', blocks, w1))
    return jnp.einsum('bnge,ef->bngf', hid, w2)


def nsa_block(q, pos, gates, kw_span, vw_span, kw_pos, k_cmp, v_cmp, ks_blk, vs_blk, cmp_end, cmp_to_slc):
    bsz, tb = q.shape[:2]
    scale = HEAD_DIM ** -0.5
    qf = q.astype(F32)
    t = pos[:, :, None]
    s_c = jnp.einsum('btghd,bngd->btghn', qf, k_cmp.astype(F32)) * scale
    p_c = masked_softmax(s_c, (cmp_end[None, None, :] <= t)[:, :, None, None, :])
    o_c = jnp.einsum('btghn,bngd->btghd', p_c, v_cmp.astype(F32))
    imp = jnp.einsum('btghn,nm->btgm', p_c, cmp_to_slc)
    n_sblk = cmp_to_slc.shape[1]
    blk_id = jnp.arange(n_sblk)[None, None, :]
    cur = t // SLC_BLOCK
    forced = (blk_id == 0) | (blk_id == cur) | (blk_id == cur - 1)
    imp = jnp.where(forced[:, :, None, :], jnp.inf, imp)
    imp = jnp.where((blk_id * SLC_BLOCK <= t)[:, :, None, :], imp, -jnp.inf)
    n_sel = min(SLC_TOPN, n_sblk)
    sel_val, sel_idx = lax.top_k(imp, n_sel)
    take = jax.vmap(jax.vmap(lambda kk, ii: kk[ii]))
    idx_bg = jnp.moveaxis(sel_idx, 2, 1)
    k_s = take(ks_blk, idx_bg).reshape(bsz, C_KV_HEADS, tb, n_sel * SLC_BLOCK, HEAD_DIM).astype(F32)
    v_s = take(vs_blk, idx_bg).reshape(bsz, C_KV_HEADS, tb, n_sel * SLC_BLOCK, HEAD_DIM).astype(F32)
    key_pos = sel_idx[..., None] * SLC_BLOCK + jnp.arange(SLC_BLOCK)
    ok = (key_pos <= pos[:, :, None, None, None]) & (sel_val > -jnp.inf)[..., None]
    ok = ok.reshape(bsz, tb, C_KV_HEADS, n_sel * SLC_BLOCK)
    s_s = jnp.einsum('btghd,bgtkd->btghk', qf, k_s) * scale
    p_s = masked_softmax(s_s, ok[:, :, :, None, :])
    o_s = jnp.einsum('btghk,bgtkd->btghd', p_s, v_s)
    kp = kw_pos[:, None, :]
    ok_w = (kp >= 0) & (kp <= t) & (t - kp < WINDOW)
    s_w = jnp.einsum('btghd,bsgd->btghs', qf, kw_span.astype(F32)) * scale
    p_w = masked_softmax(s_w, ok_w[:, :, None, None, :])
    o_w = jnp.einsum('btghs,bsgd->btghd', p_w, vw_span.astype(F32))
    g = gates.reshape(bsz, tb, C_KV_HEADS, C_HEADS // C_KV_HEADS, 3)
    o = g[..., 0:1] * o_c + g[..., 1:2] * o_s + g[..., 2:3] * o_w
    return o.reshape(bsz, tb, C_WIDTH).astype(q.dtype)


def nsa_mixer(x, start, past, w_in, pe_k, w1_k, w2_k, pe_v, w1_v, w2_v, w_out):
    bsz, T, _ = x.shape
    pos = start + jnp.arange(T)
    q, kc, vc, ks, vs, kw, vw, g = jnp.split(x @ w_in, C_SPLITS, axis=-1)
    kvr = lambda a: a.reshape(bsz, T, C_KV_HEADS, HEAD_DIM)
    q = rope(q.reshape(bsz, T, C_HEADS, HEAD_DIM), pos)
    kc, ks, kw = rope(kvr(kc), pos), rope(kvr(ks), pos), rope(kvr(kw), pos)
    vc, vs, vw = kvr(vc), kvr(vs), kvr(vw)
    gates = jax.nn.sigmoid(g.astype(F32)).reshape(bsz, T, C_HEADS, 3)
    if past is None:
        kc_all, vc_all, ks_all, vs_all = kc, vc, ks, vs
        zpad = jnp.zeros((bsz, WINDOW, C_KV_HEADS, HEAD_DIM), kw.dtype)
        pre_k, pre_v = zpad, zpad
        keep = min(WINDOW, T)
        win_k, win_v = kw[:, T - keep:], vw[:, T - keep:]
    else:
        pkc, pvc, pks, pvs, buf_k, buf_v = past
        kc_all = jnp.concatenate([pkc, kc], 1)
        vc_all = jnp.concatenate([pvc, vc], 1)
        ks_all = jnp.concatenate([pks, ks], 1)
        vs_all = jnp.concatenate([pvs, vs], 1)
        nbuf = buf_k.shape[1]
        zpad = jnp.zeros((bsz, WINDOW - nbuf, C_KV_HEADS, HEAD_DIM), kw.dtype)
        pre_k = jnp.concatenate([zpad, buf_k], 1)
        pre_v = jnp.concatenate([zpad, buf_v], 1)
        win_k = jnp.concatenate([buf_k, kw], 1)[:, T:]
        win_v = jnp.concatenate([buf_v, vw], 1)[:, T:]
    L = kc_all.shape[1]
    k_cmp = compress(kc_all, pe_k, w1_k, w2_k)
    v_cmp = compress(vc_all, pe_v, w1_v, w2_v)
    n_cmp = k_cmp.shape[1]
    cmp_start = jnp.arange(n_cmp) * CMP_STRIDE
    cmp_end = cmp_start + CMP_BLOCK - 1
    n_sblk = -(-L // SLC_BLOCK)
    padn = n_sblk * SLC_BLOCK - L
    def sel_blocks(a):
        a = jnp.pad(a, ((0, 0), (0, padn), (0, 0), (0, 0)))
        return a.reshape(bsz, n_sblk, SLC_BLOCK, C_KV_HEADS, HEAD_DIM).transpose(0, 3, 1, 2, 4)
    ks_blk, vs_blk = sel_blocks(ks_all), sel_blocks(vs_all)
    lo = jnp.arange(n_sblk) * SLC_BLOCK
    cmp_to_slc = ((cmp_start[:, None] < lo[None, :] + SLC_BLOCK)
                  & (cmp_start[:, None] + CMP_BLOCK > lo[None, :])).astype(F32)
    blk = min(Q_BLOCK, T)
    nb = T // blk
    ext_k = jnp.concatenate([pre_k, kw], 1)
    ext_v = jnp.concatenate([pre_v, vw], 1)
    span_idx = jnp.arange(nb)[:, None] * blk + jnp.arange(WINDOW + blk)[None, :]
    kw_span, vw_span = ext_k[:, span_idx], ext_v[:, span_idx]
    span_pos = (start - WINDOW + span_idx)[None]
    qg = q.reshape(bsz, T, C_KV_HEADS, C_HEADS // C_KV_HEADS, HEAD_DIM)
    out = map_blocks(
        lambda qb, pb, gb, kwb, vwb, kpb: nsa_block(qb, pb, gb, kwb, vwb, kpb, k_cmp, v_cmp, ks_blk, vs_blk,
                                                   cmp_end, cmp_to_slc),
        (to_blocks(qg, blk), to_blocks(pos[None], blk), to_blocks(gates, blk), kw_span, vw_span, span_pos))
    y = out @ w_out
    return y, (kc, vc, ks, vs, win_k, win_v)


def swiglu(x, wg, wu, wd):
    return (jax.nn.silu(x @ wg) * (x @ wu)) @ wd


def setup_inputs(seed: int = 0) -> dict:
    key = jax.random.key(seed)
    ks = jax.random.split(key, 40)
    nrm = lambda k, shape, s=1.0: jax.random.normal(k, shape, F32) * s
    n_pages = PAST_LEN // PAGE_SIZE
    n_used = DEC_BATCH * n_pages
    n_pool = n_used + max(1, n_used // 4)
    win_buf = min(WINDOW, PAST_LEN)
    page_table = jax.random.permutation(ks[0], n_pool)[:n_used].reshape(DEC_BATCH, n_pages).astype(jnp.int32)
    a_kv = (N_AB, n_pool, PAGE_SIZE, A_KV_HEADS, HEAD_DIM)
    c_kv = (N_C, n_pool, PAGE_SIZE, C_KV_HEADS, HEAD_DIM)
    c_win = (N_C, DEC_BATCH, win_buf, C_KV_HEADS, HEAD_DIM)
    return {
        "x_prompt": nrm(ks[1], (BATCH, SEQ, D_MODEL)),
        "x_sample": nrm(ks[2], (DEC_BATCH, DEC_SEQ, D_MODEL)),
        "cache_a_k": nrm(ks[3], a_kv),
        "cache_a_v": nrm(ks[4], a_kv),
        "cache_a_ik": nrm(ks[5], (N_AB, n_pool, PAGE_SIZE, IDX_DIM)),
        "state_b_conv": nrm(ks[6], (N_AB, DEC_BATCH, CONV_W - 1, B_WIDTH)),
        "cache_c_cmp_k": nrm(ks[7], c_kv),
        "cache_c_cmp_v": nrm(ks[8], c_kv),
        "cache_c_slc_k": nrm(ks[9], c_kv),
        "cache_c_slc_v": nrm(ks[10], c_kv),
        "state_c_win_k": nrm(ks[11], c_win),
        "state_c_win_v": nrm(ks[12], c_win),
        "page_table": page_table,
        "ab_w_in": nrm(ks[13], (N_AB, D_MODEL, AB_IN), D_MODEL ** -0.5),
        "ab_conv_w": nrm(ks[14], (N_AB, CONV_W, B_WIDTH), CONV_W ** -0.5),
        "ab_w_out": nrm(ks[15], (N_AB, A_WIDTH + B_WIDTH, D_MODEL), BETA * (A_WIDTH + B_WIDTH) ** -0.5),
        "c_w_in": nrm(ks[16], (N_C, D_MODEL, C_IN), D_MODEL ** -0.5),
        "c_cmp_pe_k": nrm(ks[17], (N_C, CMP_BLOCK, HEAD_DIM), 0.02),
        "c_cmp_w1_k": nrm(ks[18], (N_C, CMP_BLOCK, HEAD_DIM, HEAD_DIM), (CMP_BLOCK * HEAD_DIM) ** -0.5),
        "c_cmp_w2_k": nrm(ks[19], (N_C, HEAD_DIM, HEAD_DIM), HEAD_DIM ** -0.5),
        "c_cmp_pe_v": nrm(ks[20], (N_C, CMP_BLOCK, HEAD_DIM), 0.02),
        "c_cmp_w1_v": nrm(ks[21], (N_C, CMP_BLOCK, HEAD_DIM, HEAD_DIM), (CMP_BLOCK * HEAD_DIM) ** -0.5),
        "c_cmp_w2_v": nrm(ks[22], (N_C, HEAD_DIM, HEAD_DIM), HEAD_DIM ** -0.5),
        "c_w_out": nrm(ks[23], (N_C, C_WIDTH, D_MODEL), BETA * C_WIDTH ** -0.5),
        "ffn_w_gate": nrm(ks[24], (DEPTH, D_MODEL, D_FF), D_MODEL ** -0.5),
        "ffn_w_up": nrm(ks[25], (DEPTH, D_MODEL, D_FF), BETA * D_MODEL ** -0.5),
        "ffn_w_down": nrm(ks[26], (DEPTH, D_FF, D_MODEL), BETA * D_FF ** -0.5),
        "ln1_g": 1.0 + nrm(ks[27], (DEPTH, D_MODEL), 0.01),
        "ln1_b": nrm(ks[28], (DEPTH, D_MODEL), 0.01),
        "ln2_g": 1.0 + nrm(ks[29], (DEPTH, D_MODEL), 0.01),
        "ln2_b": nrm(ks[30], (DEPTH, D_MODEL), 0.01),
    }


def reference(x_prompt, x_sample, cache_a_k, cache_a_v, cache_a_ik, state_b_conv, cache_c_cmp_k, cache_c_cmp_v,
              cache_c_slc_k, cache_c_slc_v, state_c_win_k, state_c_win_v, page_table, ab_w_in, ab_conv_w,
              ab_w_out, c_w_in, c_cmp_pe_k, c_cmp_w1_k, c_cmp_w2_k, c_cmp_pe_v, c_cmp_w1_v, c_cmp_w2_v, c_w_out,
              ffn_w_gate, ffn_w_up, ffn_w_down, ln1_g, ln1_b, ln2_g, ln2_b):
    xp, xs = x_prompt, x_sample
    ab_p, ab_s, c_p, c_s = [], [], [], []
    for layer in range(DEPTH):
        i = layer // 2
        if layer % 2 == 0:
            mp, st_p = ab_mixer(xp, 0, None, ab_w_in[i], ab_conv_w[i], ab_w_out[i])
            past = (gather_pages(cache_a_k, i, page_table), gather_pages(cache_a_v, i, page_table),
                    gather_pages(cache_a_ik, i, page_table), state_b_conv[i])
            ms, st_s = ab_mixer(xs, PAST_LEN, past, ab_w_in[i], ab_conv_w[i], ab_w_out[i])
            ab_p.append(st_p)
            ab_s.append(st_s)
        else:
            wts = (c_w_in[i], c_cmp_pe_k[i], c_cmp_w1_k[i], c_cmp_w2_k[i], c_cmp_pe_v[i], c_cmp_w1_v[i],
                   c_cmp_w2_v[i], c_w_out[i])
            mp, st_p = nsa_mixer(xp, 0, None, *wts)
            past = (gather_pages(cache_c_cmp_k, i, page_table), gather_pages(cache_c_cmp_v, i, page_table),
                    gather_pages(cache_c_slc_k, i, page_table), gather_pages(cache_c_slc_v, i, page_table),
                    state_c_win_k[i], state_c_win_v[i])
            ms, st_s = nsa_mixer(xs, PAST_LEN, past, *wts)
            c_p.append(st_p)
            c_s.append(st_s)
        xp = layernorm(ALPHA * xp + mp, ln1_g[layer], ln1_b[layer])
        xs = layernorm(ALPHA * xs + ms, ln1_g[layer], ln1_b[layer])
        xp = layernorm(ALPHA * xp + swiglu(xp, ffn_w_gate[layer], ffn_w_up[layer], ffn_w_down[layer]),
                       ln2_g[layer], ln2_b[layer])
        xs = layernorm(ALPHA * xs + swiglu(xs, ffn_w_gate[layer], ffn_w_up[layer], ffn_w_down[layer]),
                       ln2_g[layer], ln2_b[layer])
    stk = lambda lst, j: jnp.stack([e[j] for e in lst], 0)
    return (xp, xs,
            stk(ab_p, 0), stk(ab_p, 1), stk(ab_p, 2), stk(ab_p, 3),
            stk(c_p, 0), stk(c_p, 1), stk(c_p, 2), stk(c_p, 3), stk(c_p, 4), stk(c_p, 5),
            stk(ab_s, 0), stk(ab_s, 1), stk(ab_s, 2), stk(ab_s, 3),
            stk(c_s, 0), stk(c_s, 1), stk(c_s, 2), stk(c_s, 3), stk(c_s, 4), stk(c_s, 5))
```

```python
import functools

import numpy as np
import jax
import jax.numpy as jnp
from jax import lax
from jax.experimental import pallas as pl
from jax.experimental.pallas import tpu as pltpu

D_MODEL = 1024
DEPTH = 4
PAGE_SIZE = 128
HEAD_DIM = 64
ROPE_THETA = 10000.0
A_HEADS = 8
A_KV_HEADS = 2
IDX_HEADS = 4
IDX_DIM = 64
A_TOPK_MAX = 256
A_WIDTH = A_HEADS * HEAD_DIM
B_WIDTH = D_MODEL // 2
CONV_W = 3
C_HEADS = 16
C_KV_HEADS = 4
C_WIDTH = C_HEADS * HEAD_DIM
CMP_BLOCK = 32
CMP_STRIDE = 16
SLC_BLOCK = 64
SLC_TOPN = 16
WINDOW = 512
D_FF = ((8 * D_MODEL + 3 * 256 - 1) // (3 * 256)) * 256
LN_EPS = 1e-5
ALPHA = (2 * DEPTH) ** 0.25
Q_BLOCK = 128

F32 = jnp.float32
BF16 = jnp.bfloat16
I32 = jnp.int32

LANES = 128
SUBLANES = 8
VMEM_LIMIT_BYTES = 56 * 1024 * 1024
MASKED = -1e30
INT_MIN = -(2 ** 31)
NEG_INF_KEY = int(np.int32(np.uint32(0xFF800000) ^ np.uint32(0x7FFFFFFF)))
NT_DIMS = (((1,), (1,)), ((), ()))

A_PERM = tuple(h for j in range(4) for h in (j, 4 + j))
C_PERM = tuple(h for pr in range(2) for j in range(4) for h in (8 * pr + j, 8 * pr + 4 + j))

AB_COLS = 2688
C_COLS = 2688
C_ROPED = C_WIDTH + 3 * C_KV_HEADS * HEAD_DIM


def _params(*sem):
    return pltpu.CompilerParams(dimension_semantics=sem, vmem_limit_bytes=VMEM_LIMIT_BYTES)


def _cdiv(a, b):
    return (a + b - 1) // b


def _rope128(r, c, s, first_half):
    sw = jnp.where(first_half, pltpu.roll(r, 96, 1), pltpu.roll(r, 32, 1))
    return r * c + sw * s


def _half_masked(q_bf16, mask):
    return jnp.where(mask, q_bf16.astype(F32), 0.0).astype(BF16)


def _flash(qm, k_ref, v_ref, col, c_lo, n_chunks, ac, bias_fn):
    rows = qm.shape[0]

    def body(ci, carry):
        m, l, acc = carry
        c = c_lo + ci
        off = pl.multiple_of(c * ac, ac)
        kc = k_ref[pl.ds(off, ac), col:col + LANES]
        s = lax.dot_general(qm, kc, NT_DIMS, preferred_element_type=F32) + bias_fn(c)
        mn = jnp.maximum(m, jnp.max(s, axis=1, keepdims=True))
        alpha = jnp.exp(m - mn)
        p = jnp.exp(s - mn)
        l = alpha * l + jnp.sum(p, axis=1, keepdims=True)
        vc = v_ref[pl.ds(off, ac), col:col + LANES]
        acc = alpha * acc + jnp.dot(p.astype(BF16), vc, preferred_element_type=F32)
        return mn, l, acc

    init = (jnp.full((rows, 1), MASKED, F32), jnp.zeros((rows, 1), F32), jnp.zeros((rows, LANES), F32))
    m, l, acc = lax.fori_loop(0, n_chunks, body, init)
    return jnp.where(m > 0.5 * MASKED, acc / l, 0.0)


def _layernorm(y, g, b):
    mu = jnp.mean(y, axis=-1, keepdims=True)
    d = y - mu
    var = jnp.mean(d * d, axis=-1, keepdims=True)
    return d * lax.rsqrt(var + LN_EPS) * g + b


def _proj_ab_kernel(*refs, tm, seq_tiles, sample, dec_seq):
    if sample:
        (x_ref, w_ref, c_ref, s_ref, cx_ref, sx_ref, cw_ref, p1_ref, p2_ref,
         q_ref, iq_ref, k_ref, v_ref, ikw_ref, kvb_ref, ikb_ref, bo_ref, uo_ref, ubuf) = refs
        prev_refs = (None, p1_ref, p2_ref)
    else:
        (x_ref, w_ref, c_ref, s_ref, cx_ref, sx_ref, cw_ref,
         q_ref, iq_ref, k_ref, v_ref, ikw_ref, kvb_ref, ikb_ref, bo_ref, uo_ref, ubuf) = refs
    i = pl.program_id(0)
    y = jnp.dot(x_ref[...].astype(BF16), w_ref[...], preferred_element_type=F32)
    lane = lax.broadcasted_iota(I32, (tm, LANES), 1)
    first_half = (lane & (HEAD_DIM - 1)) < HEAD_DIM // 2
    c = c_ref[...]
    s = s_ref[...]
    ro = [_rope128(y[:, j * LANES:(j + 1) * LANES], c, s, first_half) for j in range(7)]
    ro.append(_rope128(y[:, 7 * LANES:8 * LANES], cx_ref[...], sx_ref[...], first_half))
    for j in range(4):
        q_ref[:, j * LANES:(j + 1) * LANES] = (ro[j] * HEAD_DIM ** -0.5).astype(BF16)
    k = ro[4]
    v = y[:, 1024:1152]
    k_ref[...] = k
    v_ref[...] = v
    kvb_ref[:, 0:LANES] = k.astype(BF16)
    kvb_ref[:, LANES:2 * LANES] = v.astype(BF16)
    for j in range(2):
        iq_ref[:, j * LANES:(j + 1) * LANES] = (ro[5 + j] * IDX_DIM ** -0.5).astype(BF16)
    ikw = ro[7]
    ikw_ref[...] = ikw
    ikb_ref[...] = jnp.where(lane < IDX_DIM, ikw, pltpu.roll(ikw, IDX_DIM, 1)).astype(BF16)

    gate_b = y[:, 1152:1664]
    u = y[:, 1664:2176] * y[:, 2176:2688]

    @pl.when(i % seq_tiles == 0)
    def _():
        ubuf[0:SUBLANES, :] = jnp.zeros((SUBLANES, B_WIDTH), F32)

    @pl.when(i % seq_tiles != 0)
    def _():
        ubuf[0:SUBLANES, :] = ubuf[tm:tm + SUBLANES, :]

    ubuf[SUBLANES:tm + SUBLANES, :] = u
    cw = cw_ref[...]
    conv = u * cw[CONV_W - 1:CONV_W, :]
    if sample:
        t = lax.broadcasted_iota(I32, (tm, 1), 0) % dec_seq
    for d in range(1, CONV_W):
        ud = ubuf[SUBLANES - d:tm + SUBLANES - d, :]
        if sample:
            ud = jnp.where(t >= d, ud, prev_refs[d][...])
        conv = conv + ud * cw[CONV_W - 1 - d:CONV_W - d, :]
    bo_ref[...] = (gate_b * conv).astype(BF16)
    if sample:
        uo_ref[...] = u
    else:
        uo_ref[...] = u[tm - SUBLANES:tm, :]


def _proj_ab(x2d, w, tabs, conv_w8, *, tm, seq_tiles, sample, prevs=None, dec_seq=1):
    m = x2d.shape[0]
    nt = tabs[0].shape[0] // tm
    row = lambda i: (i, 0)
    const = lambda i: (0, 0)
    tab = lambda i: (i % nt, 0)
    in_specs = [pl.BlockSpec((tm, D_MODEL), row), pl.BlockSpec((D_MODEL, AB_COLS), const)]
    in_specs += [pl.BlockSpec((tm, LANES), tab)] * 4
    in_specs += [pl.BlockSpec((SUBLANES, B_WIDTH), const)]
    args = [x2d, w, *tabs, conv_w8]
    if sample:
        in_specs += [pl.BlockSpec((tm, B_WIDTH), row)] * 2
        args += list(prevs)
    u_rows = tm if sample else SUBLANES
    out_shape = [
        jax.ShapeDtypeStruct((m, A_WIDTH), BF16),
        jax.ShapeDtypeStruct((m, IDX_HEADS * IDX_DIM), BF16),
        jax.ShapeDtypeStruct((m, LANES), F32),
        jax.ShapeDtypeStruct((m, LANES), F32),
        jax.ShapeDtypeStruct((m, LANES), F32),
        jax.ShapeDtypeStruct((m, 2 * LANES), BF16),
        jax.ShapeDtypeStruct((m, LANES), BF16),
        jax.ShapeDtypeStruct((m, B_WIDTH), BF16),
        jax.ShapeDtypeStruct((m // tm * u_rows, B_WIDTH), F32),
    ]
    out_specs = [
        pl.BlockSpec((tm, A_WIDTH), row), pl.BlockSpec((tm, IDX_HEADS * IDX_DIM), row),
        pl.BlockSpec((tm, LANES), row), pl.BlockSpec((tm, LANES), row), pl.BlockSpec((tm, LANES), row),
        pl.BlockSpec((tm, 2 * LANES), row), pl.BlockSpec((tm, LANES), row),
        pl.BlockSpec((tm, B_WIDTH), row), pl.BlockSpec((u_rows, B_WIDTH), row),
    ]
    kern = functools.partial(_proj_ab_kernel, tm=tm, seq_tiles=seq_tiles, sample=sample, dec_seq=dec_seq)
    return pl.pallas_call(
        kern, grid=(m // tm,), in_specs=in_specs, out_specs=out_specs, out_shape=out_shape,
        scratch_shapes=[pltpu.VMEM((tm + SUBLANES, B_WIDTH), F32)],
        compiler_params=_params("arbitrary"))(*args)


def _count(key_ref, n_chunks, width, rows, pred):
    def body(c, acc):
        m = jnp.where(pred(key_ref[c], c), 1.0, 0.0)
        part = m[:, 0:LANES]
        for j in range(1, width // LANES):
            part = part + m[:, j * LANES:(j + 1) * LANES]
        return acc + part
    acc = lax.fori_loop(0, n_chunks, body, jnp.zeros((rows, LANES), F32))
    return jnp.sum(acc, axis=1, keepdims=True)


def _dsa_kernel(q_ref, iq_ref, ikw_ref, ikb_ref, kvb_ref, o_ref, key_ref, bias_ref,
                *, tq, klen, pos0, topk, sc, ac, causal_skip):
    i = pl.program_id(1)
    pos = pos0 + i * tq + lax.broadcasted_iota(I32, (tq, 1), 0)
    n_sc = _cdiv((i + 1) * tq, sc) if causal_skip else klen // sc
    ratio = sc // ac
    lane = lax.broadcasted_iota(I32, (tq, LANES), 1)
    lo_half = lane < HEAD_DIM
    kiota = lax.broadcasted_iota(I32, (tq, sc), 1)

    def score_body(c, carry):
        off = pl.multiple_of(c * sc, sc)
        ikc = ikb_ref[pl.ds(off, sc), :]
        sco = jnp.zeros((tq, sc), F32)
        for h in range(IDX_HEADS):
            iqp = iq_ref[:, (h // 2) * LANES:(h // 2 + 1) * LANES]
            iqm = _half_masked(iqp, lo_half if h % 2 == 0 else jnp.logical_not(lo_half))
            logits = lax.dot_general(iqm, ikc, NT_DIMS, preferred_element_type=F32)
            sco = sco + jnp.maximum(logits, 0.0) * ikw_ref[:, IDX_DIM + h:IDX_DIM + h + 1]
        sco = jnp.where(sco == 0.0, 0.0, sco)
        bits = pltpu.bitcast(sco, I32)
        key = jnp.where(bits < 0, bits ^ jnp.int32(0x7FFFFFFF), bits)
        key_ref[c] = jnp.where(c * sc + kiota <= pos, key, jnp.int32(NEG_INF_KEY))
        return carry
    lax.fori_loop(0, n_sc, score_body, 0)

    count = functools.partial(_count, key_ref, n_sc, sc, tq)
    kf = jnp.float32(topk)

    def bit_body(b, thr):
        cand = thr + jnp.left_shift(jnp.int32(1), 31 - b)
        cnt = count(lambda key, c: key >= cand)
        return jnp.where(cnt >= kf, cand, thr)
    thr = lax.fori_loop(0, 32, bit_body, jnp.full((tq, 1), INT_MIN, I32))

    need = kf - count(lambda key, c: key > thr)

    def tie_body(b, last):
        cand = last + jnp.left_shift(jnp.int32(1), klen.bit_length() - 1 - b)
        cnt = count(lambda key, c: (key == thr) & (c * sc + kiota < cand))
        return jnp.where(cnt < need, cand, last)
    last = lax.fori_loop(0, klen.bit_length(), tie_body, jnp.zeros((tq, 1), I32))

    def bias_body(c, carry):
        key = key_ref[c]
        sel = (key > thr) | ((key == thr) & (c * sc + kiota <= last))
        sel = sel & (key > jnp.int32(NEG_INF_KEY))
        bias = jnp.where(sel, 0.0, MASKED)
        for r in range(ratio):
            bias_ref[c * ratio + r] = bias[:, r * ac:(r + 1) * ac]
        return carry
    lax.fori_loop(0, n_sc, bias_body, 0)

    n_ac = n_sc * ratio
    for j in range(A_HEADS // 2):
        qp = q_ref[:, j * LANES:(j + 1) * LANES]
        o_lo = _flash(_half_masked(qp, lo_half), kvb_ref, kvb_ref.at[:, LANES:2 * LANES], 0,
                      0, n_ac, ac, lambda c: bias_ref[c])
        o_hi = _flash(_half_masked(qp, jnp.logical_not(lo_half)), kvb_ref, kvb_ref.at[:, LANES:2 * LANES], 0,
                      0, n_ac, ac, lambda c: bias_ref[c])
        o_ref[:, j * LANES:(j + 1) * LANES] = jnp.where(lo_half, o_lo, o_hi).astype(BF16)


def _dsa_attention(q, iq, ikw, ikb, kvb, *, tq, pos0, topk, sc, ac, causal_skip):
    nb, tlen, _ = q.shape
    klen = ikb.shape[1]
    qspec = lambda w: pl.BlockSpec((None, tq, w), lambda b, i: (b, i, 0))
    kspec = lambda w: pl.BlockSpec((None, klen, w), lambda b, i: (b, 0, 0))
    kern = functools.partial(_dsa_kernel, tq=tq, klen=klen, pos0=pos0, topk=topk, sc=sc, ac=ac,
                             causal_skip=causal_skip)
    return pl.pallas_call(
        kern, grid=(nb, tlen // tq),
        in_specs=[qspec(A_WIDTH), qspec(IDX_HEADS * IDX_DIM), qspec(LANES), kspec(LANES), kspec(2 * LANES)],
        out_specs=qspec(A_WIDTH),
        out_shape=jax.ShapeDtypeStruct((nb, tlen, A_WIDTH), BF16),
        scratch_shapes=[pltpu.VMEM((klen // sc, tq, sc), I32), pltpu.VMEM((klen // ac, tq, ac), F32)],
        compiler_params=_params("parallel", "arbitrary"))(q, iq, ikw, ikb, kvb)


def _proj_c_kernel(x_ref, w_ref, c_ref, s_ref, q_ref, kc_ref, ks_ref, kw_ref, vc_ref, vs_ref, vw_ref,
                   g_ref, kvb_ref, *, tm):
    y = jnp.dot(x_ref[...].astype(BF16), w_ref[...], preferred_element_type=F32)
    lane = lax.broadcasted_iota(I32, (tm, LANES), 1)
    first_half = (lane & (HEAD_DIM - 1)) < HEAD_DIM // 2
    c = c_ref[...]
    s = s_ref[...]
    ro = [_rope128(y[:, j * LANES:(j + 1) * LANES], c, s, first_half) for j in range(C_ROPED // LANES)]
    for j in range(8):
        q_ref[:, j * LANES:(j + 1) * LANES] = (ro[j] * HEAD_DIM ** -0.5).astype(BF16)
    for n, ref in enumerate((kc_ref, ks_ref, kw_ref)):
        for j in range(2):
            ref[:, j * LANES:(j + 1) * LANES] = ro[8 + 2 * n + j]
    for n, ref in enumerate((vc_ref, vs_ref, vw_ref)):
        ref[...] = y[:, C_ROPED + 256 * n:C_ROPED + 256 * (n + 1)]
    g = y[:, C_ROPED + 768:C_ROPED + 768 + LANES]
    g_ref[...] = 1.0 / (1.0 + jnp.exp(-g))
    for j in range(2):
        kvb_ref[:, j * LANES:(j + 1) * LANES] = ro[10 + j].astype(BF16)
        kvb_ref[:, 512 + j * LANES:512 + (j + 1) * LANES] = ro[12 + j].astype(BF16)
    kvb_ref[:, 256:512] = y[:, C_ROPED + 256:C_ROPED + 512].astype(BF16)
    kvb_ref[:, 768:1024] = y[:, C_ROPED + 512:C_ROPED + 768].astype(BF16)


def _proj_c(x2d, w, tabs, *, tm):
    m = x2d.shape[0]
    nt = tabs[0].shape[0] // tm
    row = lambda i: (i, 0)
    kvw = C_KV_HEADS * HEAD_DIM
    out_shape = [jax.ShapeDtypeStruct((m, C_WIDTH), BF16)]
    out_shape += [jax.ShapeDtypeStruct((m, kvw), F32)] * 6
    out_shape += [jax.ShapeDtypeStruct((m, LANES), F32), jax.ShapeDtypeStruct((m, 4 * kvw), BF16)]
    out_specs = [pl.BlockSpec((tm, C_WIDTH), row)] + [pl.BlockSpec((tm, kvw), row)] * 6
    out_specs += [pl.BlockSpec((tm, LANES), row), pl.BlockSpec((tm, 4 * kvw), row)]
    return pl.pallas_call(
        functools.partial(_proj_c_kernel, tm=tm), grid=(m // tm,),
        in_specs=[pl.BlockSpec((tm, D_MODEL), row), pl.BlockSpec((D_MODEL, C_COLS), lambda i: (0, 0)),
                  pl.BlockSpec((tm, LANES), lambda i: (i % nt, 0)), pl.BlockSpec((tm, LANES), lambda i: (i % nt, 0))],
        out_specs=out_specs, out_shape=out_shape,
        compiler_params=_params("parallel"))(x2d, w, *tabs)


def _compress_kernel(z_ref, pea_ref, peb_ref, wa_ref, wb_ref, w2_ref, o_ref, *, rows):
    z = z_ref[...].reshape(rows, z_ref.shape[-1])
    ra = jnp.dot((z + pea_ref[...]).astype(BF16), wa_ref[...], preferred_element_type=F32)
    rb = jnp.dot((z + peb_ref[...]).astype(BF16), wb_ref[...], preferred_element_type=F32)
    hid = ra + pltpu.roll(rb, rows - 1, 0)
    hid = 0.5 * hid * (1.0 + jnp.tanh(np.sqrt(2.0 / np.pi) * (hid + 0.044715 * hid * hid * hid)))
    out = jnp.dot(hid.astype(BF16), w2_ref[...], preferred_element_type=F32)
    o_ref[...] = out.reshape(o_ref.shape).astype(BF16)


def _compress(z, pea, peb, wa, wb, w2, *, nb_step):
    nb, r, zw = z.shape
    kvw = C_KV_HEADS * HEAD_DIM
    const = lambda b: (0, 0)
    return pl.pallas_call(
        functools.partial(_compress_kernel, rows=nb_step * r), grid=(nb // nb_step,),
        in_specs=[pl.BlockSpec((nb_step, r, zw), lambda b: (b, 0, 0)),
                  pl.BlockSpec((1, zw), const), pl.BlockSpec((1, zw), const),
                  pl.BlockSpec((zw, kvw), const), pl.BlockSpec((zw, kvw), const), pl.BlockSpec((kvw, kvw), const)],
        out_specs=pl.BlockSpec((nb_step, r, kvw), lambda b: (b, 0, 0)),
        out_shape=jax.ShapeDtypeStruct((nb, r, kvw), BF16),
        compiler_params=_params("parallel"))(z, pea, peb, wa, wb, w2)


def _nsa_kernel(q_ref, g_ref, kc_ref, vc_ref, kvs_ref, kvw_ref, c2s_ref, e_ref, o_ref, bias_ref, olo_ref,
                *, tq, pos0, ncp, ls, lw, woff, ac, acw, causal_skip):
    i = pl.program_id(1)
    t = pos0 + i * tq + lax.broadcasted_iota(I32, (tq, 1), 0)
    lane = lax.broadcasted_iota(I32, (tq, LANES), 1)
    lo_half = lane < HEAD_DIM
    lane_f = lane.astype(F32)
    if causal_skip:
        n_s = _cdiv((i + 1) * tq, ac)
        w_lo = jnp.maximum(i * tq - (WINDOW - 1), 0) // acw
        n_w = ((i + 1) * tq - 1) // acw - w_lo + 1
    else:
        n_s = ls // ac
        w_lo = 0
        n_w = lw // acw
    kvw = C_KV_HEADS * HEAD_DIM
    ks_ref = kvs_ref
    vs_ref = kvs_ref.at[:, kvw:2 * kvw]
    kw_ref = kvw_ref
    vw_ref = kvw_ref.at[:, kvw:2 * kvw]
    cmp_visible = lax.broadcasted_iota(I32, (tq, ncp), 1) * CMP_STRIDE + (CMP_BLOCK - 1) <= t
    kiota_s = lax.broadcasted_iota(I32, (tq, ac), 1)
    kiota_w = lax.broadcasted_iota(I32, (tq, acw), 1)
    c2s = c2s_ref[...]

    def window_bias(c):
        kp = woff + c * acw + kiota_w
        ok = (kp >= 0) & (kp <= t) & (t - kp < WINDOW)
        return jnp.where(ok, 0.0, MASKED)

    for pr in range(C_KV_HEADS // 2):
        col = pr * LANES
        kcp = kc_ref[:, col:col + LANES]
        vcp = vc_ref[:, col:col + LANES]
        for half in range(2):
            group = 2 * pr + half
            hmask = lo_half if half == 0 else jnp.logical_not(lo_half)
            psum = jnp.zeros((tq, ncp), F32)
            qms, ocs = [], []
            for j in range(4):
                slot = pr * 4 + j
                qm = _half_masked(q_ref[:, slot * LANES:(slot + 1) * LANES], hmask)
                qms.append(qm)
                s_c = lax.dot_general(qm, kcp, NT_DIMS, preferred_element_type=F32)
                s_c = jnp.where(cmp_visible, s_c, -jnp.inf)
                m = jnp.max(s_c, axis=1, keepdims=True)
                m = jnp.where(m > -jnp.inf, m, 0.0)
                p = jnp.exp(s_c - m)
                den = jnp.sum(p, axis=1, keepdims=True)
                p = p / jnp.where(den > 0.0, den, 1.0)
                psum = psum + p
                ocs.append(jnp.dot(p.astype(BF16), vcp, preferred_element_type=F32))
            p_hi = psum.astype(BF16)
            p_lo = (psum - p_hi.astype(F32)).astype(BF16)
            imp = (jnp.dot(p_hi, c2s, preferred_element_type=F32)
                   + jnp.dot(p_lo, c2s, preferred_element_type=F32))
            cur = t // SLC_BLOCK
            forced = (lane == 0) | (lane == cur) | (lane == cur - 1)
            imp = jnp.where(forced, jnp.inf, imp)
            imp = jnp.where(lane * SLC_BLOCK <= t, imp, -jnp.inf)

            def top_body(_, carry):
                val, selm = carry
                mx = jnp.max(val, axis=1, keepdims=True)
                first = jnp.min(jnp.where(val == mx, lane_f, float(LANES)), axis=1, keepdims=True)
                pick = lane_f == first
                selm = jnp.where(pick & (mx > -jnp.inf), 1.0, selm)
                return jnp.where(pick, -jnp.inf, val), selm
            _, selm = lax.fori_loop(0, SLC_TOPN, top_body, (imp, jnp.zeros((tq, LANES), F32)))
            selb = selm.astype(BF16)

            def bias_body(c, carry):
                ex = jnp.dot(selb, e_ref[c], preferred_element_type=F32)
                ok = (ex > 0.5) & (c * ac + kiota_s <= t)
                bias_ref[c] = jnp.where(ok, 0.0, MASKED)
                return carry
            lax.fori_loop(0, n_s, bias_body, 0)

            for j in range(4):
                slot = pr * 4 + j
                head = 4 * group + j
                o_s = _flash(qms[j], ks_ref, vs_ref, col, 0, n_s, ac, lambda c: bias_ref[c])
                o_w = _flash(qms[j], kw_ref, vw_ref, col, w_lo, n_w, acw, window_bias)
                o = (g_ref[:, 3 * head:3 * head + 1] * ocs[j] + g_ref[:, 3 * head + 1:3 * head + 2] * o_s
                     + g_ref[:, 3 * head + 2:3 * head + 3] * o_w)
                if half == 0:
                    olo_ref[j] = o
                else:
                    o_ref[:, slot * LANES:(slot + 1) * LANES] = jnp.where(lo_half, olo_ref[j], o).astype(BF16)


def _nsa_attention(q, gates, kcmp, vcmp, kvs, kvw, c2s, emat, *, tq, pos0, woff, ac, acw, causal_skip,
                   s_blk=0, w_blk=0):
    nb, tlen, _ = q.shape
    ncp = kcmp.shape[1]
    ls, lw = kvs.shape[1], kvw.shape[1]
    kvwid = C_KV_HEADS * HEAD_DIM
    qspec = lambda w: pl.BlockSpec((None, tq, w), lambda b, i: (b, i, 0))
    kspec = lambda n, w, blk=0: pl.BlockSpec((None, n, w), lambda b, i: (b, 0, blk))
    kern = functools.partial(_nsa_kernel, tq=tq, pos0=pos0, ncp=ncp, ls=ls, lw=lw, woff=woff, ac=ac, acw=acw,
                             causal_skip=causal_skip)
    return pl.pallas_call(
        kern, grid=(nb, tlen // tq),
        in_specs=[qspec(C_WIDTH), qspec(LANES), kspec(ncp, kvwid), kspec(ncp, kvwid),
                  kspec(ls, 2 * kvwid, s_blk), kspec(lw, 2 * kvwid, w_blk),
                  pl.BlockSpec((ncp, LANES), lambda b, i: (0, 0)),
                  pl.BlockSpec((ls // ac, LANES, ac), lambda b, i: (0, 0, 0))],
        out_specs=qspec(C_WIDTH),
        out_shape=jax.ShapeDtypeStruct((nb, tlen, C_WIDTH), BF16),
        scratch_shapes=[pltpu.VMEM((ls // ac, tq, ac), F32), pltpu.VMEM((4, tq, LANES), F32)],
        compiler_params=_params("parallel", "arbitrary"))(q, gates, kcmp, vcmp, kvs, kvw, c2s, emat)


def _outproj_ln_kernel(x_ref, a_ref, b_ref, wa_ref, wb_ref, g_ref, bt_ref, o_ref):
    y = ALPHA * x_ref[...]
    y = y + jnp.dot(a_ref[...], wa_ref[...], preferred_element_type=F32)
    y = y + jnp.dot(b_ref[...], wb_ref[...], preferred_element_type=F32)
    o_ref[...] = _layernorm(y, g_ref[...], bt_ref[...])


def _outproj_ln(x2d, a, b, a_blk, b_blk, w_out, g, bt, *, tm):
    m = x2d.shape[0]
    half = w_out.shape[0] // 2
    row = lambda i: (i, 0)
    const = lambda i: (0, 0)
    return pl.pallas_call(
        _outproj_ln_kernel, grid=(m // tm,),
        in_specs=[pl.BlockSpec((tm, D_MODEL), row),
                  pl.BlockSpec((tm, half), lambda i: (i, a_blk)), pl.BlockSpec((tm, half), lambda i: (i, b_blk)),
                  pl.BlockSpec((half, D_MODEL), lambda i: (0, 0)), pl.BlockSpec((half, D_MODEL), lambda i: (1, 0)),
                  pl.BlockSpec((1, D_MODEL), const), pl.BlockSpec((1, D_MODEL), const)],
        out_specs=pl.BlockSpec((tm, D_MODEL), row),
        out_shape=jax.ShapeDtypeStruct((m, D_MODEL), F32),
        compiler_params=_params("parallel"))(x2d, a, b, w_out, w_out, g, bt)


def _ffn_ln_kernel(x_ref, wg_ref, wu_ref, wd_ref, g_ref, bt_ref, o_ref, xb_ref, acc_ref):
    f = pl.program_id(1)

    @pl.when(f == 0)
    def _():
        xb_ref[...] = x_ref[...].astype(BF16)
        acc_ref[...] = jnp.zeros(acc_ref.shape, F32)

    xb = xb_ref[...]
    h = jnp.dot(xb, wg_ref[...], preferred_element_type=F32)
    u = jnp.dot(xb, wu_ref[...], preferred_element_type=F32)
    a = (h / (1.0 + jnp.exp(-h))) * u
    acc_ref[...] += jnp.dot(a.astype(BF16), wd_ref[...], preferred_element_type=F32)

    @pl.when(f == pl.num_programs(1) - 1)
    def _():
        o_ref[...] = _layernorm(ALPHA * x_ref[...] + acc_ref[...], g_ref[...], bt_ref[...])


def _ffn_ln(x2d, wg, wu, wd, g, bt, *, tm, tf):
    m = x2d.shape[0]
    return pl.pallas_call(
        _ffn_ln_kernel, grid=(m // tm, D_FF // tf),
        in_specs=[pl.BlockSpec((tm, D_MODEL), lambda i, f: (i, 0)),
                  pl.BlockSpec((D_MODEL, tf), lambda i, f: (0, f)), pl.BlockSpec((D_MODEL, tf), lambda i, f: (0, f)),
                  pl.BlockSpec((tf, D_MODEL), lambda i, f: (f, 0)),
                  pl.BlockSpec((1, D_MODEL), lambda i, f: (0, 0)), pl.BlockSpec((1, D_MODEL), lambda i, f: (0, 0))],
        out_specs=pl.BlockSpec((tm, D_MODEL), lambda i, f: (i, 0)),
        out_shape=jax.ShapeDtypeStruct((m, D_MODEL), F32),
        scratch_shapes=[pltpu.VMEM((tm, D_MODEL), BF16), pltpu.VMEM((tm, D_MODEL), F32)],
        compiler_params=_params("parallel", "arbitrary"))(x2d, wg, wu, wd, g, bt)


def _rope_tables(pos):
    half = HEAD_DIM // 2
    inv = ROPE_THETA ** (-jnp.arange(half, dtype=F32) / half)
    ang = pos.astype(F32)[:, None] * inv[None, :]
    cos, sin = jnp.cos(ang), jnp.sin(ang)
    c64 = jnp.concatenate([cos, cos], 1)
    s64 = jnp.concatenate([-sin, sin], 1)
    c = jnp.concatenate([c64, c64], 1)
    s = jnp.concatenate([s64, s64], 1)
    cx = jnp.concatenate([c64, jnp.full_like(c64, IDX_HEADS ** -0.5)], 1)
    sx = jnp.concatenate([s64, jnp.zeros_like(s64)], 1)
    return c, s, cx, sx


def _perm_heads(w, perm):
    lead = w.shape[:-1]
    return w.reshape(*lead, len(perm), HEAD_DIM)[..., np.asarray(perm), :].reshape(*lead, len(perm) * HEAD_DIM)


def _ab_w_in(w):
    ab_sizes = (A_WIDTH, A_KV_HEADS * HEAD_DIM, A_KV_HEADS * HEAD_DIM, IDX_HEADS * IDX_DIM, IDX_DIM, IDX_HEADS,
                B_WIDTH, B_WIDTH, B_WIDTH)
    q, k, v, iq, ik, iw, gb, gc, h = jnp.split(w, np.cumsum(ab_sizes)[:-1].tolist(), axis=-1)
    pad = jnp.zeros((w.shape[0], LANES - IDX_DIM - IDX_HEADS), w.dtype)
    return jnp.concatenate([_perm_heads(q, A_PERM), k, iq, ik, iw, pad, v, gb, gc, h], -1).astype(BF16)


def _c_w_in(w):
    kvw = C_KV_HEADS * HEAD_DIM
    c_sizes = (C_WIDTH,) + (kvw,) * 6 + (3 * C_HEADS,)
    q, kc, vc, ks, vs, kw, vw, g = jnp.split(w, np.cumsum(c_sizes)[:-1].tolist(), axis=-1)
    pad = jnp.zeros((w.shape[0], LANES - 3 * C_HEADS), w.dtype)
    return jnp.concatenate([_perm_heads(q, C_PERM), kc, ks, kw, vc, vs, vw, g, pad], -1).astype(BF16)


def _perm_rows(w_out, perm):
    return w_out.reshape(len(perm), HEAD_DIM, w_out.shape[-1])[np.asarray(perm)].reshape(-1, w_out.shape[-1])


def _block_diag(w, n):
    eye = jnp.eye(n, dtype=w.dtype)
    out = jnp.einsum('gh,...ab->...gahb', eye, w)
    return out.reshape(*w.shape[:-2], n * w.shape[-2], n * w.shape[-1])


def _compress_weights(pe, w1, w2):
    g = C_KV_HEADS
    half = CMP_BLOCK // 2
    bd = _block_diag(w1, g)
    wa = bd[:half].reshape(half * g * HEAD_DIM, g * HEAD_DIM).astype(BF16)
    wb = bd[half:].reshape(half * g * HEAD_DIM, g * HEAD_DIM).astype(BF16)
    pet = jnp.tile(pe, (1, g))
    pea = pet[:half].reshape(1, -1)
    peb = pet[half:].reshape(1, -1)
    return pea, peb, wa, wb, _block_diag(w2, g).astype(BF16)


def _cmp_to_slc(ncp, n_cmp, n_sblk):
    n = np.arange(ncp)[:, None]
    mblk = np.arange(LANES)[None, :]
    hit = ((n * CMP_STRIDE < mblk * SLC_BLOCK + SLC_BLOCK) & (n * CMP_STRIDE + CMP_BLOCK > mblk * SLC_BLOCK)
           & (n < n_cmp) & (mblk < n_sblk))
    return jnp.asarray(hit, BF16)


def _expand_matrix(ls, ac):
    k = np.arange(ls)[None, :]
    mblk = np.arange(LANES)[:, None]
    e = (k // SLC_BLOCK == mblk).astype(np.float32)
    return jnp.asarray(e.reshape(LANES, ls // ac, ac).transpose(1, 0, 2), BF16)


def _gather_pages(pool_layer, page_table):
    g = pool_layer[page_table]
    return g.reshape(g.shape[0], g.shape[1] * g.shape[2], -1)


def _pad_rows(a, n):
    return jnp.pad(a, ((0, 0), (0, n - a.shape[1]), (0, 0)))


PROMPT_TM = 512
SAMPLE_TQ = 8
FFN_TF = 1408


def _ab_layer(x2d, nb, tlen, start, past, page_table, w_in, conv_w, tabs, *, sample):
    m = nb * tlen
    conv_w8 = jnp.pad(conv_w, ((0, SUBLANES - CONV_W), (0, 0)))
    if sample:
        k_past, v_past, ik_past, prev = past
        tt = jnp.arange(tlen)
        prevs = []
        for d in range(1, CONV_W):
            idx = jnp.clip(CONV_W - 1 + tt - d, 0, CONV_W - 2)
            prevs.append(prev[:, idx].reshape(m, B_WIDTH))
        outs = _proj_ab(x2d, w_in, tabs, conv_w8, tm=m, seq_tiles=1, sample=True, prevs=prevs, dec_seq=tlen)
    else:
        outs = _proj_ab(x2d, w_in, tabs, conv_w8, tm=PROMPT_TM, seq_tiles=tlen // PROMPT_TM, sample=False)
    q, iq, k, v, ikw, kvb, ikb, b_out, uo = outs
    if sample:
        klen_real = k_past.shape[1] + tlen
        klen = _cdiv(klen_real, LANES) * LANES
        kv_all = jnp.concatenate([jnp.concatenate([k_past, v_past], -1).astype(BF16),
                                  kvb.reshape(nb, tlen, 2 * LANES)], 1)
        ik_past2 = jnp.concatenate([ik_past, ik_past], -1).astype(BF16)
        ik_all = jnp.concatenate([ik_past2, ikb.reshape(nb, tlen, LANES)], 1)
        pad_q = lambda a: _pad_rows(a.reshape(nb, tlen, -1), SAMPLE_TQ)
        a_out = _dsa_attention(pad_q(q), pad_q(iq), pad_q(ikw), _pad_rows(ik_all, klen), _pad_rows(kv_all, klen),
                               tq=SAMPLE_TQ, pos0=start, topk=min(A_TOPK_MAX, klen_real // 4), sc=klen, ac=klen,
                               causal_skip=False)
        a_out = a_out[:, :tlen].reshape(m, A_WIDTH)
        new_conv = uo.reshape(nb, tlen, B_WIDTH)[:, tlen - (CONV_W - 1):]
    else:
        r3 = lambda a: a.reshape(nb, tlen, -1)
        a_out = _dsa_attention(r3(q), r3(iq), r3(ikw), r3(ikb), r3(kvb), tq=Q_BLOCK, pos0=0,
                               topk=min(A_TOPK_MAX, tlen // 4), sc=512, ac=256, causal_skip=True)
        a_out = a_out.reshape(m, A_WIDTH)
        new_conv = uo.reshape(nb, tlen // PROMPT_TM, SUBLANES, B_WIDTH)[:, -1, SUBLANES - (CONV_W - 1):]
    state = (k.reshape(nb, tlen, A_KV_HEADS, HEAD_DIM), v.reshape(nb, tlen, A_KV_HEADS, HEAD_DIM),
             ikw[:, :IDX_DIM].reshape(nb, tlen, IDX_DIM), new_conv)
    return a_out, b_out, state


def _c_layer(x2d, nb, tlen, start, past, w_in, cw_k, cw_v, tabs, *, sample):
    m = nb * tlen
    kvw = C_KV_HEADS * HEAD_DIM
    tm = m if sample else PROMPT_TM
    q, kc, ks, kw, vc, vs, vw, gates, kvb = _proj_c(x2d, w_in, tabs[:2], tm=tm)
    if sample:
        pkc, pvc, pks, pvs, buf_k, buf_v = past
        past_len = pkc.shape[1]
        zk = pkc.reshape(nb, past_len // CMP_STRIDE, CMP_STRIDE * kvw)
        zv = pvc.reshape(nb, past_len // CMP_STRIDE, CMP_STRIDE * kvw)
        n_cmp = (past_len + tlen - CMP_BLOCK) // CMP_STRIDE + 1
        assert n_cmp <= past_len // CMP_STRIDE, "compression blocks must lie inside the cached rows"
        kcmp = _compress(zk, *cw_k, nb_step=4)
        vcmp = _compress(zv, *cw_v, nb_step=4)
        ncp = kcmp.shape[1]
        ls_real = past_len + tlen
        ls = _cdiv(ls_real, LANES) * LANES
        n_sblk = _cdiv(ls_real, SLC_BLOCK)
        kvb3 = kvb.reshape(nb, tlen, 4 * kvw)
        kvs = jnp.concatenate([jnp.concatenate([pks, pvs], -1).astype(BF16), kvb3[..., :2 * kvw]], 1)
        nbuf = buf_k.shape[1]
        kvwin = jnp.concatenate([jnp.concatenate([buf_k.reshape(nb, nbuf, kvw), buf_v.reshape(nb, nbuf, kvw)], -1)
                                 .astype(BF16), kvb3[..., 2 * kvw:]], 1)
        lw = _cdiv(nbuf + tlen, LANES) * LANES
        pad_q = lambda a: _pad_rows(a.reshape(nb, tlen, -1), SAMPLE_TQ)
        out = _nsa_attention(pad_q(q), pad_q(gates), kcmp, vcmp, _pad_rows(kvs, ls), _pad_rows(kvwin, lw),
                             _cmp_to_slc(ncp, n_cmp, n_sblk), _expand_matrix(ls, ls),
                             tq=SAMPLE_TQ, pos0=start, woff=start - nbuf, ac=ls, acw=lw, causal_skip=False)
        out = out[:, :tlen].reshape(m, C_WIDTH)
        r4 = lambda a: a.reshape(nb, tlen, C_KV_HEADS, HEAD_DIM)
        win_k = jnp.concatenate([buf_k, r4(kw)], 1)[:, tlen:]
        win_v = jnp.concatenate([buf_v, r4(vw)], 1)[:, tlen:]
    else:
        zk = kc.reshape(nb, tlen // CMP_STRIDE, CMP_STRIDE * kvw)
        zv = vc.reshape(nb, tlen // CMP_STRIDE, CMP_STRIDE * kvw)
        n_cmp = (tlen - CMP_BLOCK) // CMP_STRIDE + 1
        kcmp = _compress(zk, *cw_k, nb_step=1)
        vcmp = _compress(zv, *cw_v, nb_step=1)
        ncp = kcmp.shape[1]
        n_sblk = _cdiv(tlen, SLC_BLOCK)
        kvb3 = kvb.reshape(nb, tlen, 4 * kvw)
        out = _nsa_attention(q.reshape(nb, tlen, C_WIDTH), gates.reshape(nb, tlen, LANES), kcmp, vcmp, kvb3, kvb3,
                             _cmp_to_slc(ncp, n_cmp, n_sblk), _expand_matrix(tlen, 256),
                             tq=Q_BLOCK, pos0=0, woff=0, ac=256, acw=128, causal_skip=True, s_blk=0, w_blk=1)
        out = out.reshape(m, C_WIDTH)
        r4 = lambda a: a.reshape(nb, tlen, C_KV_HEADS, HEAD_DIM)
        keep = min(WINDOW, tlen)
        win_k = r4(kw)[:, tlen - keep:]
        win_v = r4(vw)[:, tlen - keep:]
    r4 = lambda a: a.reshape(nb, tlen, C_KV_HEADS, HEAD_DIM)
    return out, (r4(kc), r4(vc), r4(ks), r4(vs), win_k, win_v)


def kernel(x_prompt, x_sample, cache_a_k, cache_a_v, cache_a_ik, state_b_conv, cache_c_cmp_k, cache_c_cmp_v, cache_c_slc_k, cache_c_slc_v, state_c_win_k, state_c_win_v, page_table, ab_w_in, ab_conv_w, ab_w_out, c_w_in, c_cmp_pe_k, c_cmp_w1_k, c_cmp_w2_k, c_cmp_pe_v, c_cmp_w1_v, c_cmp_w2_v, c_w_out, ffn_w_gate, ffn_w_up, ffn_w_down, ln1_g, ln1_b, ln2_g, ln2_b):
    bp, tp, _ = x_prompt.shape
    bs, ts, _ = x_sample.shape
    past_len = page_table.shape[1] * PAGE_SIZE
    xp = x_prompt.reshape(bp * tp, D_MODEL)
    xs = x_sample.reshape(bs * ts, D_MODEL)
    tabs_p = _rope_tables(jnp.arange(tp))
    tabs_s = _rope_tables(past_len + jnp.arange(bs * ts) % ts)
    ab_p, ab_s, c_p, c_s = [], [], [], []
    for layer in range(DEPTH):
        i = layer // 2
        row1 = lambda a: a[layer].reshape(1, D_MODEL)
        if layer % 2 == 0:
            w_in = _ab_w_in(ab_w_in[i])
            w_out = jnp.concatenate([_perm_rows(ab_w_out[i][:A_WIDTH], A_PERM), ab_w_out[i][A_WIDTH:]], 0).astype(BF16)
            a_p, b_p, st_p = _ab_layer(xp, bp, tp, 0, None, None, w_in, ab_conv_w[i], tabs_p, sample=False)
            past = (_gather_pages(cache_a_k[i], page_table), _gather_pages(cache_a_v[i], page_table),
                    _gather_pages(cache_a_ik[i], page_table), state_b_conv[i])
            a_s, b_s, st_s = _ab_layer(xs, bs, ts, past_len, past, page_table, w_in, ab_conv_w[i], tabs_s, sample=True)
            ab_p.append(st_p)
            ab_s.append(st_s)
            xp = _outproj_ln(xp, a_p, b_p, 0, 0, w_out, row1(ln1_g), row1(ln1_b), tm=PROMPT_TM)
            xs = _outproj_ln(xs, a_s, b_s, 0, 0, w_out, row1(ln1_g), row1(ln1_b), tm=bs * ts)
        else:
            w_in = _c_w_in(c_w_in[i])
            w_out = _perm_rows(c_w_out[i], C_PERM).astype(BF16)
            cw_k = _compress_weights(c_cmp_pe_k[i], c_cmp_w1_k[i], c_cmp_w2_k[i])
            cw_v = _compress_weights(c_cmp_pe_v[i], c_cmp_w1_v[i], c_cmp_w2_v[i])
            o_p, st_p = _c_layer(xp, bp, tp, 0, None, w_in, cw_k, cw_v, tabs_p, sample=False)
            past = (_gather_pages(cache_c_cmp_k[i], page_table), _gather_pages(cache_c_cmp_v[i], page_table),
                    _gather_pages(cache_c_slc_k[i], page_table), _gather_pages(cache_c_slc_v[i], page_table),
                    state_c_win_k[i], state_c_win_v[i])
            o_s, st_s = _c_layer(xs, bs, ts, past_len, past, w_in, cw_k, cw_v, tabs_s, sample=True)
            c_p.append(st_p)
            c_s.append(st_s)
            xp = _outproj_ln(xp, o_p, o_p, 0, 1, w_out, row1(ln1_g), row1(ln1_b), tm=PROMPT_TM)
            xs = _outproj_ln(xs, o_s, o_s, 0, 1, w_out, row1(ln1_g), row1(ln1_b), tm=bs * ts)
        wg, wu, wd = ffn_w_gate[layer].astype(BF16), ffn_w_up[layer].astype(BF16), ffn_w_down[layer].astype(BF16)
        xp = _ffn_ln(xp, wg, wu, wd, row1(ln2_g), row1(ln2_b), tm=PROMPT_TM, tf=FFN_TF)
        xs = _ffn_ln(xs, wg, wu, wd, row1(ln2_g), row1(ln2_b), tm=bs * ts, tf=FFN_TF)
    stk = lambda lst, j: jnp.stack([e[j] for e in lst], 0)
    return (xp.reshape(bp, tp, D_MODEL), xs.reshape(bs, ts, D_MODEL),
            stk(ab_p, 0), stk(ab_p, 1), stk(ab_p, 2), stk(ab_p, 3),
            stk(c_p, 0), stk(c_p, 1), stk(c_p, 2), stk(c_p, 3), stk(c_p, 4), stk(c_p, 5),
            stk(ab_s, 0), stk(ab_s, 1), stk(ab_s, 2), stk(ab_s, 3),
            stk(c_s, 0), stk(c_s, 1), stk(c_s, 2), stk(c_s, 3), stk(c_s, 4), stk(c_s, 5))
```

```python
import functools

import numpy as np
import jax
import jax.numpy as jnp
from jax import lax
from jax.experimental import pallas as pl
from jax.experimental.pallas import tpu as pltpu

D_MODEL = 1024
DEPTH = 4
PAGE_SIZE = 128
HEAD_DIM = 64
ROPE_THETA = 10000.0
A_HEADS = 8
A_KV_HEADS = 2
IDX_HEADS = 4
IDX_DIM = 64
A_TOPK_MAX = 256
A_WIDTH = A_HEADS * HEAD_DIM
B_WIDTH = D_MODEL // 2
CONV_W = 3
C_HEADS = 16
C_KV_HEADS = 4
C_WIDTH = C_HEADS * HEAD_DIM
CMP_BLOCK = 32
CMP_STRIDE = 16
SLC_BLOCK = 64
SLC_TOPN = 16
WINDOW = 512
D_FF = ((8 * D_MODEL + 3 * 256 - 1) // (3 * 256)) * 256
LN_EPS = 1e-5
ALPHA = (2 * DEPTH) ** 0.25
Q_BLOCK = 128

F32 = jnp.float32
BF16 = jnp.bfloat16
I32 = jnp.int32

LANES = 128
SUBLANES = 8
VMEM_LIMIT_BYTES = 56 * 1024 * 1024
MASKED = -1e30
Q_SCALE = HEAD_DIM ** -0.5 * float(np.log2(np.e))
INT_MIN = -(2 ** 31)
NEG_INF_KEY = int(np.int32(np.uint32(0xFF800000) ^ np.uint32(0x7FFFFFFF)))
NT_DIMS = (((1,), (1,)), ((), ()))

A_PERM = tuple(h for j in range(4) for h in (j, 4 + j))
C_PERM = tuple(h for pr in range(2) for j in range(4) for h in (8 * pr + j, 8 * pr + 4 + j))

AB_COLS = 2688
C_COLS = 2688
C_ROPED = C_WIDTH + 3 * C_KV_HEADS * HEAD_DIM


def _params(*sem):
    return pltpu.CompilerParams(dimension_semantics=sem, vmem_limit_bytes=VMEM_LIMIT_BYTES)


def _cdiv(a, b):
    return (a + b - 1) // b


def _rope128(r, c, s, first_half):
    sw = jnp.where(first_half, pltpu.roll(r, 96, 1), pltpu.roll(r, 32, 1))
    return r * c + sw * s


def _half_masked(q_bf16, mask):
    return jnp.where(mask, q_bf16.astype(F32), 0.0).astype(BF16)


def _flash(qs, heads, k_ref, v_ref, col, c_lo, n_chunks, ac, bias_fn, acc_ref, m_ref, l_ref, splits=2):
    tq = qs.shape[0] // heads
    reps = ac // LANES
    acc_ref[...] = jnp.zeros(acc_ref.shape, F32)
    m_ref[...] = jnp.full(m_ref.shape, MASKED, F32)
    l_ref[...] = jnp.zeros(l_ref.shape, F32)

    hps = heads // splits

    def body(ci, carry):
        c = c_lo + ci
        off = pl.multiple_of(c * ac, ac)
        kc = k_ref[pl.ds(off, ac), col:col + LANES]
        vc = v_ref[pl.ds(off, ac), col:col + LANES]
        head_bias = bias_fn(c)
        for g in range(splits):
            s = lax.dot_general(qs[g * hps * tq:(g + 1) * hps * tq], kc, NT_DIMS, preferred_element_type=F32)
            ps, alphas = [], []
            for hh in range(hps):
                rows = slice((g * hps + hh) * tq, (g * hps + hh + 1) * tq)
                sh = s[hh * tq:(hh + 1) * tq] + head_bias(g * hps + hh)
                m_old = m_ref[rows]
                m_new = jnp.maximum(m_old, jnp.max(sh, axis=1, keepdims=True))
                alpha = jnp.exp2(m_old - m_new)
                p = jnp.exp2(sh - jnp.tile(m_new, (1, reps)))
                l_ref[rows] = alpha * l_ref[rows] + jnp.sum(p, axis=1, keepdims=True)
                m_ref[rows] = m_new
                ps.append(p.astype(BF16))
                alphas.append(alpha)
            pv = jnp.dot(jnp.concatenate(ps, 0), vc, preferred_element_type=F32)
            for hh in range(hps):
                rows = slice((g * hps + hh) * tq, (g * hps + hh + 1) * tq)
                acc_ref[rows] = alphas[hh] * acc_ref[rows] + pv[hh * tq:(hh + 1) * tq]
        return carry

    lax.fori_loop(0, n_chunks, body, 0)
    acc_ref[...] = jnp.where(m_ref[...] > 0.5 * MASKED, acc_ref[...] / l_ref[...], 0.0)


def _layernorm(y, g, b):
    mu = jnp.mean(y, axis=-1, keepdims=True)
    d = y - mu
    var = jnp.mean(d * d, axis=-1, keepdims=True)
    return d * lax.rsqrt(var + LN_EPS) * g + b


def _proj_ab_kernel(*refs, tm, seq_tiles, sample, dec_seq):
    if sample:
        (x_ref, w_ref, c_ref, s_ref, cx_ref, sx_ref, cw_ref, p1_ref, p2_ref,
         q_ref, iq_ref, k_ref, v_ref, ikw_ref, kvb_ref, ikb_ref, bo_ref, uo_ref, ubuf) = refs
        prev_refs = (None, p1_ref, p2_ref)
    else:
        (x_ref, w_ref, c_ref, s_ref, cx_ref, sx_ref, cw_ref,
         q_ref, iq_ref, k_ref, v_ref, ikw_ref, kvb_ref, ikb_ref, bo_ref, uo_ref, ubuf) = refs
    i = pl.program_id(0)
    y = jnp.dot(x_ref[...].astype(BF16), w_ref[...], preferred_element_type=F32)
    lane = lax.broadcasted_iota(I32, (tm, LANES), 1)
    first_half = (lane & (HEAD_DIM - 1)) < HEAD_DIM // 2
    c = c_ref[...]
    s = s_ref[...]
    ro = [_rope128(y[:, j * LANES:(j + 1) * LANES], c, s, first_half) for j in range(7)]
    ro.append(_rope128(y[:, 7 * LANES:8 * LANES], cx_ref[...], sx_ref[...], first_half))
    for j in range(4):
        q_ref[:, j * LANES:(j + 1) * LANES] = (ro[j] * Q_SCALE).astype(BF16)
    k = ro[4]
    v = y[:, 1024:1152]
    k_ref[...] = k
    v_ref[...] = v
    kvb_ref[:, 0:LANES] = k.astype(BF16)
    kvb_ref[:, LANES:2 * LANES] = v.astype(BF16)
    for j in range(2):
        iq_ref[:, j * LANES:(j + 1) * LANES] = (ro[5 + j] * IDX_DIM ** -0.5).astype(BF16)
    ikw = ro[7]
    ikw_ref[...] = ikw
    ikb_ref[...] = jnp.where(lane < IDX_DIM, ikw, pltpu.roll(ikw, IDX_DIM, 1)).astype(BF16)

    gate_b = y[:, 1152:1664]
    u = y[:, 1664:2176] * y[:, 2176:2688]

    @pl.when(i % seq_tiles == 0)
    def _():
        ubuf[0:SUBLANES, :] = jnp.zeros((SUBLANES, B_WIDTH), F32)

    @pl.when(i % seq_tiles != 0)
    def _():
        ubuf[0:SUBLANES, :] = ubuf[tm:tm + SUBLANES, :]

    ubuf[SUBLANES:tm + SUBLANES, :] = u
    cw = cw_ref[...]
    conv = u * cw[CONV_W - 1:CONV_W, :]
    if sample:
        t = lax.broadcasted_iota(I32, (tm, 1), 0) % dec_seq
    for d in range(1, CONV_W):
        ud = ubuf[SUBLANES - d:tm + SUBLANES - d, :]
        if sample:
            ud = jnp.where(t >= d, ud, prev_refs[d][...])
        conv = conv + ud * cw[CONV_W - 1 - d:CONV_W - d, :]
    bo_ref[...] = (gate_b * conv).astype(BF16)
    if sample:
        uo_ref[...] = u
    else:
        uo_ref[...] = u[tm - SUBLANES:tm, :]


def _proj_ab(x2d, w, tabs, conv_w8, *, tm, seq_tiles, sample, prevs=None, dec_seq=1):
    m = x2d.shape[0]
    nt = tabs[0].shape[0] // tm
    row = lambda i: (i, 0)
    const = lambda i: (0, 0)
    tab = lambda i: (i % nt, 0)
    in_specs = [pl.BlockSpec((tm, D_MODEL), row), pl.BlockSpec((D_MODEL, AB_COLS), const)]
    in_specs += [pl.BlockSpec((tm, LANES), tab)] * 4
    in_specs += [pl.BlockSpec((SUBLANES, B_WIDTH), const)]
    args = [x2d, w, *tabs, conv_w8]
    if sample:
        in_specs += [pl.BlockSpec((tm, B_WIDTH), row)] * 2
        args += list(prevs)
    u_rows = tm if sample else SUBLANES
    out_shape = [
        jax.ShapeDtypeStruct((m, A_WIDTH), BF16),
        jax.ShapeDtypeStruct((m, IDX_HEADS * IDX_DIM), BF16),
        jax.ShapeDtypeStruct((m, LANES), F32),
        jax.ShapeDtypeStruct((m, LANES), F32),
        jax.ShapeDtypeStruct((m, LANES), F32),
        jax.ShapeDtypeStruct((m, 2 * LANES), BF16),
        jax.ShapeDtypeStruct((m, LANES), BF16),
        jax.ShapeDtypeStruct((m, B_WIDTH), BF16),
        jax.ShapeDtypeStruct((m // tm * u_rows, B_WIDTH), F32),
    ]
    out_specs = [
        pl.BlockSpec((tm, A_WIDTH), row), pl.BlockSpec((tm, IDX_HEADS * IDX_DIM), row),
        pl.BlockSpec((tm, LANES), row), pl.BlockSpec((tm, LANES), row), pl.BlockSpec((tm, LANES), row),
        pl.BlockSpec((tm, 2 * LANES), row), pl.BlockSpec((tm, LANES), row),
        pl.BlockSpec((tm, B_WIDTH), row), pl.BlockSpec((u_rows, B_WIDTH), row),
    ]
    kern = functools.partial(_proj_ab_kernel, tm=tm, seq_tiles=seq_tiles, sample=sample, dec_seq=dec_seq)
    return pl.pallas_call(
        kern, grid=(m // tm,), in_specs=in_specs, out_specs=out_specs, out_shape=out_shape,
        scratch_shapes=[pltpu.VMEM((tm + SUBLANES, B_WIDTH), F32)],
        compiler_params=_params("arbitrary"))(*args)


def _count(key_ref, n_chunks, width, rows, pred):
    def body(c, acc):
        m = jnp.where(pred(key_ref[c], c), 1.0, 0.0)
        part = m[:, 0:LANES]
        for j in range(1, width // LANES):
            part = part + m[:, j * LANES:(j + 1) * LANES]
        return acc + part
    acc = lax.fori_loop(0, n_chunks, body, jnp.zeros((rows, LANES), F32))
    return jnp.sum(acc, axis=1, keepdims=True)


def _dsa_kernel(q_ref, iq_ref, ikw_ref, ikb_ref, kvb_ref, o_ref, key_ref, bias_ref, acc_ref, m_ref, l_ref,
                *, tq, klen, pos0, topk, sc, ac, causal_skip):
    i = pl.program_id(1)
    pos = pos0 + i * tq + lax.broadcasted_iota(I32, (tq, 1), 0)
    n_sc = _cdiv((i + 1) * tq, sc) if causal_skip else klen // sc
    ratio = sc // ac
    lane = lax.broadcasted_iota(I32, (tq, LANES), 1)
    lo_half = lane < HEAD_DIM
    hi_half = jnp.logical_not(lo_half)
    kiota = lax.broadcasted_iota(I32, (tq, sc), 1)

    iqs = jnp.concatenate([_half_masked(iq_ref[:, (h // 2) * LANES:(h // 2 + 1) * LANES],
                                        lo_half if h % 2 == 0 else hi_half) for h in range(IDX_HEADS)], 0)

    def score_body(c, carry):
        off = pl.multiple_of(c * sc, sc)
        logits = lax.dot_general(iqs, ikb_ref[pl.ds(off, sc), :], NT_DIMS, preferred_element_type=F32)
        sco = jnp.zeros((tq, sc), F32)
        for h in range(IDX_HEADS):
            sco = sco + jnp.maximum(logits[h * tq:(h + 1) * tq], 0.0) * ikw_ref[:, IDX_DIM + h:IDX_DIM + h + 1]
        sco = jnp.where(sco == 0.0, 0.0, sco)
        bits = pltpu.bitcast(sco, I32)
        key = jnp.where(bits < 0, bits ^ jnp.int32(0x7FFFFFFF), bits)
        key_ref[c] = jnp.where(c * sc + kiota <= pos, key, jnp.int32(NEG_INF_KEY))
        return carry
    lax.fori_loop(0, n_sc, score_body, 0)

    count = functools.partial(_count, key_ref, n_sc, sc, tq)
    kf = jnp.float32(topk)

    def bit_body(b, carry):
        thr, n_ge = carry
        cand = thr + jnp.left_shift(jnp.int32(1), 31 - b)
        cnt = count(lambda key, c: key >= cand)
        take = cnt >= kf
        return jnp.where(take, cand, thr), jnp.where(take, cnt, n_ge)
    thr, n_ge = lax.fori_loop(0, 32, bit_body, (jnp.full((tq, 1), INT_MIN, I32),
                                                jnp.full((tq, 1), (n_sc * sc).astype(F32) if causal_skip
                                                         else float(klen), F32)))

    def tie_search():
        need = kf - count(lambda key, c: key > thr)

        def tie_body(b, last):
            cand = last + jnp.left_shift(jnp.int32(1), klen.bit_length() - 1 - b)
            cnt = count(lambda key, c: (key == thr) & (c * sc + kiota < cand))
            return jnp.where(cnt < need, cand, last)
        return lax.fori_loop(0, klen.bit_length(), tie_body, jnp.zeros((tq, 1), I32))

    tied = (n_ge > kf) & (thr > jnp.int32(NEG_INF_KEY))
    last = lax.cond(jnp.max(jnp.where(tied, 1.0, 0.0)) > 0.5, tie_search, lambda: jnp.full((tq, 1), klen, I32))

    def bias_body(c, carry):
        key = key_ref[c]
        sel = (key > thr) | ((key == thr) & (c * sc + kiota <= last))
        sel = sel & (key > jnp.int32(NEG_INF_KEY))
        bias = jnp.where(sel, 0.0, MASKED)
        for r in range(ratio):
            bias_ref[c * ratio + r] = bias[:, r * ac:(r + 1) * ac]
        return carry
    lax.fori_loop(0, n_sc, bias_body, 0)

    half_slots = A_HEADS // 2
    qs = jnp.concatenate([_half_masked(q_ref[:, (h % half_slots) * LANES:(h % half_slots + 1) * LANES],
                                       lo_half if h < half_slots else hi_half) for h in range(A_HEADS)], 0)
    def shared_bias(c):
        bias = bias_ref[c]
        return lambda h: bias
    _flash(qs, A_HEADS, kvb_ref, kvb_ref.at[:, LANES:2 * LANES], 0, 0, n_sc * ratio, ac, shared_bias,
           acc_ref, m_ref, l_ref)
    for j in range(half_slots):
        o_lo = acc_ref[j * tq:(j + 1) * tq]
        o_hi = acc_ref[(half_slots + j) * tq:(half_slots + j + 1) * tq]
        o_ref[:, j * LANES:(j + 1) * LANES] = jnp.where(lo_half, o_lo, o_hi).astype(BF16)


def _dsa_attention(q, iq, ikw, ikb, kvb, *, tq, pos0, topk, sc, ac, causal_skip):
    nb, tlen, _ = q.shape
    klen = ikb.shape[1]
    qspec = lambda w: pl.BlockSpec((None, tq, w), lambda b, i: (b, i, 0))
    kspec = lambda w: pl.BlockSpec((None, klen, w), lambda b, i: (b, 0, 0))
    kern = functools.partial(_dsa_kernel, tq=tq, klen=klen, pos0=pos0, topk=topk, sc=sc, ac=ac,
                             causal_skip=causal_skip)
    return pl.pallas_call(
        kern, grid=(nb, tlen // tq),
        in_specs=[qspec(A_WIDTH), qspec(IDX_HEADS * IDX_DIM), qspec(LANES), kspec(LANES), kspec(2 * LANES)],
        out_specs=qspec(A_WIDTH),
        out_shape=jax.ShapeDtypeStruct((nb, tlen, A_WIDTH), BF16),
        scratch_shapes=[pltpu.VMEM((klen // sc, tq, sc), I32), pltpu.VMEM((klen // ac, tq, ac), F32)]
        + [pltpu.VMEM((A_HEADS * tq, LANES), F32)] * 3,
        compiler_params=_params("parallel", "arbitrary"))(q, iq, ikw, ikb, kvb)


def _proj_c_kernel(x_ref, w_ref, c_ref, s_ref, q_ref, kc_ref, ks_ref, kw_ref, vc_ref, vs_ref, vw_ref,
                   g_ref, kvb_ref, *, tm):
    y = jnp.dot(x_ref[...].astype(BF16), w_ref[...], preferred_element_type=F32)
    lane = lax.broadcasted_iota(I32, (tm, LANES), 1)
    first_half = (lane & (HEAD_DIM - 1)) < HEAD_DIM // 2
    c = c_ref[...]
    s = s_ref[...]
    ro = [_rope128(y[:, j * LANES:(j + 1) * LANES], c, s, first_half) for j in range(C_ROPED // LANES)]
    for j in range(8):
        q_ref[:, j * LANES:(j + 1) * LANES] = (ro[j] * Q_SCALE).astype(BF16)
    for n, ref in enumerate((kc_ref, ks_ref, kw_ref)):
        for j in range(2):
            ref[:, j * LANES:(j + 1) * LANES] = ro[8 + 2 * n + j]
    for n, ref in enumerate((vc_ref, vs_ref, vw_ref)):
        ref[...] = y[:, C_ROPED + 256 * n:C_ROPED + 256 * (n + 1)]
    g = y[:, C_ROPED + 768:C_ROPED + 768 + LANES]
    g_ref[...] = 1.0 / (1.0 + jnp.exp(-g))
    for j in range(2):
        kvb_ref[:, j * LANES:(j + 1) * LANES] = ro[10 + j].astype(BF16)
        kvb_ref[:, 512 + j * LANES:512 + (j + 1) * LANES] = ro[12 + j].astype(BF16)
    kvb_ref[:, 256:512] = y[:, C_ROPED + 256:C_ROPED + 512].astype(BF16)
    kvb_ref[:, 768:1024] = y[:, C_ROPED + 512:C_ROPED + 768].astype(BF16)


def _proj_c(x2d, w, tabs, *, tm):
    m = x2d.shape[0]
    nt = tabs[0].shape[0] // tm
    row = lambda i: (i, 0)
    kvw = C_KV_HEADS * HEAD_DIM
    out_shape = [jax.ShapeDtypeStruct((m, C_WIDTH), BF16)]
    out_shape += [jax.ShapeDtypeStruct((m, kvw), F32)] * 6
    out_shape += [jax.ShapeDtypeStruct((m, LANES), F32), jax.ShapeDtypeStruct((m, 4 * kvw), BF16)]
    out_specs = [pl.BlockSpec((tm, C_WIDTH), row)] + [pl.BlockSpec((tm, kvw), row)] * 6
    out_specs += [pl.BlockSpec((tm, LANES), row), pl.BlockSpec((tm, 4 * kvw), row)]
    return pl.pallas_call(
        functools.partial(_proj_c_kernel, tm=tm), grid=(m // tm,),
        in_specs=[pl.BlockSpec((tm, D_MODEL), row), pl.BlockSpec((D_MODEL, C_COLS), lambda i: (0, 0)),
                  pl.BlockSpec((tm, LANES), lambda i: (i % nt, 0)), pl.BlockSpec((tm, LANES), lambda i: (i % nt, 0))],
        out_specs=out_specs, out_shape=out_shape,
        compiler_params=_params("parallel"))(x2d, w, *tabs)


def _compress_kernel(z_ref, pea_ref, peb_ref, wa_ref, wb_ref, w2_ref, o_ref, *, rows):
    z = z_ref[...].reshape(rows, z_ref.shape[-1])
    ra = jnp.dot((z + pea_ref[...]).astype(BF16), wa_ref[...], preferred_element_type=F32)
    rb = jnp.dot((z + peb_ref[...]).astype(BF16), wb_ref[...], preferred_element_type=F32)
    hid = ra + pltpu.roll(rb, rows - 1, 0)
    hid = 0.5 * hid * (1.0 + jnp.tanh(np.sqrt(2.0 / np.pi) * (hid + 0.044715 * hid * hid * hid)))
    out = jnp.dot(hid.astype(BF16), w2_ref[...], preferred_element_type=F32)
    o_ref[...] = out.reshape(o_ref.shape).astype(BF16)


def _compress(z, pea, peb, wa, wb, w2, *, nb_step):
    nb, r, zw = z.shape
    kvw = C_KV_HEADS * HEAD_DIM
    const = lambda b: (0, 0)
    return pl.pallas_call(
        functools.partial(_compress_kernel, rows=nb_step * r), grid=(nb // nb_step,),
        in_specs=[pl.BlockSpec((nb_step, r, zw), lambda b: (b, 0, 0)),
                  pl.BlockSpec((1, zw), const), pl.BlockSpec((1, zw), const),
                  pl.BlockSpec((zw, kvw), const), pl.BlockSpec((zw, kvw), const), pl.BlockSpec((kvw, kvw), const)],
        out_specs=pl.BlockSpec((nb_step, r, kvw), lambda b: (b, 0, 0)),
        out_shape=jax.ShapeDtypeStruct((nb, r, kvw), BF16),
        compiler_params=_params("parallel"))(z, pea, peb, wa, wb, w2)


def _nsa_kernel(q_ref, g_ref, kc_ref, vc_ref, kvs_ref, kvw_ref, c2s_ref, e_ref, o_ref,
                bias_ref, part_ref, acc_ref, m_ref, l_ref,
                *, tq, pos0, ncp, ls, lw, woff, ac, acw, causal_skip):
    i = pl.program_id(1)
    t = pos0 + i * tq + lax.broadcasted_iota(I32, (tq, 1), 0)
    lane = lax.broadcasted_iota(I32, (tq, LANES), 1)
    lo_half = lane < HEAD_DIM
    if causal_skip:
        n_s = _cdiv((i + 1) * tq, ac)
        w_lo = jnp.maximum(i * tq - (WINDOW - 1), 0) // acw
        n_w = ((i + 1) * tq - 1) // acw - w_lo + 1
    else:
        n_s = ls // ac
        w_lo = 0
        n_w = lw // acw
    kvw = C_KV_HEADS * HEAD_DIM
    ks_ref = kvs_ref
    vs_ref = kvs_ref.at[:, kvw:2 * kvw]
    kw_ref = kvw_ref
    vw_ref = kvw_ref.at[:, kvw:2 * kvw]
    hi_half = jnp.logical_not(lo_half)
    cmp_visible = lax.broadcasted_iota(I32, (tq, ncp), 1) * CMP_STRIDE + (CMP_BLOCK - 1) <= t
    kiota_w = lax.broadcasted_iota(I32, (tq, acw), 1)
    c2s = c2s_ref[...]
    ng = C_KV_HEADS
    hpg = C_HEADS // C_KV_HEADS
    nh = 2 * hpg
    head_rows = lambda h: slice(h * tq, (h + 1) * tq)
    tg = jnp.concatenate([t] * ng, 0)
    lane_g = lax.broadcasted_iota(I32, (ng * tq, LANES), 1)
    lane_gf = lane_g.astype(F32)
    kiota_sg = lax.broadcasted_iota(I32, (ng * tq, ac), 1)

    def gate(pr, h, branch):
        head = hpg * (2 * pr + h // hpg) + h % hpg
        return g_ref[:, 3 * head + branch:3 * head + branch + 1]

    def window_bias(c):
        kp = woff + c * acw + kiota_w
        ok = (kp >= 0) & (kp <= t) & (t - kp < WINDOW)
        bias = jnp.where(ok, 0.0, MASKED)
        return lambda h: bias

    def stacked_q(pr):
        return jnp.concatenate(
            [_half_masked(q_ref[:, (pr * hpg + h % hpg) * LANES:(pr * hpg + h % hpg + 1) * LANES],
                          lo_half if h < hpg else hi_half) for h in range(nh)], 0)

    psums = []
    for pr in range(ng // 2):
        col = pr * LANES
        s_c = lax.dot_general(stacked_q(pr), kc_ref[:, col:col + LANES], NT_DIMS, preferred_element_type=F32)
        pair_sums = [jnp.zeros((tq, ncp), F32), jnp.zeros((tq, ncp), F32)]
        ps = []
        for h in range(nh):
            sh = jnp.where(cmp_visible, s_c[head_rows(h)], -jnp.inf)
            m = jnp.max(sh, axis=1, keepdims=True)
            m = jnp.where(m > -jnp.inf, m, 0.0)
            p = jnp.exp2(sh - m)
            den = jnp.sum(p, axis=1, keepdims=True)
            p = p / jnp.where(den > 0.0, den, 1.0)
            pair_sums[h // hpg] = pair_sums[h // hpg] + p
            ps.append(p.astype(BF16))
        part_ref[pr * nh * tq:(pr + 1) * nh * tq] = jnp.dot(jnp.concatenate(ps, 0), vc_ref[:, col:col + LANES],
                                                           preferred_element_type=F32)
        psums += pair_sums

    psum = jnp.concatenate(psums, 0)
    p_hi = psum.astype(BF16)
    p_lo = (psum - p_hi.astype(F32)).astype(BF16)
    imp = jnp.dot(p_hi, c2s, preferred_element_type=F32) + jnp.dot(p_lo, c2s, preferred_element_type=F32)
    cur = tg // SLC_BLOCK
    forced = (lane_g == 0) | (lane_g == cur) | (lane_g == cur - 1)
    imp = jnp.where(forced, jnp.inf, imp)
    imp = jnp.where(lane_g * SLC_BLOCK <= tg, imp, -jnp.inf)

    def top_body(_, carry):
        val, selm = carry
        mx = jnp.max(val, axis=1, keepdims=True)
        first = jnp.min(jnp.where(val == mx, lane_gf, float(LANES)), axis=1, keepdims=True)
        pick = lane_gf == first
        selm = jnp.where(pick & (mx > -jnp.inf), 1.0, selm)
        return jnp.where(pick, -jnp.inf, val), selm
    _, selm = lax.fori_loop(0, SLC_TOPN, top_body, (imp, jnp.zeros((ng * tq, LANES), F32)))
    selb = selm.astype(BF16)

    def bias_body(c, carry):
        ex = jnp.dot(selb, e_ref[c], preferred_element_type=F32)
        ok = (ex > 0.5) & (c * ac + kiota_sg <= tg)
        bias_ref[c] = jnp.where(ok, 0.0, MASKED)
        return carry
    lax.fori_loop(0, n_s, bias_body, 0)

    for pr in range(ng // 2):
        col = pr * LANES
        qs = stacked_q(pr)
        part = part_ref.at[pr * nh * tq:(pr + 1) * nh * tq]

        def selected_bias(c, pr=pr):
            return lambda h: bias_ref[c, (2 * pr + h // hpg) * tq:(2 * pr + h // hpg + 1) * tq, :]
        _flash(qs, nh, ks_ref, vs_ref, col, 0, n_s, ac, selected_bias, acc_ref, m_ref, l_ref)
        for h in range(nh):
            r = head_rows(h)
            part[r] = gate(pr, h, 0) * part[r] + gate(pr, h, 1) * acc_ref[r]
        _flash(qs, nh, kw_ref, vw_ref, col, w_lo, n_w, acw, window_bias, acc_ref, m_ref, l_ref)
        for j in range(hpg):
            slot = pr * hpg + j
            lo, hi = head_rows(j), head_rows(hpg + j)
            o_lo = part[lo] + gate(pr, j, 2) * acc_ref[lo]
            o_hi = part[hi] + gate(pr, hpg + j, 2) * acc_ref[hi]
            o_ref[:, slot * LANES:(slot + 1) * LANES] = jnp.where(lo_half, o_lo, o_hi).astype(BF16)


def _nsa_attention(q, gates, kcmp, vcmp, kvs, kvw, c2s, emat, *, tq, pos0, woff, ac, acw, causal_skip,
                   s_blk=0, w_blk=0):
    nb, tlen, _ = q.shape
    ncp = kcmp.shape[1]
    ls, lw = kvs.shape[1], kvw.shape[1]
    kvwid = C_KV_HEADS * HEAD_DIM
    qspec = lambda w: pl.BlockSpec((None, tq, w), lambda b, i: (b, i, 0))
    kspec = lambda n, w, blk=0: pl.BlockSpec((None, n, w), lambda b, i: (b, 0, blk))
    kern = functools.partial(_nsa_kernel, tq=tq, pos0=pos0, ncp=ncp, ls=ls, lw=lw, woff=woff, ac=ac, acw=acw,
                             causal_skip=causal_skip)
    return pl.pallas_call(
        kern, grid=(nb, tlen // tq),
        in_specs=[qspec(C_WIDTH), qspec(LANES), kspec(ncp, kvwid), kspec(ncp, kvwid),
                  kspec(ls, 2 * kvwid, s_blk), kspec(lw, 2 * kvwid, w_blk),
                  pl.BlockSpec((ncp, LANES), lambda b, i: (0, 0)),
                  pl.BlockSpec((ls // ac, LANES, ac), lambda b, i: (0, 0, 0))],
        out_specs=qspec(C_WIDTH),
        out_shape=jax.ShapeDtypeStruct((nb, tlen, C_WIDTH), BF16),
        scratch_shapes=[pltpu.VMEM((ls // ac, C_KV_HEADS * tq, ac), F32), pltpu.VMEM((C_HEADS * tq, LANES), F32)]
        + [pltpu.VMEM((2 * (C_HEADS // C_KV_HEADS) * tq, LANES), F32)] * 3,
        compiler_params=_params("parallel", "arbitrary"))(q, gates, kcmp, vcmp, kvs, kvw, c2s, emat)


def _outproj_ln_kernel(x_ref, a_ref, b_ref, wa_ref, wb_ref, g_ref, bt_ref, o_ref):
    y = ALPHA * x_ref[...]
    y = y + jnp.dot(a_ref[...], wa_ref[...], preferred_element_type=F32)
    y = y + jnp.dot(b_ref[...], wb_ref[...], preferred_element_type=F32)
    o_ref[...] = _layernorm(y, g_ref[...], bt_ref[...])


def _outproj_ln(x2d, a, b, a_blk, b_blk, w_out, g, bt, *, tm):
    m = x2d.shape[0]
    half = w_out.shape[0] // 2
    row = lambda i: (i, 0)
    const = lambda i: (0, 0)
    return pl.pallas_call(
        _outproj_ln_kernel, grid=(m // tm,),
        in_specs=[pl.BlockSpec((tm, D_MODEL), row),
                  pl.BlockSpec((tm, half), lambda i: (i, a_blk)), pl.BlockSpec((tm, half), lambda i: (i, b_blk)),
                  pl.BlockSpec((half, D_MODEL), lambda i: (0, 0)), pl.BlockSpec((half, D_MODEL), lambda i: (1, 0)),
                  pl.BlockSpec((1, D_MODEL), const), pl.BlockSpec((1, D_MODEL), const)],
        out_specs=pl.BlockSpec((tm, D_MODEL), row),
        out_shape=jax.ShapeDtypeStruct((m, D_MODEL), F32),
        compiler_params=_params("parallel"))(x2d, a, b, w_out, w_out, g, bt)


def _ffn_ln_kernel(x_ref, wg_ref, wu_ref, wd_ref, g_ref, bt_ref, o_ref, xb_ref, acc_ref):
    f = pl.program_id(1)

    @pl.when(f == 0)
    def _():
        xb_ref[...] = x_ref[...].astype(BF16)
        acc_ref[...] = jnp.zeros(acc_ref.shape, F32)

    xb = xb_ref[...]
    h = jnp.dot(xb, wg_ref[...], preferred_element_type=F32)
    u = jnp.dot(xb, wu_ref[...], preferred_element_type=F32)
    a = (h / (1.0 + jnp.exp(-h))) * u
    acc_ref[...] += jnp.dot(a.astype(BF16), wd_ref[...], preferred_element_type=F32)

    @pl.when(f == pl.num_programs(1) - 1)
    def _():
        o_ref[...] = _layernorm(ALPHA * x_ref[...] + acc_ref[...], g_ref[...], bt_ref[...])


def _ffn_ln(x2d, wg, wu, wd, g, bt, *, tm, tf):
    m = x2d.shape[0]
    return pl.pallas_call(
        _ffn_ln_kernel, grid=(m // tm, D_FF // tf),
        in_specs=[pl.BlockSpec((tm, D_MODEL), lambda i, f: (i, 0)),
                  pl.BlockSpec((D_MODEL, tf), lambda i, f: (0, f)), pl.BlockSpec((D_MODEL, tf), lambda i, f: (0, f)),
                  pl.BlockSpec((tf, D_MODEL), lambda i, f: (f, 0)),
                  pl.BlockSpec((1, D_MODEL), lambda i, f: (0, 0)), pl.BlockSpec((1, D_MODEL), lambda i, f: (0, 0))],
        out_specs=pl.BlockSpec((tm, D_MODEL), lambda i, f: (i, 0)),
        out_shape=jax.ShapeDtypeStruct((m, D_MODEL), F32),
        scratch_shapes=[pltpu.VMEM((tm, D_MODEL), BF16), pltpu.VMEM((tm, D_MODEL), F32)],
        compiler_params=_params("parallel", "arbitrary"))(x2d, wg, wu, wd, g, bt)


def _rope_tables(pos):
    half = HEAD_DIM // 2
    inv = ROPE_THETA ** (-jnp.arange(half, dtype=F32) / half)
    ang = pos.astype(F32)[:, None] * inv[None, :]
    cos, sin = jnp.cos(ang), jnp.sin(ang)
    c64 = jnp.concatenate([cos, cos], 1)
    s64 = jnp.concatenate([-sin, sin], 1)
    c = jnp.concatenate([c64, c64], 1)
    s = jnp.concatenate([s64, s64], 1)
    cx = jnp.concatenate([c64, jnp.full_like(c64, IDX_HEADS ** -0.5)], 1)
    sx = jnp.concatenate([s64, jnp.zeros_like(s64)], 1)
    return c, s, cx, sx


def _perm_heads(w, perm):
    lead = w.shape[:-1]
    return w.reshape(*lead, len(perm), HEAD_DIM)[..., np.asarray(perm), :].reshape(*lead, len(perm) * HEAD_DIM)


def _ab_w_in(w):
    ab_sizes = (A_WIDTH, A_KV_HEADS * HEAD_DIM, A_KV_HEADS * HEAD_DIM, IDX_HEADS * IDX_DIM, IDX_DIM, IDX_HEADS,
                B_WIDTH, B_WIDTH, B_WIDTH)
    q, k, v, iq, ik, iw, gb, gc, h = jnp.split(w, np.cumsum(ab_sizes)[:-1].tolist(), axis=-1)
    pad = jnp.zeros((w.shape[0], LANES - IDX_DIM - IDX_HEADS), w.dtype)
    return jnp.concatenate([_perm_heads(q, A_PERM), k, iq, ik, iw, pad, v, gb, gc, h], -1).astype(BF16)


def _c_w_in(w):
    kvw = C_KV_HEADS * HEAD_DIM
    c_sizes = (C_WIDTH,) + (kvw,) * 6 + (3 * C_HEADS,)
    q, kc, vc, ks, vs, kw, vw, g = jnp.split(w, np.cumsum(c_sizes)[:-1].tolist(), axis=-1)
    pad = jnp.zeros((w.shape[0], LANES - 3 * C_HEADS), w.dtype)
    return jnp.concatenate([_perm_heads(q, C_PERM), kc, ks, kw, vc, vs, vw, g, pad], -1).astype(BF16)


def _perm_rows(w_out, perm):
    return w_out.reshape(len(perm), HEAD_DIM, w_out.shape[-1])[np.asarray(perm)].reshape(-1, w_out.shape[-1])


def _block_diag(w, n):
    eye = jnp.eye(n, dtype=w.dtype)
    out = jnp.einsum('gh,...ab->...gahb', eye, w)
    return out.reshape(*w.shape[:-2], n * w.shape[-2], n * w.shape[-1])


def _compress_weights(pe, w1, w2):
    g = C_KV_HEADS
    half = CMP_BLOCK // 2
    bd = _block_diag(w1, g)
    wa = bd[:half].reshape(half * g * HEAD_DIM, g * HEAD_DIM).astype(BF16)
    wb = bd[half:].reshape(half * g * HEAD_DIM, g * HEAD_DIM).astype(BF16)
    pet = jnp.tile(pe, (1, g))
    pea = pet[:half].reshape(1, -1)
    peb = pet[half:].reshape(1, -1)
    return pea, peb, wa, wb, _block_diag(w2, g).astype(BF16)


def _cmp_to_slc(ncp, n_cmp, n_sblk):
    n = np.arange(ncp)[:, None]
    mblk = np.arange(LANES)[None, :]
    hit = ((n * CMP_STRIDE < mblk * SLC_BLOCK + SLC_BLOCK) & (n * CMP_STRIDE + CMP_BLOCK > mblk * SLC_BLOCK)
           & (n < n_cmp) & (mblk < n_sblk))
    return jnp.asarray(hit, BF16)


def _expand_matrix(ls, ac):
    k = np.arange(ls)[None, :]
    mblk = np.arange(LANES)[:, None]
    e = (k // SLC_BLOCK == mblk).astype(np.float32)
    return jnp.asarray(e.reshape(LANES, ls // ac, ac).transpose(1, 0, 2), BF16)


def _gather_pages(pool_layer, page_table):
    g = pool_layer[page_table]
    return g.reshape(g.shape[0], g.shape[1] * g.shape[2], -1)


def _pad_rows(a, n):
    return jnp.pad(a, ((0, 0), (0, n - a.shape[1]), (0, 0)))


PROMPT_TM = 512
SAMPLE_TQ = 8
FFN_TF = 1408


def _ab_layer(x2d, nb, tlen, start, past, page_table, w_in, conv_w, tabs, *, sample):
    m = nb * tlen
    conv_w8 = jnp.pad(conv_w, ((0, SUBLANES - CONV_W), (0, 0)))
    if sample:
        k_past, v_past, ik_past, prev = past
        tt = jnp.arange(tlen)
        prevs = []
        for d in range(1, CONV_W):
            idx = jnp.clip(CONV_W - 1 + tt - d, 0, CONV_W - 2)
            prevs.append(prev[:, idx].reshape(m, B_WIDTH))
        outs = _proj_ab(x2d, w_in, tabs, conv_w8, tm=m, seq_tiles=1, sample=True, prevs=prevs, dec_seq=tlen)
    else:
        outs = _proj_ab(x2d, w_in, tabs, conv_w8, tm=PROMPT_TM, seq_tiles=tlen // PROMPT_TM, sample=False)
    q, iq, k, v, ikw, kvb, ikb, b_out, uo = outs
    if sample:
        klen_real = k_past.shape[1] + tlen
        klen = _cdiv(klen_real, LANES) * LANES
        kv_all = jnp.concatenate([jnp.concatenate([k_past, v_past], -1).astype(BF16),
                                  kvb.reshape(nb, tlen, 2 * LANES)], 1)
        ik_past2 = jnp.concatenate([ik_past, ik_past], -1).astype(BF16)
        ik_all = jnp.concatenate([ik_past2, ikb.reshape(nb, tlen, LANES)], 1)
        pad_q = lambda a: _pad_rows(a.reshape(nb, tlen, -1), SAMPLE_TQ)
        a_out = _dsa_attention(pad_q(q), pad_q(iq), pad_q(ikw), _pad_rows(ik_all, klen), _pad_rows(kv_all, klen),
                               tq=SAMPLE_TQ, pos0=start, topk=min(A_TOPK_MAX, klen_real // 4), sc=klen, ac=klen,
                               causal_skip=False)
        a_out = a_out[:, :tlen].reshape(m, A_WIDTH)
        new_conv = uo.reshape(nb, tlen, B_WIDTH)[:, tlen - (CONV_W - 1):]
    else:
        r3 = lambda a: a.reshape(nb, tlen, -1)
        a_out = _dsa_attention(r3(q), r3(iq), r3(ikw), r3(ikb), r3(kvb), tq=Q_BLOCK, pos0=0,
                               topk=min(A_TOPK_MAX, tlen // 4), sc=512, ac=256, causal_skip=True)
        a_out = a_out.reshape(m, A_WIDTH)
        new_conv = uo.reshape(nb, tlen // PROMPT_TM, SUBLANES, B_WIDTH)[:, -1, SUBLANES - (CONV_W - 1):]
    state = (k.reshape(nb, tlen, A_KV_HEADS, HEAD_DIM), v.reshape(nb, tlen, A_KV_HEADS, HEAD_DIM),
             ikw[:, :IDX_DIM].reshape(nb, tlen, IDX_DIM), new_conv)
    return a_out, b_out, state


def _c_layer(x2d, nb, tlen, start, past, w_in, cw_k, cw_v, tabs, *, sample):
    m = nb * tlen
    kvw = C_KV_HEADS * HEAD_DIM
    tm = m if sample else PROMPT_TM
    q, kc, ks, kw, vc, vs, vw, gates, kvb = _proj_c(x2d, w_in, tabs[:2], tm=tm)
    if sample:
        pkc, pvc, pks, pvs, buf_k, buf_v = past
        past_len = pkc.shape[1]
        zk = pkc.reshape(nb, past_len // CMP_STRIDE, CMP_STRIDE * kvw)
        zv = pvc.reshape(nb, past_len // CMP_STRIDE, CMP_STRIDE * kvw)
        n_cmp = (past_len + tlen - CMP_BLOCK) // CMP_STRIDE + 1
        assert n_cmp <= past_len // CMP_STRIDE, "compression blocks must lie inside the cached rows"
        kcmp = _compress(zk, *cw_k, nb_step=4)
        vcmp = _compress(zv, *cw_v, nb_step=4)
        ncp = kcmp.shape[1]
        ls_real = past_len + tlen
        ls = _cdiv(ls_real, LANES) * LANES
        n_sblk = _cdiv(ls_real, SLC_BLOCK)
        kvb3 = kvb.reshape(nb, tlen, 4 * kvw)
        kvs = jnp.concatenate([jnp.concatenate([pks, pvs], -1).astype(BF16), kvb3[..., :2 * kvw]], 1)
        nbuf = buf_k.shape[1]
        kvwin = jnp.concatenate([jnp.concatenate([buf_k.reshape(nb, nbuf, kvw), buf_v.reshape(nb, nbuf, kvw)], -1)
                                 .astype(BF16), kvb3[..., 2 * kvw:]], 1)
        lw = _cdiv(nbuf + tlen, LANES) * LANES
        pad_q = lambda a: _pad_rows(a.reshape(nb, tlen, -1), SAMPLE_TQ)
        out = _nsa_attention(pad_q(q), pad_q(gates), kcmp, vcmp, _pad_rows(kvs, ls), _pad_rows(kvwin, lw),
                             _cmp_to_slc(ncp, n_cmp, n_sblk), _expand_matrix(ls, ls),
                             tq=SAMPLE_TQ, pos0=start, woff=start - nbuf, ac=ls, acw=lw, causal_skip=False)
        out = out[:, :tlen].reshape(m, C_WIDTH)
        r4 = lambda a: a.reshape(nb, tlen, C_KV_HEADS, HEAD_DIM)
        win_k = jnp.concatenate([buf_k, r4(kw)], 1)[:, tlen:]
        win_v = jnp.concatenate([buf_v, r4(vw)], 1)[:, tlen:]
    else:
        zk = kc.reshape(nb, tlen // CMP_STRIDE, CMP_STRIDE * kvw)
        zv = vc.reshape(nb, tlen // CMP_STRIDE, CMP_STRIDE * kvw)
        n_cmp = (tlen - CMP_BLOCK) // CMP_STRIDE + 1
        kcmp = _compress(zk, *cw_k, nb_step=1)
        vcmp = _compress(zv, *cw_v, nb_step=1)
        ncp = kcmp.shape[1]
        n_sblk = _cdiv(tlen, SLC_BLOCK)
        kvb3 = kvb.reshape(nb, tlen, 4 * kvw)
        out = _nsa_attention(q.reshape(nb, tlen, C_WIDTH), gates.reshape(nb, tlen, LANES), kcmp, vcmp, kvb3, kvb3,
                             _cmp_to_slc(ncp, n_cmp, n_sblk), _expand_matrix(tlen, 256),
                             tq=Q_BLOCK, pos0=0, woff=0, ac=256, acw=256, causal_skip=True, s_blk=0, w_blk=1)
        out = out.reshape(m, C_WIDTH)
        r4 = lambda a: a.reshape(nb, tlen, C_KV_HEADS, HEAD_DIM)
        keep = min(WINDOW, tlen)
        win_k = r4(kw)[:, tlen - keep:]
        win_v = r4(vw)[:, tlen - keep:]
    r4 = lambda a: a.reshape(nb, tlen, C_KV_HEADS, HEAD_DIM)
    return out, (r4(kc), r4(vc), r4(ks), r4(vs), win_k, win_v)


def kernel(x_prompt, x_sample, cache_a_k, cache_a_v, cache_a_ik, state_b_conv, cache_c_cmp_k, cache_c_cmp_v, cache_c_slc_k, cache_c_slc_v, state_c_win_k, state_c_win_v, page_table, ab_w_in, ab_conv_w, ab_w_out, c_w_in, c_cmp_pe_k, c_cmp_w1_k, c_cmp_w2_k, c_cmp_pe_v, c_cmp_w1_v, c_cmp_w2_v, c_w_out, ffn_w_gate, ffn_w_up, ffn_w_down, ln1_g, ln1_b, ln2_g, ln2_b):
    bp, tp, _ = x_prompt.shape
    bs, ts, _ = x_sample.shape
    past_len = page_table.shape[1] * PAGE_SIZE
    xp = x_prompt.reshape(bp * tp, D_MODEL)
    xs = x_sample.reshape(bs * ts, D_MODEL)
    tabs_p = _rope_tables(jnp.arange(tp))
    tabs_s = _rope_tables(past_len + jnp.arange(bs * ts) % ts)
    ab_p, ab_s, c_p, c_s = [], [], [], []
    for layer in range(DEPTH):
        i = layer // 2
        row1 = lambda a: a[layer].reshape(1, D_MODEL)
        if layer % 2 == 0:
            w_in = _ab_w_in(ab_w_in[i])
            w_out = jnp.concatenate([_perm_rows(ab_w_out[i][:A_WIDTH], A_PERM), ab_w_out[i][A_WIDTH:]], 0).astype(BF16)
            a_p, b_p, st_p = _ab_layer(xp, bp, tp, 0, None, None, w_in, ab_conv_w[i], tabs_p, sample=False)
            past = (_gather_pages(cache_a_k[i], page_table), _gather_pages(cache_a_v[i], page_table),
                    _gather_pages(cache_a_ik[i], page_table), state_b_conv[i])
            a_s, b_s, st_s = _ab_layer(xs, bs, ts, past_len, past, page_table, w_in, ab_conv_w[i], tabs_s, sample=True)
            ab_p.append(st_p)
            ab_s.append(st_s)
            xp = _outproj_ln(xp, a_p, b_p, 0, 0, w_out, row1(ln1_g), row1(ln1_b), tm=PROMPT_TM)
            xs = _outproj_ln(xs, a_s, b_s, 0, 0, w_out, row1(ln1_g), row1(ln1_b), tm=bs * ts)
        else:
            w_in = _c_w_in(c_w_in[i])
            w_out = _perm_rows(c_w_out[i], C_PERM).astype(BF16)
            cw_k = _compress_weights(c_cmp_pe_k[i], c_cmp_w1_k[i], c_cmp_w2_k[i])
            cw_v = _compress_weights(c_cmp_pe_v[i], c_cmp_w1_v[i], c_cmp_w2_v[i])
            o_p, st_p = _c_layer(xp, bp, tp, 0, None, w_in, cw_k, cw_v, tabs_p, sample=False)
            past = (_gather_pages(cache_c_cmp_k[i], page_table), _gather_pages(cache_c_cmp_v[i], page_table),
                    _gather_pages(cache_c_slc_k[i], page_table), _gather_pages(cache_c_slc_v[i], page_table),
                    state_c_win_k[i], state_c_win_v[i])
            o_s, st_s = _c_layer(xs, bs, ts, past_len, past, w_in, cw_k, cw_v, tabs_s, sample=True)
            c_p.append(st_p)
            c_s.append(st_s)
            xp = _outproj_ln(xp, o_p, o_p, 0, 1, w_out, row1(ln1_g), row1(ln1_b), tm=PROMPT_TM)
            xs = _outproj_ln(xs, o_s, o_s, 0, 1, w_out, row1(ln1_g), row1(ln1_b), tm=bs * ts)
        wg, wu, wd = ffn_w_gate[layer].astype(BF16), ffn_w_up[layer].astype(BF16), ffn_w_down[layer].astype(BF16)
        xp = _ffn_ln(xp, wg, wu, wd, row1(ln2_g), row1(ln2_b), tm=PROMPT_TM, tf=FFN_TF)
        xs = _ffn_ln(xs, wg, wu, wd, row1(ln2_g), row1(ln2_b), tm=bs * ts, tf=FFN_TF)
    stk = lambda lst, j: jnp.stack([e[j] for e in lst], 0)
    return (xp.reshape(bp, tp, D_MODEL), xs.reshape(bs, ts, D_MODEL),
            stk(ab_p, 0), stk(ab_p, 1), stk(ab_p, 2), stk(ab_p, 3),
            stk(c_p, 0), stk(c_p, 1), stk(c_p, 2), stk(c_p, 3), stk(c_p, 4), stk(c_p, 5),
            stk(ab_s, 0), stk(ab_s, 1), stk(ab_s, 2), stk(ab_s, 3),
            stk(c_s, 0), stk(c_s, 1), stk(c_s, 2), stk(c_s, 3), stk(c_s, 4), stk(c_s, 5))
```

```python
import functools

import numpy as np
import jax
import jax.numpy as jnp
from jax import lax
from jax.experimental import pallas as pl
from jax.experimental.pallas import tpu as pltpu

D_MODEL = 1024
DEPTH = 4
PAGE_SIZE = 128
HEAD_DIM = 64
ROPE_THETA = 10000.0
A_HEADS = 8
A_KV_HEADS = 2
IDX_HEADS = 4
IDX_DIM = 64
A_TOPK_MAX = 256
A_WIDTH = A_HEADS * HEAD_DIM
B_WIDTH = D_MODEL // 2
CONV_W = 3
C_HEADS = 16
C_KV_HEADS = 4
C_WIDTH = C_HEADS * HEAD_DIM
CMP_BLOCK = 32
CMP_STRIDE = 16
SLC_BLOCK = 64
SLC_TOPN = 16
WINDOW = 512
D_FF = ((8 * D_MODEL + 3 * 256 - 1) // (3 * 256)) * 256
LN_EPS = 1e-5
ALPHA = (2 * DEPTH) ** 0.25
Q_BLOCK = 128

F32 = jnp.float32
BF16 = jnp.bfloat16
I32 = jnp.int32

LANES = 128
SUBLANES = 8
VMEM_LIMIT_BYTES = 56 * 1024 * 1024
MASKED = -1e30
Q_SCALE = HEAD_DIM ** -0.5 * float(np.log2(np.e))
INT_MIN = -(2 ** 31)
NEG_INF_KEY = int(np.int32(np.uint32(0xFF800000) ^ np.uint32(0x7FFFFFFF)))
NT_DIMS = (((1,), (1,)), ((), ()))

A_PERM = tuple(h for j in range(4) for h in (j, 4 + j))
C_PERM = tuple(h for pr in range(2) for j in range(4) for h in (8 * pr + j, 8 * pr + 4 + j))

AB_COLS = 2688
C_COLS = 2688
C_ROPED = C_WIDTH + 3 * C_KV_HEADS * HEAD_DIM


def _params(*sem):
    return pltpu.CompilerParams(dimension_semantics=sem, vmem_limit_bytes=VMEM_LIMIT_BYTES)


def _cdiv(a, b):
    return (a + b - 1) // b


def _rope128(r, c, s, first_half):
    sw = jnp.where(first_half, pltpu.roll(r, 96, 1), pltpu.roll(r, 32, 1))
    return r * c + sw * s


def _half_masked(q_bf16, mask):
    return jnp.where(mask, q_bf16.astype(F32), 0.0).astype(BF16)


def _flash(qs, heads, k_ref, v_ref, col, c_lo, n_chunks, ac, bias_fn, acc_ref, m_ref, l_ref, splits=2):
    tq = qs.shape[0] // heads
    reps = ac // LANES
    acc_ref[...] = jnp.zeros(acc_ref.shape, F32)
    m_ref[...] = jnp.full(m_ref.shape, MASKED, F32)
    l_ref[...] = jnp.zeros(l_ref.shape, F32)

    hps = heads // splits

    def body(ci, carry):
        c = c_lo + ci
        off = pl.multiple_of(c * ac, ac)
        kc = k_ref[pl.ds(off, ac), col:col + LANES]
        vc = v_ref[pl.ds(off, ac), col:col + LANES]
        head_bias = bias_fn(c)
        for g in range(splits):
            s = lax.dot_general(qs[g * hps * tq:(g + 1) * hps * tq], kc, NT_DIMS, preferred_element_type=F32)
            ps, alphas = [], []
            for hh in range(hps):
                rows = slice((g * hps + hh) * tq, (g * hps + hh + 1) * tq)
                sh = s[hh * tq:(hh + 1) * tq] + head_bias(g * hps + hh)
                m_old = m_ref[rows]
                m_new = jnp.maximum(m_old, jnp.max(sh, axis=1, keepdims=True))
                alpha = jnp.exp2(m_old - m_new)
                p = jnp.exp2(sh - jnp.tile(m_new, (1, reps)))
                l_ref[rows] = alpha * l_ref[rows] + jnp.sum(p, axis=1, keepdims=True)
                m_ref[rows] = m_new
                ps.append(p.astype(BF16))
                alphas.append(alpha)
            pv = jnp.dot(jnp.concatenate(ps, 0), vc, preferred_element_type=F32)
            for hh in range(hps):
                rows = slice((g * hps + hh) * tq, (g * hps + hh + 1) * tq)
                acc_ref[rows] = alphas[hh] * acc_ref[rows] + pv[hh * tq:(hh + 1) * tq]
        return carry

    lax.fori_loop(0, n_chunks, body, 0)
    acc_ref[...] = jnp.where(m_ref[...] > 0.5 * MASKED, acc_ref[...] / l_ref[...], 0.0)


def _layernorm(y, g, b):
    mu = jnp.mean(y, axis=-1, keepdims=True)
    d = y - mu
    var = jnp.mean(d * d, axis=-1, keepdims=True)
    return d * lax.rsqrt(var + LN_EPS) * g + b


def _proj_ab_kernel(*refs, tm, seq_tiles, sample, dec_seq):
    if sample:
        (x_ref, w_ref, c_ref, s_ref, cx_ref, sx_ref, cw_ref, p1_ref, p2_ref,
         q_ref, iq_ref, k_ref, v_ref, ikw_ref, kvb_ref, ikb_ref, bo_ref, uo_ref, ubuf) = refs
        prev_refs = (None, p1_ref, p2_ref)
    else:
        (x_ref, w_ref, c_ref, s_ref, cx_ref, sx_ref, cw_ref,
         q_ref, iq_ref, k_ref, v_ref, ikw_ref, kvb_ref, ikb_ref, bo_ref, uo_ref, ubuf) = refs
    i = pl.program_id(0)
    y = jnp.dot(x_ref[...].astype(BF16), w_ref[...], preferred_element_type=F32)
    lane = lax.broadcasted_iota(I32, (tm, LANES), 1)
    first_half = (lane & (HEAD_DIM - 1)) < HEAD_DIM // 2
    c = c_ref[...]
    s = s_ref[...]
    ro = [_rope128(y[:, j * LANES:(j + 1) * LANES], c, s, first_half) for j in range(7)]
    ro.append(_rope128(y[:, 7 * LANES:8 * LANES], cx_ref[...], sx_ref[...], first_half))
    for j in range(4):
        q_ref[:, j * LANES:(j + 1) * LANES] = (ro[j] * Q_SCALE).astype(BF16)
    k = ro[4]
    v = y[:, 1024:1152]
    k_ref[...] = k
    v_ref[...] = v
    kvb_ref[:, 0:LANES] = k.astype(BF16)
    kvb_ref[:, LANES:2 * LANES] = v.astype(BF16)
    for j in range(2):
        iq_ref[:, j * LANES:(j + 1) * LANES] = (ro[5 + j] * IDX_DIM ** -0.5).astype(BF16)
    ikw = ro[7]
    ikw_ref[...] = ikw
    ikb_ref[...] = jnp.where(lane < IDX_DIM, ikw, pltpu.roll(ikw, IDX_DIM, 1)).astype(BF16)

    gate_b = y[:, 1152:1664]
    u = y[:, 1664:2176] * y[:, 2176:2688]

    @pl.when(i % seq_tiles == 0)
    def _():
        ubuf[0:SUBLANES, :] = jnp.zeros((SUBLANES, B_WIDTH), F32)

    @pl.when(i % seq_tiles != 0)
    def _():
        ubuf[0:SUBLANES, :] = ubuf[tm:tm + SUBLANES, :]

    ubuf[SUBLANES:tm + SUBLANES, :] = u
    cw = cw_ref[...]
    conv = u * cw[CONV_W - 1:CONV_W, :]
    if sample:
        t = lax.broadcasted_iota(I32, (tm, 1), 0) % dec_seq
    for d in range(1, CONV_W):
        ud = ubuf[SUBLANES - d:tm + SUBLANES - d, :]
        if sample:
            ud = jnp.where(t >= d, ud, prev_refs[d][...])
        conv = conv + ud * cw[CONV_W - 1 - d:CONV_W - d, :]
    bo_ref[...] = (gate_b * conv).astype(BF16)
    if sample:
        uo_ref[...] = u
    else:
        uo_ref[...] = u[tm - SUBLANES:tm, :]


def _proj_ab(x2d, w, tabs, conv_w8, *, tm, seq_tiles, sample, prevs=None, dec_seq=1):
    m = x2d.shape[0]
    nt = tabs[0].shape[0] // tm
    row = lambda i: (i, 0)
    const = lambda i: (0, 0)
    tab = lambda i: (i % nt, 0)
    in_specs = [pl.BlockSpec((tm, D_MODEL), row), pl.BlockSpec((D_MODEL, AB_COLS), const)]
    in_specs += [pl.BlockSpec((tm, LANES), tab)] * 4
    in_specs += [pl.BlockSpec((SUBLANES, B_WIDTH), const)]
    args = [x2d, w, *tabs, conv_w8]
    if sample:
        in_specs += [pl.BlockSpec((tm, B_WIDTH), row)] * 2
        args += list(prevs)
    u_rows = tm if sample else SUBLANES
    out_shape = [
        jax.ShapeDtypeStruct((m, A_WIDTH), BF16),
        jax.ShapeDtypeStruct((m, IDX_HEADS * IDX_DIM), BF16),
        jax.ShapeDtypeStruct((m, LANES), F32),
        jax.ShapeDtypeStruct((m, LANES), F32),
        jax.ShapeDtypeStruct((m, LANES), F32),
        jax.ShapeDtypeStruct((m, 2 * LANES), BF16),
        jax.ShapeDtypeStruct((m, LANES), BF16),
        jax.ShapeDtypeStruct((m, B_WIDTH), BF16),
        jax.ShapeDtypeStruct((m // tm * u_rows, B_WIDTH), F32),
    ]
    out_specs = [
        pl.BlockSpec((tm, A_WIDTH), row), pl.BlockSpec((tm, IDX_HEADS * IDX_DIM), row),
        pl.BlockSpec((tm, LANES), row), pl.BlockSpec((tm, LANES), row), pl.BlockSpec((tm, LANES), row),
        pl.BlockSpec((tm, 2 * LANES), row), pl.BlockSpec((tm, LANES), row),
        pl.BlockSpec((tm, B_WIDTH), row), pl.BlockSpec((u_rows, B_WIDTH), row),
    ]
    kern = functools.partial(_proj_ab_kernel, tm=tm, seq_tiles=seq_tiles, sample=sample, dec_seq=dec_seq)
    return pl.pallas_call(
        kern, grid=(m // tm,), in_specs=in_specs, out_specs=out_specs, out_shape=out_shape,
        scratch_shapes=[pltpu.VMEM((tm + SUBLANES, B_WIDTH), F32)],
        compiler_params=_params("arbitrary"))(*args)


def _count(key_ref, n_chunks, width, rows, pred):
    def body(c, acc):
        m = jnp.where(pred(key_ref[c], c), 1.0, 0.0)
        part = m[:, 0:LANES]
        for j in range(1, width // LANES):
            part = part + m[:, j * LANES:(j + 1) * LANES]
        return acc + part
    acc = lax.fori_loop(0, n_chunks, body, jnp.zeros((rows, LANES), F32))
    return jnp.sum(acc, axis=1, keepdims=True)


def _topk_bias(key_ref, bias_ref, n_sc, sc, ac, rows, topk, klen):
    ratio = sc // ac
    kiota = lax.broadcasted_iota(I32, (rows, sc), 1)
    count = functools.partial(_count, key_ref, n_sc, sc, rows)
    kf = jnp.float32(topk)
    n_all = jnp.asarray(n_sc * sc, F32)

    def bit_body(b, carry):
        thr, n_ge = carry
        cand = thr + jnp.left_shift(jnp.int32(1), 31 - b)
        cnt = count(lambda key, c: key >= cand)
        take = cnt >= kf
        return jnp.where(take, cand, thr), jnp.where(take, cnt, n_ge)
    thr, n_ge = lax.fori_loop(0, 32, bit_body, (jnp.full((rows, 1), INT_MIN, I32), jnp.full((rows, 1), n_all, F32)))

    def tie_search():
        need = kf - count(lambda key, c: key > thr)

        def tie_body(b, last):
            cand = last + jnp.left_shift(jnp.int32(1), klen.bit_length() - 1 - b)
            cnt = count(lambda key, c: (key == thr) & (c * sc + kiota < cand))
            return jnp.where(cnt < need, cand, last)
        return lax.fori_loop(0, klen.bit_length(), tie_body, jnp.zeros((rows, 1), I32))

    tied = (n_ge > kf) & (thr > jnp.int32(NEG_INF_KEY))
    last = lax.cond(jnp.max(jnp.where(tied, 1.0, 0.0)) > 0.5, tie_search, lambda: jnp.full((rows, 1), klen, I32))

    def bias_body(c, carry):
        key = key_ref[c]
        sel = (key > thr) | ((key == thr) & (c * sc + kiota <= last))
        sel = sel & (key > jnp.int32(NEG_INF_KEY))
        bias = jnp.where(sel, 0.0, MASKED)
        for r in range(ratio):
            bias_ref[c * ratio + r] = bias[:, r * ac:(r + 1) * ac]
        return carry
    lax.fori_loop(0, n_sc, bias_body, 0)


def _dsa_kernel(q_ref, iq_ref, ikw_ref, ikb_ref, kvb_ref, o_ref, key_ref, bias_ref, acc_ref, m_ref, l_ref,
                *, tq, klen, pos0, topk, sc, ac, causal_skip):
    i = pl.program_id(1)
    pos = pos0 + i * tq + lax.broadcasted_iota(I32, (tq, 1), 0)
    n_sc = _cdiv((i + 1) * tq, sc) if causal_skip else klen // sc
    ratio = sc // ac
    lane = lax.broadcasted_iota(I32, (tq, LANES), 1)
    lo_half = lane < HEAD_DIM
    hi_half = jnp.logical_not(lo_half)
    kiota = lax.broadcasted_iota(I32, (tq, sc), 1)

    iqs = jnp.concatenate([_half_masked(iq_ref[:, (h // 2) * LANES:(h // 2 + 1) * LANES],
                                        lo_half if h % 2 == 0 else hi_half) for h in range(IDX_HEADS)], 0)

    def score_body(c, carry):
        off = pl.multiple_of(c * sc, sc)
        logits = lax.dot_general(iqs, ikb_ref[pl.ds(off, sc), :], NT_DIMS, preferred_element_type=F32)
        sco = jnp.zeros((tq, sc), F32)
        for h in range(IDX_HEADS):
            sco = sco + jnp.maximum(logits[h * tq:(h + 1) * tq], 0.0) * ikw_ref[:, IDX_DIM + h:IDX_DIM + h + 1]
        sco = jnp.where(sco == 0.0, 0.0, sco)
        bits = pltpu.bitcast(sco, I32)
        key = jnp.where(bits < 0, bits ^ jnp.int32(0x7FFFFFFF), bits)
        key_ref[c] = jnp.where(c * sc + kiota <= pos, key, jnp.int32(NEG_INF_KEY))
        return carry
    lax.fori_loop(0, n_sc, score_body, 0)
    _topk_bias(key_ref, bias_ref, n_sc, sc, ac, tq, topk, klen)

    half_slots = A_HEADS // 2
    qs = jnp.concatenate([_half_masked(q_ref[:, (h % half_slots) * LANES:(h % half_slots + 1) * LANES],
                                       lo_half if h < half_slots else hi_half) for h in range(A_HEADS)], 0)
    def shared_bias(c):
        bias = bias_ref[c]
        return lambda h: bias
    _flash(qs, A_HEADS, kvb_ref, kvb_ref.at[:, LANES:2 * LANES], 0, 0, n_sc * ratio, ac, shared_bias,
           acc_ref, m_ref, l_ref)
    for j in range(half_slots):
        o_lo = acc_ref[j * tq:(j + 1) * tq]
        o_hi = acc_ref[(half_slots + j) * tq:(half_slots + j + 1) * tq]
        o_ref[:, j * LANES:(j + 1) * LANES] = jnp.where(lo_half, o_lo, o_hi).astype(BF16)


def _dsa_attention(q, iq, ikw, ikb, kvb, *, tq, pos0, topk, sc, ac, causal_skip):
    nb, tlen, _ = q.shape
    klen = ikb.shape[1]
    qspec = lambda w: pl.BlockSpec((None, tq, w), lambda b, i: (b, i, 0))
    kspec = lambda w: pl.BlockSpec((None, klen, w), lambda b, i: (b, 0, 0))
    kern = functools.partial(_dsa_kernel, tq=tq, klen=klen, pos0=pos0, topk=topk, sc=sc, ac=ac,
                             causal_skip=causal_skip)
    return pl.pallas_call(
        kern, grid=(nb, tlen // tq),
        in_specs=[qspec(A_WIDTH), qspec(IDX_HEADS * IDX_DIM), qspec(LANES), kspec(LANES), kspec(2 * LANES)],
        out_specs=qspec(A_WIDTH),
        out_shape=jax.ShapeDtypeStruct((nb, tlen, A_WIDTH), BF16),
        scratch_shapes=[pltpu.VMEM((klen // sc, tq, sc), I32), pltpu.VMEM((klen // ac, tq, ac), F32)]
        + [pltpu.VMEM((A_HEADS * tq, LANES), F32)] * 3,
        compiler_params=_params("parallel", "arbitrary"))(q, iq, ikw, ikb, kvb)


def _page_specs(shape, layer, seqs, n_pages):
    specs = []
    for s in range(seqs):
        for p in range(n_pages):
            specs.append(pl.BlockSpec((None, None) + shape,
                                      lambda i, pt, s=s, p=p: (layer, pt[i * seqs + s, p], 0, 0)))
    return specs


def _softmax_rows(s):
    m = jnp.max(s, axis=1, keepdims=True)
    p = jnp.exp2(s - m)
    return p, jnp.sum(p, axis=1, keepdims=True)


def _dsa_sample_kernel(pt_ref, *refs, seqs, n_pages, pos0, topk):
    del pt_ref
    np_all = seqs * n_pages
    qs_ref, iq_ref, ikw_ref, knew_ref, vnew_ref, iknew_ref = refs[:6]
    kt_pages = refs[6:6 + np_all]
    vt_pages = refs[6 + np_all:6 + 2 * np_all]
    ikt_pages = refs[6 + 2 * np_all:6 + 3 * np_all]
    o_ref, ktb, vtb, iktb, key_ref, bias_ref = refs[6 + 3 * np_all:]
    tq = SUBLANES
    past = n_pages * PAGE_SIZE
    klen = past + LANES
    for s in range(seqs):
        for p in range(n_pages):
            cols = slice(p * PAGE_SIZE, (p + 1) * PAGE_SIZE)
            ktb[s, :, cols] = kt_pages[s * n_pages + p][...].astype(BF16)
            vtb[s, :, cols] = vt_pages[s * n_pages + p][...].astype(BF16)
            iktb[s, :, cols] = ikt_pages[s * n_pages + p][...].astype(BF16)

    pos = pos0 + lax.broadcasted_iota(I32, (tq, 1), 0)
    kidx = lax.broadcasted_iota(I32, (tq, klen), 1)
    for s in range(seqs):
        iq = iq_ref[s]
        logits = jnp.concatenate(
            [jnp.dot(iq, iktb[s], preferred_element_type=F32),
             lax.dot_general(iq, iknew_ref[s], NT_DIMS, preferred_element_type=F32)], 1)
        sco = jnp.zeros((tq, klen), F32)
        for h in range(IDX_HEADS):
            sco = sco + jnp.maximum(logits[h * tq:(h + 1) * tq], 0.0) * ikw_ref[s, :, IDX_DIM + h:IDX_DIM + h + 1]
        sco = jnp.where(sco == 0.0, 0.0, sco)
        bits = pltpu.bitcast(sco, I32)
        key = jnp.where(bits < 0, bits ^ jnp.int32(0x7FFFFFFF), bits)
        key_ref[0, s * tq:(s + 1) * tq, :] = jnp.where(kidx <= pos, key, jnp.int32(NEG_INF_KEY))
    _topk_bias(key_ref, bias_ref, 1, klen, klen, seqs * tq, topk, klen)

    lane = lax.broadcasted_iota(I32, (tq, LANES), 1)
    lo_half = lane < HEAD_DIM
    half_slots = A_HEADS // 2
    for s in range(seqs):
        qs = qs_ref[s]
        sc_all = jnp.concatenate(
            [jnp.dot(qs, ktb[s], preferred_element_type=F32),
             lax.dot_general(qs, knew_ref[s], NT_DIMS, preferred_element_type=F32)], 1)
        p, l = _softmax_rows(sc_all + jnp.tile(bias_ref[0, s * tq:(s + 1) * tq, :], (A_HEADS, 1)))
        pb = p.astype(BF16)
        o = (lax.dot_general(pb[:, :past], vtb[s], NT_DIMS, preferred_element_type=F32)
             + jnp.dot(pb[:, past:], vnew_ref[s], preferred_element_type=F32)) / l
        for j in range(half_slots):
            o_ref[s, :, j * LANES:(j + 1) * LANES] = jnp.where(
                lo_half, o[j * tq:(j + 1) * tq], o[(half_slots + j) * tq:(half_slots + j + 1) * tq]).astype(BF16)


def _dsa_sample(qs, iq, ikw, knew, vnew, iknew, kt_cache, vt_cache, ikt_cache, page_table, layer, *, seqs, pos0,
                topk):
    nb = qs.shape[0]
    n_pages = page_table.shape[1]
    klen = n_pages * PAGE_SIZE + LANES
    tq = SUBLANES
    seq_spec = lambda a: pl.BlockSpec((seqs,) + a.shape[1:], lambda i, pt: (i, 0, 0))
    in_specs = [seq_spec(a) for a in (qs, iq, ikw, knew, vnew, iknew)]
    in_specs += _page_specs(kt_cache.shape[2:], layer, seqs, n_pages)
    in_specs += _page_specs(vt_cache.shape[2:], layer, seqs, n_pages)
    in_specs += _page_specs(ikt_cache.shape[2:], layer, seqs, n_pages)
    np_all = seqs * n_pages
    kvw = A_KV_HEADS * HEAD_DIM
    grid_spec = pltpu.PrefetchScalarGridSpec(
        num_scalar_prefetch=1, grid=(nb // seqs,), in_specs=in_specs,
        out_specs=pl.BlockSpec((seqs, tq, A_WIDTH), lambda i, pt: (i, 0, 0)),
        scratch_shapes=[pltpu.VMEM((seqs, kvw, n_pages * PAGE_SIZE), BF16),
                        pltpu.VMEM((seqs, kvw, n_pages * PAGE_SIZE), BF16),
                        pltpu.VMEM((seqs, IDX_DIM, n_pages * PAGE_SIZE), BF16),
                        pltpu.VMEM((1, seqs * tq, klen), I32), pltpu.VMEM((1, seqs * tq, klen), F32)])
    kern = functools.partial(_dsa_sample_kernel, seqs=seqs, n_pages=n_pages, pos0=pos0, topk=topk)
    return pl.pallas_call(
        kern, grid_spec=grid_spec, out_shape=jax.ShapeDtypeStruct((nb, tq, A_WIDTH), BF16),
        compiler_params=_params("arbitrary"))(
            page_table, qs, iq, ikw, knew, vnew, iknew,
            *([kt_cache] * np_all), *([vt_cache] * np_all), *([ikt_cache] * np_all))


def _gelu_tanh(x):
    return 0.5 * x * (1.0 + jnp.tanh(np.sqrt(2.0 / np.pi) * (x + 0.044715 * x * x * x)))


def _compress_sample_kernel(pt_ref, pet_ref, wab_ref, w2_ref, *refs, seqs, n_pages):
    del pt_ref
    pages = refs[:seqs * n_pages]
    o_ref, tok_ref = refs[seqs * n_pages:]
    width = C_KV_HEADS * HEAD_DIM
    for s in range(seqs):
        for p in range(n_pages):
            base = (s * n_pages + p) * PAGE_SIZE
            for c in range(width // LANES):
                blk = pages[s * n_pages + p][c * LANES:(c + 1) * LANES, :]
                tok_ref[c, base:base + PAGE_SIZE, :] = blk.T
    rows = seqs * n_pages * PAGE_SIZE // CMP_STRIDE
    half = CMP_BLOCK // 2
    ra = jnp.zeros((rows, width), F32)
    rb = jnp.zeros((rows, width), F32)
    for j in range(half):
        xj = jnp.concatenate([tok_ref[c, pl.ds(j, rows, stride=CMP_STRIDE), :] for c in range(width // LANES)], 1)
        ra = ra + jnp.dot((xj + pet_ref[j:j + 1, :]).astype(BF16), wab_ref[j], preferred_element_type=F32)
        rb = rb + jnp.dot((xj + pet_ref[half + j:half + j + 1, :]).astype(BF16), wab_ref[half + j],
                          preferred_element_type=F32)
    hid = _gelu_tanh(ra + pltpu.roll(rb, rows - 1, 0))
    out = jnp.dot(hid.astype(BF16), w2_ref[...], preferred_element_type=F32)
    o_ref[...] = out.reshape(o_ref.shape).astype(BF16)


def _compress_sample(cache_t, page_table, layer, pet, wab, w2, *, seqs):
    nb, n_pages = page_table.shape
    width = C_KV_HEADS * HEAD_DIM
    r = n_pages * PAGE_SIZE // CMP_STRIDE
    const2 = lambda i, pt: (0, 0)
    grid_spec = pltpu.PrefetchScalarGridSpec(
        num_scalar_prefetch=1, grid=(nb // seqs,),
        in_specs=[pl.BlockSpec(pet.shape, const2), pl.BlockSpec(wab.shape, lambda i, pt: (0, 0, 0)),
                  pl.BlockSpec(w2.shape, const2)] + _page_specs(cache_t.shape[2:], layer, seqs, n_pages),
        out_specs=pl.BlockSpec((seqs, r, width), lambda i, pt: (i, 0, 0)),
        scratch_shapes=[pltpu.VMEM((width // LANES, seqs * n_pages * PAGE_SIZE, LANES), F32)])
    return pl.pallas_call(
        functools.partial(_compress_sample_kernel, seqs=seqs, n_pages=n_pages), grid_spec=grid_spec,
        out_shape=jax.ShapeDtypeStruct((nb, r, width), BF16),
        compiler_params=_params("arbitrary"))(page_table, pet, wab, w2, *([cache_t] * (seqs * n_pages)))


def _nsa_sample_kernel(pt_ref, *refs, seqs, n_pages, pos0):
    del pt_ref
    np_all = seqs * n_pages
    (qs_ref, g_ref, kc_ref, vc_ref, ksn_ref, vsn_ref, kwn_ref, vwn_ref, wink_ref, winv_ref, c2s_ref,
     e_ref) = refs[:12]
    kst_pages = refs[12:12 + np_all]
    vst_pages = refs[12 + np_all:12 + 2 * np_all]
    o_ref, kst, vst, part_ref, bias_ref = refs[12 + 2 * np_all:]
    tq = SUBLANES
    hpg = C_HEADS // C_KV_HEADS
    nh = 2 * hpg
    npair = C_KV_HEADS // 2
    past = n_pages * PAGE_SIZE
    ls = past + LANES
    nbuf = wink_ref.shape[-1]
    lw = nbuf + LANES
    ncp = kc_ref.shape[1]
    for s in range(seqs):
        for p in range(n_pages):
            cols = slice(p * PAGE_SIZE, (p + 1) * PAGE_SIZE)
            kst[s, :, cols] = kst_pages[s * n_pages + p][...].astype(BF16)
            vst[s, :, cols] = vst_pages[s * n_pages + p][...].astype(BF16)

    rows = nh * tq
    t_row = pos0 + lax.broadcasted_iota(I32, (rows, 1), 0) % tq
    cmp_visible = lax.broadcasted_iota(I32, (rows, ncp), 1) * CMP_STRIDE + (CMP_BLOCK - 1) <= t_row
    lane = lax.broadcasted_iota(I32, (tq, LANES), 1)
    lo_half = lane < HEAD_DIM
    pair_cols = lambda pr: slice(pr * LANES, (pr + 1) * LANES)

    psums = []
    for s in range(seqs):
        for pr in range(npair):
            qp = qs_ref[s, pr * rows:(pr + 1) * rows, :]
            s_c = lax.dot_general(qp, kc_ref[s, :, pair_cols(pr)], NT_DIMS, preferred_element_type=F32)
            s_c = jnp.where(cmp_visible, s_c, -jnp.inf)
            m = jnp.max(s_c, axis=1, keepdims=True)
            m = jnp.where(m > -jnp.inf, m, 0.0)
            p = jnp.exp2(s_c - m)
            den = jnp.sum(p, axis=1, keepdims=True)
            p = p / jnp.where(den > 0.0, den, 1.0)
            part_ref[s, pr] = jnp.dot(p.astype(BF16), vc_ref[s, :, pair_cols(pr)], preferred_element_type=F32)
            for half in range(2):
                acc = p[half * hpg * tq:(half * hpg + 1) * tq]
                for h in range(1, hpg):
                    acc = acc + p[(half * hpg + h) * tq:(half * hpg + h + 1) * tq]
                psums.append(acc)

    ng_rows = seqs * C_KV_HEADS * tq
    psum = jnp.concatenate(psums, 0)
    p_hi = psum.astype(BF16)
    p_lo = (psum - p_hi.astype(F32)).astype(BF16)
    c2s = c2s_ref[...]
    imp = jnp.dot(p_hi, c2s, preferred_element_type=F32) + jnp.dot(p_lo, c2s, preferred_element_type=F32)
    tg = pos0 + lax.broadcasted_iota(I32, (ng_rows, 1), 0) % tq
    lane_g = lax.broadcasted_iota(I32, (ng_rows, LANES), 1)
    lane_gf = lane_g.astype(F32)
    cur = tg // SLC_BLOCK
    forced = (lane_g == 0) | (lane_g == cur) | (lane_g == cur - 1)
    imp = jnp.where(forced, jnp.inf, imp)
    imp = jnp.where(lane_g * SLC_BLOCK <= tg, imp, -jnp.inf)

    def top_body(_, carry):
        val, selm = carry
        mx = jnp.max(val, axis=1, keepdims=True)
        first = jnp.min(jnp.where(val == mx, lane_gf, float(LANES)), axis=1, keepdims=True)
        pick = lane_gf == first
        selm = jnp.where(pick & (mx > -jnp.inf), 1.0, selm)
        return jnp.where(pick, -jnp.inf, val), selm
    _, selm = lax.fori_loop(0, SLC_TOPN, top_body, (imp, jnp.zeros((ng_rows, LANES), F32)))
    ex = jnp.dot(selm.astype(BF16), e_ref[...], preferred_element_type=F32)
    ok = (ex > 0.5) & (lax.broadcasted_iota(I32, (ng_rows, ls), 1) <= tg)
    bias_ref[...] = jnp.where(ok, 0.0, MASKED)

    wpos = jnp.concatenate([pos0 - nbuf + lax.broadcasted_iota(I32, (rows, nbuf), 1),
                            pos0 + lax.broadcasted_iota(I32, (rows, LANES), 1)], 1)
    wbias = jnp.where((wpos >= 0) & (wpos <= t_row) & (t_row - wpos < WINDOW), 0.0, MASKED)

    for s in range(seqs):
        for pr in range(npair):
            qp = qs_ref[s, pr * rows:(pr + 1) * rows, :]
            feat = slice(pr * LANES, (pr + 1) * LANES)
            gbase = (s * C_KV_HEADS + 2 * pr) * tq
            bias = jnp.concatenate([jnp.tile(bias_ref[gbase:gbase + tq, :], (hpg, 1)),
                                    jnp.tile(bias_ref[gbase + tq:gbase + 2 * tq, :], (hpg, 1))], 0)
            sc_s = jnp.concatenate(
                [jnp.dot(qp, kst[s, feat, :], preferred_element_type=F32),
                 lax.dot_general(qp, ksn_ref[s, :, pair_cols(pr)], NT_DIMS, preferred_element_type=F32)], 1)
            p, l = _softmax_rows(sc_s + bias)
            pb = p.astype(BF16)
            o_s = (lax.dot_general(pb[:, :past], vst[s, feat, :], NT_DIMS, preferred_element_type=F32)
                   + jnp.dot(pb[:, past:], vsn_ref[s, :, pair_cols(pr)], preferred_element_type=F32)) / l
            sc_w = jnp.concatenate(
                [jnp.dot(qp, wink_ref[s, feat, :].astype(BF16), preferred_element_type=F32),
                 lax.dot_general(qp, kwn_ref[s, :, pair_cols(pr)], NT_DIMS, preferred_element_type=F32)], 1)
            p, l = _softmax_rows(sc_w + wbias)
            pb = p.astype(BF16)
            o_w = (lax.dot_general(pb[:, :nbuf], winv_ref[s, feat, :].astype(BF16), NT_DIMS,
                                   preferred_element_type=F32)
                   + jnp.dot(pb[:, nbuf:], vwn_ref[s, :, pair_cols(pr)], preferred_element_type=F32)) / l

            def gate(branch):
                cols = [3 * (hpg * (2 * pr + h // hpg) + h % hpg) + branch for h in range(nh)]
                return jnp.concatenate([g_ref[s, :, c:c + 1] for c in cols], 0)
            o = gate(0) * part_ref[s, pr] + gate(1) * o_s + gate(2) * o_w
            for j in range(hpg):
                slot = pr * hpg + j
                o_ref[s, :, slot * LANES:(slot + 1) * LANES] = jnp.where(
                    lo_half, o[j * tq:(j + 1) * tq], o[(hpg + j) * tq:(hpg + j + 1) * tq]).astype(BF16)


def _nsa_sample(qs, gates, kcmp, vcmp, ksn, vsn, kwn, vwn, wink_t, winv_t, c2s, emat, kst_cache, vst_cache,
                page_table, layer, *, seqs, pos0):
    nb, n_pages = page_table.shape
    tq = SUBLANES
    width = C_KV_HEADS * HEAD_DIM
    past = n_pages * PAGE_SIZE
    seq_spec = lambda a: pl.BlockSpec((seqs,) + a.shape[1:], lambda i, pt: (i, 0, 0))
    win_spec = pl.BlockSpec((None, seqs) + wink_t.shape[2:], lambda i, pt: (layer, i, 0, 0))
    const2 = lambda i, pt: (0, 0)
    in_specs = [seq_spec(a) for a in (qs, gates, kcmp, vcmp, ksn, vsn, kwn, vwn)]
    in_specs += [win_spec, win_spec, pl.BlockSpec(c2s.shape, const2), pl.BlockSpec(emat.shape, const2)]
    in_specs += _page_specs(kst_cache.shape[2:], layer, seqs, n_pages)
    in_specs += _page_specs(vst_cache.shape[2:], layer, seqs, n_pages)
    np_all = seqs * n_pages
    rows = 2 * (C_HEADS // C_KV_HEADS) * tq
    grid_spec = pltpu.PrefetchScalarGridSpec(
        num_scalar_prefetch=1, grid=(nb // seqs,), in_specs=in_specs,
        out_specs=pl.BlockSpec((seqs, tq, C_WIDTH), lambda i, pt: (i, 0, 0)),
        scratch_shapes=[pltpu.VMEM((seqs, width, past), BF16), pltpu.VMEM((seqs, width, past), BF16),
                        pltpu.VMEM((seqs, C_KV_HEADS // 2, rows, LANES), F32),
                        pltpu.VMEM((seqs * C_KV_HEADS * tq, past + LANES), F32)])
    kern = functools.partial(_nsa_sample_kernel, seqs=seqs, n_pages=n_pages, pos0=pos0)
    return pl.pallas_call(
        kern, grid_spec=grid_spec, out_shape=jax.ShapeDtypeStruct((nb, tq, C_WIDTH), BF16),
        compiler_params=_params("arbitrary"))(
            page_table, qs, gates, kcmp, vcmp, ksn, vsn, kwn, vwn, wink_t, winv_t, c2s, emat,
            *([kst_cache] * np_all), *([vst_cache] * np_all))


def _proj_c_kernel(x_ref, w_ref, c_ref, s_ref, q_ref, kc_ref, ks_ref, kw_ref, vc_ref, vs_ref, vw_ref,
                   g_ref, kvb_ref, *, tm):
    y = jnp.dot(x_ref[...].astype(BF16), w_ref[...], preferred_element_type=F32)
    lane = lax.broadcasted_iota(I32, (tm, LANES), 1)
    first_half = (lane & (HEAD_DIM - 1)) < HEAD_DIM // 2
    c = c_ref[...]
    s = s_ref[...]
    ro = [_rope128(y[:, j * LANES:(j + 1) * LANES], c, s, first_half) for j in range(C_ROPED // LANES)]
    for j in range(8):
        q_ref[:, j * LANES:(j + 1) * LANES] = (ro[j] * Q_SCALE).astype(BF16)
    for n, ref in enumerate((kc_ref, ks_ref, kw_ref)):
        for j in range(2):
            ref[:, j * LANES:(j + 1) * LANES] = ro[8 + 2 * n + j]
    for n, ref in enumerate((vc_ref, vs_ref, vw_ref)):
        ref[...] = y[:, C_ROPED + 256 * n:C_ROPED + 256 * (n + 1)]
    g = y[:, C_ROPED + 768:C_ROPED + 768 + LANES]
    g_ref[...] = 1.0 / (1.0 + jnp.exp(-g))
    for j in range(2):
        kvb_ref[:, j * LANES:(j + 1) * LANES] = ro[10 + j].astype(BF16)
        kvb_ref[:, 512 + j * LANES:512 + (j + 1) * LANES] = ro[12 + j].astype(BF16)
    kvb_ref[:, 256:512] = y[:, C_ROPED + 256:C_ROPED + 512].astype(BF16)
    kvb_ref[:, 768:1024] = y[:, C_ROPED + 512:C_ROPED + 768].astype(BF16)


def _proj_c(x2d, w, tabs, *, tm):
    m = x2d.shape[0]
    nt = tabs[0].shape[0] // tm
    row = lambda i: (i, 0)
    kvw = C_KV_HEADS * HEAD_DIM
    out_shape = [jax.ShapeDtypeStruct((m, C_WIDTH), BF16)]
    out_shape += [jax.ShapeDtypeStruct((m, kvw), F32)] * 6
    out_shape += [jax.ShapeDtypeStruct((m, LANES), F32), jax.ShapeDtypeStruct((m, 4 * kvw), BF16)]
    out_specs = [pl.BlockSpec((tm, C_WIDTH), row)] + [pl.BlockSpec((tm, kvw), row)] * 6
    out_specs += [pl.BlockSpec((tm, LANES), row), pl.BlockSpec((tm, 4 * kvw), row)]
    return pl.pallas_call(
        functools.partial(_proj_c_kernel, tm=tm), grid=(m // tm,),
        in_specs=[pl.BlockSpec((tm, D_MODEL), row), pl.BlockSpec((D_MODEL, C_COLS), lambda i: (0, 0)),
                  pl.BlockSpec((tm, LANES), lambda i: (i % nt, 0)), pl.BlockSpec((tm, LANES), lambda i: (i % nt, 0))],
        out_specs=out_specs, out_shape=out_shape,
        compiler_params=_params("parallel"))(x2d, w, *tabs)


def _compress_kernel(z_ref, pea_ref, peb_ref, wa_ref, wb_ref, w2_ref, o_ref, *, rows):
    z = z_ref[...].reshape(rows, z_ref.shape[-1])
    ra = jnp.dot((z + pea_ref[...]).astype(BF16), wa_ref[...], preferred_element_type=F32)
    rb = jnp.dot((z + peb_ref[...]).astype(BF16), wb_ref[...], preferred_element_type=F32)
    hid = ra + pltpu.roll(rb, rows - 1, 0)
    hid = 0.5 * hid * (1.0 + jnp.tanh(np.sqrt(2.0 / np.pi) * (hid + 0.044715 * hid * hid * hid)))
    out = jnp.dot(hid.astype(BF16), w2_ref[...], preferred_element_type=F32)
    o_ref[...] = out.reshape(o_ref.shape).astype(BF16)


def _compress(z, pea, peb, wa, wb, w2, *, nb_step):
    nb, r, zw = z.shape
    kvw = C_KV_HEADS * HEAD_DIM
    const = lambda b: (0, 0)
    return pl.pallas_call(
        functools.partial(_compress_kernel, rows=nb_step * r), grid=(nb // nb_step,),
        in_specs=[pl.BlockSpec((nb_step, r, zw), lambda b: (b, 0, 0)),
                  pl.BlockSpec((1, zw), const), pl.BlockSpec((1, zw), const),
                  pl.BlockSpec((zw, kvw), const), pl.BlockSpec((zw, kvw), const), pl.BlockSpec((kvw, kvw), const)],
        out_specs=pl.BlockSpec((nb_step, r, kvw), lambda b: (b, 0, 0)),
        out_shape=jax.ShapeDtypeStruct((nb, r, kvw), BF16),
        compiler_params=_params("parallel"))(z, pea, peb, wa, wb, w2)


def _nsa_kernel(q_ref, g_ref, kc_ref, vc_ref, kvs_ref, kvw_ref, c2s_ref, e_ref, o_ref,
                bias_ref, part_ref, acc_ref, m_ref, l_ref,
                *, tq, pos0, ncp, ls, lw, woff, ac, acw, causal_skip):
    i = pl.program_id(1)
    t = pos0 + i * tq + lax.broadcasted_iota(I32, (tq, 1), 0)
    lane = lax.broadcasted_iota(I32, (tq, LANES), 1)
    lo_half = lane < HEAD_DIM
    if causal_skip:
        n_s = _cdiv((i + 1) * tq, ac)
        w_lo = jnp.maximum(i * tq - (WINDOW - 1), 0) // acw
        n_w = ((i + 1) * tq - 1) // acw - w_lo + 1
    else:
        n_s = ls // ac
        w_lo = 0
        n_w = lw // acw
    kvw = C_KV_HEADS * HEAD_DIM
    ks_ref = kvs_ref
    vs_ref = kvs_ref.at[:, kvw:2 * kvw]
    kw_ref = kvw_ref
    vw_ref = kvw_ref.at[:, kvw:2 * kvw]
    hi_half = jnp.logical_not(lo_half)
    cmp_visible = lax.broadcasted_iota(I32, (tq, ncp), 1) * CMP_STRIDE + (CMP_BLOCK - 1) <= t
    kiota_w = lax.broadcasted_iota(I32, (tq, acw), 1)
    c2s = c2s_ref[...]
    ng = C_KV_HEADS
    hpg = C_HEADS // C_KV_HEADS
    nh = 2 * hpg
    head_rows = lambda h: slice(h * tq, (h + 1) * tq)
    tg = jnp.concatenate([t] * ng, 0)
    lane_g = lax.broadcasted_iota(I32, (ng * tq, LANES), 1)
    lane_gf = lane_g.astype(F32)
    kiota_sg = lax.broadcasted_iota(I32, (ng * tq, ac), 1)

    def gate(pr, h, branch):
        head = hpg * (2 * pr + h // hpg) + h % hpg
        return g_ref[:, 3 * head + branch:3 * head + branch + 1]

    def window_bias(c):
        kp = woff + c * acw + kiota_w
        ok = (kp >= 0) & (kp <= t) & (t - kp < WINDOW)
        bias = jnp.where(ok, 0.0, MASKED)
        return lambda h: bias

    def stacked_q(pr):
        return jnp.concatenate(
            [_half_masked(q_ref[:, (pr * hpg + h % hpg) * LANES:(pr * hpg + h % hpg + 1) * LANES],
                          lo_half if h < hpg else hi_half) for h in range(nh)], 0)

    psums = []
    for pr in range(ng // 2):
        col = pr * LANES
        s_c = lax.dot_general(stacked_q(pr), kc_ref[:, col:col + LANES], NT_DIMS, preferred_element_type=F32)
        pair_sums = [jnp.zeros((tq, ncp), F32), jnp.zeros((tq, ncp), F32)]
        ps = []
        for h in range(nh):
            sh = jnp.where(cmp_visible, s_c[head_rows(h)], -jnp.inf)
            m = jnp.max(sh, axis=1, keepdims=True)
            m = jnp.where(m > -jnp.inf, m, 0.0)
            p = jnp.exp2(sh - m)
            den = jnp.sum(p, axis=1, keepdims=True)
            p = p / jnp.where(den > 0.0, den, 1.0)
            pair_sums[h // hpg] = pair_sums[h // hpg] + p
            ps.append(p.astype(BF16))
        part_ref[pr * nh * tq:(pr + 1) * nh * tq] = jnp.dot(jnp.concatenate(ps, 0), vc_ref[:, col:col + LANES],
                                                           preferred_element_type=F32)
        psums += pair_sums

    psum = jnp.concatenate(psums, 0)
    p_hi = psum.astype(BF16)
    p_lo = (psum - p_hi.astype(F32)).astype(BF16)
    imp = jnp.dot(p_hi, c2s, preferred_element_type=F32) + jnp.dot(p_lo, c2s, preferred_element_type=F32)
    cur = tg // SLC_BLOCK
    forced = (lane_g == 0) | (lane_g == cur) | (lane_g == cur - 1)
    imp = jnp.where(forced, jnp.inf, imp)
    imp = jnp.where(lane_g * SLC_BLOCK <= tg, imp, -jnp.inf)

    def top_body(_, carry):
        val, selm = carry
        mx = jnp.max(val, axis=1, keepdims=True)
        first = jnp.min(jnp.where(val == mx, lane_gf, float(LANES)), axis=1, keepdims=True)
        pick = lane_gf == first
        selm = jnp.where(pick & (mx > -jnp.inf), 1.0, selm)
        return jnp.where(pick, -jnp.inf, val), selm
    _, selm = lax.fori_loop(0, SLC_TOPN, top_body, (imp, jnp.zeros((ng * tq, LANES), F32)))
    selb = selm.astype(BF16)

    def bias_body(c, carry):
        ex = jnp.dot(selb, e_ref[c], preferred_element_type=F32)
        ok = (ex > 0.5) & (c * ac + kiota_sg <= tg)
        bias_ref[c] = jnp.where(ok, 0.0, MASKED)
        return carry
    lax.fori_loop(0, n_s, bias_body, 0)

    for pr in range(ng // 2):
        col = pr * LANES
        qs = stacked_q(pr)
        part = part_ref.at[pr * nh * tq:(pr + 1) * nh * tq]

        def selected_bias(c, pr=pr):
            return lambda h: bias_ref[c, (2 * pr + h // hpg) * tq:(2 * pr + h // hpg + 1) * tq, :]
        _flash(qs, nh, ks_ref, vs_ref, col, 0, n_s, ac, selected_bias, acc_ref, m_ref, l_ref)
        for h in range(nh):
            r = head_rows(h)
            part[r] = gate(pr, h, 0) * part[r] + gate(pr, h, 1) * acc_ref[r]
        _flash(qs, nh, kw_ref, vw_ref, col, w_lo, n_w, acw, window_bias, acc_ref, m_ref, l_ref)
        for j in range(hpg):
            slot = pr * hpg + j
            lo, hi = head_rows(j), head_rows(hpg + j)
            o_lo = part[lo] + gate(pr, j, 2) * acc_ref[lo]
            o_hi = part[hi] + gate(pr, hpg + j, 2) * acc_ref[hi]
            o_ref[:, slot * LANES:(slot + 1) * LANES] = jnp.where(lo_half, o_lo, o_hi).astype(BF16)


def _nsa_attention(q, gates, kcmp, vcmp, kvs, kvw, c2s, emat, *, tq, pos0, woff, ac, acw, causal_skip,
                   s_blk=0, w_blk=0):
    nb, tlen, _ = q.shape
    ncp = kcmp.shape[1]
    ls, lw = kvs.shape[1], kvw.shape[1]
    kvwid = C_KV_HEADS * HEAD_DIM
    qspec = lambda w: pl.BlockSpec((None, tq, w), lambda b, i: (b, i, 0))
    kspec = lambda n, w, blk=0: pl.BlockSpec((None, n, w), lambda b, i: (b, 0, blk))
    kern = functools.partial(_nsa_kernel, tq=tq, pos0=pos0, ncp=ncp, ls=ls, lw=lw, woff=woff, ac=ac, acw=acw,
                             causal_skip=causal_skip)
    return pl.pallas_call(
        kern, grid=(nb, tlen // tq),
        in_specs=[qspec(C_WIDTH), qspec(LANES), kspec(ncp, kvwid), kspec(ncp, kvwid),
                  kspec(ls, 2 * kvwid, s_blk), kspec(lw, 2 * kvwid, w_blk),
                  pl.BlockSpec((ncp, LANES), lambda b, i: (0, 0)),
                  pl.BlockSpec((ls // ac, LANES, ac), lambda b, i: (0, 0, 0))],
        out_specs=qspec(C_WIDTH),
        out_shape=jax.ShapeDtypeStruct((nb, tlen, C_WIDTH), BF16),
        scratch_shapes=[pltpu.VMEM((ls // ac, C_KV_HEADS * tq, ac), F32), pltpu.VMEM((C_HEADS * tq, LANES), F32)]
        + [pltpu.VMEM((2 * (C_HEADS // C_KV_HEADS) * tq, LANES), F32)] * 3,
        compiler_params=_params("parallel", "arbitrary"))(q, gates, kcmp, vcmp, kvs, kvw, c2s, emat)


def _outproj_ln_kernel(x_ref, a_ref, b_ref, wa_ref, wb_ref, g_ref, bt_ref, o_ref):
    y = ALPHA * x_ref[...]
    y = y + jnp.dot(a_ref[...], wa_ref[...], preferred_element_type=F32)
    y = y + jnp.dot(b_ref[...], wb_ref[...], preferred_element_type=F32)
    o_ref[...] = _layernorm(y, g_ref[...], bt_ref[...])


def _outproj_ln(x2d, a, b, a_blk, b_blk, w_out, g, bt, *, tm):
    m = x2d.shape[0]
    half = w_out.shape[0] // 2
    row = lambda i: (i, 0)
    const = lambda i: (0, 0)
    return pl.pallas_call(
        _outproj_ln_kernel, grid=(m // tm,),
        in_specs=[pl.BlockSpec((tm, D_MODEL), row),
                  pl.BlockSpec((tm, half), lambda i: (i, a_blk)), pl.BlockSpec((tm, half), lambda i: (i, b_blk)),
                  pl.BlockSpec((half, D_MODEL), lambda i: (0, 0)), pl.BlockSpec((half, D_MODEL), lambda i: (1, 0)),
                  pl.BlockSpec((1, D_MODEL), const), pl.BlockSpec((1, D_MODEL), const)],
        out_specs=pl.BlockSpec((tm, D_MODEL), row),
        out_shape=jax.ShapeDtypeStruct((m, D_MODEL), F32),
        compiler_params=_params("parallel"))(x2d, a, b, w_out, w_out, g, bt)


def _ffn_ln_kernel(x_ref, wg_ref, wu_ref, wd_ref, g_ref, bt_ref, o_ref, xb_ref, acc_ref):
    f = pl.program_id(1)

    @pl.when(f == 0)
    def _():
        xb_ref[...] = x_ref[...].astype(BF16)
        acc_ref[...] = jnp.zeros(acc_ref.shape, F32)

    xb = xb_ref[...]
    h = jnp.dot(xb, wg_ref[...], preferred_element_type=F32)
    u = jnp.dot(xb, wu_ref[...], preferred_element_type=F32)
    a = (h / (1.0 + jnp.exp(-h))) * u
    acc_ref[...] += jnp.dot(a.astype(BF16), wd_ref[...], preferred_element_type=F32)

    @pl.when(f == pl.num_programs(1) - 1)
    def _():
        o_ref[...] = _layernorm(ALPHA * x_ref[...] + acc_ref[...], g_ref[...], bt_ref[...])


def _ffn_ln(x2d, wg, wu, wd, g, bt, *, tm, tf):
    m = x2d.shape[0]
    return pl.pallas_call(
        _ffn_ln_kernel, grid=(m // tm, D_FF // tf),
        in_specs=[pl.BlockSpec((tm, D_MODEL), lambda i, f: (i, 0)),
                  pl.BlockSpec((D_MODEL, tf), lambda i, f: (0, f)), pl.BlockSpec((D_MODEL, tf), lambda i, f: (0, f)),
                  pl.BlockSpec((tf, D_MODEL), lambda i, f: (f, 0)),
                  pl.BlockSpec((1, D_MODEL), lambda i, f: (0, 0)), pl.BlockSpec((1, D_MODEL), lambda i, f: (0, 0))],
        out_specs=pl.BlockSpec((tm, D_MODEL), lambda i, f: (i, 0)),
        out_shape=jax.ShapeDtypeStruct((m, D_MODEL), F32),
        scratch_shapes=[pltpu.VMEM((tm, D_MODEL), BF16), pltpu.VMEM((tm, D_MODEL), F32)],
        compiler_params=_params("parallel", "arbitrary"))(x2d, wg, wu, wd, g, bt)


def _rope_tables(pos):
    half = HEAD_DIM // 2
    inv = ROPE_THETA ** (-jnp.arange(half, dtype=F32) / half)
    ang = pos.astype(F32)[:, None] * inv[None, :]
    cos, sin = jnp.cos(ang), jnp.sin(ang)
    c64 = jnp.concatenate([cos, cos], 1)
    s64 = jnp.concatenate([-sin, sin], 1)
    c = jnp.concatenate([c64, c64], 1)
    s = jnp.concatenate([s64, s64], 1)
    cx = jnp.concatenate([c64, jnp.full_like(c64, IDX_HEADS ** -0.5)], 1)
    sx = jnp.concatenate([s64, jnp.zeros_like(s64)], 1)
    return c, s, cx, sx


def _perm_heads(w, perm):
    lead = w.shape[:-1]
    return w.reshape(*lead, len(perm), HEAD_DIM)[..., np.asarray(perm), :].reshape(*lead, len(perm) * HEAD_DIM)


def _ab_w_in(w):
    ab_sizes = (A_WIDTH, A_KV_HEADS * HEAD_DIM, A_KV_HEADS * HEAD_DIM, IDX_HEADS * IDX_DIM, IDX_DIM, IDX_HEADS,
                B_WIDTH, B_WIDTH, B_WIDTH)
    q, k, v, iq, ik, iw, gb, gc, h = jnp.split(w, np.cumsum(ab_sizes)[:-1].tolist(), axis=-1)
    pad = jnp.zeros((w.shape[0], LANES - IDX_DIM - IDX_HEADS), w.dtype)
    return jnp.concatenate([_perm_heads(q, A_PERM), k, iq, ik, iw, pad, v, gb, gc, h], -1).astype(BF16)


def _c_w_in(w):
    kvw = C_KV_HEADS * HEAD_DIM
    c_sizes = (C_WIDTH,) + (kvw,) * 6 + (3 * C_HEADS,)
    q, kc, vc, ks, vs, kw, vw, g = jnp.split(w, np.cumsum(c_sizes)[:-1].tolist(), axis=-1)
    pad = jnp.zeros((w.shape[0], LANES - 3 * C_HEADS), w.dtype)
    return jnp.concatenate([_perm_heads(q, C_PERM), kc, ks, kw, vc, vs, vw, g, pad], -1).astype(BF16)


def _perm_rows(w_out, perm):
    return w_out.reshape(len(perm), HEAD_DIM, w_out.shape[-1])[np.asarray(perm)].reshape(-1, w_out.shape[-1])


def _block_diag(w, n):
    eye = jnp.eye(n, dtype=w.dtype)
    out = jnp.einsum('gh,...ab->...gahb', eye, w)
    return out.reshape(*w.shape[:-2], n * w.shape[-2], n * w.shape[-1])


def _compress_weights(pe, w1, w2):
    g = C_KV_HEADS
    half = CMP_BLOCK // 2
    bd = _block_diag(w1, g)
    wa = bd[:half].reshape(half * g * HEAD_DIM, g * HEAD_DIM).astype(BF16)
    wb = bd[half:].reshape(half * g * HEAD_DIM, g * HEAD_DIM).astype(BF16)
    pet = jnp.tile(pe, (1, g))
    pea = pet[:half].reshape(1, -1)
    peb = pet[half:].reshape(1, -1)
    return pea, peb, wa, wb, _block_diag(w2, g).astype(BF16)


def _cmp_to_slc(ncp, n_cmp, n_sblk):
    n = np.arange(ncp)[:, None]
    mblk = np.arange(LANES)[None, :]
    hit = ((n * CMP_STRIDE < mblk * SLC_BLOCK + SLC_BLOCK) & (n * CMP_STRIDE + CMP_BLOCK > mblk * SLC_BLOCK)
           & (n < n_cmp) & (mblk < n_sblk))
    return jnp.asarray(hit, BF16)


def _expand_matrix(ls, ac):
    k = np.arange(ls)[None, :]
    mblk = np.arange(LANES)[:, None]
    e = (k // SLC_BLOCK == mblk).astype(np.float32)
    return jnp.asarray(e.reshape(LANES, ls // ac, ac).transpose(1, 0, 2), BF16)


def _gather_pages(pool_layer, page_table):
    g = pool_layer[page_table]
    return g.reshape(g.shape[0], g.shape[1] * g.shape[2], -1)


def _pad_rows(a, n):
    return jnp.pad(a, ((0, 0), (0, n - a.shape[1]), (0, 0)))


def _stack_heads(q3, slots):
    nb, tq, _ = q3.shape
    qt = q3.reshape(nb, tq, slots, LANES).transpose(0, 2, 1, 3)
    lo = jnp.arange(LANES) < HEAD_DIM
    zero = jnp.zeros((), q3.dtype)
    stacked = jnp.concatenate([jnp.where(lo, qt, zero), jnp.where(lo, zero, qt)], 1)
    return stacked.reshape(nb, 2 * slots * tq, LANES)


def _token_minor(cache):
    nd = cache.ndim
    t = jnp.transpose(cache, (0, 1) + tuple(range(3, nd)) + (2,))
    return t.reshape(t.shape[0], t.shape[1], -1, t.shape[-1])


PROMPT_TM = 512
SAMPLE_TQ = 8
FFN_TF = 1408
DSA_SAMPLE_SEQS = 4
NSA_SAMPLE_SEQS = 2
CMP_SAMPLE_SEQS = 4


def _ab_layer(x2d, nb, tlen, start, past, page_table, w_in, conv_w, tabs, *, sample):
    m = nb * tlen
    conv_w8 = jnp.pad(conv_w, ((0, SUBLANES - CONV_W), (0, 0)))
    if sample:
        kt_cache, vt_cache, ikt_cache, prev, layer = past
        tt = jnp.arange(tlen)
        prevs = []
        for d in range(1, CONV_W):
            idx = jnp.clip(CONV_W - 1 + tt - d, 0, CONV_W - 2)
            prevs.append(prev[:, idx].reshape(m, B_WIDTH))
        outs = _proj_ab(x2d, w_in, tabs, conv_w8, tm=m, seq_tiles=1, sample=True, prevs=prevs, dec_seq=tlen)
    else:
        outs = _proj_ab(x2d, w_in, tabs, conv_w8, tm=PROMPT_TM, seq_tiles=tlen // PROMPT_TM, sample=False)
    q, iq, k, v, ikw, kvb, ikb, b_out, uo = outs
    if sample:
        klen_real = page_table.shape[1] * PAGE_SIZE + tlen
        pad_q = lambda a: _pad_rows(a.reshape(nb, tlen, -1), SAMPLE_TQ)
        pad_k = lambda a: _pad_rows(a.reshape(nb, tlen, -1), LANES)
        kvb3 = kvb.reshape(nb, tlen, 2 * LANES)
        iq_heads = pad_q(iq).reshape(nb, SAMPLE_TQ, IDX_HEADS, IDX_DIM).transpose(0, 2, 1, 3)
        a_out = _dsa_sample(_stack_heads(pad_q(q), A_HEADS // 2), iq_heads.reshape(nb, IDX_HEADS * SAMPLE_TQ, IDX_DIM),
                            pad_q(ikw), pad_k(kvb3[..., :LANES]), pad_k(kvb3[..., LANES:]),
                            pad_k(ikb.reshape(nb, tlen, LANES)[..., :IDX_DIM]),
                            kt_cache, vt_cache, ikt_cache, page_table, layer,
                            seqs=DSA_SAMPLE_SEQS, pos0=start, topk=min(A_TOPK_MAX, klen_real // 4))
        a_out = a_out[:, :tlen].reshape(m, A_WIDTH)
        new_conv = uo.reshape(nb, tlen, B_WIDTH)[:, tlen - (CONV_W - 1):]
    else:
        r3 = lambda a: a.reshape(nb, tlen, -1)
        a_out = _dsa_attention(r3(q), r3(iq), r3(ikw), r3(ikb), r3(kvb), tq=Q_BLOCK, pos0=0,
                               topk=min(A_TOPK_MAX, tlen // 4), sc=512, ac=256, causal_skip=True)
        a_out = a_out.reshape(m, A_WIDTH)
        new_conv = uo.reshape(nb, tlen // PROMPT_TM, SUBLANES, B_WIDTH)[:, -1, SUBLANES - (CONV_W - 1):]
    state = (k.reshape(nb, tlen, A_KV_HEADS, HEAD_DIM), v.reshape(nb, tlen, A_KV_HEADS, HEAD_DIM),
             ikw[:, :IDX_DIM].reshape(nb, tlen, IDX_DIM), new_conv)
    return a_out, b_out, state


def _c_layer(x2d, nb, tlen, start, past, w_in, cw_k, cw_v, tabs, *, sample):
    m = nb * tlen
    kvw = C_KV_HEADS * HEAD_DIM
    tm = m if sample else PROMPT_TM
    q, kc, ks, kw, vc, vs, vw, gates, kvb = _proj_c(x2d, w_in, tabs[:2], tm=tm)
    if sample:
        kct, vct, kst, vst, wink_t, winv_t, buf_k, buf_v, page_table, layer = past
        past_len = page_table.shape[1] * PAGE_SIZE
        n_cmp = (past_len + tlen - CMP_BLOCK) // CMP_STRIDE + 1
        assert n_cmp <= past_len // CMP_STRIDE, "compression blocks must lie inside the cached rows"
        assert tlen <= LANES and start == past_len

        def tap_major(cw):
            pea, peb, wa, wb, w2 = cw
            pet = jnp.concatenate([pea.reshape(-1, kvw), peb.reshape(-1, kvw)], 0)
            wab = jnp.concatenate([wa.reshape(-1, kvw, kvw), wb.reshape(-1, kvw, kvw)], 0)
            return pet, wab, w2
        kcmp = _compress_sample(kct, page_table, layer, *tap_major(cw_k), seqs=CMP_SAMPLE_SEQS)
        vcmp = _compress_sample(vct, page_table, layer, *tap_major(cw_v), seqs=CMP_SAMPLE_SEQS)
        ncp = kcmp.shape[1]
        ls = past_len + LANES
        n_sblk = _cdiv(past_len + tlen, SLC_BLOCK)
        kvb3 = kvb.reshape(nb, tlen, 4 * kvw)
        new_rows = [_pad_rows(kvb3[..., n * kvw:(n + 1) * kvw], LANES) for n in range(4)]
        q3 = _pad_rows(q.reshape(nb, tlen, C_WIDTH), SAMPLE_TQ)
        half_w = C_WIDTH // 2
        qs = jnp.concatenate([_stack_heads(q3[..., :half_w], C_HEADS // 4),
                              _stack_heads(q3[..., half_w:], C_HEADS // 4)], 1)
        out = _nsa_sample(qs, _pad_rows(gates.reshape(nb, tlen, LANES), SAMPLE_TQ), kcmp, vcmp, *new_rows,
                          wink_t, winv_t, _cmp_to_slc(ncp, n_cmp, n_sblk), _expand_matrix(ls, ls)[0],
                          kst, vst, page_table, layer, seqs=NSA_SAMPLE_SEQS, pos0=start)
        out = out[:, :tlen].reshape(m, C_WIDTH)
        r4 = lambda a: a.reshape(nb, tlen, C_KV_HEADS, HEAD_DIM)
        win_k = jnp.concatenate([buf_k, r4(kw)], 1)[:, tlen:]
        win_v = jnp.concatenate([buf_v, r4(vw)], 1)[:, tlen:]
    else:
        zk = kc.reshape(nb, tlen // CMP_STRIDE, CMP_STRIDE * kvw)
        zv = vc.reshape(nb, tlen // CMP_STRIDE, CMP_STRIDE * kvw)
        n_cmp = (tlen - CMP_BLOCK) // CMP_STRIDE + 1
        kcmp = _compress(zk, *cw_k, nb_step=1)
        vcmp = _compress(zv, *cw_v, nb_step=1)
        ncp = kcmp.shape[1]
        n_sblk = _cdiv(tlen, SLC_BLOCK)
        kvb3 = kvb.reshape(nb, tlen, 4 * kvw)
        out = _nsa_attention(q.reshape(nb, tlen, C_WIDTH), gates.reshape(nb, tlen, LANES), kcmp, vcmp, kvb3, kvb3,
                             _cmp_to_slc(ncp, n_cmp, n_sblk), _expand_matrix(tlen, 256),
                             tq=Q_BLOCK, pos0=0, woff=0, ac=256, acw=256, causal_skip=True, s_blk=0, w_blk=1)
        out = out.reshape(m, C_WIDTH)
        r4 = lambda a: a.reshape(nb, tlen, C_KV_HEADS, HEAD_DIM)
        keep = min(WINDOW, tlen)
        win_k = r4(kw)[:, tlen - keep:]
        win_v = r4(vw)[:, tlen - keep:]
    r4 = lambda a: a.reshape(nb, tlen, C_KV_HEADS, HEAD_DIM)
    return out, (r4(kc), r4(vc), r4(ks), r4(vs), win_k, win_v)


def kernel(x_prompt, x_sample, cache_a_k, cache_a_v, cache_a_ik, state_b_conv, cache_c_cmp_k, cache_c_cmp_v, cache_c_slc_k, cache_c_slc_v, state_c_win_k, state_c_win_v, page_table, ab_w_in, ab_conv_w, ab_w_out, c_w_in, c_cmp_pe_k, c_cmp_w1_k, c_cmp_w2_k, c_cmp_pe_v, c_cmp_w1_v, c_cmp_w2_v, c_w_out, ffn_w_gate, ffn_w_up, ffn_w_down, ln1_g, ln1_b, ln2_g, ln2_b):
    bp, tp, _ = x_prompt.shape
    bs, ts, _ = x_sample.shape
    past_len = page_table.shape[1] * PAGE_SIZE
    xp = x_prompt.reshape(bp * tp, D_MODEL)
    xs = x_sample.reshape(bs * ts, D_MODEL)
    tabs_p = _rope_tables(jnp.arange(tp))
    tabs_s = _rope_tables(past_len + jnp.arange(bs * ts) % ts)
    kt_a, vt_a, ikt_a = _token_minor(cache_a_k), _token_minor(cache_a_v), _token_minor(cache_a_ik)
    kct_c, vct_c = _token_minor(cache_c_cmp_k), _token_minor(cache_c_cmp_v)
    kst_c, vst_c = _token_minor(cache_c_slc_k), _token_minor(cache_c_slc_v)
    wink_t, winv_t = _token_minor(state_c_win_k), _token_minor(state_c_win_v)
    ab_p, ab_s, c_p, c_s = [], [], [], []
    for layer in range(DEPTH):
        i = layer // 2
        row1 = lambda a: a[layer].reshape(1, D_MODEL)
        if layer % 2 == 0:
            w_in = _ab_w_in(ab_w_in[i])
            w_out = jnp.concatenate([_perm_rows(ab_w_out[i][:A_WIDTH], A_PERM), ab_w_out[i][A_WIDTH:]], 0).astype(BF16)
            a_p, b_p, st_p = _ab_layer(xp, bp, tp, 0, None, None, w_in, ab_conv_w[i], tabs_p, sample=False)
            past = (kt_a, vt_a, ikt_a, state_b_conv[i], i)
            a_s, b_s, st_s = _ab_layer(xs, bs, ts, past_len, past, page_table, w_in, ab_conv_w[i], tabs_s, sample=True)
            ab_p.append(st_p)
            ab_s.append(st_s)
            xp = _outproj_ln(xp, a_p, b_p, 0, 0, w_out, row1(ln1_g), row1(ln1_b), tm=PROMPT_TM)
            xs = _outproj_ln(xs, a_s, b_s, 0, 0, w_out, row1(ln1_g), row1(ln1_b), tm=bs * ts)
        else:
            w_in = _c_w_in(c_w_in[i])
            w_out = _perm_rows(c_w_out[i], C_PERM).astype(BF16)
            cw_k = _compress_weights(c_cmp_pe_k[i], c_cmp_w1_k[i], c_cmp_w2_k[i])
            cw_v = _compress_weights(c_cmp_pe_v[i], c_cmp_w1_v[i], c_cmp_w2_v[i])
            o_p, st_p = _c_layer(xp, bp, tp, 0, None, w_in, cw_k, cw_v, tabs_p, sample=False)
            past = (kct_c, vct_c, kst_c, vst_c, wink_t, winv_t, state_c_win_k[i], state_c_win_v[i], page_table, i)
            o_s, st_s = _c_layer(xs, bs, ts, past_len, past, w_in, cw_k, cw_v, tabs_s, sample=True)
            c_p.append(st_p)
            c_s.append(st_s)
            xp = _outproj_ln(xp, o_p, o_p, 0, 1, w_out, row1(ln1_g), row1(ln1_b), tm=PROMPT_TM)
            xs = _outproj_ln(xs, o_s, o_s, 0, 1, w_out, row1(ln1_g), row1(ln1_b), tm=bs * ts)
        wg, wu, wd = ffn_w_gate[layer].astype(BF16), ffn_w_up[layer].astype(BF16), ffn_w_down[layer].astype(BF16)
        xp = _ffn_ln(xp, wg, wu, wd, row1(ln2_g), row1(ln2_b), tm=PROMPT_TM, tf=FFN_TF)
        xs = _ffn_ln(xs, wg, wu, wd, row1(ln2_g), row1(ln2_b), tm=bs * ts, tf=FFN_TF)
    stk = lambda lst, j: jnp.stack([e[j] for e in lst], 0)
    return (xp.reshape(bp, tp, D_MODEL), xs.reshape(bs, ts, D_MODEL),
            stk(ab_p, 0), stk(ab_p, 1), stk(ab_p, 2), stk(ab_p, 3),
            stk(c_p, 0), stk(c_p, 1), stk(c_p, 2), stk(c_p, 3), stk(c_p, 4), stk(c_p, 5),
            stk(ab_s, 0), stk(ab_s, 1), stk(ab_s, 2), stk(ab_s, 3),
            stk(c_s, 0), stk(c_s, 1), stk(c_s, 2), stk(c_s, 3), stk(c_s, 4), stk(c_s, 5))
```

```python
import functools

import numpy as np
import jax
import jax.numpy as jnp
from jax import lax
from jax.experimental import pallas as pl
from jax.experimental.pallas import tpu as pltpu

D_MODEL = 1024
DEPTH = 4
PAGE_SIZE = 128
HEAD_DIM = 64
ROPE_THETA = 10000.0
A_HEADS = 8
A_KV_HEADS = 2
IDX_HEADS = 4
IDX_DIM = 64
A_TOPK_MAX = 256
A_WIDTH = A_HEADS * HEAD_DIM
B_WIDTH = D_MODEL // 2
CONV_W = 3
C_HEADS = 16
C_KV_HEADS = 4
C_WIDTH = C_HEADS * HEAD_DIM
CMP_BLOCK = 32
CMP_STRIDE = 16
SLC_BLOCK = 64
SLC_TOPN = 16
WINDOW = 512
D_FF = ((8 * D_MODEL + 3 * 256 - 1) // (3 * 256)) * 256
LN_EPS = 1e-5
ALPHA = (2 * DEPTH) ** 0.25
Q_BLOCK = 128

F32 = jnp.float32
BF16 = jnp.bfloat16
I32 = jnp.int32

LANES = 128
SUBLANES = 8
VMEM_LIMIT_BYTES = 56 * 1024 * 1024
MASKED = -1e30
Q_SCALE = HEAD_DIM ** -0.5 * float(np.log2(np.e))
INT_MIN = -(2 ** 31)
NEG_INF_KEY = int(np.int32(np.uint32(0xFF800000) ^ np.uint32(0x7FFFFFFF)))
NT_DIMS = (((1,), (1,)), ((), ()))

A_PERM = tuple(h for j in range(4) for h in (j, 4 + j))
C_PERM = tuple(h for pr in range(2) for j in range(4) for h in (8 * pr + j, 8 * pr + 4 + j))

AB_COLS = 2688
C_COLS = 2688
C_ROPED = C_WIDTH + 3 * C_KV_HEADS * HEAD_DIM


def _params(*sem):
    return pltpu.CompilerParams(dimension_semantics=sem, vmem_limit_bytes=VMEM_LIMIT_BYTES)


def _cdiv(a, b):
    return (a + b - 1) // b


def _rope128(r, c, s, first_half):
    sw = jnp.where(first_half, pltpu.roll(r, 96, 1), pltpu.roll(r, 32, 1))
    return r * c + sw * s


def _half_masked(q_bf16, mask):
    return jnp.where(mask, q_bf16.astype(F32), 0.0).astype(BF16)


def _flash(qs, heads, k_ref, v_ref, col, c_lo, n_chunks, ac, bias_fn, acc_ref, m_ref, l_ref, splits=2):
    tq = qs.shape[0] // heads
    reps = ac // LANES
    acc_ref[...] = jnp.zeros(acc_ref.shape, F32)
    m_ref[...] = jnp.full(m_ref.shape, MASKED, F32)
    l_ref[...] = jnp.zeros(l_ref.shape, F32)

    hps = heads // splits

    def body(ci, carry):
        c = c_lo + ci
        off = pl.multiple_of(c * ac, ac)
        kc = k_ref[pl.ds(off, ac), col:col + LANES]
        vc = v_ref[pl.ds(off, ac), col:col + LANES]
        head_bias = bias_fn(c)
        for g in range(splits):
            s = lax.dot_general(qs[g * hps * tq:(g + 1) * hps * tq], kc, NT_DIMS, preferred_element_type=F32)
            ps, alphas = [], []
            for hh in range(hps):
                rows = slice((g * hps + hh) * tq, (g * hps + hh + 1) * tq)
                sh = s[hh * tq:(hh + 1) * tq] + head_bias(g * hps + hh)
                m_old = m_ref[rows]
                m_new = jnp.maximum(m_old, jnp.max(sh, axis=1, keepdims=True))
                alpha = jnp.exp2(m_old - m_new)
                p = jnp.exp2(sh - jnp.tile(m_new, (1, reps)))
                l_ref[rows] = alpha * l_ref[rows] + jnp.sum(p, axis=1, keepdims=True)
                m_ref[rows] = m_new
                ps.append(p.astype(BF16))
                alphas.append(alpha)
            pv = jnp.dot(jnp.concatenate(ps, 0), vc, preferred_element_type=F32)
            for hh in range(hps):
                rows = slice((g * hps + hh) * tq, (g * hps + hh + 1) * tq)
                acc_ref[rows] = alphas[hh] * acc_ref[rows] + pv[hh * tq:(hh + 1) * tq]
        return carry

    lax.fori_loop(0, n_chunks, body, 0)
    acc_ref[...] = jnp.where(m_ref[...] > 0.5 * MASKED, acc_ref[...] / l_ref[...], 0.0)


def _layernorm(y, g, b):
    mu = jnp.mean(y, axis=-1, keepdims=True)
    d = y - mu
    var = jnp.mean(d * d, axis=-1, keepdims=True)
    return d * lax.rsqrt(var + LN_EPS) * g + b


def _proj_ab_kernel(*refs, tm, seq_tiles, sample, dec_seq):
    if sample:
        (x_ref, w_ref, c_ref, s_ref, cx_ref, sx_ref, cw_ref, p1_ref, p2_ref,
         q_ref, iq_ref, k_ref, v_ref, ikw_ref, kvb_ref, ikb_ref, bo_ref, uo_ref, ubuf) = refs
        prev_refs = (None, p1_ref, p2_ref)
    else:
        (x_ref, w_ref, c_ref, s_ref, cx_ref, sx_ref, cw_ref,
         q_ref, iq_ref, k_ref, v_ref, ikw_ref, kvb_ref, ikb_ref, bo_ref, uo_ref, ubuf) = refs
    i = pl.program_id(0)
    y = jnp.dot(x_ref[...].astype(BF16), w_ref[...], preferred_element_type=F32)
    lane = lax.broadcasted_iota(I32, (tm, LANES), 1)
    first_half = (lane & (HEAD_DIM - 1)) < HEAD_DIM // 2
    c = c_ref[...]
    s = s_ref[...]
    ro = [_rope128(y[:, j * LANES:(j + 1) * LANES], c, s, first_half) for j in range(7)]
    ro.append(_rope128(y[:, 7 * LANES:8 * LANES], cx_ref[...], sx_ref[...], first_half))
    for j in range(4):
        q_ref[:, j * LANES:(j + 1) * LANES] = (ro[j] * Q_SCALE).astype(BF16)
    k = ro[4]
    v = y[:, 1024:1152]
    k_ref[...] = k
    v_ref[...] = v
    kvb_ref[:, 0:LANES] = k.astype(BF16)
    kvb_ref[:, LANES:2 * LANES] = v.astype(BF16)
    for j in range(2):
        iq_ref[:, j * LANES:(j + 1) * LANES] = (ro[5 + j] * IDX_DIM ** -0.5).astype(BF16)
    ikw = ro[7]
    ikw_ref[...] = ikw
    ikb_ref[...] = jnp.where(lane < IDX_DIM, ikw, pltpu.roll(ikw, IDX_DIM, 1)).astype(BF16)

    gate_b = y[:, 1152:1664]
    u = y[:, 1664:2176] * y[:, 2176:2688]

    @pl.when(i % seq_tiles == 0)
    def _():
        ubuf[0:SUBLANES, :] = jnp.zeros((SUBLANES, B_WIDTH), F32)

    @pl.when(i % seq_tiles != 0)
    def _():
        ubuf[0:SUBLANES, :] = ubuf[tm:tm + SUBLANES, :]

    ubuf[SUBLANES:tm + SUBLANES, :] = u
    cw = cw_ref[...]
    conv = u * cw[CONV_W - 1:CONV_W, :]
    if sample:
        t = lax.broadcasted_iota(I32, (tm, 1), 0) % dec_seq
    for d in range(1, CONV_W):
        ud = ubuf[SUBLANES - d:tm + SUBLANES - d, :]
        if sample:
            ud = jnp.where(t >= d, ud, prev_refs[d][...])
        conv = conv + ud * cw[CONV_W - 1 - d:CONV_W - d, :]
    bo_ref[...] = (gate_b * conv).astype(BF16)
    if sample:
        uo_ref[...] = u
    else:
        uo_ref[...] = u[tm - SUBLANES:tm, :]


def _proj_ab(x2d, w, tabs, conv_w8, *, tm, seq_tiles, sample, prevs=None, dec_seq=1):
    m = x2d.shape[0]
    nt = tabs[0].shape[0] // tm
    row = lambda i: (i, 0)
    const = lambda i: (0, 0)
    tab = lambda i: (i % nt, 0)
    in_specs = [pl.BlockSpec((tm, D_MODEL), row), pl.BlockSpec((D_MODEL, AB_COLS), const)]
    in_specs += [pl.BlockSpec((tm, LANES), tab)] * 4
    in_specs += [pl.BlockSpec((SUBLANES, B_WIDTH), const)]
    args = [x2d, w, *tabs, conv_w8]
    if sample:
        in_specs += [pl.BlockSpec((tm, B_WIDTH), row)] * 2
        args += list(prevs)
    u_rows = tm if sample else SUBLANES
    out_shape = [
        jax.ShapeDtypeStruct((m, A_WIDTH), BF16),
        jax.ShapeDtypeStruct((m, IDX_HEADS * IDX_DIM), BF16),
        jax.ShapeDtypeStruct((m, LANES), F32),
        jax.ShapeDtypeStruct((m, LANES), F32),
        jax.ShapeDtypeStruct((m, LANES), F32),
        jax.ShapeDtypeStruct((m, 2 * LANES), BF16),
        jax.ShapeDtypeStruct((m, LANES), BF16),
        jax.ShapeDtypeStruct((m, B_WIDTH), BF16),
        jax.ShapeDtypeStruct((m // tm * u_rows, B_WIDTH), F32),
    ]
    out_specs = [
        pl.BlockSpec((tm, A_WIDTH), row), pl.BlockSpec((tm, IDX_HEADS * IDX_DIM), row),
        pl.BlockSpec((tm, LANES), row), pl.BlockSpec((tm, LANES), row), pl.BlockSpec((tm, LANES), row),
        pl.BlockSpec((tm, 2 * LANES), row), pl.BlockSpec((tm, LANES), row),
        pl.BlockSpec((tm, B_WIDTH), row), pl.BlockSpec((u_rows, B_WIDTH), row),
    ]
    kern = functools.partial(_proj_ab_kernel, tm=tm, seq_tiles=seq_tiles, sample=sample, dec_seq=dec_seq)
    return pl.pallas_call(
        kern, grid=(m // tm,), in_specs=in_specs, out_specs=out_specs, out_shape=out_shape,
        scratch_shapes=[pltpu.VMEM((tm + SUBLANES, B_WIDTH), F32)],
        compiler_params=_params("arbitrary"))(*args)


def _count(key_ref, n_chunks, width, rows, pred):
    def body(c, acc):
        m = jnp.where(pred(key_ref[c], c), 1.0, 0.0)
        part = m[:, 0:LANES]
        for j in range(1, width // LANES):
            part = part + m[:, j * LANES:(j + 1) * LANES]
        return acc + part
    acc = lax.fori_loop(0, n_chunks, body, jnp.zeros((rows, LANES), F32))
    return jnp.sum(acc, axis=1, keepdims=True)


def _topk_bias(key_ref, bias_ref, n_sc, sc, ac, rows, topk, klen):
    ratio = sc // ac
    kiota = lax.broadcasted_iota(I32, (rows, sc), 1)
    count = functools.partial(_count, key_ref, n_sc, sc, rows)
    kf = jnp.float32(topk)
    n_all = jnp.asarray(n_sc * sc, F32)

    def bit_body(b, carry):
        thr, n_ge = carry
        cand = thr + jnp.left_shift(jnp.int32(1), 31 - b)
        cnt = count(lambda key, c: key >= cand)
        take = cnt >= kf
        return jnp.where(take, cand, thr), jnp.where(take, cnt, n_ge)
    thr, n_ge = lax.fori_loop(0, 32, bit_body, (jnp.full((rows, 1), INT_MIN, I32), jnp.full((rows, 1), n_all, F32)))

    def tie_search():
        need = kf - count(lambda key, c: key > thr)

        def tie_body(b, last):
            cand = last + jnp.left_shift(jnp.int32(1), klen.bit_length() - 1 - b)
            cnt = count(lambda key, c: (key == thr) & (c * sc + kiota < cand))
            return jnp.where(cnt < need, cand, last)
        return lax.fori_loop(0, klen.bit_length(), tie_body, jnp.zeros((rows, 1), I32))

    tied = (n_ge > kf) & (thr > jnp.int32(NEG_INF_KEY))
    last = lax.cond(jnp.max(jnp.where(tied, 1.0, 0.0)) > 0.5, tie_search, lambda: jnp.full((rows, 1), klen, I32))

    def bias_body(c, carry):
        key = key_ref[c]
        sel = (key > thr) | ((key == thr) & (c * sc + kiota <= last))
        sel = sel & (key > jnp.int32(NEG_INF_KEY))
        bias = jnp.where(sel, 0.0, MASKED)
        for r in range(ratio):
            bias_ref[c * ratio + r] = bias[:, r * ac:(r + 1) * ac]
        return carry
    lax.fori_loop(0, n_sc, bias_body, 0)


def _dsa_kernel(q_ref, iq_ref, ikw_ref, ikb_ref, kvb_ref, o_ref, key_ref, bias_ref, acc_ref, m_ref, l_ref,
                *, tq, klen, topk, sc, ac):
    i = pl.program_id(1)
    pos = i * tq + lax.broadcasted_iota(I32, (tq, 1), 0)
    n_sc = _cdiv((i + 1) * tq, sc)
    ratio = sc // ac
    lane = lax.broadcasted_iota(I32, (tq, LANES), 1)
    lo_half = lane < HEAD_DIM
    hi_half = jnp.logical_not(lo_half)
    kiota = lax.broadcasted_iota(I32, (tq, sc), 1)

    iqs = jnp.concatenate([_half_masked(iq_ref[:, (h // 2) * LANES:(h // 2 + 1) * LANES],
                                        lo_half if h % 2 == 0 else hi_half) for h in range(IDX_HEADS)], 0)

    def score_body(c, carry):
        off = pl.multiple_of(c * sc, sc)
        logits = lax.dot_general(iqs, ikb_ref[pl.ds(off, sc), :], NT_DIMS, preferred_element_type=F32)
        sco = jnp.zeros((tq, sc), F32)
        for h in range(IDX_HEADS):
            sco = sco + jnp.maximum(logits[h * tq:(h + 1) * tq], 0.0) * ikw_ref[:, IDX_DIM + h:IDX_DIM + h + 1]
        sco = jnp.where(sco == 0.0, 0.0, sco)
        bits = pltpu.bitcast(sco, I32)
        key = jnp.where(bits < 0, bits ^ jnp.int32(0x7FFFFFFF), bits)
        key_ref[c] = jnp.where(c * sc + kiota <= pos, key, jnp.int32(NEG_INF_KEY))
        return carry
    lax.fori_loop(0, n_sc, score_body, 0)
    _topk_bias(key_ref, bias_ref, n_sc, sc, ac, tq, topk, klen)

    half_slots = A_HEADS // 2
    qs = jnp.concatenate([_half_masked(q_ref[:, (h % half_slots) * LANES:(h % half_slots + 1) * LANES],
                                       lo_half if h < half_slots else hi_half) for h in range(A_HEADS)], 0)
    def shared_bias(c):
        bias = bias_ref[c]
        return lambda h: bias
    _flash(qs, A_HEADS, kvb_ref, kvb_ref.at[:, LANES:2 * LANES], 0, 0, n_sc * ratio, ac, shared_bias,
           acc_ref, m_ref, l_ref)
    for j in range(half_slots):
        o_lo = acc_ref[j * tq:(j + 1) * tq]
        o_hi = acc_ref[(half_slots + j) * tq:(half_slots + j + 1) * tq]
        o_ref[:, j * LANES:(j + 1) * LANES] = jnp.where(lo_half, o_lo, o_hi).astype(BF16)


def _dsa_attention(q, iq, ikw, ikb, kvb, *, tq, topk, sc, ac):
    nb, tlen, _ = q.shape
    klen = ikb.shape[1]
    qspec = lambda w: pl.BlockSpec((None, tq, w), lambda b, i: (b, i, 0))
    kspec = lambda w: pl.BlockSpec((None, klen, w), lambda b, i: (b, 0, 0))
    kern = functools.partial(_dsa_kernel, tq=tq, klen=klen, topk=topk, sc=sc, ac=ac)
    return pl.pallas_call(
        kern, grid=(nb, tlen // tq),
        in_specs=[qspec(A_WIDTH), qspec(IDX_HEADS * IDX_DIM), qspec(LANES), kspec(LANES), kspec(2 * LANES)],
        out_specs=qspec(A_WIDTH),
        out_shape=jax.ShapeDtypeStruct((nb, tlen, A_WIDTH), BF16),
        scratch_shapes=[pltpu.VMEM((klen // sc, tq, sc), I32), pltpu.VMEM((klen // ac, tq, ac), F32)]
        + [pltpu.VMEM((A_HEADS * tq, LANES), F32)] * 3,
        compiler_params=_params("parallel", "arbitrary"))(q, iq, ikw, ikb, kvb)


def _page_specs(shape, layer, seqs, n_pages):
    specs = []
    for s in range(seqs):
        for p in range(n_pages):
            specs.append(pl.BlockSpec((None, None) + shape,
                                      lambda i, pt, s=s, p=p: (layer, pt[i * seqs + s, p], 0, 0)))
    return specs


def _softmax_rows(s):
    m = jnp.max(s, axis=1, keepdims=True)
    p = jnp.exp2(s - m)
    return p, jnp.sum(p, axis=1, keepdims=True)


def _dsa_sample_kernel(pt_ref, *refs, seqs, n_pages, pos0, topk):
    del pt_ref
    np_all = seqs * n_pages
    qs_ref, iq_ref, ikw_ref, knew_ref, vnew_ref, iknew_ref = refs[:6]
    kt_pages = refs[6:6 + np_all]
    vt_pages = refs[6 + np_all:6 + 2 * np_all]
    ikt_pages = refs[6 + 2 * np_all:6 + 3 * np_all]
    o_ref, ktb, vtb, iktb, key_ref, bias_ref = refs[6 + 3 * np_all:]
    tq = SUBLANES
    past = n_pages * PAGE_SIZE
    klen = past + LANES
    for s in range(seqs):
        for p in range(n_pages):
            cols = slice(p * PAGE_SIZE, (p + 1) * PAGE_SIZE)
            ktb[s, :, cols] = kt_pages[s * n_pages + p][...].astype(BF16)
            vtb[s, :, cols] = vt_pages[s * n_pages + p][...].astype(BF16)
            iktb[s, :, cols] = ikt_pages[s * n_pages + p][...].astype(BF16)

    pos = pos0 + lax.broadcasted_iota(I32, (tq, 1), 0)
    kidx = lax.broadcasted_iota(I32, (tq, klen), 1)
    for s in range(seqs):
        iq = iq_ref[s]
        logits = jnp.concatenate(
            [jnp.dot(iq, iktb[s], preferred_element_type=F32),
             lax.dot_general(iq, iknew_ref[s], NT_DIMS, preferred_element_type=F32)], 1)
        sco = jnp.zeros((tq, klen), F32)
        for h in range(IDX_HEADS):
            sco = sco + jnp.maximum(logits[h * tq:(h + 1) * tq], 0.0) * ikw_ref[s, :, IDX_DIM + h:IDX_DIM + h + 1]
        sco = jnp.where(sco == 0.0, 0.0, sco)
        bits = pltpu.bitcast(sco, I32)
        key = jnp.where(bits < 0, bits ^ jnp.int32(0x7FFFFFFF), bits)
        key_ref[0, s * tq:(s + 1) * tq, :] = jnp.where(kidx <= pos, key, jnp.int32(NEG_INF_KEY))
    _topk_bias(key_ref, bias_ref, 1, klen, klen, seqs * tq, topk, klen)

    lane = lax.broadcasted_iota(I32, (tq, LANES), 1)
    lo_half = lane < HEAD_DIM
    half_slots = A_HEADS // 2
    for s in range(seqs):
        qs = qs_ref[s]
        sc_all = jnp.concatenate(
            [jnp.dot(qs, ktb[s], preferred_element_type=F32),
             lax.dot_general(qs, knew_ref[s], NT_DIMS, preferred_element_type=F32)], 1)
        p, l = _softmax_rows(sc_all + jnp.tile(bias_ref[0, s * tq:(s + 1) * tq, :], (A_HEADS, 1)))
        pb = p.astype(BF16)
        o = (lax.dot_general(pb[:, :past], vtb[s], NT_DIMS, preferred_element_type=F32)
             + jnp.dot(pb[:, past:], vnew_ref[s], preferred_element_type=F32)) / l
        for j in range(half_slots):
            o_ref[s, :, j * LANES:(j + 1) * LANES] = jnp.where(
                lo_half, o[j * tq:(j + 1) * tq], o[(half_slots + j) * tq:(half_slots + j + 1) * tq]).astype(BF16)


def _dsa_sample(qs, iq, ikw, knew, vnew, iknew, kt_cache, vt_cache, ikt_cache, page_table, layer, *, seqs, pos0,
                topk):
    nb = qs.shape[0]
    n_pages = page_table.shape[1]
    klen = n_pages * PAGE_SIZE + LANES
    tq = SUBLANES
    seq_spec = lambda a: pl.BlockSpec((seqs,) + a.shape[1:], lambda i, pt: (i, 0, 0))
    in_specs = [seq_spec(a) for a in (qs, iq, ikw, knew, vnew, iknew)]
    in_specs += _page_specs(kt_cache.shape[2:], layer, seqs, n_pages)
    in_specs += _page_specs(vt_cache.shape[2:], layer, seqs, n_pages)
    in_specs += _page_specs(ikt_cache.shape[2:], layer, seqs, n_pages)
    np_all = seqs * n_pages
    kvw = A_KV_HEADS * HEAD_DIM
    grid_spec = pltpu.PrefetchScalarGridSpec(
        num_scalar_prefetch=1, grid=(nb // seqs,), in_specs=in_specs,
        out_specs=pl.BlockSpec((seqs, tq, A_WIDTH), lambda i, pt: (i, 0, 0)),
        scratch_shapes=[pltpu.VMEM((seqs, kvw, n_pages * PAGE_SIZE), BF16),
                        pltpu.VMEM((seqs, kvw, n_pages * PAGE_SIZE), BF16),
                        pltpu.VMEM((seqs, IDX_DIM, n_pages * PAGE_SIZE), BF16),
                        pltpu.VMEM((1, seqs * tq, klen), I32), pltpu.VMEM((1, seqs * tq, klen), F32)])
    kern = functools.partial(_dsa_sample_kernel, seqs=seqs, n_pages=n_pages, pos0=pos0, topk=topk)
    return pl.pallas_call(
        kern, grid_spec=grid_spec, out_shape=jax.ShapeDtypeStruct((nb, tq, A_WIDTH), BF16),
        compiler_params=_params("arbitrary"))(
            page_table, qs, iq, ikw, knew, vnew, iknew,
            *([kt_cache] * np_all), *([vt_cache] * np_all), *([ikt_cache] * np_all))


def _gelu_tanh(x):
    return 0.5 * x * (1.0 + jnp.tanh(np.sqrt(2.0 / np.pi) * (x + 0.044715 * x * x * x)))


def _compress_sample_kernel(pt_ref, pet_ref, wab_ref, w2_ref, *refs, seqs, n_pages):
    del pt_ref
    pages = refs[:seqs * n_pages]
    o_ref, tok_ref = refs[seqs * n_pages:]
    width = C_KV_HEADS * HEAD_DIM
    for s in range(seqs):
        for p in range(n_pages):
            base = (s * n_pages + p) * PAGE_SIZE
            for c in range(width // LANES):
                blk = pages[s * n_pages + p][c * LANES:(c + 1) * LANES, :]
                tok_ref[c, base:base + PAGE_SIZE, :] = blk.T
    rows = seqs * n_pages * PAGE_SIZE // CMP_STRIDE
    half = CMP_BLOCK // 2
    ra = jnp.zeros((rows, width), F32)
    rb = jnp.zeros((rows, width), F32)
    for j in range(half):
        xj = jnp.concatenate([tok_ref[c, pl.ds(j, rows, stride=CMP_STRIDE), :] for c in range(width // LANES)], 1)
        ra = ra + jnp.dot((xj + pet_ref[j:j + 1, :]).astype(BF16), wab_ref[j], preferred_element_type=F32)
        rb = rb + jnp.dot((xj + pet_ref[half + j:half + j + 1, :]).astype(BF16), wab_ref[half + j],
                          preferred_element_type=F32)
    hid = _gelu_tanh(ra + pltpu.roll(rb, rows - 1, 0))
    out = jnp.dot(hid.astype(BF16), w2_ref[...], preferred_element_type=F32)
    o_ref[...] = out.reshape(o_ref.shape).astype(BF16)


def _compress_sample(cache_t, page_table, layer, pet, wab, w2, *, seqs):
    nb, n_pages = page_table.shape
    width = C_KV_HEADS * HEAD_DIM
    r = n_pages * PAGE_SIZE // CMP_STRIDE
    const2 = lambda i, pt: (0, 0)
    grid_spec = pltpu.PrefetchScalarGridSpec(
        num_scalar_prefetch=1, grid=(nb // seqs,),
        in_specs=[pl.BlockSpec(pet.shape, const2), pl.BlockSpec(wab.shape, lambda i, pt: (0, 0, 0)),
                  pl.BlockSpec(w2.shape, const2)] + _page_specs(cache_t.shape[2:], layer, seqs, n_pages),
        out_specs=pl.BlockSpec((seqs, r, width), lambda i, pt: (i, 0, 0)),
        scratch_shapes=[pltpu.VMEM((width // LANES, seqs * n_pages * PAGE_SIZE, LANES), F32)])
    return pl.pallas_call(
        functools.partial(_compress_sample_kernel, seqs=seqs, n_pages=n_pages), grid_spec=grid_spec,
        out_shape=jax.ShapeDtypeStruct((nb, r, width), BF16),
        compiler_params=_params("arbitrary"))(page_table, pet, wab, w2, *([cache_t] * (seqs * n_pages)))


def _nsa_sample_kernel(pt_ref, *refs, seqs, n_pages, pos0, nblk):
    del pt_ref
    np_all = seqs * n_pages
    (qs_ref, g_ref, kc_ref, vc_ref, ksn_ref, vsn_ref, kwn_ref, vwn_ref, wink_ref, winv_ref, c2st_ref,
     e_ref) = refs[:12]
    kst_pages = refs[12:12 + np_all]
    vst_pages = refs[12 + np_all:12 + 2 * np_all]
    o_ref, kst, vst, part_ref, bias_ref = refs[12 + 2 * np_all:]
    tq = SUBLANES
    hpg = C_HEADS // C_KV_HEADS
    nh = 2 * hpg
    npair = C_KV_HEADS // 2
    past = n_pages * PAGE_SIZE
    ls = past + LANES
    nbuf = wink_ref.shape[-1]
    lw = nbuf + LANES
    ncp = kc_ref.shape[1]
    for s in range(seqs):
        for p in range(n_pages):
            cols = slice(p * PAGE_SIZE, (p + 1) * PAGE_SIZE)
            kst[s, :, cols] = kst_pages[s * n_pages + p][...].astype(BF16)
            vst[s, :, cols] = vst_pages[s * n_pages + p][...].astype(BF16)

    rows = nh * tq
    t_row = pos0 + lax.broadcasted_iota(I32, (rows, 1), 0) % tq
    cmp_visible = lax.broadcasted_iota(I32, (rows, ncp), 1) * CMP_STRIDE + (CMP_BLOCK - 1) <= t_row
    lane = lax.broadcasted_iota(I32, (tq, LANES), 1)
    lo_half = lane < HEAD_DIM
    pair_cols = lambda pr: slice(pr * LANES, (pr + 1) * LANES)

    psums = []
    for s in range(seqs):
        for pr in range(npair):
            qp = qs_ref[s, pr * rows:(pr + 1) * rows, :]
            s_c = lax.dot_general(qp, kc_ref[s, :, pair_cols(pr)], NT_DIMS, preferred_element_type=F32)
            s_c = jnp.where(cmp_visible, s_c, -jnp.inf)
            m = jnp.max(s_c, axis=1, keepdims=True)
            m = jnp.where(m > -jnp.inf, m, 0.0)
            p = jnp.exp2(s_c - m)
            den = jnp.sum(p, axis=1, keepdims=True)
            p = p / jnp.where(den > 0.0, den, 1.0)
            part_ref[s, pr] = jnp.dot(p.astype(BF16), vc_ref[s, :, pair_cols(pr)], preferred_element_type=F32)
            for half in range(2):
                acc = p[half * hpg * tq:(half * hpg + 1) * tq]
                for h in range(1, hpg):
                    acc = acc + p[(half * hpg + h) * tq:(half * hpg + h + 1) * tq]
                psums.append(acc)

    ng_rows = seqs * C_KV_HEADS * tq
    sel_rows = _cdiv(ng_rows, LANES) * LANES
    psum = jnp.concatenate(psums + [jnp.zeros((sel_rows - ng_rows, ncp), F32)] * (sel_rows > ng_rows), 0)
    t_lane = pos0 + lax.broadcasted_iota(I32, (1, sel_rows), 1) % tq
    selm = _select_blocks(psum, c2st_ref[...], t_lane, nblk)[:ng_rows]
    tg = pos0 + lax.broadcasted_iota(I32, (ng_rows, 1), 0) % tq
    ex = jnp.dot(selm.astype(BF16), e_ref[...], preferred_element_type=F32)
    ok = (ex > 0.5) & (lax.broadcasted_iota(I32, (ng_rows, ls), 1) <= tg)
    bias_ref[...] = jnp.where(ok, 0.0, MASKED)

    wpos = jnp.concatenate([pos0 - nbuf + lax.broadcasted_iota(I32, (rows, nbuf), 1),
                            pos0 + lax.broadcasted_iota(I32, (rows, LANES), 1)], 1)
    wbias = jnp.where((wpos >= 0) & (wpos <= t_row) & (t_row - wpos < WINDOW), 0.0, MASKED)

    for s in range(seqs):
        for pr in range(npair):
            qp = qs_ref[s, pr * rows:(pr + 1) * rows, :]
            feat = slice(pr * LANES, (pr + 1) * LANES)
            gbase = (s * C_KV_HEADS + 2 * pr) * tq
            bias = jnp.concatenate([jnp.tile(bias_ref[gbase:gbase + tq, :], (hpg, 1)),
                                    jnp.tile(bias_ref[gbase + tq:gbase + 2 * tq, :], (hpg, 1))], 0)
            sc_s = jnp.concatenate(
                [jnp.dot(qp, kst[s, feat, :], preferred_element_type=F32),
                 lax.dot_general(qp, ksn_ref[s, :, pair_cols(pr)], NT_DIMS, preferred_element_type=F32)], 1)
            p, l = _softmax_rows(sc_s + bias)
            pb = p.astype(BF16)
            o_s = (lax.dot_general(pb[:, :past], vst[s, feat, :], NT_DIMS, preferred_element_type=F32)
                   + jnp.dot(pb[:, past:], vsn_ref[s, :, pair_cols(pr)], preferred_element_type=F32)) / l
            sc_w = jnp.concatenate(
                [jnp.dot(qp, wink_ref[s, feat, :].astype(BF16), preferred_element_type=F32),
                 lax.dot_general(qp, kwn_ref[s, :, pair_cols(pr)], NT_DIMS, preferred_element_type=F32)], 1)
            p, l = _softmax_rows(sc_w + wbias)
            pb = p.astype(BF16)
            o_w = (lax.dot_general(pb[:, :nbuf], winv_ref[s, feat, :].astype(BF16), NT_DIMS,
                                   preferred_element_type=F32)
                   + jnp.dot(pb[:, nbuf:], vwn_ref[s, :, pair_cols(pr)], preferred_element_type=F32)) / l

            def gate(branch):
                cols = [3 * (hpg * (2 * pr + h // hpg) + h % hpg) + branch for h in range(nh)]
                return jnp.concatenate([g_ref[s, :, c:c + 1] for c in cols], 0)
            o = gate(0) * part_ref[s, pr] + gate(1) * o_s + gate(2) * o_w
            for j in range(hpg):
                slot = pr * hpg + j
                o_ref[s, :, slot * LANES:(slot + 1) * LANES] = jnp.where(
                    lo_half, o[j * tq:(j + 1) * tq], o[(hpg + j) * tq:(hpg + j + 1) * tq]).astype(BF16)


def _nsa_sample(qs, gates, kcmp, vcmp, ksn, vsn, kwn, vwn, wink_t, winv_t, c2s, emat, kst_cache, vst_cache,
                page_table, layer, *, seqs, pos0, n_sblk):
    nb, n_pages = page_table.shape
    tq = SUBLANES
    width = C_KV_HEADS * HEAD_DIM
    past = n_pages * PAGE_SIZE
    seq_spec = lambda a: pl.BlockSpec((seqs,) + a.shape[1:], lambda i, pt: (i, 0, 0))
    win_spec = pl.BlockSpec((None, seqs) + wink_t.shape[2:], lambda i, pt: (layer, i, 0, 0))
    const2 = lambda i, pt: (0, 0)
    in_specs = [seq_spec(a) for a in (qs, gates, kcmp, vcmp, ksn, vsn, kwn, vwn)]
    in_specs += [win_spec, win_spec, pl.BlockSpec(c2s.shape, const2), pl.BlockSpec(emat.shape, const2)]
    in_specs += _page_specs(kst_cache.shape[2:], layer, seqs, n_pages)
    in_specs += _page_specs(vst_cache.shape[2:], layer, seqs, n_pages)
    np_all = seqs * n_pages
    rows = 2 * (C_HEADS // C_KV_HEADS) * tq
    grid_spec = pltpu.PrefetchScalarGridSpec(
        num_scalar_prefetch=1, grid=(nb // seqs,), in_specs=in_specs,
        out_specs=pl.BlockSpec((seqs, tq, C_WIDTH), lambda i, pt: (i, 0, 0)),
        scratch_shapes=[pltpu.VMEM((seqs, width, past), BF16), pltpu.VMEM((seqs, width, past), BF16),
                        pltpu.VMEM((seqs, C_KV_HEADS // 2, rows, LANES), F32),
                        pltpu.VMEM((seqs * C_KV_HEADS * tq, past + LANES), F32)])
    kern = functools.partial(_nsa_sample_kernel, seqs=seqs, n_pages=n_pages, pos0=pos0,
                             nblk=_cdiv(n_sblk, SUBLANES) * SUBLANES)
    return pl.pallas_call(
        kern, grid_spec=grid_spec, out_shape=jax.ShapeDtypeStruct((nb, tq, C_WIDTH), BF16),
        compiler_params=_params("arbitrary"))(
            page_table, qs, gates, kcmp, vcmp, ksn, vsn, kwn, vwn, wink_t, winv_t, c2s, emat,
            *([kst_cache] * np_all), *([vst_cache] * np_all))


def _proj_c_kernel(x_ref, w_ref, c_ref, s_ref, q_ref, kc_ref, ks_ref, kw_ref, vc_ref, vs_ref, vw_ref,
                   g_ref, kvb_ref, *, tm):
    y = jnp.dot(x_ref[...].astype(BF16), w_ref[...], preferred_element_type=F32)
    lane = lax.broadcasted_iota(I32, (tm, LANES), 1)
    first_half = (lane & (HEAD_DIM - 1)) < HEAD_DIM // 2
    c = c_ref[...]
    s = s_ref[...]
    ro = [_rope128(y[:, j * LANES:(j + 1) * LANES], c, s, first_half) for j in range(C_ROPED // LANES)]
    for j in range(8):
        q_ref[:, j * LANES:(j + 1) * LANES] = (ro[j] * Q_SCALE).astype(BF16)
    for n, ref in enumerate((kc_ref, ks_ref, kw_ref)):
        for j in range(2):
            ref[:, j * LANES:(j + 1) * LANES] = ro[8 + 2 * n + j]
    for n, ref in enumerate((vc_ref, vs_ref, vw_ref)):
        ref[...] = y[:, C_ROPED + 256 * n:C_ROPED + 256 * (n + 1)]
    g = y[:, C_ROPED + 768:C_ROPED + 768 + LANES]
    g_ref[...] = 1.0 / (1.0 + jnp.exp(-g))
    for j in range(2):
        kvb_ref[:, j * LANES:(j + 1) * LANES] = ro[10 + j].astype(BF16)
        kvb_ref[:, 512 + j * LANES:512 + (j + 1) * LANES] = ro[12 + j].astype(BF16)
    kvb_ref[:, 256:512] = y[:, C_ROPED + 256:C_ROPED + 512].astype(BF16)
    kvb_ref[:, 768:1024] = y[:, C_ROPED + 512:C_ROPED + 768].astype(BF16)


def _proj_c(x2d, w, tabs, *, tm):
    m = x2d.shape[0]
    nt = tabs[0].shape[0] // tm
    row = lambda i: (i, 0)
    kvw = C_KV_HEADS * HEAD_DIM
    out_shape = [jax.ShapeDtypeStruct((m, C_WIDTH), BF16)]
    out_shape += [jax.ShapeDtypeStruct((m, kvw), F32)] * 6
    out_shape += [jax.ShapeDtypeStruct((m, LANES), F32), jax.ShapeDtypeStruct((m, 4 * kvw), BF16)]
    out_specs = [pl.BlockSpec((tm, C_WIDTH), row)] + [pl.BlockSpec((tm, kvw), row)] * 6
    out_specs += [pl.BlockSpec((tm, LANES), row), pl.BlockSpec((tm, 4 * kvw), row)]
    return pl.pallas_call(
        functools.partial(_proj_c_kernel, tm=tm), grid=(m // tm,),
        in_specs=[pl.BlockSpec((tm, D_MODEL), row), pl.BlockSpec((D_MODEL, C_COLS), lambda i: (0, 0)),
                  pl.BlockSpec((tm, LANES), lambda i: (i % nt, 0)), pl.BlockSpec((tm, LANES), lambda i: (i % nt, 0))],
        out_specs=out_specs, out_shape=out_shape,
        compiler_params=_params("parallel"))(x2d, w, *tabs)


def _compress_kernel(z_ref, pea_ref, peb_ref, wa_ref, wb_ref, w2_ref, o_ref, *, rows):
    z = z_ref[...].reshape(rows, z_ref.shape[-1])
    ra = jnp.dot((z + pea_ref[...]).astype(BF16), wa_ref[...], preferred_element_type=F32)
    rb = jnp.dot((z + peb_ref[...]).astype(BF16), wb_ref[...], preferred_element_type=F32)
    hid = ra + pltpu.roll(rb, rows - 1, 0)
    hid = 0.5 * hid * (1.0 + jnp.tanh(np.sqrt(2.0 / np.pi) * (hid + 0.044715 * hid * hid * hid)))
    out = jnp.dot(hid.astype(BF16), w2_ref[...], preferred_element_type=F32)
    o_ref[...] = out.reshape(o_ref.shape).astype(BF16)


def _compress(z, pea, peb, wa, wb, w2, *, nb_step):
    nb, r, zw = z.shape
    kvw = C_KV_HEADS * HEAD_DIM
    const = lambda b: (0, 0)
    return pl.pallas_call(
        functools.partial(_compress_kernel, rows=nb_step * r), grid=(nb // nb_step,),
        in_specs=[pl.BlockSpec((nb_step, r, zw), lambda b: (b, 0, 0)),
                  pl.BlockSpec((1, zw), const), pl.BlockSpec((1, zw), const),
                  pl.BlockSpec((zw, kvw), const), pl.BlockSpec((zw, kvw), const), pl.BlockSpec((kvw, kvw), const)],
        out_specs=pl.BlockSpec((nb_step, r, kvw), lambda b: (b, 0, 0)),
        out_shape=jax.ShapeDtypeStruct((nb, r, kvw), BF16),
        compiler_params=_params("parallel"))(z, pea, peb, wa, wb, w2)


def _select_blocks(psum, c2st, t_lane, nblk):
    rows = psum.shape[0]
    p_hi = psum.astype(BF16)
    p_lo = (psum - p_hi.astype(F32)).astype(BF16)
    imp = (lax.dot_general(c2st, p_hi, NT_DIMS, preferred_element_type=F32)
           + lax.dot_general(c2st, p_lo, NT_DIMS, preferred_element_type=F32))[:nblk]
    blk = lax.broadcasted_iota(I32, (nblk, rows), 0)
    blk_f = blk.astype(F32)
    cur = t_lane // SLC_BLOCK
    forced = (blk == 0) | (blk == cur) | (blk == cur - 1)
    imp = jnp.where(forced, jnp.inf, imp)
    imp = jnp.where(blk * SLC_BLOCK <= t_lane, imp, -jnp.inf)

    def top_body(_, carry):
        val, sel = carry
        mx = jnp.max(val, axis=0, keepdims=True)
        first = jnp.min(jnp.where(val == mx, blk_f, float(LANES)), axis=0, keepdims=True)
        pick = blk_f == first
        sel = jnp.where(pick & (mx > -jnp.inf), 1.0, sel)
        return jnp.where(pick, -jnp.inf, val), sel
    _, sel = lax.fori_loop(0, SLC_TOPN, top_body, (imp, jnp.zeros((nblk, rows), F32)))
    sel = jnp.concatenate([sel, jnp.zeros((LANES - nblk, rows), F32)], 0)
    return jnp.concatenate([sel[:, j * LANES:(j + 1) * LANES].T for j in range(rows // LANES)], 0)


def _nsa_kernel(q_ref, g_ref, kc_ref, vc_ref, kvs_ref, kvw_ref, c2st_ref, e_ref, o_ref,
                bias_ref, part_ref, acc_ref, m_ref, l_ref,
                *, tq, ncp, ac, nblk):
    i = pl.program_id(1)
    t = i * tq + lax.broadcasted_iota(I32, (tq, 1), 0)
    lane = lax.broadcasted_iota(I32, (tq, LANES), 1)
    lo_half = lane < HEAD_DIM
    n_s = _cdiv((i + 1) * tq, ac)
    span = WINDOW + tq
    w_start = pl.multiple_of(jnp.maximum(i * tq - WINDOW, 0), tq)
    kvw = C_KV_HEADS * HEAD_DIM
    ks_ref = kvs_ref
    vs_ref = kvs_ref.at[:, kvw:2 * kvw]
    kw_ref = kvw_ref
    vw_ref = kvw_ref.at[:, kvw:2 * kvw]
    hi_half = jnp.logical_not(lo_half)
    cmp_visible = lax.broadcasted_iota(I32, (tq, ncp), 1) * CMP_STRIDE + (CMP_BLOCK - 1) <= t
    ng = C_KV_HEADS
    hpg = C_HEADS // C_KV_HEADS
    nh = 2 * hpg
    head_rows = lambda h: slice(h * tq, (h + 1) * tq)
    tg = jnp.concatenate([t] * ng, 0)
    kiota_sg = lax.broadcasted_iota(I32, (ng * tq, ac), 1)

    def gate(pr, h, branch):
        head = hpg * (2 * pr + h // hpg) + h % hpg
        return g_ref[:, 3 * head + branch:3 * head + branch + 1]

    wpos = w_start + lax.broadcasted_iota(I32, (tq, span), 1)
    wbias = jnp.tile(jnp.where((wpos <= t) & (t - wpos < WINDOW), 0.0, MASKED), (nh, 1))

    def stacked_q(pr):
        return jnp.concatenate(
            [_half_masked(q_ref[:, (pr * hpg + h % hpg) * LANES:(pr * hpg + h % hpg + 1) * LANES],
                          lo_half if h < hpg else hi_half) for h in range(nh)], 0)

    psums = []
    for pr in range(ng // 2):
        col = pr * LANES
        s_c = lax.dot_general(stacked_q(pr), kc_ref[:, col:col + LANES], NT_DIMS, preferred_element_type=F32)
        pair_sums = [jnp.zeros((tq, ncp), F32), jnp.zeros((tq, ncp), F32)]
        ps = []
        for h in range(nh):
            sh = jnp.where(cmp_visible, s_c[head_rows(h)], -jnp.inf)
            m = jnp.max(sh, axis=1, keepdims=True)
            m = jnp.where(m > -jnp.inf, m, 0.0)
            p = jnp.exp2(sh - m)
            den = jnp.sum(p, axis=1, keepdims=True)
            p = p / jnp.where(den > 0.0, den, 1.0)
            pair_sums[h // hpg] = pair_sums[h // hpg] + p
            ps.append(p.astype(BF16))
        part_ref[pr * nh * tq:(pr + 1) * nh * tq] = jnp.dot(jnp.concatenate(ps, 0), vc_ref[:, col:col + LANES],
                                                           preferred_element_type=F32)
        psums += pair_sums

    t_lane = i * tq + lax.broadcasted_iota(I32, (1, ng * tq), 1) % tq
    selb = _select_blocks(jnp.concatenate(psums, 0), c2st_ref[...], t_lane, nblk).astype(BF16)

    def bias_body(c, carry):
        ex = jnp.dot(selb, e_ref[c], preferred_element_type=F32)
        ok = (ex > 0.5) & (c * ac + kiota_sg <= tg)
        bias_ref[c] = jnp.where(ok, 0.0, MASKED)
        return carry
    lax.fori_loop(0, n_s, bias_body, 0)

    for pr in range(ng // 2):
        col = pr * LANES
        qs = stacked_q(pr)
        part = part_ref.at[pr * nh * tq:(pr + 1) * nh * tq]

        def selected_bias(c, pr=pr):
            return lambda h: bias_ref[c, (2 * pr + h // hpg) * tq:(2 * pr + h // hpg + 1) * tq, :]
        _flash(qs, nh, ks_ref, vs_ref, col, 0, n_s, ac, selected_bias, acc_ref, m_ref, l_ref)
        for h in range(nh):
            r = head_rows(h)
            part[r] = gate(pr, h, 0) * part[r] + gate(pr, h, 1) * acc_ref[r]
        s_w = lax.dot_general(qs, kw_ref[pl.ds(w_start, span), col:col + LANES], NT_DIMS,
                              preferred_element_type=F32)
        p_w, l_w = _softmax_rows(s_w + wbias)
        o_w = jnp.dot(p_w.astype(BF16), vw_ref[pl.ds(w_start, span), col:col + LANES],
                      preferred_element_type=F32) / l_w
        for j in range(hpg):
            slot = pr * hpg + j
            lo, hi = head_rows(j), head_rows(hpg + j)
            o_lo = part[lo] + gate(pr, j, 2) * o_w[lo]
            o_hi = part[hi] + gate(pr, hpg + j, 2) * o_w[hi]
            o_ref[:, slot * LANES:(slot + 1) * LANES] = jnp.where(lo_half, o_lo, o_hi).astype(BF16)


def _nsa_attention(q, gates, kcmp, vcmp, kvs, kvw, c2st, emat, *, tq, ac, n_sblk, s_blk=0, w_blk=0):
    nb, tlen, _ = q.shape
    ncp = kcmp.shape[1]
    ls, lw = kvs.shape[1], kvw.shape[1]
    assert WINDOW % tq == 0 and lw >= WINDOW + tq and (C_KV_HEADS * tq) % LANES == 0
    kvwid = C_KV_HEADS * HEAD_DIM
    qspec = lambda w: pl.BlockSpec((None, tq, w), lambda b, i: (b, i, 0))
    kspec = lambda n, w, blk=0: pl.BlockSpec((None, n, w), lambda b, i: (b, 0, blk))
    kern = functools.partial(_nsa_kernel, tq=tq, ncp=ncp, ac=ac, nblk=_cdiv(n_sblk, SUBLANES) * SUBLANES)
    return pl.pallas_call(
        kern, grid=(nb, tlen // tq),
        in_specs=[qspec(C_WIDTH), qspec(LANES), kspec(ncp, kvwid), kspec(ncp, kvwid),
                  kspec(ls, 2 * kvwid, s_blk), kspec(lw, 2 * kvwid, w_blk),
                  pl.BlockSpec((LANES, ncp), lambda b, i: (0, 0)),
                  pl.BlockSpec((ls // ac, LANES, ac), lambda b, i: (0, 0, 0))],
        out_specs=qspec(C_WIDTH),
        out_shape=jax.ShapeDtypeStruct((nb, tlen, C_WIDTH), BF16),
        scratch_shapes=[pltpu.VMEM((ls // ac, C_KV_HEADS * tq, ac), F32), pltpu.VMEM((C_HEADS * tq, LANES), F32)]
        + [pltpu.VMEM((2 * (C_HEADS // C_KV_HEADS) * tq, LANES), F32)] * 3,
        compiler_params=_params("parallel", "arbitrary"))(q, gates, kcmp, vcmp, kvs, kvw, c2st, emat)


def _outproj_ln_kernel(x_ref, a_ref, b_ref, wa_ref, wb_ref, g_ref, bt_ref, o_ref):
    y = ALPHA * x_ref[...]
    y = y + jnp.dot(a_ref[...], wa_ref[...], preferred_element_type=F32)
    y = y + jnp.dot(b_ref[...], wb_ref[...], preferred_element_type=F32)
    o_ref[...] = _layernorm(y, g_ref[...], bt_ref[...])


def _outproj_ln(x2d, a, b, a_blk, b_blk, w_out, g, bt, *, tm):
    m = x2d.shape[0]
    half = w_out.shape[0] // 2
    row = lambda i: (i, 0)
    const = lambda i: (0, 0)
    return pl.pallas_call(
        _outproj_ln_kernel, grid=(m // tm,),
        in_specs=[pl.BlockSpec((tm, D_MODEL), row),
                  pl.BlockSpec((tm, half), lambda i: (i, a_blk)), pl.BlockSpec((tm, half), lambda i: (i, b_blk)),
                  pl.BlockSpec((half, D_MODEL), lambda i: (0, 0)), pl.BlockSpec((half, D_MODEL), lambda i: (1, 0)),
                  pl.BlockSpec((1, D_MODEL), const), pl.BlockSpec((1, D_MODEL), const)],
        out_specs=pl.BlockSpec((tm, D_MODEL), row),
        out_shape=jax.ShapeDtypeStruct((m, D_MODEL), F32),
        compiler_params=_params("parallel"))(x2d, a, b, w_out, w_out, g, bt)


def _ffn_ln_kernel(x_ref, wg_ref, wu_ref, wd_ref, g_ref, bt_ref, o_ref, xb_ref, acc_ref):
    f = pl.program_id(1)

    @pl.when(f == 0)
    def _():
        xb_ref[...] = x_ref[...].astype(BF16)
        acc_ref[...] = jnp.zeros(acc_ref.shape, F32)

    xb = xb_ref[...]
    h = jnp.dot(xb, wg_ref[...], preferred_element_type=F32)
    u = jnp.dot(xb, wu_ref[...], preferred_element_type=F32)
    a = (h / (1.0 + jnp.exp(-h))) * u
    acc_ref[...] += jnp.dot(a.astype(BF16), wd_ref[...], preferred_element_type=F32)

    @pl.when(f == pl.num_programs(1) - 1)
    def _():
        o_ref[...] = _layernorm(ALPHA * x_ref[...] + acc_ref[...], g_ref[...], bt_ref[...])


def _ffn_ln(x2d, wg, wu, wd, g, bt, *, tm, tf):
    m = x2d.shape[0]
    return pl.pallas_call(
        _ffn_ln_kernel, grid=(m // tm, D_FF // tf),
        in_specs=[pl.BlockSpec((tm, D_MODEL), lambda i, f: (i, 0)),
                  pl.BlockSpec((D_MODEL, tf), lambda i, f: (0, f)), pl.BlockSpec((D_MODEL, tf), lambda i, f: (0, f)),
                  pl.BlockSpec((tf, D_MODEL), lambda i, f: (f, 0)),
                  pl.BlockSpec((1, D_MODEL), lambda i, f: (0, 0)), pl.BlockSpec((1, D_MODEL), lambda i, f: (0, 0))],
        out_specs=pl.BlockSpec((tm, D_MODEL), lambda i, f: (i, 0)),
        out_shape=jax.ShapeDtypeStruct((m, D_MODEL), F32),
        scratch_shapes=[pltpu.VMEM((tm, D_MODEL), BF16), pltpu.VMEM((tm, D_MODEL), F32)],
        compiler_params=_params("parallel", "arbitrary"))(x2d, wg, wu, wd, g, bt)


def _rope_tables(pos):
    half = HEAD_DIM // 2
    inv = ROPE_THETA ** (-jnp.arange(half, dtype=F32) / half)
    ang = pos.astype(F32)[:, None] * inv[None, :]
    cos, sin = jnp.cos(ang), jnp.sin(ang)
    c64 = jnp.concatenate([cos, cos], 1)
    s64 = jnp.concatenate([-sin, sin], 1)
    c = jnp.concatenate([c64, c64], 1)
    s = jnp.concatenate([s64, s64], 1)
    cx = jnp.concatenate([c64, jnp.full_like(c64, IDX_HEADS ** -0.5)], 1)
    sx = jnp.concatenate([s64, jnp.zeros_like(s64)], 1)
    return c, s, cx, sx


def _perm_heads(w, perm):
    lead = w.shape[:-1]
    return w.reshape(*lead, len(perm), HEAD_DIM)[..., np.asarray(perm), :].reshape(*lead, len(perm) * HEAD_DIM)


def _ab_w_in(w):
    ab_sizes = (A_WIDTH, A_KV_HEADS * HEAD_DIM, A_KV_HEADS * HEAD_DIM, IDX_HEADS * IDX_DIM, IDX_DIM, IDX_HEADS,
                B_WIDTH, B_WIDTH, B_WIDTH)
    q, k, v, iq, ik, iw, gb, gc, h = jnp.split(w, np.cumsum(ab_sizes)[:-1].tolist(), axis=-1)
    pad = jnp.zeros((w.shape[0], LANES - IDX_DIM - IDX_HEADS), w.dtype)
    return jnp.concatenate([_perm_heads(q, A_PERM), k, iq, ik, iw, pad, v, gb, gc, h], -1).astype(BF16)


def _c_w_in(w):
    kvw = C_KV_HEADS * HEAD_DIM
    c_sizes = (C_WIDTH,) + (kvw,) * 6 + (3 * C_HEADS,)
    q, kc, vc, ks, vs, kw, vw, g = jnp.split(w, np.cumsum(c_sizes)[:-1].tolist(), axis=-1)
    pad = jnp.zeros((w.shape[0], LANES - 3 * C_HEADS), w.dtype)
    return jnp.concatenate([_perm_heads(q, C_PERM), kc, ks, kw, vc, vs, vw, g, pad], -1).astype(BF16)


def _perm_rows(w_out, perm):
    return w_out.reshape(len(perm), HEAD_DIM, w_out.shape[-1])[np.asarray(perm)].reshape(-1, w_out.shape[-1])


def _block_diag(w, n):
    eye = jnp.eye(n, dtype=w.dtype)
    out = jnp.einsum('gh,...ab->...gahb', eye, w)
    return out.reshape(*w.shape[:-2], n * w.shape[-2], n * w.shape[-1])


def _compress_weights(pe, w1, w2):
    g = C_KV_HEADS
    half = CMP_BLOCK // 2
    bd = _block_diag(w1, g)
    wa = bd[:half].reshape(half * g * HEAD_DIM, g * HEAD_DIM).astype(BF16)
    wb = bd[half:].reshape(half * g * HEAD_DIM, g * HEAD_DIM).astype(BF16)
    pet = jnp.tile(pe, (1, g))
    pea = pet[:half].reshape(1, -1)
    peb = pet[half:].reshape(1, -1)
    return pea, peb, wa, wb, _block_diag(w2, g).astype(BF16)


def _slc_from_cmp(ncp, n_cmp, n_sblk):
    n = np.arange(ncp)[None, :]
    mblk = np.arange(LANES)[:, None]
    hit = ((n * CMP_STRIDE < mblk * SLC_BLOCK + SLC_BLOCK) & (n * CMP_STRIDE + CMP_BLOCK > mblk * SLC_BLOCK)
           & (n < n_cmp) & (mblk < n_sblk))
    return jnp.asarray(hit, BF16)


def _expand_matrix(ls, ac):
    k = np.arange(ls)[None, :]
    mblk = np.arange(LANES)[:, None]
    e = (k // SLC_BLOCK == mblk).astype(np.float32)
    return jnp.asarray(e.reshape(LANES, ls // ac, ac).transpose(1, 0, 2), BF16)


def _gather_pages(pool_layer, page_table):
    g = pool_layer[page_table]
    return g.reshape(g.shape[0], g.shape[1] * g.shape[2], -1)


def _pad_rows(a, n):
    return jnp.pad(a, ((0, 0), (0, n - a.shape[1]), (0, 0)))


def _stack_heads(q3, slots):
    nb, tq, _ = q3.shape
    qt = q3.reshape(nb, tq, slots, LANES).transpose(0, 2, 1, 3)
    lo = jnp.arange(LANES) < HEAD_DIM
    zero = jnp.zeros((), q3.dtype)
    stacked = jnp.concatenate([jnp.where(lo, qt, zero), jnp.where(lo, zero, qt)], 1)
    return stacked.reshape(nb, 2 * slots * tq, LANES)


def _token_minor(cache):
    nd = cache.ndim
    t = jnp.transpose(cache, (0, 1) + tuple(range(3, nd)) + (2,))
    return t.reshape(t.shape[0], t.shape[1], -1, t.shape[-1])


PROMPT_TM = 512
PROMPT_AC = 512
SAMPLE_TQ = 8
FFN_TF = 1408
DSA_SAMPLE_SEQS = 4
NSA_SAMPLE_SEQS = 2
CMP_SAMPLE_SEQS = 4


def _ab_layer(x2d, nb, tlen, start, past, page_table, w_in, conv_w, tabs, *, sample):
    m = nb * tlen
    conv_w8 = jnp.pad(conv_w, ((0, SUBLANES - CONV_W), (0, 0)))
    if sample:
        kt_cache, vt_cache, ikt_cache, prev, layer = past
        tt = jnp.arange(tlen)
        prevs = []
        for d in range(1, CONV_W):
            idx = jnp.clip(CONV_W - 1 + tt - d, 0, CONV_W - 2)
            prevs.append(prev[:, idx].reshape(m, B_WIDTH))
        outs = _proj_ab(x2d, w_in, tabs, conv_w8, tm=m, seq_tiles=1, sample=True, prevs=prevs, dec_seq=tlen)
    else:
        outs = _proj_ab(x2d, w_in, tabs, conv_w8, tm=PROMPT_TM, seq_tiles=tlen // PROMPT_TM, sample=False)
    q, iq, k, v, ikw, kvb, ikb, b_out, uo = outs
    if sample:
        klen_real = page_table.shape[1] * PAGE_SIZE + tlen
        pad_q = lambda a: _pad_rows(a.reshape(nb, tlen, -1), SAMPLE_TQ)
        pad_k = lambda a: _pad_rows(a.reshape(nb, tlen, -1), LANES)
        kvb3 = kvb.reshape(nb, tlen, 2 * LANES)
        iq_heads = pad_q(iq).reshape(nb, SAMPLE_TQ, IDX_HEADS, IDX_DIM).transpose(0, 2, 1, 3)
        a_out = _dsa_sample(_stack_heads(pad_q(q), A_HEADS // 2), iq_heads.reshape(nb, IDX_HEADS * SAMPLE_TQ, IDX_DIM),
                            pad_q(ikw), pad_k(kvb3[..., :LANES]), pad_k(kvb3[..., LANES:]),
                            pad_k(ikb.reshape(nb, tlen, LANES)[..., :IDX_DIM]),
                            kt_cache, vt_cache, ikt_cache, page_table, layer,
                            seqs=DSA_SAMPLE_SEQS, pos0=start, topk=min(A_TOPK_MAX, klen_real // 4))
        a_out = a_out[:, :tlen].reshape(m, A_WIDTH)
        new_conv = uo.reshape(nb, tlen, B_WIDTH)[:, tlen - (CONV_W - 1):]
    else:
        r3 = lambda a: a.reshape(nb, tlen, -1)
        a_out = _dsa_attention(r3(q), r3(iq), r3(ikw), r3(ikb), r3(kvb), tq=Q_BLOCK,
                               topk=min(A_TOPK_MAX, tlen // 4), sc=512, ac=PROMPT_AC)
        a_out = a_out.reshape(m, A_WIDTH)
        new_conv = uo.reshape(nb, tlen // PROMPT_TM, SUBLANES, B_WIDTH)[:, -1, SUBLANES - (CONV_W - 1):]
    state = (k.reshape(nb, tlen, A_KV_HEADS, HEAD_DIM), v.reshape(nb, tlen, A_KV_HEADS, HEAD_DIM),
             ikw[:, :IDX_DIM].reshape(nb, tlen, IDX_DIM), new_conv)
    return a_out, b_out, state


def _c_layer(x2d, nb, tlen, start, past, w_in, cw_k, cw_v, tabs, *, sample):
    m = nb * tlen
    kvw = C_KV_HEADS * HEAD_DIM
    tm = m if sample else PROMPT_TM
    q, kc, ks, kw, vc, vs, vw, gates, kvb = _proj_c(x2d, w_in, tabs[:2], tm=tm)
    if sample:
        kct, vct, kst, vst, wink_t, winv_t, buf_k, buf_v, page_table, layer = past
        past_len = page_table.shape[1] * PAGE_SIZE
        n_cmp = (past_len + tlen - CMP_BLOCK) // CMP_STRIDE + 1
        assert n_cmp <= past_len // CMP_STRIDE, "compression blocks must lie inside the cached rows"
        assert tlen <= LANES and start == past_len

        def tap_major(cw):
            pea, peb, wa, wb, w2 = cw
            pet = jnp.concatenate([pea.reshape(-1, kvw), peb.reshape(-1, kvw)], 0)
            wab = jnp.concatenate([wa.reshape(-1, kvw, kvw), wb.reshape(-1, kvw, kvw)], 0)
            return pet, wab, w2
        kcmp = _compress_sample(kct, page_table, layer, *tap_major(cw_k), seqs=CMP_SAMPLE_SEQS)
        vcmp = _compress_sample(vct, page_table, layer, *tap_major(cw_v), seqs=CMP_SAMPLE_SEQS)
        ncp = kcmp.shape[1]
        ls = past_len + LANES
        n_sblk = _cdiv(past_len + tlen, SLC_BLOCK)
        kvb3 = kvb.reshape(nb, tlen, 4 * kvw)
        new_rows = [_pad_rows(kvb3[..., n * kvw:(n + 1) * kvw], LANES) for n in range(4)]
        q3 = _pad_rows(q.reshape(nb, tlen, C_WIDTH), SAMPLE_TQ)
        half_w = C_WIDTH // 2
        qs = jnp.concatenate([_stack_heads(q3[..., :half_w], C_HEADS // 4),
                              _stack_heads(q3[..., half_w:], C_HEADS // 4)], 1)
        out = _nsa_sample(qs, _pad_rows(gates.reshape(nb, tlen, LANES), SAMPLE_TQ), kcmp, vcmp, *new_rows,
                          wink_t, winv_t, _slc_from_cmp(ncp, n_cmp, n_sblk), _expand_matrix(ls, ls)[0],
                          kst, vst, page_table, layer, seqs=NSA_SAMPLE_SEQS, pos0=start, n_sblk=n_sblk)
        out = out[:, :tlen].reshape(m, C_WIDTH)
        r4 = lambda a: a.reshape(nb, tlen, C_KV_HEADS, HEAD_DIM)
        win_k = jnp.concatenate([buf_k, r4(kw)], 1)[:, tlen:]
        win_v = jnp.concatenate([buf_v, r4(vw)], 1)[:, tlen:]
    else:
        zk = kc.reshape(nb, tlen // CMP_STRIDE, CMP_STRIDE * kvw)
        zv = vc.reshape(nb, tlen // CMP_STRIDE, CMP_STRIDE * kvw)
        n_cmp = (tlen - CMP_BLOCK) // CMP_STRIDE + 1
        kcmp = _compress(zk, *cw_k, nb_step=1)
        vcmp = _compress(zv, *cw_v, nb_step=1)
        ncp = kcmp.shape[1]
        n_sblk = _cdiv(tlen, SLC_BLOCK)
        kvb3 = kvb.reshape(nb, tlen, 4 * kvw)
        out = _nsa_attention(q.reshape(nb, tlen, C_WIDTH), gates.reshape(nb, tlen, LANES), kcmp, vcmp, kvb3, kvb3,
                             _slc_from_cmp(ncp, n_cmp, n_sblk), _expand_matrix(tlen, PROMPT_AC),
                             tq=Q_BLOCK, ac=PROMPT_AC, n_sblk=n_sblk, s_blk=0, w_blk=1)
        out = out.reshape(m, C_WIDTH)
        r4 = lambda a: a.reshape(nb, tlen, C_KV_HEADS, HEAD_DIM)
        keep = min(WINDOW, tlen)
        win_k = r4(kw)[:, tlen - keep:]
        win_v = r4(vw)[:, tlen - keep:]
    r4 = lambda a: a.reshape(nb, tlen, C_KV_HEADS, HEAD_DIM)
    return out, (r4(kc), r4(vc), r4(ks), r4(vs), win_k, win_v)


def kernel(x_prompt, x_sample, cache_a_k, cache_a_v, cache_a_ik, state_b_conv, cache_c_cmp_k, cache_c_cmp_v, cache_c_slc_k, cache_c_slc_v, state_c_win_k, state_c_win_v, page_table, ab_w_in, ab_conv_w, ab_w_out, c_w_in, c_cmp_pe_k, c_cmp_w1_k, c_cmp_w2_k, c_cmp_pe_v, c_cmp_w1_v, c_cmp_w2_v, c_w_out, ffn_w_gate, ffn_w_up, ffn_w_down, ln1_g, ln1_b, ln2_g, ln2_b):
    bp, tp, _ = x_prompt.shape
    bs, ts, _ = x_sample.shape
    past_len = page_table.shape[1] * PAGE_SIZE
    xp = x_prompt.reshape(bp * tp, D_MODEL)
    xs = x_sample.reshape(bs * ts, D_MODEL)
    tabs_p = _rope_tables(jnp.arange(tp))
    tabs_s = _rope_tables(past_len + jnp.arange(bs * ts) % ts)
    kt_a, vt_a, ikt_a = _token_minor(cache_a_k), _token_minor(cache_a_v), _token_minor(cache_a_ik)
    kct_c, vct_c = _token_minor(cache_c_cmp_k), _token_minor(cache_c_cmp_v)
    kst_c, vst_c = _token_minor(cache_c_slc_k), _token_minor(cache_c_slc_v)
    wink_t, winv_t = _token_minor(state_c_win_k), _token_minor(state_c_win_v)
    ab_p, ab_s, c_p, c_s = [], [], [], []
    for layer in range(DEPTH):
        i = layer // 2
        row1 = lambda a: a[layer].reshape(1, D_MODEL)
        if layer % 2 == 0:
            w_in = _ab_w_in(ab_w_in[i])
            w_out = jnp.concatenate([_perm_rows(ab_w_out[i][:A_WIDTH], A_PERM), ab_w_out[i][A_WIDTH:]], 0).astype(BF16)
            a_p, b_p, st_p = _ab_layer(xp, bp, tp, 0, None, None, w_in, ab_conv_w[i], tabs_p, sample=False)
            past = (kt_a, vt_a, ikt_a, state_b_conv[i], i)
            a_s, b_s, st_s = _ab_layer(xs, bs, ts, past_len, past, page_table, w_in, ab_conv_w[i], tabs_s, sample=True)
            ab_p.append(st_p)
            ab_s.append(st_s)
            xp = _outproj_ln(xp, a_p, b_p, 0, 0, w_out, row1(ln1_g), row1(ln1_b), tm=PROMPT_TM)
            xs = _outproj_ln(xs, a_s, b_s, 0, 0, w_out, row1(ln1_g), row1(ln1_b), tm=bs * ts)
        else:
            w_in = _c_w_in(c_w_in[i])
            w_out = _perm_rows(c_w_out[i], C_PERM).astype(BF16)
            cw_k = _compress_weights(c_cmp_pe_k[i], c_cmp_w1_k[i], c_cmp_w2_k[i])
            cw_v = _compress_weights(c_cmp_pe_v[i], c_cmp_w1_v[i], c_cmp_w2_v[i])
            o_p, st_p = _c_layer(xp, bp, tp, 0, None, w_in, cw_k, cw_v, tabs_p, sample=False)
            past = (kct_c, vct_c, kst_c, vst_c, wink_t, winv_t, state_c_win_k[i], state_c_win_v[i], page_table, i)
            o_s, st_s = _c_layer(xs, bs, ts, past_len, past, w_in, cw_k, cw_v, tabs_s, sample=True)
            c_p.append(st_p)
            c_s.append(st_s)
            xp = _outproj_ln(xp, o_p, o_p, 0, 1, w_out, row1(ln1_g), row1(ln1_b), tm=PROMPT_TM)
            xs = _outproj_ln(xs, o_s, o_s, 0, 1, w_out, row1(ln1_g), row1(ln1_b), tm=bs * ts)
        wg, wu, wd = ffn_w_gate[layer].astype(BF16), ffn_w_up[layer].astype(BF16), ffn_w_down[layer].astype(BF16)
        xp = _ffn_ln(xp, wg, wu, wd, row1(ln2_g), row1(ln2_b), tm=PROMPT_TM, tf=FFN_TF)
        xs = _ffn_ln(xs, wg, wu, wd, row1(ln2_g), row1(ln2_b), tm=bs * ts, tf=FFN_TF)
    stk = lambda lst, j: jnp.stack([e[j] for e in lst], 0)
    return (xp.reshape(bp, tp, D_MODEL), xs.reshape(bs, ts, D_MODEL),
            stk(ab_p, 0), stk(ab_p, 1), stk(ab_p, 2), stk(ab_p, 3),
            stk(c_p, 0), stk(c_p, 1), stk(c_p, 2), stk(c_p, 3), stk(c_p, 4), stk(c_p, 5),
            stk(ab_s, 0), stk(ab_s, 1), stk(ab_s, 2), stk(ab_s, 3),
            stk(c_s, 0), stk(c_s, 1), stk(c_s, 2), stk(c_s, 3), stk(c_s, 4), stk(c_s, 5))
```

```python
import functools

import numpy as np
import jax
import jax.numpy as jnp
from jax import lax
from jax.experimental import pallas as pl
from jax.experimental.pallas import tpu as pltpu

D_MODEL = 1024
DEPTH = 4
PAGE_SIZE = 128
HEAD_DIM = 64
ROPE_THETA = 10000.0
A_HEADS = 8
A_KV_HEADS = 2
IDX_HEADS = 4
IDX_DIM = 64
A_TOPK_MAX = 256
A_WIDTH = A_HEADS * HEAD_DIM
B_WIDTH = D_MODEL // 2
CONV_W = 3
C_HEADS = 16
C_KV_HEADS = 4
C_WIDTH = C_HEADS * HEAD_DIM
CMP_BLOCK = 32
CMP_STRIDE = 16
SLC_BLOCK = 64
SLC_TOPN = 16
WINDOW = 512
D_FF = ((8 * D_MODEL + 3 * 256 - 1) // (3 * 256)) * 256
LN_EPS = 1e-5
ALPHA = (2 * DEPTH) ** 0.25
Q_BLOCK = 128

F32 = jnp.float32
BF16 = jnp.bfloat16
I32 = jnp.int32

LANES = 128
SUBLANES = 8
VMEM_LIMIT_BYTES = 56 * 1024 * 1024
MASKED = -1e30
Q_SCALE = HEAD_DIM ** -0.5 * float(np.log2(np.e))
INT_MIN = -(2 ** 31)
NEG_INF_KEY = int(np.int32(np.uint32(0xFF800000) ^ np.uint32(0x7FFFFFFF)))
NT_DIMS = (((1,), (1,)), ((), ()))

A_PERM = tuple(h for j in range(4) for h in (j, 4 + j))
C_PERM = tuple(h for pr in range(2) for j in range(4) for h in (8 * pr + j, 8 * pr + 4 + j))

AB_COLS = 2688
C_COLS = 2688
C_ROPED = C_WIDTH + 3 * C_KV_HEADS * HEAD_DIM


def _params(*sem):
    return pltpu.CompilerParams(dimension_semantics=sem, vmem_limit_bytes=VMEM_LIMIT_BYTES)


def _cdiv(a, b):
    return (a + b - 1) // b


def _rope128(r, c, s, first_half):
    sw = jnp.where(first_half, pltpu.roll(r, 96, 1), pltpu.roll(r, 32, 1))
    return r * c + sw * s


def _half_masked(q_bf16, mask):
    return jnp.where(mask, q_bf16.astype(F32), 0.0).astype(BF16)


def _flash(qs, heads, k_ref, v_ref, col, c_lo, n_chunks, ac, bias_fn, acc_ref, m_ref, l_ref, splits=1):
    tq = qs.shape[0] // heads
    reps = ac // LANES
    acc_ref[...] = jnp.zeros(acc_ref.shape, F32)
    m_ref[...] = jnp.full(m_ref.shape, MASKED, F32)
    l_ref[...] = jnp.zeros(l_ref.shape, F32)

    hps = heads // splits

    def body(ci, carry):
        c = c_lo + ci
        off = pl.multiple_of(c * ac, ac)
        kc = k_ref[pl.ds(off, ac), col:col + LANES]
        vc = v_ref[pl.ds(off, ac), col:col + LANES]
        head_bias = bias_fn(c)
        for g in range(splits):
            s = lax.dot_general(qs[g * hps * tq:(g + 1) * hps * tq], kc, NT_DIMS, preferred_element_type=F32)
            ps, alphas = [], []
            for hh in range(hps):
                rows = slice((g * hps + hh) * tq, (g * hps + hh + 1) * tq)
                sh = s[hh * tq:(hh + 1) * tq] + head_bias(g * hps + hh)
                m_old = m_ref[rows]
                m_new = jnp.maximum(m_old, jnp.max(sh, axis=1, keepdims=True))
                alpha = jnp.exp2(m_old - m_new)
                p = jnp.exp2(sh - jnp.tile(m_new, (1, reps)))
                l_ref[rows] = alpha * l_ref[rows] + jnp.sum(p, axis=1, keepdims=True)
                m_ref[rows] = m_new
                ps.append(p.astype(BF16))
                alphas.append(alpha)
            pv = jnp.dot(jnp.concatenate(ps, 0), vc, preferred_element_type=F32)
            for hh in range(hps):
                rows = slice((g * hps + hh) * tq, (g * hps + hh + 1) * tq)
                acc_ref[rows] = alphas[hh] * acc_ref[rows] + pv[hh * tq:(hh + 1) * tq]
        return carry

    lax.fori_loop(0, n_chunks, body, 0)
    acc_ref[...] = jnp.where(m_ref[...] > 0.5 * MASKED, acc_ref[...] / l_ref[...], 0.0)


def _layernorm(y, g, b):
    mu = jnp.mean(y, axis=-1, keepdims=True)
    d = y - mu
    var = jnp.mean(d * d, axis=-1, keepdims=True)
    return d * lax.rsqrt(var + LN_EPS) * g + b


def _proj_ab_kernel(*refs, tm, seq_tiles, sample, dec_seq):
    if sample:
        (x_ref, w_ref, c_ref, s_ref, cx_ref, sx_ref, cw_ref, p1_ref, p2_ref,
         q_ref, iq_ref, k_ref, v_ref, ikw_ref, kvb_ref, ikb_ref, bo_ref, uo_ref, ubuf) = refs
        prev_refs = (None, p1_ref, p2_ref)
    else:
        (x_ref, w_ref, c_ref, s_ref, cx_ref, sx_ref, cw_ref,
         q_ref, iq_ref, k_ref, v_ref, ikw_ref, kvb_ref, ikb_ref, bo_ref, uo_ref, ubuf) = refs
    i = pl.program_id(0)
    y = jnp.dot(x_ref[...].astype(BF16), w_ref[...], preferred_element_type=F32)
    lane = lax.broadcasted_iota(I32, (tm, LANES), 1)
    first_half = (lane & (HEAD_DIM - 1)) < HEAD_DIM // 2
    c = c_ref[...]
    s = s_ref[...]
    ro = [_rope128(y[:, j * LANES:(j + 1) * LANES], c, s, first_half) for j in range(7)]
    ro.append(_rope128(y[:, 7 * LANES:8 * LANES], cx_ref[...], sx_ref[...], first_half))
    for j in range(4):
        q_ref[:, j * LANES:(j + 1) * LANES] = (ro[j] * Q_SCALE).astype(BF16)
    k = ro[4]
    v = y[:, 1024:1152]
    k_ref[...] = k
    v_ref[...] = v
    kvb_ref[:, 0:LANES] = k.astype(BF16)
    kvb_ref[:, LANES:2 * LANES] = v.astype(BF16)
    for j in range(2):
        iq_ref[:, j * LANES:(j + 1) * LANES] = (ro[5 + j] * IDX_DIM ** -0.5).astype(BF16)
    ikw = ro[7]
    ikw_ref[...] = ikw
    ikb_ref[...] = jnp.where(lane < IDX_DIM, ikw, pltpu.roll(ikw, IDX_DIM, 1)).astype(BF16)

    gate_b = y[:, 1152:1664]
    u = y[:, 1664:2176] * y[:, 2176:2688]

    @pl.when(i % seq_tiles == 0)
    def _():
        ubuf[0:SUBLANES, :] = jnp.zeros((SUBLANES, B_WIDTH), F32)

    @pl.when(i % seq_tiles != 0)
    def _():
        ubuf[0:SUBLANES, :] = ubuf[tm:tm + SUBLANES, :]

    ubuf[SUBLANES:tm + SUBLANES, :] = u
    cw = cw_ref[...]
    conv = u * cw[CONV_W - 1:CONV_W, :]
    if sample:
        t = lax.broadcasted_iota(I32, (tm, 1), 0) % dec_seq
    for d in range(1, CONV_W):
        ud = ubuf[SUBLANES - d:tm + SUBLANES - d, :]
        if sample:
            ud = jnp.where(t >= d, ud, prev_refs[d][...])
        conv = conv + ud * cw[CONV_W - 1 - d:CONV_W - d, :]
    bo_ref[...] = (gate_b * conv).astype(BF16)
    if sample:
        uo_ref[...] = u
    else:
        uo_ref[...] = u[tm - SUBLANES:tm, :]


def _proj_ab(x2d, w, tabs, conv_w8, *, tm, seq_tiles, sample, prevs=None, dec_seq=1):
    m = x2d.shape[0]
    nt = tabs[0].shape[0] // tm
    row = lambda i: (i, 0)
    const = lambda i: (0, 0)
    tab = lambda i: (i % nt, 0)
    in_specs = [pl.BlockSpec((tm, D_MODEL), row), pl.BlockSpec((D_MODEL, AB_COLS), const)]
    in_specs += [pl.BlockSpec((tm, LANES), tab)] * 4
    in_specs += [pl.BlockSpec((SUBLANES, B_WIDTH), const)]
    args = [x2d, w, *tabs, conv_w8]
    if sample:
        in_specs += [pl.BlockSpec((tm, B_WIDTH), row)] * 2
        args += list(prevs)
    u_rows = tm if sample else SUBLANES
    out_shape = [
        jax.ShapeDtypeStruct((m, A_WIDTH), BF16),
        jax.ShapeDtypeStruct((m, IDX_HEADS * IDX_DIM), BF16),
        jax.ShapeDtypeStruct((m, LANES), F32),
        jax.ShapeDtypeStruct((m, LANES), F32),
        jax.ShapeDtypeStruct((m, LANES), F32),
        jax.ShapeDtypeStruct((m, 2 * LANES), BF16),
        jax.ShapeDtypeStruct((m, LANES), BF16),
        jax.ShapeDtypeStruct((m, B_WIDTH), BF16),
        jax.ShapeDtypeStruct((m // tm * u_rows, B_WIDTH), F32),
    ]
    out_specs = [
        pl.BlockSpec((tm, A_WIDTH), row), pl.BlockSpec((tm, IDX_HEADS * IDX_DIM), row),
        pl.BlockSpec((tm, LANES), row), pl.BlockSpec((tm, LANES), row), pl.BlockSpec((tm, LANES), row),
        pl.BlockSpec((tm, 2 * LANES), row), pl.BlockSpec((tm, LANES), row),
        pl.BlockSpec((tm, B_WIDTH), row), pl.BlockSpec((u_rows, B_WIDTH), row),
    ]
    kern = functools.partial(_proj_ab_kernel, tm=tm, seq_tiles=seq_tiles, sample=sample, dec_seq=dec_seq)
    return pl.pallas_call(
        kern, grid=(m // tm,), in_specs=in_specs, out_specs=out_specs, out_shape=out_shape,
        scratch_shapes=[pltpu.VMEM((tm + SUBLANES, B_WIDTH), F32)],
        compiler_params=_params("arbitrary"))(*args)


def _count(key_ref, n_chunks, sc, nq, keys_on_sublanes, pred):
    if keys_on_sublanes:
        def body(c, acc):
            part = jnp.where(pred(key_ref[c], c), 1.0, 0.0)
            rows = sc
            while rows > SUBLANES:
                rows //= 2
                part = part[:rows] + part[rows:2 * rows]
            return acc + part
        assert sc % SUBLANES == 0 and (sc // SUBLANES) & (sc // SUBLANES - 1) == 0
        acc = lax.fori_loop(0, n_chunks, body, jnp.zeros((SUBLANES, nq), F32))
        return jnp.sum(acc, axis=0, keepdims=True)

    def body(c, acc):
        m = jnp.where(pred(key_ref[c], c), 1.0, 0.0)
        part = m[:, 0:LANES]
        for j in range(1, sc // LANES):
            part = part + m[:, j * LANES:(j + 1) * LANES]
        return acc + part
    acc = lax.fori_loop(0, n_chunks, body, jnp.zeros((nq, LANES), F32))
    return jnp.sum(acc, axis=1, keepdims=True)


def _topk_bias(key_ref, bias_ref, n_sc, sc, ac, nq, topk, klen, keys_on_sublanes=False):
    ratio = sc // ac
    kshape, kaxis, vshape = ((sc, nq), 0, (1, nq)) if keys_on_sublanes else ((nq, sc), 1, (nq, 1))
    kiota = lax.broadcasted_iota(I32, kshape, kaxis)
    count = functools.partial(_count, key_ref, n_sc, sc, nq, keys_on_sublanes)
    kf = jnp.float32(topk)
    n_all = jnp.asarray(n_sc * sc, F32)

    def bit_body(b, carry):
        thr, n_ge = carry
        cand = thr + jnp.left_shift(jnp.int32(1), 31 - b)
        cnt = count(lambda key, c: key >= cand)
        take = cnt >= kf
        return jnp.where(take, cand, thr), jnp.where(take, cnt, n_ge)
    thr, n_ge = lax.fori_loop(0, 32, bit_body, (jnp.full(vshape, INT_MIN, I32), jnp.full(vshape, n_all, F32)))

    def tie_search():
        need = kf - count(lambda key, c: key > thr)

        def tie_body(b, last):
            cand = last + jnp.left_shift(jnp.int32(1), klen.bit_length() - 1 - b)
            cnt = count(lambda key, c: (key == thr) & (c * sc + kiota < cand))
            return jnp.where(cnt < need, cand, last)
        return lax.fori_loop(0, klen.bit_length(), tie_body, jnp.zeros(vshape, I32))

    tied = (n_ge > kf) & (thr > jnp.int32(NEG_INF_KEY))
    last = lax.cond(jnp.max(jnp.where(tied, 1.0, 0.0)) > 0.5, tie_search, lambda: jnp.full(vshape, klen, I32))

    def bias_body(c, carry):
        key = key_ref[c]
        sel = (key > thr) | ((key == thr) & (c * sc + kiota <= last))
        sel = sel & (key > jnp.int32(NEG_INF_KEY))
        bias = jnp.where(sel, 0.0, MASKED)
        if keys_on_sublanes:
            for r in range(sc // LANES):
                blk = bias[r * LANES:(r + 1) * LANES, :].T
                bias_ref[c * ratio + r * LANES // ac, :, (r * LANES) % ac:(r * LANES) % ac + LANES] = blk
        else:
            for r in range(ratio):
                bias_ref[c * ratio + r] = bias[:, r * ac:(r + 1) * ac]
        return carry
    lax.fori_loop(0, n_sc, bias_body, 0)


def _dsa_kernel(q_ref, iq_ref, ikw_ref, ikb_ref, kvb_ref, o_ref, key_ref, bias_ref, acc_ref, m_ref, l_ref,
                *, tq, klen, topk, sc, ac):
    i = pl.program_id(1)
    n_sc = _cdiv((i + 1) * tq, sc)
    ratio = sc // ac
    lane = lax.broadcasted_iota(I32, (tq, LANES), 1)
    lo_half = lane < HEAD_DIM
    hi_half = jnp.logical_not(lo_half)

    iqs = jnp.concatenate([_half_masked(iq_ref[:, (h // 2) * LANES:(h // 2 + 1) * LANES],
                                        lo_half if h % 2 == 0 else hi_half) for h in range(IDX_HEADS)], 0)
    iw_t = ikw_ref[...].T
    pos_l = i * tq + lax.broadcasted_iota(I32, (1, tq), 1)
    kidx = lax.broadcasted_iota(I32, (sc, tq), 0)

    def score_body(c, carry):
        off = pl.multiple_of(c * sc, sc)
        logits = lax.dot_general(ikb_ref[pl.ds(off, sc), :], iqs, NT_DIMS, preferred_element_type=F32)
        sco = jnp.zeros((sc, tq), F32)
        for h in range(IDX_HEADS):
            sco = sco + jnp.maximum(logits[:, h * tq:(h + 1) * tq], 0.0) * iw_t[IDX_DIM + h:IDX_DIM + h + 1, :]
        sco = jnp.where(sco == 0.0, 0.0, sco)
        bits = pltpu.bitcast(sco, I32)
        key = jnp.where(bits < 0, bits ^ jnp.int32(0x7FFFFFFF), bits)
        key_ref[c] = jnp.where(c * sc + kidx <= pos_l, key, jnp.int32(NEG_INF_KEY))
        return carry
    lax.fori_loop(0, n_sc, score_body, 0)
    _topk_bias(key_ref, bias_ref, n_sc, sc, ac, tq, topk, klen, keys_on_sublanes=True)

    half_slots = A_HEADS // 2
    qs = jnp.concatenate([_half_masked(q_ref[:, (h % half_slots) * LANES:(h % half_slots + 1) * LANES],
                                       lo_half if h < half_slots else hi_half) for h in range(A_HEADS)], 0)
    def shared_bias(c):
        bias = bias_ref[c]
        return lambda h: bias
    _flash(qs, A_HEADS, kvb_ref, kvb_ref.at[:, LANES:2 * LANES], 0, 0, n_sc * ratio, ac, shared_bias,
           acc_ref, m_ref, l_ref)
    for j in range(half_slots):
        o_lo = acc_ref[j * tq:(j + 1) * tq]
        o_hi = acc_ref[(half_slots + j) * tq:(half_slots + j + 1) * tq]
        o_ref[:, j * LANES:(j + 1) * LANES] = jnp.where(lo_half, o_lo, o_hi).astype(BF16)


def _dsa_attention(q, iq, ikw, ikb, kvb, *, tq, topk, sc, ac):
    nb, tlen, _ = q.shape
    klen = ikb.shape[1]
    qspec = lambda w: pl.BlockSpec((None, tq, w), lambda b, i: (b, i, 0))
    kspec = lambda w: pl.BlockSpec((None, klen, w), lambda b, i: (b, 0, 0))
    kern = functools.partial(_dsa_kernel, tq=tq, klen=klen, topk=topk, sc=sc, ac=ac)
    return pl.pallas_call(
        kern, grid=(nb, tlen // tq),
        in_specs=[qspec(A_WIDTH), qspec(IDX_HEADS * IDX_DIM), qspec(LANES), kspec(LANES), kspec(2 * LANES)],
        out_specs=qspec(A_WIDTH),
        out_shape=jax.ShapeDtypeStruct((nb, tlen, A_WIDTH), BF16),
        scratch_shapes=[pltpu.VMEM((klen // sc, sc, tq), I32), pltpu.VMEM((klen // ac, tq, ac), F32)]
        + [pltpu.VMEM((A_HEADS * tq, LANES), F32)] * 3,
        compiler_params=_params("parallel", "arbitrary"))(q, iq, ikw, ikb, kvb)


def _page_specs(shape, layer, seqs, n_pages):
    specs = []
    for s in range(seqs):
        for p in range(n_pages):
            specs.append(pl.BlockSpec((None, None) + shape,
                                      lambda i, pt, s=s, p=p: (layer, pt[i * seqs + s, p], 0, 0)))
    return specs


def _softmax_rows(s):
    m = jnp.max(s, axis=1, keepdims=True)
    p = jnp.exp2(s - m)
    return p, jnp.sum(p, axis=1, keepdims=True)


def _dsa_sample_kernel(pt_ref, *refs, seqs, n_pages, pos0, topk):
    del pt_ref
    np_all = seqs * n_pages
    qs_ref, iq_ref, ikw_ref, knew_ref, vnew_ref, iknew_ref = refs[:6]
    kt_pages = refs[6:6 + np_all]
    vt_pages = refs[6 + np_all:6 + 2 * np_all]
    ikt_pages = refs[6 + 2 * np_all:6 + 3 * np_all]
    o_ref, ktb, vtb, iktb, key_ref, bias_ref = refs[6 + 3 * np_all:]
    tq = SUBLANES
    past = n_pages * PAGE_SIZE
    klen = past + LANES
    for s in range(seqs):
        for p in range(n_pages):
            cols = slice(p * PAGE_SIZE, (p + 1) * PAGE_SIZE)
            ktb[s, :, cols] = kt_pages[s * n_pages + p][...].astype(BF16)
            vtb[s, :, cols] = vt_pages[s * n_pages + p][...].astype(BF16)
            iktb[s, :, cols] = ikt_pages[s * n_pages + p][...].astype(BF16)

    pos = pos0 + lax.broadcasted_iota(I32, (tq, 1), 0)
    kidx = lax.broadcasted_iota(I32, (tq, klen), 1)
    for s in range(seqs):
        iq = iq_ref[s]
        logits = jnp.concatenate(
            [jnp.dot(iq, iktb[s], preferred_element_type=F32),
             lax.dot_general(iq, iknew_ref[s], NT_DIMS, preferred_element_type=F32)], 1)
        sco = jnp.zeros((tq, klen), F32)
        for h in range(IDX_HEADS):
            sco = sco + jnp.maximum(logits[h * tq:(h + 1) * tq], 0.0) * ikw_ref[s, :, IDX_DIM + h:IDX_DIM + h + 1]
        sco = jnp.where(sco == 0.0, 0.0, sco)
        bits = pltpu.bitcast(sco, I32)
        key = jnp.where(bits < 0, bits ^ jnp.int32(0x7FFFFFFF), bits)
        key_ref[0, s * tq:(s + 1) * tq, :] = jnp.where(kidx <= pos, key, jnp.int32(NEG_INF_KEY))
    _topk_bias(key_ref, bias_ref, 1, klen, klen, seqs * tq, topk, klen)

    lane = lax.broadcasted_iota(I32, (tq, LANES), 1)
    lo_half = lane < HEAD_DIM
    half_slots = A_HEADS // 2
    for s in range(seqs):
        qs = qs_ref[s]
        sc_all = jnp.concatenate(
            [jnp.dot(qs, ktb[s], preferred_element_type=F32),
             lax.dot_general(qs, knew_ref[s], NT_DIMS, preferred_element_type=F32)], 1)
        p, l = _softmax_rows(sc_all + jnp.tile(bias_ref[0, s * tq:(s + 1) * tq, :], (A_HEADS, 1)))
        pb = p.astype(BF16)
        o = (lax.dot_general(pb[:, :past], vtb[s], NT_DIMS, preferred_element_type=F32)
             + jnp.dot(pb[:, past:], vnew_ref[s], preferred_element_type=F32)) / l
        for j in range(half_slots):
            o_ref[s, :, j * LANES:(j + 1) * LANES] = jnp.where(
                lo_half, o[j * tq:(j + 1) * tq], o[(half_slots + j) * tq:(half_slots + j + 1) * tq]).astype(BF16)


def _dsa_sample(qs, iq, ikw, knew, vnew, iknew, kt_cache, vt_cache, ikt_cache, page_table, layer, *, seqs, pos0,
                topk):
    nb = qs.shape[0]
    n_pages = page_table.shape[1]
    klen = n_pages * PAGE_SIZE + LANES
    tq = SUBLANES
    seq_spec = lambda a: pl.BlockSpec((seqs,) + a.shape[1:], lambda i, pt: (i, 0, 0))
    in_specs = [seq_spec(a) for a in (qs, iq, ikw, knew, vnew, iknew)]
    in_specs += _page_specs(kt_cache.shape[2:], layer, seqs, n_pages)
    in_specs += _page_specs(vt_cache.shape[2:], layer, seqs, n_pages)
    in_specs += _page_specs(ikt_cache.shape[2:], layer, seqs, n_pages)
    np_all = seqs * n_pages
    kvw = A_KV_HEADS * HEAD_DIM
    grid_spec = pltpu.PrefetchScalarGridSpec(
        num_scalar_prefetch=1, grid=(nb // seqs,), in_specs=in_specs,
        out_specs=pl.BlockSpec((seqs, tq, A_WIDTH), lambda i, pt: (i, 0, 0)),
        scratch_shapes=[pltpu.VMEM((seqs, kvw, n_pages * PAGE_SIZE), BF16),
                        pltpu.VMEM((seqs, kvw, n_pages * PAGE_SIZE), BF16),
                        pltpu.VMEM((seqs, IDX_DIM, n_pages * PAGE_SIZE), BF16),
                        pltpu.VMEM((1, seqs * tq, klen), I32), pltpu.VMEM((1, seqs * tq, klen), F32)])
    kern = functools.partial(_dsa_sample_kernel, seqs=seqs, n_pages=n_pages, pos0=pos0, topk=topk)
    return pl.pallas_call(
        kern, grid_spec=grid_spec, out_shape=jax.ShapeDtypeStruct((nb, tq, A_WIDTH), BF16),
        compiler_params=_params("arbitrary"))(
            page_table, qs, iq, ikw, knew, vnew, iknew,
            *([kt_cache] * np_all), *([vt_cache] * np_all), *([ikt_cache] * np_all))


def _gelu_tanh(x):
    return 0.5 * x * (1.0 + jnp.tanh(np.sqrt(2.0 / np.pi) * (x + 0.044715 * x * x * x)))


def _compress_sample_kernel(pt_ref, pet_ref, wab_ref, w2_ref, *refs, seqs, n_pages):
    del pt_ref
    pages = refs[:seqs * n_pages]
    o_ref, tok_ref = refs[seqs * n_pages:]
    width = C_KV_HEADS * HEAD_DIM
    for s in range(seqs):
        for p in range(n_pages):
            base = (s * n_pages + p) * PAGE_SIZE
            for c in range(width // LANES):
                blk = pages[s * n_pages + p][c * LANES:(c + 1) * LANES, :]
                tok_ref[c, base:base + PAGE_SIZE, :] = blk.T
    rows = seqs * n_pages * PAGE_SIZE // CMP_STRIDE
    half = CMP_BLOCK // 2
    ra = jnp.zeros((rows, width), F32)
    rb = jnp.zeros((rows, width), F32)
    for j in range(half):
        xj = jnp.concatenate([tok_ref[c, pl.ds(j, rows, stride=CMP_STRIDE), :] for c in range(width // LANES)], 1)
        ra = ra + jnp.dot((xj + pet_ref[j:j + 1, :]).astype(BF16), wab_ref[j], preferred_element_type=F32)
        rb = rb + jnp.dot((xj + pet_ref[half + j:half + j + 1, :]).astype(BF16), wab_ref[half + j],
                          preferred_element_type=F32)
    hid = _gelu_tanh(ra + pltpu.roll(rb, rows - 1, 0))
    out = jnp.dot(hid.astype(BF16), w2_ref[...], preferred_element_type=F32)
    o_ref[...] = out.reshape(o_ref.shape).astype(BF16)


def _compress_sample(cache_t, page_table, layer, pet, wab, w2, *, seqs):
    nb, n_pages = page_table.shape
    width = C_KV_HEADS * HEAD_DIM
    r = n_pages * PAGE_SIZE // CMP_STRIDE
    const2 = lambda i, pt: (0, 0)
    grid_spec = pltpu.PrefetchScalarGridSpec(
        num_scalar_prefetch=1, grid=(nb // seqs,),
        in_specs=[pl.BlockSpec(pet.shape, const2), pl.BlockSpec(wab.shape, lambda i, pt: (0, 0, 0)),
                  pl.BlockSpec(w2.shape, const2)] + _page_specs(cache_t.shape[2:], layer, seqs, n_pages),
        out_specs=pl.BlockSpec((seqs, r, width), lambda i, pt: (i, 0, 0)),
        scratch_shapes=[pltpu.VMEM((width // LANES, seqs * n_pages * PAGE_SIZE, LANES), F32)])
    return pl.pallas_call(
        functools.partial(_compress_sample_kernel, seqs=seqs, n_pages=n_pages), grid_spec=grid_spec,
        out_shape=jax.ShapeDtypeStruct((nb, r, width), BF16),
        compiler_params=_params("arbitrary"))(page_table, pet, wab, w2, *([cache_t] * (seqs * n_pages)))


def _nsa_sample_kernel(pt_ref, *refs, seqs, n_pages, pos0, nblk):
    del pt_ref
    np_all = seqs * n_pages
    (qs_ref, g_ref, kc_ref, vc_ref, ksn_ref, vsn_ref, kwn_ref, vwn_ref, wink_ref, winv_ref, c2st_ref,
     e_ref) = refs[:12]
    kst_pages = refs[12:12 + np_all]
    vst_pages = refs[12 + np_all:12 + 2 * np_all]
    o_ref, kst, vst, part_ref, bias_ref = refs[12 + 2 * np_all:]
    tq = SUBLANES
    hpg = C_HEADS // C_KV_HEADS
    nh = 2 * hpg
    npair = C_KV_HEADS // 2
    past = n_pages * PAGE_SIZE
    ls = past + LANES
    nbuf = wink_ref.shape[-1]
    lw = nbuf + LANES
    ncp = kc_ref.shape[1]
    for s in range(seqs):
        for p in range(n_pages):
            cols = slice(p * PAGE_SIZE, (p + 1) * PAGE_SIZE)
            kst[s, :, cols] = kst_pages[s * n_pages + p][...].astype(BF16)
            vst[s, :, cols] = vst_pages[s * n_pages + p][...].astype(BF16)

    rows = nh * tq
    t_row = pos0 + lax.broadcasted_iota(I32, (rows, 1), 0) % tq
    cmp_visible = lax.broadcasted_iota(I32, (rows, ncp), 1) * CMP_STRIDE + (CMP_BLOCK - 1) <= t_row
    lane = lax.broadcasted_iota(I32, (tq, LANES), 1)
    lo_half = lane < HEAD_DIM
    pair_cols = lambda pr: slice(pr * LANES, (pr + 1) * LANES)

    psums = []
    for s in range(seqs):
        for pr in range(npair):
            qp = qs_ref[s, pr * rows:(pr + 1) * rows, :]
            s_c = lax.dot_general(qp, kc_ref[s, :, pair_cols(pr)], NT_DIMS, preferred_element_type=F32)
            s_c = jnp.where(cmp_visible, s_c, -jnp.inf)
            m = jnp.max(s_c, axis=1, keepdims=True)
            m = jnp.where(m > -jnp.inf, m, 0.0)
            p = jnp.exp2(s_c - m)
            den = jnp.sum(p, axis=1, keepdims=True)
            p = p / jnp.where(den > 0.0, den, 1.0)
            part_ref[s, pr] = jnp.dot(p.astype(BF16), vc_ref[s, :, pair_cols(pr)], preferred_element_type=F32)
            for half in range(2):
                acc = p[half * hpg * tq:(half * hpg + 1) * tq]
                for h in range(1, hpg):
                    acc = acc + p[(half * hpg + h) * tq:(half * hpg + h + 1) * tq]
                psums.append(acc)

    ng_rows = seqs * C_KV_HEADS * tq
    sel_rows = _cdiv(ng_rows, LANES) * LANES
    psum = jnp.concatenate(psums + [jnp.zeros((sel_rows - ng_rows, ncp), F32)] * (sel_rows > ng_rows), 0)
    t_lane = pos0 + lax.broadcasted_iota(I32, (1, sel_rows), 1) % tq
    selm = _select_blocks(psum, c2st_ref[...], t_lane, nblk)[:ng_rows]
    tg = pos0 + lax.broadcasted_iota(I32, (ng_rows, 1), 0) % tq
    ex = jnp.dot(selm.astype(BF16), e_ref[...], preferred_element_type=F32)
    ok = (ex > 0.5) & (lax.broadcasted_iota(I32, (ng_rows, ls), 1) <= tg)
    bias_ref[...] = jnp.where(ok, 0.0, MASKED)

    wpos = jnp.concatenate([pos0 - nbuf + lax.broadcasted_iota(I32, (rows, nbuf), 1),
                            pos0 + lax.broadcasted_iota(I32, (rows, LANES), 1)], 1)
    wbias = jnp.where((wpos >= 0) & (wpos <= t_row) & (t_row - wpos < WINDOW), 0.0, MASKED)

    for s in range(seqs):
        for pr in range(npair):
            qp = qs_ref[s, pr * rows:(pr + 1) * rows, :]
            feat = slice(pr * LANES, (pr + 1) * LANES)
            gbase = (s * C_KV_HEADS + 2 * pr) * tq
            bias = jnp.concatenate([jnp.tile(bias_ref[gbase:gbase + tq, :], (hpg, 1)),
                                    jnp.tile(bias_ref[gbase + tq:gbase + 2 * tq, :], (hpg, 1))], 0)
            sc_s = jnp.concatenate(
                [jnp.dot(qp, kst[s, feat, :], preferred_element_type=F32),
                 lax.dot_general(qp, ksn_ref[s, :, pair_cols(pr)], NT_DIMS, preferred_element_type=F32)], 1)
            p, l = _softmax_rows(sc_s + bias)
            pb = p.astype(BF16)
            o_s = (lax.dot_general(pb[:, :past], vst[s, feat, :], NT_DIMS, preferred_element_type=F32)
                   + jnp.dot(pb[:, past:], vsn_ref[s, :, pair_cols(pr)], preferred_element_type=F32)) / l
            sc_w = jnp.concatenate(
                [jnp.dot(qp, wink_ref[s, feat, :].astype(BF16), preferred_element_type=F32),
                 lax.dot_general(qp, kwn_ref[s, :, pair_cols(pr)], NT_DIMS, preferred_element_type=F32)], 1)
            p, l = _softmax_rows(sc_w + wbias)
            pb = p.astype(BF16)
            o_w = (lax.dot_general(pb[:, :nbuf], winv_ref[s, feat, :].astype(BF16), NT_DIMS,
                                   preferred_element_type=F32)
                   + jnp.dot(pb[:, nbuf:], vwn_ref[s, :, pair_cols(pr)], preferred_element_type=F32)) / l

            def gate(branch):
                cols = [3 * (hpg * (2 * pr + h // hpg) + h % hpg) + branch for h in range(nh)]
                return jnp.concatenate([g_ref[s, :, c:c + 1] for c in cols], 0)
            o = gate(0) * part_ref[s, pr] + gate(1) * o_s + gate(2) * o_w
            for j in range(hpg):
                slot = pr * hpg + j
                o_ref[s, :, slot * LANES:(slot + 1) * LANES] = jnp.where(
                    lo_half, o[j * tq:(j + 1) * tq], o[(hpg + j) * tq:(hpg + j + 1) * tq]).astype(BF16)


def _nsa_sample(qs, gates, kcmp, vcmp, ksn, vsn, kwn, vwn, wink_t, winv_t, c2s, emat, kst_cache, vst_cache,
                page_table, layer, *, seqs, pos0, n_sblk):
    nb, n_pages = page_table.shape
    tq = SUBLANES
    width = C_KV_HEADS * HEAD_DIM
    past = n_pages * PAGE_SIZE
    seq_spec = lambda a: pl.BlockSpec((seqs,) + a.shape[1:], lambda i, pt: (i, 0, 0))
    win_spec = pl.BlockSpec((None, seqs) + wink_t.shape[2:], lambda i, pt: (layer, i, 0, 0))
    const2 = lambda i, pt: (0, 0)
    in_specs = [seq_spec(a) for a in (qs, gates, kcmp, vcmp, ksn, vsn, kwn, vwn)]
    in_specs += [win_spec, win_spec, pl.BlockSpec(c2s.shape, const2), pl.BlockSpec(emat.shape, const2)]
    in_specs += _page_specs(kst_cache.shape[2:], layer, seqs, n_pages)
    in_specs += _page_specs(vst_cache.shape[2:], layer, seqs, n_pages)
    np_all = seqs * n_pages
    rows = 2 * (C_HEADS // C_KV_HEADS) * tq
    grid_spec = pltpu.PrefetchScalarGridSpec(
        num_scalar_prefetch=1, grid=(nb // seqs,), in_specs=in_specs,
        out_specs=pl.BlockSpec((seqs, tq, C_WIDTH), lambda i, pt: (i, 0, 0)),
        scratch_shapes=[pltpu.VMEM((seqs, width, past), BF16), pltpu.VMEM((seqs, width, past), BF16),
                        pltpu.VMEM((seqs, C_KV_HEADS // 2, rows, LANES), F32),
                        pltpu.VMEM((seqs * C_KV_HEADS * tq, past + LANES), F32)])
    kern = functools.partial(_nsa_sample_kernel, seqs=seqs, n_pages=n_pages, pos0=pos0,
                             nblk=_cdiv(n_sblk, SUBLANES) * SUBLANES)
    return pl.pallas_call(
        kern, grid_spec=grid_spec, out_shape=jax.ShapeDtypeStruct((nb, tq, C_WIDTH), BF16),
        compiler_params=_params("arbitrary"))(
            page_table, qs, gates, kcmp, vcmp, ksn, vsn, kwn, vwn, wink_t, winv_t, c2s, emat,
            *([kst_cache] * np_all), *([vst_cache] * np_all))


def _proj_c_kernel(x_ref, w_ref, c_ref, s_ref, q_ref, kc_ref, ks_ref, kw_ref, vc_ref, vs_ref, vw_ref,
                   g_ref, kvb_ref, *, tm):
    y = jnp.dot(x_ref[...].astype(BF16), w_ref[...], preferred_element_type=F32)
    lane = lax.broadcasted_iota(I32, (tm, LANES), 1)
    first_half = (lane & (HEAD_DIM - 1)) < HEAD_DIM // 2
    c = c_ref[...]
    s = s_ref[...]
    ro = [_rope128(y[:, j * LANES:(j + 1) * LANES], c, s, first_half) for j in range(C_ROPED // LANES)]
    for j in range(8):
        q_ref[:, j * LANES:(j + 1) * LANES] = (ro[j] * Q_SCALE).astype(BF16)
    for n, ref in enumerate((kc_ref, ks_ref, kw_ref)):
        for j in range(2):
            ref[:, j * LANES:(j + 1) * LANES] = ro[8 + 2 * n + j]
    for n, ref in enumerate((vc_ref, vs_ref, vw_ref)):
        ref[...] = y[:, C_ROPED + 256 * n:C_ROPED + 256 * (n + 1)]
    g = y[:, C_ROPED + 768:C_ROPED + 768 + LANES]
    g_ref[...] = 1.0 / (1.0 + jnp.exp(-g))
    for j in range(2):
        kvb_ref[:, j * LANES:(j + 1) * LANES] = ro[10 + j].astype(BF16)
        kvb_ref[:, 512 + j * LANES:512 + (j + 1) * LANES] = ro[12 + j].astype(BF16)
    kvb_ref[:, 256:512] = y[:, C_ROPED + 256:C_ROPED + 512].astype(BF16)
    kvb_ref[:, 768:1024] = y[:, C_ROPED + 512:C_ROPED + 768].astype(BF16)


def _proj_c(x2d, w, tabs, *, tm):
    m = x2d.shape[0]
    nt = tabs[0].shape[0] // tm
    row = lambda i: (i, 0)
    kvw = C_KV_HEADS * HEAD_DIM
    out_shape = [jax.ShapeDtypeStruct((m, C_WIDTH), BF16)]
    out_shape += [jax.ShapeDtypeStruct((m, kvw), F32)] * 6
    out_shape += [jax.ShapeDtypeStruct((m, LANES), F32), jax.ShapeDtypeStruct((m, 4 * kvw), BF16)]
    out_specs = [pl.BlockSpec((tm, C_WIDTH), row)] + [pl.BlockSpec((tm, kvw), row)] * 6
    out_specs += [pl.BlockSpec((tm, LANES), row), pl.BlockSpec((tm, 4 * kvw), row)]
    return pl.pallas_call(
        functools.partial(_proj_c_kernel, tm=tm), grid=(m // tm,),
        in_specs=[pl.BlockSpec((tm, D_MODEL), row), pl.BlockSpec((D_MODEL, C_COLS), lambda i: (0, 0)),
                  pl.BlockSpec((tm, LANES), lambda i: (i % nt, 0)), pl.BlockSpec((tm, LANES), lambda i: (i % nt, 0))],
        out_specs=out_specs, out_shape=out_shape,
        compiler_params=_params("parallel"))(x2d, w, *tabs)


def _compress_kernel(z_ref, pea_ref, peb_ref, wa_ref, wb_ref, w2_ref, o_ref, *, rows):
    z = z_ref[...].reshape(rows, z_ref.shape[-1])
    ra = jnp.dot((z + pea_ref[...]).astype(BF16), wa_ref[...], preferred_element_type=F32)
    rb = jnp.dot((z + peb_ref[...]).astype(BF16), wb_ref[...], preferred_element_type=F32)
    hid = ra + pltpu.roll(rb, rows - 1, 0)
    hid = 0.5 * hid * (1.0 + jnp.tanh(np.sqrt(2.0 / np.pi) * (hid + 0.044715 * hid * hid * hid)))
    out = jnp.dot(hid.astype(BF16), w2_ref[...], preferred_element_type=F32)
    o_ref[...] = out.reshape(o_ref.shape).astype(BF16)


def _compress(z, pea, peb, wa, wb, w2, *, nb_step):
    nb, r, zw = z.shape
    kvw = C_KV_HEADS * HEAD_DIM
    const = lambda b: (0, 0)
    return pl.pallas_call(
        functools.partial(_compress_kernel, rows=nb_step * r), grid=(nb // nb_step,),
        in_specs=[pl.BlockSpec((nb_step, r, zw), lambda b: (b, 0, 0)),
                  pl.BlockSpec((1, zw), const), pl.BlockSpec((1, zw), const),
                  pl.BlockSpec((zw, kvw), const), pl.BlockSpec((zw, kvw), const), pl.BlockSpec((kvw, kvw), const)],
        out_specs=pl.BlockSpec((nb_step, r, kvw), lambda b: (b, 0, 0)),
        out_shape=jax.ShapeDtypeStruct((nb, r, kvw), BF16),
        compiler_params=_params("parallel"))(z, pea, peb, wa, wb, w2)


def _select_blocks(psum, c2st, t_lane, nblk):
    rows = psum.shape[0]
    p_hi = psum.astype(BF16)
    p_lo = (psum - p_hi.astype(F32)).astype(BF16)
    imp = (lax.dot_general(c2st, p_hi, NT_DIMS, preferred_element_type=F32)
           + lax.dot_general(c2st, p_lo, NT_DIMS, preferred_element_type=F32))[:nblk]
    blk = lax.broadcasted_iota(I32, (nblk, rows), 0)
    blk_f = blk.astype(F32)
    cur = t_lane // SLC_BLOCK
    forced = (blk == 0) | (blk == cur) | (blk == cur - 1)
    imp = jnp.where(forced, jnp.inf, imp)
    imp = jnp.where(blk * SLC_BLOCK <= t_lane, imp, -jnp.inf)

    def top_body(_, carry):
        val, sel = carry
        mx = jnp.max(val, axis=0, keepdims=True)
        first = jnp.min(jnp.where(val == mx, blk_f, float(LANES)), axis=0, keepdims=True)
        pick = blk_f == first
        sel = jnp.where(pick & (mx > -jnp.inf), 1.0, sel)
        return jnp.where(pick, -jnp.inf, val), sel
    _, sel = lax.fori_loop(0, SLC_TOPN, top_body, (imp, jnp.zeros((nblk, rows), F32)))
    sel = jnp.concatenate([sel, jnp.zeros((LANES - nblk, rows), F32)], 0)
    return jnp.concatenate([sel[:, j * LANES:(j + 1) * LANES].T for j in range(rows // LANES)], 0)


def _nsa_kernel(q_ref, g_ref, kc_ref, vc_ref, kvs_ref, kvw_ref, c2st_ref, e_ref, o_ref,
                bias_ref, part_ref, acc_ref, m_ref, l_ref,
                *, tq, ncp, ac, nblk):
    i = pl.program_id(1)
    t = i * tq + lax.broadcasted_iota(I32, (tq, 1), 0)
    lane = lax.broadcasted_iota(I32, (tq, LANES), 1)
    lo_half = lane < HEAD_DIM
    n_s = _cdiv((i + 1) * tq, ac)
    span = WINDOW + tq
    w_start = pl.multiple_of(jnp.maximum(i * tq - WINDOW, 0), tq)
    kvw = C_KV_HEADS * HEAD_DIM
    ks_ref = kvs_ref
    vs_ref = kvs_ref.at[:, kvw:2 * kvw]
    kw_ref = kvw_ref
    vw_ref = kvw_ref.at[:, kvw:2 * kvw]
    hi_half = jnp.logical_not(lo_half)
    cmp_visible = lax.broadcasted_iota(I32, (tq, ncp), 1) * CMP_STRIDE + (CMP_BLOCK - 1) <= t
    ng = C_KV_HEADS
    hpg = C_HEADS // C_KV_HEADS
    nh = 2 * hpg
    head_rows = lambda h: slice(h * tq, (h + 1) * tq)
    tg = jnp.concatenate([t] * ng, 0)
    kiota_sg = lax.broadcasted_iota(I32, (ng * tq, ac), 1)

    def gate(pr, h, branch):
        head = hpg * (2 * pr + h // hpg) + h % hpg
        return g_ref[:, 3 * head + branch:3 * head + branch + 1]

    wpos = w_start + lax.broadcasted_iota(I32, (tq, span), 1)
    wbias = jnp.tile(jnp.where((wpos <= t) & (t - wpos < WINDOW), 0.0, MASKED), (nh, 1))

    def stacked_q(pr):
        return jnp.concatenate(
            [_half_masked(q_ref[:, (pr * hpg + h % hpg) * LANES:(pr * hpg + h % hpg + 1) * LANES],
                          lo_half if h < hpg else hi_half) for h in range(nh)], 0)

    psums = []
    for pr in range(ng // 2):
        col = pr * LANES
        s_c = lax.dot_general(stacked_q(pr), kc_ref[:, col:col + LANES], NT_DIMS, preferred_element_type=F32)
        pair_sums = [jnp.zeros((tq, ncp), F32), jnp.zeros((tq, ncp), F32)]
        ps = []
        for h in range(nh):
            sh = jnp.where(cmp_visible, s_c[head_rows(h)], -jnp.inf)
            m = jnp.max(sh, axis=1, keepdims=True)
            m = jnp.where(m > -jnp.inf, m, 0.0)
            p = jnp.exp2(sh - m)
            den = jnp.sum(p, axis=1, keepdims=True)
            p = p / jnp.where(den > 0.0, den, 1.0)
            pair_sums[h // hpg] = pair_sums[h // hpg] + p
            ps.append(p.astype(BF16))
        part_ref[pr * nh * tq:(pr + 1) * nh * tq] = jnp.dot(jnp.concatenate(ps, 0), vc_ref[:, col:col + LANES],
                                                           preferred_element_type=F32)
        psums += pair_sums

    t_lane = i * tq + lax.broadcasted_iota(I32, (1, ng * tq), 1) % tq
    selb = _select_blocks(jnp.concatenate(psums, 0), c2st_ref[...], t_lane, nblk).astype(BF16)

    def bias_body(c, carry):
        ex = jnp.dot(selb, e_ref[c], preferred_element_type=F32)
        ok = (ex > 0.5) & (c * ac + kiota_sg <= tg)
        bias_ref[c] = jnp.where(ok, 0.0, MASKED)
        return carry
    lax.fori_loop(0, n_s, bias_body, 0)

    for pr in range(ng // 2):
        col = pr * LANES
        qs = stacked_q(pr)
        part = part_ref.at[pr * nh * tq:(pr + 1) * nh * tq]

        def selected_bias(c, pr=pr):
            return lambda h: bias_ref[c, (2 * pr + h // hpg) * tq:(2 * pr + h // hpg + 1) * tq, :]
        _flash(qs, nh, ks_ref, vs_ref, col, 0, n_s, ac, selected_bias, acc_ref, m_ref, l_ref)
        for h in range(nh):
            r = head_rows(h)
            part[r] = gate(pr, h, 0) * part[r] + gate(pr, h, 1) * acc_ref[r]
        s_w = lax.dot_general(qs, kw_ref[pl.ds(w_start, span), col:col + LANES], NT_DIMS,
                              preferred_element_type=F32)
        p_w, l_w = _softmax_rows(s_w + wbias)
        o_w = jnp.dot(p_w.astype(BF16), vw_ref[pl.ds(w_start, span), col:col + LANES],
                      preferred_element_type=F32) / l_w
        for j in range(hpg):
            slot = pr * hpg + j
            lo, hi = head_rows(j), head_rows(hpg + j)
            o_lo = part[lo] + gate(pr, j, 2) * o_w[lo]
            o_hi = part[hi] + gate(pr, hpg + j, 2) * o_w[hi]
            o_ref[:, slot * LANES:(slot + 1) * LANES] = jnp.where(lo_half, o_lo, o_hi).astype(BF16)


def _nsa_attention(q, gates, kcmp, vcmp, kvs, kvw, c2st, emat, *, tq, ac, n_sblk, s_blk=0, w_blk=0):
    nb, tlen, _ = q.shape
    ncp = kcmp.shape[1]
    ls, lw = kvs.shape[1], kvw.shape[1]
    assert WINDOW % tq == 0 and lw >= WINDOW + tq and (C_KV_HEADS * tq) % LANES == 0
    kvwid = C_KV_HEADS * HEAD_DIM
    qspec = lambda w: pl.BlockSpec((None, tq, w), lambda b, i: (b, i, 0))
    kspec = lambda n, w, blk=0: pl.BlockSpec((None, n, w), lambda b, i: (b, 0, blk))
    kern = functools.partial(_nsa_kernel, tq=tq, ncp=ncp, ac=ac, nblk=_cdiv(n_sblk, SUBLANES) * SUBLANES)
    return pl.pallas_call(
        kern, grid=(nb, tlen // tq),
        in_specs=[qspec(C_WIDTH), qspec(LANES), kspec(ncp, kvwid), kspec(ncp, kvwid),
                  kspec(ls, 2 * kvwid, s_blk), kspec(lw, 2 * kvwid, w_blk),
                  pl.BlockSpec((LANES, ncp), lambda b, i: (0, 0)),
                  pl.BlockSpec((ls // ac, LANES, ac), lambda b, i: (0, 0, 0))],
        out_specs=qspec(C_WIDTH),
        out_shape=jax.ShapeDtypeStruct((nb, tlen, C_WIDTH), BF16),
        scratch_shapes=[pltpu.VMEM((ls // ac, C_KV_HEADS * tq, ac), F32), pltpu.VMEM((C_HEADS * tq, LANES), F32)]
        + [pltpu.VMEM((2 * (C_HEADS // C_KV_HEADS) * tq, LANES), F32)] * 3,
        compiler_params=_params("parallel", "arbitrary"))(q, gates, kcmp, vcmp, kvs, kvw, c2st, emat)


def _outproj_ln_kernel(x_ref, a_ref, b_ref, wa_ref, wb_ref, g_ref, bt_ref, o_ref):
    y = ALPHA * x_ref[...]
    y = y + jnp.dot(a_ref[...], wa_ref[...], preferred_element_type=F32)
    y = y + jnp.dot(b_ref[...], wb_ref[...], preferred_element_type=F32)
    o_ref[...] = _layernorm(y, g_ref[...], bt_ref[...])


def _outproj_ln(x2d, a, b, a_blk, b_blk, w_out, g, bt, *, tm):
    m = x2d.shape[0]
    half = w_out.shape[0] // 2
    row = lambda i: (i, 0)
    const = lambda i: (0, 0)
    return pl.pallas_call(
        _outproj_ln_kernel, grid=(m // tm,),
        in_specs=[pl.BlockSpec((tm, D_MODEL), row),
                  pl.BlockSpec((tm, half), lambda i: (i, a_blk)), pl.BlockSpec((tm, half), lambda i: (i, b_blk)),
                  pl.BlockSpec((half, D_MODEL), lambda i: (0, 0)), pl.BlockSpec((half, D_MODEL), lambda i: (1, 0)),
                  pl.BlockSpec((1, D_MODEL), const), pl.BlockSpec((1, D_MODEL), const)],
        out_specs=pl.BlockSpec((tm, D_MODEL), row),
        out_shape=jax.ShapeDtypeStruct((m, D_MODEL), F32),
        compiler_params=_params("parallel"))(x2d, a, b, w_out, w_out, g, bt)


def _ffn_ln_kernel(x_ref, wg_ref, wu_ref, wd_ref, g_ref, bt_ref, o_ref, xb_ref, acc_ref):
    f = pl.program_id(1)

    @pl.when(f == 0)
    def _():
        xb_ref[...] = x_ref[...].astype(BF16)
        acc_ref[...] = jnp.zeros(acc_ref.shape, F32)

    xb = xb_ref[...]
    h = jnp.dot(xb, wg_ref[...], preferred_element_type=F32)
    u = jnp.dot(xb, wu_ref[...], preferred_element_type=F32)
    a = (h / (1.0 + jnp.exp(-h))) * u
    acc_ref[...] += jnp.dot(a.astype(BF16), wd_ref[...], preferred_element_type=F32)

    @pl.when(f == pl.num_programs(1) - 1)
    def _():
        o_ref[...] = _layernorm(ALPHA * x_ref[...] + acc_ref[...], g_ref[...], bt_ref[...])


def _ffn_ln(x2d, wg, wu, wd, g, bt, *, tm, tf):
    m = x2d.shape[0]
    return pl.pallas_call(
        _ffn_ln_kernel, grid=(m // tm, D_FF // tf),
        in_specs=[pl.BlockSpec((tm, D_MODEL), lambda i, f: (i, 0)),
                  pl.BlockSpec((D_MODEL, tf), lambda i, f: (0, f)), pl.BlockSpec((D_MODEL, tf), lambda i, f: (0, f)),
                  pl.BlockSpec((tf, D_MODEL), lambda i, f: (f, 0)),
                  pl.BlockSpec((1, D_MODEL), lambda i, f: (0, 0)), pl.BlockSpec((1, D_MODEL), lambda i, f: (0, 0))],
        out_specs=pl.BlockSpec((tm, D_MODEL), lambda i, f: (i, 0)),
        out_shape=jax.ShapeDtypeStruct((m, D_MODEL), F32),
        scratch_shapes=[pltpu.VMEM((tm, D_MODEL), BF16), pltpu.VMEM((tm, D_MODEL), F32)],
        compiler_params=_params("parallel", "arbitrary"))(x2d, wg, wu, wd, g, bt)


def _rope_tables(pos):
    half = HEAD_DIM // 2
    inv = ROPE_THETA ** (-jnp.arange(half, dtype=F32) / half)
    ang = pos.astype(F32)[:, None] * inv[None, :]
    cos, sin = jnp.cos(ang), jnp.sin(ang)
    c64 = jnp.concatenate([cos, cos], 1)
    s64 = jnp.concatenate([-sin, sin], 1)
    c = jnp.concatenate([c64, c64], 1)
    s = jnp.concatenate([s64, s64], 1)
    cx = jnp.concatenate([c64, jnp.full_like(c64, IDX_HEADS ** -0.5)], 1)
    sx = jnp.concatenate([s64, jnp.zeros_like(s64)], 1)
    return c, s, cx, sx


def _perm_heads(w, perm):
    lead = w.shape[:-1]
    return w.reshape(*lead, len(perm), HEAD_DIM)[..., np.asarray(perm), :].reshape(*lead, len(perm) * HEAD_DIM)


def _ab_w_in(w):
    ab_sizes = (A_WIDTH, A_KV_HEADS * HEAD_DIM, A_KV_HEADS * HEAD_DIM, IDX_HEADS * IDX_DIM, IDX_DIM, IDX_HEADS,
                B_WIDTH, B_WIDTH, B_WIDTH)
    q, k, v, iq, ik, iw, gb, gc, h = jnp.split(w, np.cumsum(ab_sizes)[:-1].tolist(), axis=-1)
    pad = jnp.zeros((w.shape[0], LANES - IDX_DIM - IDX_HEADS), w.dtype)
    return jnp.concatenate([_perm_heads(q, A_PERM), k, iq, ik, iw, pad, v, gb, gc, h], -1).astype(BF16)


def _c_w_in(w):
    kvw = C_KV_HEADS * HEAD_DIM
    c_sizes = (C_WIDTH,) + (kvw,) * 6 + (3 * C_HEADS,)
    q, kc, vc, ks, vs, kw, vw, g = jnp.split(w, np.cumsum(c_sizes)[:-1].tolist(), axis=-1)
    pad = jnp.zeros((w.shape[0], LANES - 3 * C_HEADS), w.dtype)
    return jnp.concatenate([_perm_heads(q, C_PERM), kc, ks, kw, vc, vs, vw, g, pad], -1).astype(BF16)


def _perm_rows(w_out, perm):
    return w_out.reshape(len(perm), HEAD_DIM, w_out.shape[-1])[np.asarray(perm)].reshape(-1, w_out.shape[-1])


def _block_diag(w, n):
    eye = jnp.eye(n, dtype=w.dtype)
    out = jnp.einsum('gh,...ab->...gahb', eye, w)
    return out.reshape(*w.shape[:-2], n * w.shape[-2], n * w.shape[-1])


def _compress_weights(pe, w1, w2):
    g = C_KV_HEADS
    half = CMP_BLOCK // 2
    bd = _block_diag(w1, g)
    wa = bd[:half].reshape(half * g * HEAD_DIM, g * HEAD_DIM).astype(BF16)
    wb = bd[half:].reshape(half * g * HEAD_DIM, g * HEAD_DIM).astype(BF16)
    pet = jnp.tile(pe, (1, g))
    pea = pet[:half].reshape(1, -1)
    peb = pet[half:].reshape(1, -1)
    return pea, peb, wa, wb, _block_diag(w2, g).astype(BF16)


def _slc_from_cmp(ncp, n_cmp, n_sblk):
    n = np.arange(ncp)[None, :]
    mblk = np.arange(LANES)[:, None]
    hit = ((n * CMP_STRIDE < mblk * SLC_BLOCK + SLC_BLOCK) & (n * CMP_STRIDE + CMP_BLOCK > mblk * SLC_BLOCK)
           & (n < n_cmp) & (mblk < n_sblk))
    return jnp.asarray(hit, BF16)


def _expand_matrix(ls, ac):
    k = np.arange(ls)[None, :]
    mblk = np.arange(LANES)[:, None]
    e = (k // SLC_BLOCK == mblk).astype(np.float32)
    return jnp.asarray(e.reshape(LANES, ls // ac, ac).transpose(1, 0, 2), BF16)


def _gather_pages(pool_layer, page_table):
    g = pool_layer[page_table]
    return g.reshape(g.shape[0], g.shape[1] * g.shape[2], -1)


def _pad_rows(a, n):
    return jnp.pad(a, ((0, 0), (0, n - a.shape[1]), (0, 0)))


def _stack_heads(q3, slots):
    nb, tq, _ = q3.shape
    qt = q3.reshape(nb, tq, slots, LANES).transpose(0, 2, 1, 3)
    lo = jnp.arange(LANES) < HEAD_DIM
    zero = jnp.zeros((), q3.dtype)
    stacked = jnp.concatenate([jnp.where(lo, qt, zero), jnp.where(lo, zero, qt)], 1)
    return stacked.reshape(nb, 2 * slots * tq, LANES)


def _token_minor(cache):
    nd = cache.ndim
    t = jnp.transpose(cache, (0, 1) + tuple(range(3, nd)) + (2,))
    return t.reshape(t.shape[0], t.shape[1], -1, t.shape[-1])


PROMPT_TM = 512
PROMPT_AC = 512
SAMPLE_TQ = 8
FFN_TF = 1408
DSA_SAMPLE_SEQS = 4
NSA_SAMPLE_SEQS = 2
CMP_SAMPLE_SEQS = 4


def _ab_layer(x2d, nb, tlen, start, past, page_table, w_in, conv_w, tabs, *, sample):
    m = nb * tlen
    conv_w8 = jnp.pad(conv_w, ((0, SUBLANES - CONV_W), (0, 0)))
    if sample:
        kt_cache, vt_cache, ikt_cache, prev, layer = past
        tt = jnp.arange(tlen)
        prevs = []
        for d in range(1, CONV_W):
            idx = jnp.clip(CONV_W - 1 + tt - d, 0, CONV_W - 2)
            prevs.append(prev[:, idx].reshape(m, B_WIDTH))
        outs = _proj_ab(x2d, w_in, tabs, conv_w8, tm=m, seq_tiles=1, sample=True, prevs=prevs, dec_seq=tlen)
    else:
        outs = _proj_ab(x2d, w_in, tabs, conv_w8, tm=PROMPT_TM, seq_tiles=tlen // PROMPT_TM, sample=False)
    q, iq, k, v, ikw, kvb, ikb, b_out, uo = outs
    if sample:
        klen_real = page_table.shape[1] * PAGE_SIZE + tlen
        pad_q = lambda a: _pad_rows(a.reshape(nb, tlen, -1), SAMPLE_TQ)
        pad_k = lambda a: _pad_rows(a.reshape(nb, tlen, -1), LANES)
        kvb3 = kvb.reshape(nb, tlen, 2 * LANES)
        iq_heads = pad_q(iq).reshape(nb, SAMPLE_TQ, IDX_HEADS, IDX_DIM).transpose(0, 2, 1, 3)
        a_out = _dsa_sample(_stack_heads(pad_q(q), A_HEADS // 2), iq_heads.reshape(nb, IDX_HEADS * SAMPLE_TQ, IDX_DIM),
                            pad_q(ikw), pad_k(kvb3[..., :LANES]), pad_k(kvb3[..., LANES:]),
                            pad_k(ikb.reshape(nb, tlen, LANES)[..., :IDX_DIM]),
                            kt_cache, vt_cache, ikt_cache, page_table, layer,
                            seqs=DSA_SAMPLE_SEQS, pos0=start, topk=min(A_TOPK_MAX, klen_real // 4))
        a_out = a_out[:, :tlen].reshape(m, A_WIDTH)
        new_conv = uo.reshape(nb, tlen, B_WIDTH)[:, tlen - (CONV_W - 1):]
    else:
        r3 = lambda a: a.reshape(nb, tlen, -1)
        a_out = _dsa_attention(r3(q), r3(iq), r3(ikw), r3(ikb), r3(kvb), tq=Q_BLOCK,
                               topk=min(A_TOPK_MAX, tlen // 4), sc=512, ac=PROMPT_AC)
        a_out = a_out.reshape(m, A_WIDTH)
        new_conv = uo.reshape(nb, tlen // PROMPT_TM, SUBLANES, B_WIDTH)[:, -1, SUBLANES - (CONV_W - 1):]
    state = (k.reshape(nb, tlen, A_KV_HEADS, HEAD_DIM), v.reshape(nb, tlen, A_KV_HEADS, HEAD_DIM),
             ikw[:, :IDX_DIM].reshape(nb, tlen, IDX_DIM), new_conv)
    return a_out, b_out, state


def _c_layer(x2d, nb, tlen, start, past, w_in, cw_k, cw_v, tabs, *, sample):
    m = nb * tlen
    kvw = C_KV_HEADS * HEAD_DIM
    tm = m if sample else PROMPT_TM
    q, kc, ks, kw, vc, vs, vw, gates, kvb = _proj_c(x2d, w_in, tabs[:2], tm=tm)
    if sample:
        kct, vct, kst, vst, wink_t, winv_t, buf_k, buf_v, page_table, layer = past
        past_len = page_table.shape[1] * PAGE_SIZE
        n_cmp = (past_len + tlen - CMP_BLOCK) // CMP_STRIDE + 1
        assert n_cmp <= past_len // CMP_STRIDE, "compression blocks must lie inside the cached rows"
        assert tlen <= LANES and start == past_len

        def tap_major(cw):
            pea, peb, wa, wb, w2 = cw
            pet = jnp.concatenate([pea.reshape(-1, kvw), peb.reshape(-1, kvw)], 0)
            wab = jnp.concatenate([wa.reshape(-1, kvw, kvw), wb.reshape(-1, kvw, kvw)], 0)
            return pet, wab, w2
        kcmp = _compress_sample(kct, page_table, layer, *tap_major(cw_k), seqs=CMP_SAMPLE_SEQS)
        vcmp = _compress_sample(vct, page_table, layer, *tap_major(cw_v), seqs=CMP_SAMPLE_SEQS)
        ncp = kcmp.shape[1]
        ls = past_len + LANES
        n_sblk = _cdiv(past_len + tlen, SLC_BLOCK)
        kvb3 = kvb.reshape(nb, tlen, 4 * kvw)
        new_rows = [_pad_rows(kvb3[..., n * kvw:(n + 1) * kvw], LANES) for n in range(4)]
        q3 = _pad_rows(q.reshape(nb, tlen, C_WIDTH), SAMPLE_TQ)
        half_w = C_WIDTH // 2
        qs = jnp.concatenate([_stack_heads(q3[..., :half_w], C_HEADS // 4),
                              _stack_heads(q3[..., half_w:], C_HEADS // 4)], 1)
        out = _nsa_sample(qs, _pad_rows(gates.reshape(nb, tlen, LANES), SAMPLE_TQ), kcmp, vcmp, *new_rows,
                          wink_t, winv_t, _slc_from_cmp(ncp, n_cmp, n_sblk), _expand_matrix(ls, ls)[0],
                          kst, vst, page_table, layer, seqs=NSA_SAMPLE_SEQS, pos0=start, n_sblk=n_sblk)
        out = out[:, :tlen].reshape(m, C_WIDTH)
        r4 = lambda a: a.reshape(nb, tlen, C_KV_HEADS, HEAD_DIM)
        win_k = jnp.concatenate([buf_k, r4(kw)], 1)[:, tlen:]
        win_v = jnp.concatenate([buf_v, r4(vw)], 1)[:, tlen:]
    else:
        zk = kc.reshape(nb, tlen // CMP_STRIDE, CMP_STRIDE * kvw)
        zv = vc.reshape(nb, tlen // CMP_STRIDE, CMP_STRIDE * kvw)
        n_cmp = (tlen - CMP_BLOCK) // CMP_STRIDE + 1
        kcmp = _compress(zk, *cw_k, nb_step=1)
        vcmp = _compress(zv, *cw_v, nb_step=1)
        ncp = kcmp.shape[1]
        n_sblk = _cdiv(tlen, SLC_BLOCK)
        kvb3 = kvb.reshape(nb, tlen, 4 * kvw)
        out = _nsa_attention(q.reshape(nb, tlen, C_WIDTH), gates.reshape(nb, tlen, LANES), kcmp, vcmp, kvb3, kvb3,
                             _slc_from_cmp(ncp, n_cmp, n_sblk), _expand_matrix(tlen, PROMPT_AC),
                             tq=Q_BLOCK, ac=PROMPT_AC, n_sblk=n_sblk, s_blk=0, w_blk=1)
        out = out.reshape(m, C_WIDTH)
        r4 = lambda a: a.reshape(nb, tlen, C_KV_HEADS, HEAD_DIM)
        keep = min(WINDOW, tlen)
        win_k = r4(kw)[:, tlen - keep:]
        win_v = r4(vw)[:, tlen - keep:]
    r4 = lambda a: a.reshape(nb, tlen, C_KV_HEADS, HEAD_DIM)
    return out, (r4(kc), r4(vc), r4(ks), r4(vs), win_k, win_v)


def kernel(x_prompt, x_sample, cache_a_k, cache_a_v, cache_a_ik, state_b_conv, cache_c_cmp_k, cache_c_cmp_v, cache_c_slc_k, cache_c_slc_v, state_c_win_k, state_c_win_v, page_table, ab_w_in, ab_conv_w, ab_w_out, c_w_in, c_cmp_pe_k, c_cmp_w1_k, c_cmp_w2_k, c_cmp_pe_v, c_cmp_w1_v, c_cmp_w2_v, c_w_out, ffn_w_gate, ffn_w_up, ffn_w_down, ln1_g, ln1_b, ln2_g, ln2_b):
    bp, tp, _ = x_prompt.shape
    bs, ts, _ = x_sample.shape
    past_len = page_table.shape[1] * PAGE_SIZE
    xp = x_prompt.reshape(bp * tp, D_MODEL)
    xs = x_sample.reshape(bs * ts, D_MODEL)
    tabs_p = _rope_tables(jnp.arange(tp))
    tabs_s = _rope_tables(past_len + jnp.arange(bs * ts) % ts)
    kt_a, vt_a, ikt_a = _token_minor(cache_a_k), _token_minor(cache_a_v), _token_minor(cache_a_ik)
    kct_c, vct_c = _token_minor(cache_c_cmp_k), _token_minor(cache_c_cmp_v)
    kst_c, vst_c = _token_minor(cache_c_slc_k), _token_minor(cache_c_slc_v)
    wink_t, winv_t = _token_minor(state_c_win_k), _token_minor(state_c_win_v)
    ab_p, ab_s, c_p, c_s = [], [], [], []
    for layer in range(DEPTH):
        i = layer // 2
        row1 = lambda a: a[layer].reshape(1, D_MODEL)
        if layer % 2 == 0:
            w_in = _ab_w_in(ab_w_in[i])
            w_out = jnp.concatenate([_perm_rows(ab_w_out[i][:A_WIDTH], A_PERM), ab_w_out[i][A_WIDTH:]], 0).astype(BF16)
            a_p, b_p, st_p = _ab_layer(xp, bp, tp, 0, None, None, w_in, ab_conv_w[i], tabs_p, sample=False)
            past = (kt_a, vt_a, ikt_a, state_b_conv[i], i)
            a_s, b_s, st_s = _ab_layer(xs, bs, ts, past_len, past, page_table, w_in, ab_conv_w[i], tabs_s, sample=True)
            ab_p.append(st_p)
            ab_s.append(st_s)
            xp = _outproj_ln(xp, a_p, b_p, 0, 0, w_out, row1(ln1_g), row1(ln1_b), tm=PROMPT_TM)
            xs = _outproj_ln(xs, a_s, b_s, 0, 0, w_out, row1(ln1_g), row1(ln1_b), tm=bs * ts)
        else:
            w_in = _c_w_in(c_w_in[i])
            w_out = _perm_rows(c_w_out[i], C_PERM).astype(BF16)
            cw_k = _compress_weights(c_cmp_pe_k[i], c_cmp_w1_k[i], c_cmp_w2_k[i])
            cw_v = _compress_weights(c_cmp_pe_v[i], c_cmp_w1_v[i], c_cmp_w2_v[i])
            o_p, st_p = _c_layer(xp, bp, tp, 0, None, w_in, cw_k, cw_v, tabs_p, sample=False)
            past = (kct_c, vct_c, kst_c, vst_c, wink_t, winv_t, state_c_win_k[i], state_c_win_v[i], page_table, i)
            o_s, st_s = _c_layer(xs, bs, ts, past_len, past, w_in, cw_k, cw_v, tabs_s, sample=True)
            c_p.append(st_p)
            c_s.append(st_s)
            xp = _outproj_ln(xp, o_p, o_p, 0, 1, w_out, row1(ln1_g), row1(ln1_b), tm=PROMPT_TM)
            xs = _outproj_ln(xs, o_s, o_s, 0, 1, w_out, row1(ln1_g), row1(ln1_b), tm=bs * ts)
        wg, wu, wd = ffn_w_gate[layer].astype(BF16), ffn_w_up[layer].astype(BF16), ffn_w_down[layer].astype(BF16)
        xp = _ffn_ln(xp, wg, wu, wd, row1(ln2_g), row1(ln2_b), tm=PROMPT_TM, tf=FFN_TF)
        xs = _ffn_ln(xs, wg, wu, wd, row1(ln2_g), row1(ln2_b), tm=bs * ts, tf=FFN_TF)
    stk = lambda lst, j: jnp.stack([e[j] for e in lst], 0)
    return (xp.reshape(bp, tp, D_MODEL), xs.reshape(bs, ts, D_MODEL),
            stk(ab_p, 0), stk(ab_p, 1), stk(ab_p, 2), stk(ab_p, 3),
            stk(c_p, 0), stk(c_p, 1), stk(c_p, 2), stk(c_p, 3), stk(c_p, 4), stk(c_p, 5),
            stk(ab_s, 0), stk(ab_s, 1), stk(ab_s, 2), stk(ab_s, 3),
            stk(c_s, 0), stk(c_s, 1), stk(c_s, 2), stk(c_s, 3), stk(c_s, 4), stk(c_s, 5))
```

```python
import functools

import numpy as np
import jax
import jax.numpy as jnp
from jax import lax
from jax.experimental import pallas as pl
from jax.experimental.pallas import tpu as pltpu

D_MODEL = 1024
DEPTH = 4
PAGE_SIZE = 128
HEAD_DIM = 64
ROPE_THETA = 10000.0
A_HEADS = 8
A_KV_HEADS = 2
IDX_HEADS = 4
IDX_DIM = 64
A_TOPK_MAX = 256
A_WIDTH = A_HEADS * HEAD_DIM
B_WIDTH = D_MODEL // 2
CONV_W = 3
C_HEADS = 16
C_KV_HEADS = 4
C_WIDTH = C_HEADS * HEAD_DIM
CMP_BLOCK = 32
CMP_STRIDE = 16
SLC_BLOCK = 64
SLC_TOPN = 16
WINDOW = 512
D_FF = ((8 * D_MODEL + 3 * 256 - 1) // (3 * 256)) * 256
LN_EPS = 1e-5
ALPHA = (2 * DEPTH) ** 0.25
Q_BLOCK = 128

F32 = jnp.float32
BF16 = jnp.bfloat16
I32 = jnp.int32

LANES = 128
SUBLANES = 8
VMEM_LIMIT_BYTES = 56 * 1024 * 1024
MASKED = -1e30
Q_SCALE = HEAD_DIM ** -0.5 * float(np.log2(np.e))
INT_MIN = -(2 ** 31)
NEG_INF_KEY = int(np.int32(np.uint32(0xFF800000) ^ np.uint32(0x7FFFFFFF)))
NT_DIMS = (((1,), (1,)), ((), ()))

A_PERM = tuple(h for j in range(4) for h in (j, 4 + j))
C_PERM = tuple(h for pr in range(2) for j in range(4) for h in (8 * pr + j, 8 * pr + 4 + j))

AB_COLS = 2688
C_COLS = 2688
C_ROPED = C_WIDTH + 3 * C_KV_HEADS * HEAD_DIM


def _params(*sem):
    return pltpu.CompilerParams(dimension_semantics=sem, vmem_limit_bytes=VMEM_LIMIT_BYTES)


def _cdiv(a, b):
    return (a + b - 1) // b


def _rope128(r, c, s, first_half):
    sw = jnp.where(first_half, pltpu.roll(r, 96, 1), pltpu.roll(r, 32, 1))
    return r * c + sw * s


def _half_masked(q_bf16, mask):
    return jnp.where(mask, q_bf16.astype(F32), 0.0).astype(BF16)


def _flash(qs, heads, k_ref, v_ref, col, c_lo, n_chunks, ac, bias_fn, acc_ref, m_ref, l_ref):
    tq = qs.shape[0] // heads
    reps = ac // LANES
    acc_ref[...] = jnp.zeros(acc_ref.shape, F32)
    m_ref[...] = jnp.full(m_ref.shape, MASKED, F32)
    l_ref[...] = jnp.zeros(l_ref.shape, F32)

    def body(ci, carry):
        c = c_lo + ci
        off = pl.multiple_of(c * ac, ac)
        s = lax.dot_general(qs, k_ref[pl.ds(off, ac), col:col + LANES], NT_DIMS, preferred_element_type=F32)
        head_bias = bias_fn(c)
        ps, alphas = [], []
        for h in range(heads):
            rows = slice(h * tq, (h + 1) * tq)
            sh = s[rows] + head_bias(h)
            m_old = m_ref[rows]
            m_new = jnp.maximum(m_old, jnp.max(sh, axis=1, keepdims=True))
            alpha = jnp.exp2(m_old - m_new)
            p = jnp.exp2(sh - jnp.tile(m_new, (1, reps)))
            l_ref[rows] = alpha * l_ref[rows] + jnp.sum(p, axis=1, keepdims=True)
            m_ref[rows] = m_new
            ps.append(p.astype(BF16))
            alphas.append(alpha)
        pv = jnp.dot(jnp.concatenate(ps, 0), v_ref[pl.ds(off, ac), col:col + LANES], preferred_element_type=F32)
        for h in range(heads):
            rows = slice(h * tq, (h + 1) * tq)
            acc_ref[rows] = alphas[h] * acc_ref[rows] + pv[rows]
        return carry

    lax.fori_loop(0, n_chunks, body, 0)
    acc_ref[...] = jnp.where(m_ref[...] > 0.5 * MASKED, acc_ref[...] / l_ref[...], 0.0)


def _layernorm(y, g, b):
    mu = jnp.mean(y, axis=-1, keepdims=True)
    d = y - mu
    var = jnp.mean(d * d, axis=-1, keepdims=True)
    return d * lax.rsqrt(var + LN_EPS) * g + b


def _proj_ab_kernel(*refs, tm, seq_tiles, sample, dec_seq):
    if sample:
        (x_ref, w_ref, c_ref, s_ref, cx_ref, sx_ref, cw_ref, p1_ref, p2_ref,
         q_ref, iq_ref, k_ref, v_ref, ikw_ref, kvb_ref, ikb_ref, bo_ref, uo_ref, ubuf) = refs
        prev_refs = (None, p1_ref, p2_ref)
    else:
        (x_ref, w_ref, c_ref, s_ref, cx_ref, sx_ref, cw_ref,
         q_ref, iq_ref, k_ref, v_ref, ikw_ref, kvb_ref, ikb_ref, bo_ref, uo_ref, ubuf) = refs
    i = pl.program_id(0)
    y = jnp.dot(x_ref[...].astype(BF16), w_ref[...], preferred_element_type=F32)
    lane = lax.broadcasted_iota(I32, (tm, LANES), 1)
    first_half = (lane & (HEAD_DIM - 1)) < HEAD_DIM // 2
    c = c_ref[...]
    s = s_ref[...]
    ro = [_rope128(y[:, j * LANES:(j + 1) * LANES], c, s, first_half) for j in range(7)]
    ro.append(_rope128(y[:, 7 * LANES:8 * LANES], cx_ref[...], sx_ref[...], first_half))
    for j in range(4):
        q_ref[:, j * LANES:(j + 1) * LANES] = (ro[j] * Q_SCALE).astype(BF16)
    k = ro[4]
    v = y[:, 1024:1152]
    k_ref[...] = k
    v_ref[...] = v
    kvb_ref[:, 0:LANES] = k.astype(BF16)
    kvb_ref[:, LANES:2 * LANES] = v.astype(BF16)
    for j in range(2):
        iq_ref[:, j * LANES:(j + 1) * LANES] = (ro[5 + j] * IDX_DIM ** -0.5).astype(BF16)
    ikw = ro[7]
    ikw_ref[...] = ikw
    ikb_ref[...] = jnp.where(lane < IDX_DIM, ikw, pltpu.roll(ikw, IDX_DIM, 1)).astype(BF16)

    gate_b = y[:, 1152:1664]
    u = y[:, 1664:2176] * y[:, 2176:2688]

    @pl.when(i % seq_tiles == 0)
    def _():
        ubuf[0:SUBLANES, :] = jnp.zeros((SUBLANES, B_WIDTH), F32)

    @pl.when(i % seq_tiles != 0)
    def _():
        ubuf[0:SUBLANES, :] = ubuf[tm:tm + SUBLANES, :]

    ubuf[SUBLANES:tm + SUBLANES, :] = u
    cw = cw_ref[...]
    conv = u * cw[CONV_W - 1:CONV_W, :]
    if sample:
        t = lax.broadcasted_iota(I32, (tm, 1), 0) % dec_seq
    for d in range(1, CONV_W):
        ud = ubuf[SUBLANES - d:tm + SUBLANES - d, :]
        if sample:
            ud = jnp.where(t >= d, ud, prev_refs[d][...])
        conv = conv + ud * cw[CONV_W - 1 - d:CONV_W - d, :]
    bo_ref[...] = (gate_b * conv).astype(BF16)
    if sample:
        uo_ref[...] = u
    else:
        uo_ref[...] = u[tm - SUBLANES:tm, :]


def _proj_ab(x2d, w, tabs, conv_w8, *, tm, seq_tiles, sample, prevs=None, dec_seq=1):
    m = x2d.shape[0]
    nt = tabs[0].shape[0] // tm
    row = lambda i: (i, 0)
    const = lambda i: (0, 0)
    tab = lambda i: (i % nt, 0)
    in_specs = [pl.BlockSpec((tm, D_MODEL), row), pl.BlockSpec((D_MODEL, AB_COLS), const)]
    in_specs += [pl.BlockSpec((tm, LANES), tab)] * 4
    in_specs += [pl.BlockSpec((SUBLANES, B_WIDTH), const)]
    args = [x2d, w, *tabs, conv_w8]
    if sample:
        in_specs += [pl.BlockSpec((tm, B_WIDTH), row)] * 2
        args += list(prevs)
    u_rows = tm if sample else SUBLANES
    out_shape = [
        jax.ShapeDtypeStruct((m, A_WIDTH), BF16),
        jax.ShapeDtypeStruct((m, IDX_HEADS * IDX_DIM), BF16),
        jax.ShapeDtypeStruct((m, LANES), F32),
        jax.ShapeDtypeStruct((m, LANES), F32),
        jax.ShapeDtypeStruct((m, LANES), F32),
        jax.ShapeDtypeStruct((m, 2 * LANES), BF16),
        jax.ShapeDtypeStruct((m, LANES), BF16),
        jax.ShapeDtypeStruct((m, B_WIDTH), BF16),
        jax.ShapeDtypeStruct((m // tm * u_rows, B_WIDTH), F32),
    ]
    out_specs = [
        pl.BlockSpec((tm, A_WIDTH), row), pl.BlockSpec((tm, IDX_HEADS * IDX_DIM), row),
        pl.BlockSpec((tm, LANES), row), pl.BlockSpec((tm, LANES), row), pl.BlockSpec((tm, LANES), row),
        pl.BlockSpec((tm, 2 * LANES), row), pl.BlockSpec((tm, LANES), row),
        pl.BlockSpec((tm, B_WIDTH), row), pl.BlockSpec((u_rows, B_WIDTH), row),
    ]
    kern = functools.partial(_proj_ab_kernel, tm=tm, seq_tiles=seq_tiles, sample=sample, dec_seq=dec_seq)
    return pl.pallas_call(
        kern, grid=(m // tm,), in_specs=in_specs, out_specs=out_specs, out_shape=out_shape,
        scratch_shapes=[pltpu.VMEM((tm + SUBLANES, B_WIDTH), F32)],
        compiler_params=_params("arbitrary"))(*args)


def _count(key_ref, n_chunks, sc, nq, keys_on_sublanes, pred):
    def chunks(body, init):
        if isinstance(n_chunks, int):
            acc = init
            for c in range(n_chunks):
                acc = body(c, acc)
            return acc
        return lax.fori_loop(0, n_chunks, body, init)

    if keys_on_sublanes:
        def body(c, acc):
            part = jnp.where(pred(key_ref[c], c), 1.0, 0.0)
            rows = sc
            while rows > SUBLANES:
                rows //= 2
                part = part[:rows] + part[rows:2 * rows]
            return acc + part
        assert sc % SUBLANES == 0 and (sc // SUBLANES) & (sc // SUBLANES - 1) == 0
        return jnp.sum(chunks(body, jnp.zeros((SUBLANES, nq), F32)), axis=0, keepdims=True)

    def body(c, acc):
        m = jnp.where(pred(key_ref[c], c), 1.0, 0.0)
        part = m[:, 0:LANES]
        for j in range(1, sc // LANES):
            part = part + m[:, j * LANES:(j + 1) * LANES]
        return acc + part
    return jnp.sum(chunks(body, jnp.zeros((nq, LANES), F32)), axis=1, keepdims=True)


def _topk_bias(key_ref, bias_ref, n_sc, sc, ac, nq, topk, klen, keys_on_sublanes=False):
    ratio = sc // ac
    kshape, kaxis, vshape = ((sc, nq), 0, (1, nq)) if keys_on_sublanes else ((nq, sc), 1, (nq, 1))
    kiota = lax.broadcasted_iota(I32, kshape, kaxis)
    count = functools.partial(_count, key_ref, n_sc, sc, nq, keys_on_sublanes)
    kf = jnp.float32(topk)
    n_all = jnp.asarray(n_sc * sc, F32)

    def descend(n_static):
        count_n = functools.partial(_count, key_ref, n_static, sc, nq, keys_on_sublanes)

        def bit_body(b, carry):
            thr, n_ge = carry
            cand = thr + jnp.left_shift(jnp.int32(1), 31 - b)
            cnt = count_n(lambda key, c: key >= cand)
            take = cnt >= kf
            return jnp.where(take, cand, thr), jnp.where(take, cnt, n_ge)
        return lax.fori_loop(0, 32, bit_body, (jnp.full(vshape, INT_MIN, I32), jnp.full(vshape, n_all, F32)))

    if isinstance(n_sc, int):
        thr, n_ge = descend(n_sc)
    else:
        thr, n_ge = lax.switch(n_sc - 1, [functools.partial(descend, n) for n in range(1, klen // sc + 1)])

    def tie_search():
        need = kf - count(lambda key, c: key > thr)

        def tie_body(b, last):
            cand = last + jnp.left_shift(jnp.int32(1), klen.bit_length() - 1 - b)
            cnt = count(lambda key, c: (key == thr) & (c * sc + kiota < cand))
            return jnp.where(cnt < need, cand, last)
        return lax.fori_loop(0, klen.bit_length(), tie_body, jnp.zeros(vshape, I32))

    tied = (n_ge > kf) & (thr > jnp.int32(NEG_INF_KEY))
    last = lax.cond(jnp.max(jnp.where(tied, 1.0, 0.0)) > 0.5, tie_search, lambda: jnp.full(vshape, klen, I32))

    def bias_body(c, carry):
        key = key_ref[c]
        sel = (key > thr) | ((key == thr) & (c * sc + kiota <= last))
        sel = sel & (key > jnp.int32(NEG_INF_KEY))
        bias = jnp.where(sel, 0.0, MASKED)
        if keys_on_sublanes:
            for r in range(sc // LANES):
                blk = bias[r * LANES:(r + 1) * LANES, :].T
                bias_ref[c * ratio + r * LANES // ac, :, (r * LANES) % ac:(r * LANES) % ac + LANES] = blk
        else:
            for r in range(ratio):
                bias_ref[c * ratio + r] = bias[:, r * ac:(r + 1) * ac]
        return carry
    lax.fori_loop(0, n_sc, bias_body, 0)


def _dsa_kernel(q_ref, iq_ref, ikw_ref, ikb_ref, kvb_ref, o_ref, key_ref, bias_ref, acc_ref, m_ref, l_ref,
                *, tq, klen, topk, sc, ac):
    i = pl.program_id(1)
    n_sc = _cdiv((i + 1) * tq, sc)
    ratio = sc // ac
    lane = lax.broadcasted_iota(I32, (tq, LANES), 1)
    lo_half = lane < HEAD_DIM
    hi_half = jnp.logical_not(lo_half)

    iqs = jnp.concatenate([_half_masked(iq_ref[:, (h // 2) * LANES:(h // 2 + 1) * LANES],
                                        lo_half if h % 2 == 0 else hi_half) for h in range(IDX_HEADS)], 0)
    iw_t = ikw_ref[...].T
    pos_l = i * tq + lax.broadcasted_iota(I32, (1, tq), 1)
    kidx = lax.broadcasted_iota(I32, (sc, tq), 0)

    def score_body(c, carry):
        off = pl.multiple_of(c * sc, sc)
        logits = lax.dot_general(ikb_ref[pl.ds(off, sc), :], iqs, NT_DIMS, preferred_element_type=F32)
        sco = jnp.zeros((sc, tq), F32)
        for h in range(IDX_HEADS):
            sco = sco + jnp.maximum(logits[:, h * tq:(h + 1) * tq], 0.0) * iw_t[IDX_DIM + h:IDX_DIM + h + 1, :]
        sco = jnp.where(sco == 0.0, 0.0, sco)
        bits = pltpu.bitcast(sco, I32)
        key = jnp.where(bits < 0, bits ^ jnp.int32(0x7FFFFFFF), bits)
        key_ref[c] = jnp.where(c * sc + kidx <= pos_l, key, jnp.int32(NEG_INF_KEY))
        return carry
    lax.fori_loop(0, n_sc, score_body, 0)
    _topk_bias(key_ref, bias_ref, n_sc, sc, ac, tq, topk, klen, keys_on_sublanes=True)

    half_slots = A_HEADS // 2
    qs = jnp.concatenate([_half_masked(q_ref[:, (h % half_slots) * LANES:(h % half_slots + 1) * LANES],
                                       lo_half if h < half_slots else hi_half) for h in range(A_HEADS)], 0)
    def shared_bias(c):
        bias = bias_ref[c]
        return lambda h: bias
    _flash(qs, A_HEADS, kvb_ref, kvb_ref.at[:, LANES:2 * LANES], 0, 0, n_sc * ratio, ac, shared_bias,
           acc_ref, m_ref, l_ref)
    for j in range(half_slots):
        o_lo = acc_ref[j * tq:(j + 1) * tq]
        o_hi = acc_ref[(half_slots + j) * tq:(half_slots + j + 1) * tq]
        o_ref[:, j * LANES:(j + 1) * LANES] = jnp.where(lo_half, o_lo, o_hi).astype(BF16)


def _dsa_attention(q, iq, ikw, ikb, kvb, *, tq, topk, sc, ac):
    nb, tlen, _ = q.shape
    klen = ikb.shape[1]
    qspec = lambda w: pl.BlockSpec((None, tq, w), lambda b, i: (b, i, 0))
    kspec = lambda w: pl.BlockSpec((None, klen, w), lambda b, i: (b, 0, 0))
    kern = functools.partial(_dsa_kernel, tq=tq, klen=klen, topk=topk, sc=sc, ac=ac)
    return pl.pallas_call(
        kern, grid=(nb, tlen // tq),
        in_specs=[qspec(A_WIDTH), qspec(IDX_HEADS * IDX_DIM), qspec(LANES), kspec(LANES), kspec(2 * LANES)],
        out_specs=qspec(A_WIDTH),
        out_shape=jax.ShapeDtypeStruct((nb, tlen, A_WIDTH), BF16),
        scratch_shapes=[pltpu.VMEM((klen // sc, sc, tq), I32), pltpu.VMEM((klen // ac, tq, ac), F32)]
        + [pltpu.VMEM((A_HEADS * tq, LANES), F32)] * 3,
        compiler_params=_params("parallel", "arbitrary"))(q, iq, ikw, ikb, kvb)


def _page_specs(shape, layer, seqs, n_pages):
    specs = []
    for s in range(seqs):
        for p in range(n_pages):
            specs.append(pl.BlockSpec((None, None) + shape,
                                      lambda i, pt, s=s, p=p: (layer, pt[i * seqs + s, p], 0, 0)))
    return specs


def _softmax_rows(s):
    m = jnp.max(s, axis=1, keepdims=True)
    p = jnp.exp2(s - m)
    return p, jnp.sum(p, axis=1, keepdims=True)


def _dsa_sample_kernel(pt_ref, *refs, seqs, n_pages, pos0, topk):
    del pt_ref
    np_all = seqs * n_pages
    qs_ref, iq_ref, ikw_ref, knew_ref, vnew_ref, iknew_ref = refs[:6]
    kt_pages = refs[6:6 + np_all]
    vt_pages = refs[6 + np_all:6 + 2 * np_all]
    ikt_pages = refs[6 + 2 * np_all:6 + 3 * np_all]
    o_ref, ktb, vtb, iktb, key_ref, bias_ref = refs[6 + 3 * np_all:]
    tq = SUBLANES
    past = n_pages * PAGE_SIZE
    klen = past + LANES
    for s in range(seqs):
        for p in range(n_pages):
            cols = slice(p * PAGE_SIZE, (p + 1) * PAGE_SIZE)
            ktb[s, :, cols] = kt_pages[s * n_pages + p][...].astype(BF16)
            vtb[s, :, cols] = vt_pages[s * n_pages + p][...].astype(BF16)
            iktb[s, :, cols] = ikt_pages[s * n_pages + p][...].astype(BF16)

    pos = pos0 + lax.broadcasted_iota(I32, (tq, 1), 0)
    kidx = lax.broadcasted_iota(I32, (tq, klen), 1)
    for s in range(seqs):
        iq = iq_ref[s]
        logits = jnp.concatenate(
            [jnp.dot(iq, iktb[s], preferred_element_type=F32),
             lax.dot_general(iq, iknew_ref[s], NT_DIMS, preferred_element_type=F32)], 1)
        sco = jnp.zeros((tq, klen), F32)
        for h in range(IDX_HEADS):
            sco = sco + jnp.maximum(logits[h * tq:(h + 1) * tq], 0.0) * ikw_ref[s, :, IDX_DIM + h:IDX_DIM + h + 1]
        sco = jnp.where(sco == 0.0, 0.0, sco)
        bits = pltpu.bitcast(sco, I32)
        key = jnp.where(bits < 0, bits ^ jnp.int32(0x7FFFFFFF), bits)
        key_ref[0, s * tq:(s + 1) * tq, :] = jnp.where(kidx <= pos, key, jnp.int32(NEG_INF_KEY))
    _topk_bias(key_ref, bias_ref, 1, klen, klen, seqs * tq, topk, klen)

    lane = lax.broadcasted_iota(I32, (tq, LANES), 1)
    lo_half = lane < HEAD_DIM
    half_slots = A_HEADS // 2
    for s in range(seqs):
        qs = qs_ref[s]
        sc_all = jnp.concatenate(
            [jnp.dot(qs, ktb[s], preferred_element_type=F32),
             lax.dot_general(qs, knew_ref[s], NT_DIMS, preferred_element_type=F32)], 1)
        p, l = _softmax_rows(sc_all + jnp.tile(bias_ref[0, s * tq:(s + 1) * tq, :], (A_HEADS, 1)))
        pb = p.astype(BF16)
        o = (lax.dot_general(pb[:, :past], vtb[s], NT_DIMS, preferred_element_type=F32)
             + jnp.dot(pb[:, past:], vnew_ref[s], preferred_element_type=F32)) / l
        for j in range(half_slots):
            o_ref[s, :, j * LANES:(j + 1) * LANES] = jnp.where(
                lo_half, o[j * tq:(j + 1) * tq], o[(half_slots + j) * tq:(half_slots + j + 1) * tq]).astype(BF16)


def _dsa_sample(qs, iq, ikw, knew, vnew, iknew, kt_cache, vt_cache, ikt_cache, page_table, layer, *, seqs, pos0,
                topk):
    nb = qs.shape[0]
    n_pages = page_table.shape[1]
    klen = n_pages * PAGE_SIZE + LANES
    tq = SUBLANES
    seq_spec = lambda a: pl.BlockSpec((seqs,) + a.shape[1:], lambda i, pt: (i, 0, 0))
    in_specs = [seq_spec(a) for a in (qs, iq, ikw, knew, vnew, iknew)]
    in_specs += _page_specs(kt_cache.shape[2:], layer, seqs, n_pages)
    in_specs += _page_specs(vt_cache.shape[2:], layer, seqs, n_pages)
    in_specs += _page_specs(ikt_cache.shape[2:], layer, seqs, n_pages)
    np_all = seqs * n_pages
    kvw = A_KV_HEADS * HEAD_DIM
    grid_spec = pltpu.PrefetchScalarGridSpec(
        num_scalar_prefetch=1, grid=(nb // seqs,), in_specs=in_specs,
        out_specs=pl.BlockSpec((seqs, tq, A_WIDTH), lambda i, pt: (i, 0, 0)),
        scratch_shapes=[pltpu.VMEM((seqs, kvw, n_pages * PAGE_SIZE), BF16),
                        pltpu.VMEM((seqs, kvw, n_pages * PAGE_SIZE), BF16),
                        pltpu.VMEM((seqs, IDX_DIM, n_pages * PAGE_SIZE), BF16),
                        pltpu.VMEM((1, seqs * tq, klen), I32), pltpu.VMEM((1, seqs * tq, klen), F32)])
    kern = functools.partial(_dsa_sample_kernel, seqs=seqs, n_pages=n_pages, pos0=pos0, topk=topk)
    return pl.pallas_call(
        kern, grid_spec=grid_spec, out_shape=jax.ShapeDtypeStruct((nb, tq, A_WIDTH), BF16),
        compiler_params=_params("arbitrary"))(
            page_table, qs, iq, ikw, knew, vnew, iknew,
            *([kt_cache] * np_all), *([vt_cache] * np_all), *([ikt_cache] * np_all))


def _gelu_tanh(x):
    return 0.5 * x * (1.0 + jnp.tanh(np.sqrt(2.0 / np.pi) * (x + 0.044715 * x * x * x)))


def _compress_sample_kernel(pt_ref, pet_ref, wab_ref, w2_ref, *refs, seqs, n_pages):
    del pt_ref
    pages = refs[:seqs * n_pages]
    o_ref, tok_ref = refs[seqs * n_pages:]
    width = C_KV_HEADS * HEAD_DIM
    for s in range(seqs):
        for p in range(n_pages):
            base = (s * n_pages + p) * PAGE_SIZE
            for c in range(width // LANES):
                blk = pages[s * n_pages + p][c * LANES:(c + 1) * LANES, :]
                tok_ref[c, base:base + PAGE_SIZE, :] = blk.T
    rows = seqs * n_pages * PAGE_SIZE // CMP_STRIDE
    half = CMP_BLOCK // 2
    ra = jnp.zeros((rows, width), F32)
    rb = jnp.zeros((rows, width), F32)
    for j in range(half):
        xj = jnp.concatenate([tok_ref[c, pl.ds(j, rows, stride=CMP_STRIDE), :] for c in range(width // LANES)], 1)
        ra = ra + jnp.dot((xj + pet_ref[j:j + 1, :]).astype(BF16), wab_ref[j], preferred_element_type=F32)
        rb = rb + jnp.dot((xj + pet_ref[half + j:half + j + 1, :]).astype(BF16), wab_ref[half + j],
                          preferred_element_type=F32)
    hid = _gelu_tanh(ra + pltpu.roll(rb, rows - 1, 0))
    out = jnp.dot(hid.astype(BF16), w2_ref[...], preferred_element_type=F32)
    o_ref[...] = out.reshape(o_ref.shape).astype(BF16)


def _compress_sample(cache_t, page_table, layer, pet, wab, w2, *, seqs):
    nb, n_pages = page_table.shape
    width = C_KV_HEADS * HEAD_DIM
    r = n_pages * PAGE_SIZE // CMP_STRIDE
    const2 = lambda i, pt: (0, 0)
    grid_spec = pltpu.PrefetchScalarGridSpec(
        num_scalar_prefetch=1, grid=(nb // seqs,),
        in_specs=[pl.BlockSpec(pet.shape, const2), pl.BlockSpec(wab.shape, lambda i, pt: (0, 0, 0)),
                  pl.BlockSpec(w2.shape, const2)] + _page_specs(cache_t.shape[2:], layer, seqs, n_pages),
        out_specs=pl.BlockSpec((seqs, r, width), lambda i, pt: (i, 0, 0)),
        scratch_shapes=[pltpu.VMEM((width // LANES, seqs * n_pages * PAGE_SIZE, LANES), F32)])
    return pl.pallas_call(
        functools.partial(_compress_sample_kernel, seqs=seqs, n_pages=n_pages), grid_spec=grid_spec,
        out_shape=jax.ShapeDtypeStruct((nb, r, width), BF16),
        compiler_params=_params("arbitrary"))(page_table, pet, wab, w2, *([cache_t] * (seqs * n_pages)))


def _nsa_sample_kernel(pt_ref, *refs, seqs, n_pages, pos0, nblk):
    del pt_ref
    np_all = seqs * n_pages
    (qs_ref, g_ref, kc_ref, vc_ref, ksn_ref, vsn_ref, kwn_ref, vwn_ref, wink_ref, winv_ref, c2st_ref,
     e_ref) = refs[:12]
    kst_pages = refs[12:12 + np_all]
    vst_pages = refs[12 + np_all:12 + 2 * np_all]
    o_ref, kst, vst, part_ref, bias_ref = refs[12 + 2 * np_all:]
    tq = SUBLANES
    hpg = C_HEADS // C_KV_HEADS
    nh = 2 * hpg
    npair = C_KV_HEADS // 2
    past = n_pages * PAGE_SIZE
    ls = past + LANES
    nbuf = wink_ref.shape[-1]
    lw = nbuf + LANES
    ncp = kc_ref.shape[1]
    for s in range(seqs):
        for p in range(n_pages):
            cols = slice(p * PAGE_SIZE, (p + 1) * PAGE_SIZE)
            kst[s, :, cols] = kst_pages[s * n_pages + p][...].astype(BF16)
            vst[s, :, cols] = vst_pages[s * n_pages + p][...].astype(BF16)

    rows = nh * tq
    t_row = pos0 + lax.broadcasted_iota(I32, (rows, 1), 0) % tq
    cmp_visible = lax.broadcasted_iota(I32, (rows, ncp), 1) * CMP_STRIDE + (CMP_BLOCK - 1) <= t_row
    lane = lax.broadcasted_iota(I32, (tq, LANES), 1)
    lo_half = lane < HEAD_DIM
    pair_cols = lambda pr: slice(pr * LANES, (pr + 1) * LANES)

    psums = []
    for s in range(seqs):
        for pr in range(npair):
            qp = qs_ref[s, pr * rows:(pr + 1) * rows, :]
            s_c = lax.dot_general(qp, kc_ref[s, :, pair_cols(pr)], NT_DIMS, preferred_element_type=F32)
            s_c = jnp.where(cmp_visible, s_c, -jnp.inf)
            m = jnp.max(s_c, axis=1, keepdims=True)
            m = jnp.where(m > -jnp.inf, m, 0.0)
            p = jnp.exp2(s_c - m)
            den = jnp.sum(p, axis=1, keepdims=True)
            p = p / jnp.where(den > 0.0, den, 1.0)
            part_ref[s, pr] = jnp.dot(p.astype(BF16), vc_ref[s, :, pair_cols(pr)], preferred_element_type=F32)
            for half in range(2):
                acc = p[half * hpg * tq:(half * hpg + 1) * tq]
                for h in range(1, hpg):
                    acc = acc + p[(half * hpg + h) * tq:(half * hpg + h + 1) * tq]
                psums.append(acc)

    ng_rows = seqs * C_KV_HEADS * tq
    sel_rows = _cdiv(ng_rows, LANES) * LANES
    psum = jnp.concatenate(psums + [jnp.zeros((sel_rows - ng_rows, ncp), F32)] * (sel_rows > ng_rows), 0)
    t_lane = pos0 + lax.broadcasted_iota(I32, (1, sel_rows), 1) % tq
    selm = _select_blocks(psum, c2st_ref[...], t_lane, nblk)[:ng_rows]
    tg = pos0 + lax.broadcasted_iota(I32, (ng_rows, 1), 0) % tq
    ex = jnp.dot(selm.astype(BF16), e_ref[...], preferred_element_type=F32)
    ok = (ex > 0.5) & (lax.broadcasted_iota(I32, (ng_rows, ls), 1) <= tg)
    bias_ref[...] = jnp.where(ok, 0.0, MASKED)

    wpos = jnp.concatenate([pos0 - nbuf + lax.broadcasted_iota(I32, (rows, nbuf), 1),
                            pos0 + lax.broadcasted_iota(I32, (rows, LANES), 1)], 1)
    wbias = jnp.where((wpos >= 0) & (wpos <= t_row) & (t_row - wpos < WINDOW), 0.0, MASKED)

    for s in range(seqs):
        for pr in range(npair):
            qp = qs_ref[s, pr * rows:(pr + 1) * rows, :]
            feat = slice(pr * LANES, (pr + 1) * LANES)
            gbase = (s * C_KV_HEADS + 2 * pr) * tq
            bias = jnp.concatenate([jnp.tile(bias_ref[gbase:gbase + tq, :], (hpg, 1)),
                                    jnp.tile(bias_ref[gbase + tq:gbase + 2 * tq, :], (hpg, 1))], 0)
            sc_s = jnp.concatenate(
                [jnp.dot(qp, kst[s, feat, :], preferred_element_type=F32),
                 lax.dot_general(qp, ksn_ref[s, :, pair_cols(pr)], NT_DIMS, preferred_element_type=F32)], 1)
            p, l = _softmax_rows(sc_s + bias)
            pb = p.astype(BF16)
            o_s = (lax.dot_general(pb[:, :past], vst[s, feat, :], NT_DIMS, preferred_element_type=F32)
                   + jnp.dot(pb[:, past:], vsn_ref[s, :, pair_cols(pr)], preferred_element_type=F32)) / l
            sc_w = jnp.concatenate(
                [jnp.dot(qp, wink_ref[s, feat, :].astype(BF16), preferred_element_type=F32),
                 lax.dot_general(qp, kwn_ref[s, :, pair_cols(pr)], NT_DIMS, preferred_element_type=F32)], 1)
            p, l = _softmax_rows(sc_w + wbias)
            pb = p.astype(BF16)
            o_w = (lax.dot_general(pb[:, :nbuf], winv_ref[s, feat, :].astype(BF16), NT_DIMS,
                                   preferred_element_type=F32)
                   + jnp.dot(pb[:, nbuf:], vwn_ref[s, :, pair_cols(pr)], preferred_element_type=F32)) / l

            def gate(branch):
                cols = [3 * (hpg * (2 * pr + h // hpg) + h % hpg) + branch for h in range(nh)]
                return jnp.concatenate([g_ref[s, :, c:c + 1] for c in cols], 0)
            o = gate(0) * part_ref[s, pr] + gate(1) * o_s + gate(2) * o_w
            for j in range(hpg):
                slot = pr * hpg + j
                o_ref[s, :, slot * LANES:(slot + 1) * LANES] = jnp.where(
                    lo_half, o[j * tq:(j + 1) * tq], o[(hpg + j) * tq:(hpg + j + 1) * tq]).astype(BF16)


def _nsa_sample(qs, gates, kcmp, vcmp, ksn, vsn, kwn, vwn, wink_t, winv_t, c2s, emat, kst_cache, vst_cache,
                page_table, layer, *, seqs, pos0, n_sblk):
    nb, n_pages = page_table.shape
    tq = SUBLANES
    width = C_KV_HEADS * HEAD_DIM
    past = n_pages * PAGE_SIZE
    seq_spec = lambda a: pl.BlockSpec((seqs,) + a.shape[1:], lambda i, pt: (i, 0, 0))
    win_spec = pl.BlockSpec((None, seqs) + wink_t.shape[2:], lambda i, pt: (layer, i, 0, 0))
    const2 = lambda i, pt: (0, 0)
    in_specs = [seq_spec(a) for a in (qs, gates, kcmp, vcmp, ksn, vsn, kwn, vwn)]
    in_specs += [win_spec, win_spec, pl.BlockSpec(c2s.shape, const2), pl.BlockSpec(emat.shape, const2)]
    in_specs += _page_specs(kst_cache.shape[2:], layer, seqs, n_pages)
    in_specs += _page_specs(vst_cache.shape[2:], layer, seqs, n_pages)
    np_all = seqs * n_pages
    rows = 2 * (C_HEADS // C_KV_HEADS) * tq
    grid_spec = pltpu.PrefetchScalarGridSpec(
        num_scalar_prefetch=1, grid=(nb // seqs,), in_specs=in_specs,
        out_specs=pl.BlockSpec((seqs, tq, C_WIDTH), lambda i, pt: (i, 0, 0)),
        scratch_shapes=[pltpu.VMEM((seqs, width, past), BF16), pltpu.VMEM((seqs, width, past), BF16),
                        pltpu.VMEM((seqs, C_KV_HEADS // 2, rows, LANES), F32),
                        pltpu.VMEM((seqs * C_KV_HEADS * tq, past + LANES), F32)])
    kern = functools.partial(_nsa_sample_kernel, seqs=seqs, n_pages=n_pages, pos0=pos0,
                             nblk=_cdiv(n_sblk, SUBLANES) * SUBLANES)
    return pl.pallas_call(
        kern, grid_spec=grid_spec, out_shape=jax.ShapeDtypeStruct((nb, tq, C_WIDTH), BF16),
        compiler_params=_params("arbitrary"))(
            page_table, qs, gates, kcmp, vcmp, ksn, vsn, kwn, vwn, wink_t, winv_t, c2s, emat,
            *([kst_cache] * np_all), *([vst_cache] * np_all))


def _proj_c_kernel(x_ref, w_ref, c_ref, s_ref, q_ref, kc_ref, ks_ref, kw_ref, vc_ref, vs_ref, vw_ref,
                   g_ref, kvb_ref, *, tm):
    y = jnp.dot(x_ref[...].astype(BF16), w_ref[...], preferred_element_type=F32)
    lane = lax.broadcasted_iota(I32, (tm, LANES), 1)
    first_half = (lane & (HEAD_DIM - 1)) < HEAD_DIM // 2
    c = c_ref[...]
    s = s_ref[...]
    ro = [_rope128(y[:, j * LANES:(j + 1) * LANES], c, s, first_half) for j in range(C_ROPED // LANES)]
    for j in range(8):
        q_ref[:, j * LANES:(j + 1) * LANES] = (ro[j] * Q_SCALE).astype(BF16)
    for n, ref in enumerate((kc_ref, ks_ref, kw_ref)):
        for j in range(2):
            ref[:, j * LANES:(j + 1) * LANES] = ro[8 + 2 * n + j]
    for n, ref in enumerate((vc_ref, vs_ref, vw_ref)):
        ref[...] = y[:, C_ROPED + 256 * n:C_ROPED + 256 * (n + 1)]
    g = y[:, C_ROPED + 768:C_ROPED + 768 + LANES]
    g_ref[...] = 1.0 / (1.0 + jnp.exp(-g))
    for j in range(2):
        kvb_ref[:, j * LANES:(j + 1) * LANES] = ro[10 + j].astype(BF16)
        kvb_ref[:, 512 + j * LANES:512 + (j + 1) * LANES] = ro[12 + j].astype(BF16)
    kvb_ref[:, 256:512] = y[:, C_ROPED + 256:C_ROPED + 512].astype(BF16)
    kvb_ref[:, 768:1024] = y[:, C_ROPED + 512:C_ROPED + 768].astype(BF16)


def _proj_c(x2d, w, tabs, *, tm):
    m = x2d.shape[0]
    nt = tabs[0].shape[0] // tm
    row = lambda i: (i, 0)
    kvw = C_KV_HEADS * HEAD_DIM
    out_shape = [jax.ShapeDtypeStruct((m, C_WIDTH), BF16)]
    out_shape += [jax.ShapeDtypeStruct((m, kvw), F32)] * 6
    out_shape += [jax.ShapeDtypeStruct((m, LANES), F32), jax.ShapeDtypeStruct((m, 4 * kvw), BF16)]
    out_specs = [pl.BlockSpec((tm, C_WIDTH), row)] + [pl.BlockSpec((tm, kvw), row)] * 6
    out_specs += [pl.BlockSpec((tm, LANES), row), pl.BlockSpec((tm, 4 * kvw), row)]
    return pl.pallas_call(
        functools.partial(_proj_c_kernel, tm=tm), grid=(m // tm,),
        in_specs=[pl.BlockSpec((tm, D_MODEL), row), pl.BlockSpec((D_MODEL, C_COLS), lambda i: (0, 0)),
                  pl.BlockSpec((tm, LANES), lambda i: (i % nt, 0)), pl.BlockSpec((tm, LANES), lambda i: (i % nt, 0))],
        out_specs=out_specs, out_shape=out_shape,
        compiler_params=_params("parallel"))(x2d, w, *tabs)


def _compress_kernel(z_ref, pea_ref, peb_ref, wa_ref, wb_ref, w2_ref, o_ref, *, rows):
    z = z_ref[...].reshape(rows, z_ref.shape[-1])
    ra = jnp.dot((z + pea_ref[...]).astype(BF16), wa_ref[...], preferred_element_type=F32)
    rb = jnp.dot((z + peb_ref[...]).astype(BF16), wb_ref[...], preferred_element_type=F32)
    hid = ra + pltpu.roll(rb, rows - 1, 0)
    hid = 0.5 * hid * (1.0 + jnp.tanh(np.sqrt(2.0 / np.pi) * (hid + 0.044715 * hid * hid * hid)))
    out = jnp.dot(hid.astype(BF16), w2_ref[...], preferred_element_type=F32)
    o_ref[...] = out.reshape(o_ref.shape).astype(BF16)


def _compress(z, pea, peb, wa, wb, w2, *, nb_step):
    nb, r, zw = z.shape
    kvw = C_KV_HEADS * HEAD_DIM
    const = lambda b: (0, 0)
    return pl.pallas_call(
        functools.partial(_compress_kernel, rows=nb_step * r), grid=(nb // nb_step,),
        in_specs=[pl.BlockSpec((nb_step, r, zw), lambda b: (b, 0, 0)),
                  pl.BlockSpec((1, zw), const), pl.BlockSpec((1, zw), const),
                  pl.BlockSpec((zw, kvw), const), pl.BlockSpec((zw, kvw), const), pl.BlockSpec((kvw, kvw), const)],
        out_specs=pl.BlockSpec((nb_step, r, kvw), lambda b: (b, 0, 0)),
        out_shape=jax.ShapeDtypeStruct((nb, r, kvw), BF16),
        compiler_params=_params("parallel"))(z, pea, peb, wa, wb, w2)


def _select_blocks(psum, c2st, t_lane, nblk):
    rows = psum.shape[0]
    p_hi = psum.astype(BF16)
    p_lo = (psum - p_hi.astype(F32)).astype(BF16)
    imp = (lax.dot_general(c2st, p_hi, NT_DIMS, preferred_element_type=F32)
           + lax.dot_general(c2st, p_lo, NT_DIMS, preferred_element_type=F32))[:nblk]
    blk = lax.broadcasted_iota(I32, (nblk, rows), 0)
    blk_f = blk.astype(F32)
    cur = t_lane // SLC_BLOCK
    forced = (blk == 0) | (blk == cur) | (blk == cur - 1)
    imp = jnp.where(forced, jnp.inf, imp)
    imp = jnp.where(blk * SLC_BLOCK <= t_lane, imp, -jnp.inf)

    def top_body(_, carry):
        val, sel = carry
        mx = jnp.max(val, axis=0, keepdims=True)
        first = jnp.min(jnp.where(val == mx, blk_f, float(LANES)), axis=0, keepdims=True)
        pick = blk_f == first
        sel = jnp.where(pick & (mx > -jnp.inf), 1.0, sel)
        return jnp.where(pick, -jnp.inf, val), sel
    _, sel = lax.fori_loop(0, SLC_TOPN, top_body, (imp, jnp.zeros((nblk, rows), F32)))
    sel = jnp.concatenate([sel, jnp.zeros((LANES - nblk, rows), F32)], 0)
    return jnp.concatenate([sel[:, j * LANES:(j + 1) * LANES].T for j in range(rows // LANES)], 0)


def _nsa_kernel(q_ref, g_ref, kc_ref, vc_ref, kvs_ref, kvw_ref, c2st_ref, e_ref, o_ref,
                bias_ref, part_ref, acc_ref, m_ref, l_ref,
                *, tq, ncp, ac, nblk):
    i = pl.program_id(1)
    t = i * tq + lax.broadcasted_iota(I32, (tq, 1), 0)
    lane = lax.broadcasted_iota(I32, (tq, LANES), 1)
    lo_half = lane < HEAD_DIM
    n_s = _cdiv((i + 1) * tq, ac)
    span = WINDOW + tq
    w_start = pl.multiple_of(jnp.maximum(i * tq - WINDOW, 0), tq)
    kvw = C_KV_HEADS * HEAD_DIM
    ks_ref = kvs_ref
    vs_ref = kvs_ref.at[:, kvw:2 * kvw]
    kw_ref = kvw_ref
    vw_ref = kvw_ref.at[:, kvw:2 * kvw]
    hi_half = jnp.logical_not(lo_half)
    cmp_visible = lax.broadcasted_iota(I32, (tq, ncp), 1) * CMP_STRIDE + (CMP_BLOCK - 1) <= t
    ng = C_KV_HEADS
    hpg = C_HEADS // C_KV_HEADS
    nh = 2 * hpg
    head_rows = lambda h: slice(h * tq, (h + 1) * tq)
    tg = jnp.concatenate([t] * ng, 0)
    kiota_sg = lax.broadcasted_iota(I32, (ng * tq, ac), 1)

    def gate(pr, h, branch):
        head = hpg * (2 * pr + h // hpg) + h % hpg
        return g_ref[:, 3 * head + branch:3 * head + branch + 1]

    wpos = w_start + lax.broadcasted_iota(I32, (tq, span), 1)
    wbias = jnp.tile(jnp.where((wpos <= t) & (t - wpos < WINDOW), 0.0, MASKED), (nh, 1))

    def stacked_q(pr):
        return jnp.concatenate(
            [_half_masked(q_ref[:, (pr * hpg + h % hpg) * LANES:(pr * hpg + h % hpg + 1) * LANES],
                          lo_half if h < hpg else hi_half) for h in range(nh)], 0)

    psums = []
    for pr in range(ng // 2):
        col = pr * LANES
        s_c = lax.dot_general(stacked_q(pr), kc_ref[:, col:col + LANES], NT_DIMS, preferred_element_type=F32)
        pair_sums = [jnp.zeros((tq, ncp), F32), jnp.zeros((tq, ncp), F32)]
        ps = []
        for h in range(nh):
            sh = jnp.where(cmp_visible, s_c[head_rows(h)], -jnp.inf)
            m = jnp.max(sh, axis=1, keepdims=True)
            m = jnp.where(m > -jnp.inf, m, 0.0)
            p = jnp.exp2(sh - m)
            den = jnp.sum(p, axis=1, keepdims=True)
            p = p / jnp.where(den > 0.0, den, 1.0)
            pair_sums[h // hpg] = pair_sums[h // hpg] + p
            ps.append(p.astype(BF16))
        part_ref[pr * nh * tq:(pr + 1) * nh * tq] = jnp.dot(jnp.concatenate(ps, 0), vc_ref[:, col:col + LANES],
                                                           preferred_element_type=F32)
        psums += pair_sums

    t_lane = i * tq + lax.broadcasted_iota(I32, (1, ng * tq), 1) % tq
    selb = _select_blocks(jnp.concatenate(psums, 0), c2st_ref[...], t_lane, nblk).astype(BF16)

    def bias_body(c, carry):
        ex = jnp.dot(selb, e_ref[c], preferred_element_type=F32)
        ok = (ex > 0.5) & (c * ac + kiota_sg <= tg)
        bias_ref[c] = jnp.where(ok, 0.0, MASKED)
        return carry
    lax.fori_loop(0, n_s, bias_body, 0)

    for pr in range(ng // 2):
        col = pr * LANES
        qs = stacked_q(pr)
        part = part_ref.at[pr * nh * tq:(pr + 1) * nh * tq]

        def selected_bias(c, pr=pr):
            return lambda h: bias_ref[c, (2 * pr + h // hpg) * tq:(2 * pr + h // hpg + 1) * tq, :]
        _flash(qs, nh, ks_ref, vs_ref, col, 0, n_s, ac, selected_bias, acc_ref, m_ref, l_ref)
        for h in range(nh):
            r = head_rows(h)
            part[r] = gate(pr, h, 0) * part[r] + gate(pr, h, 1) * acc_ref[r]
        s_w = lax.dot_general(qs, kw_ref[pl.ds(w_start, span), col:col + LANES], NT_DIMS,
                              preferred_element_type=F32)
        p_w, l_w = _softmax_rows(s_w + wbias)
        o_w = jnp.dot(p_w.astype(BF16), vw_ref[pl.ds(w_start, span), col:col + LANES],
                      preferred_element_type=F32) / l_w
        for j in range(hpg):
            slot = pr * hpg + j
            lo, hi = head_rows(j), head_rows(hpg + j)
            o_lo = part[lo] + gate(pr, j, 2) * o_w[lo]
            o_hi = part[hi] + gate(pr, hpg + j, 2) * o_w[hi]
            o_ref[:, slot * LANES:(slot + 1) * LANES] = jnp.where(lo_half, o_lo, o_hi).astype(BF16)


def _nsa_attention(q, gates, kcmp, vcmp, kvs, kvw, c2st, emat, *, tq, ac, n_sblk, s_blk=0, w_blk=0):
    nb, tlen, _ = q.shape
    ncp = kcmp.shape[1]
    ls, lw = kvs.shape[1], kvw.shape[1]
    assert WINDOW % tq == 0 and lw >= WINDOW + tq and (C_KV_HEADS * tq) % LANES == 0
    kvwid = C_KV_HEADS * HEAD_DIM
    qspec = lambda w: pl.BlockSpec((None, tq, w), lambda b, i: (b, i, 0))
    kspec = lambda n, w, blk=0: pl.BlockSpec((None, n, w), lambda b, i: (b, 0, blk))
    kern = functools.partial(_nsa_kernel, tq=tq, ncp=ncp, ac=ac, nblk=_cdiv(n_sblk, SUBLANES) * SUBLANES)
    return pl.pallas_call(
        kern, grid=(nb, tlen // tq),
        in_specs=[qspec(C_WIDTH), qspec(LANES), kspec(ncp, kvwid), kspec(ncp, kvwid),
                  kspec(ls, 2 * kvwid, s_blk), kspec(lw, 2 * kvwid, w_blk),
                  pl.BlockSpec((LANES, ncp), lambda b, i: (0, 0)),
                  pl.BlockSpec((ls // ac, LANES, ac), lambda b, i: (0, 0, 0))],
        out_specs=qspec(C_WIDTH),
        out_shape=jax.ShapeDtypeStruct((nb, tlen, C_WIDTH), BF16),
        scratch_shapes=[pltpu.VMEM((ls // ac, C_KV_HEADS * tq, ac), F32), pltpu.VMEM((C_HEADS * tq, LANES), F32)]
        + [pltpu.VMEM((2 * (C_HEADS // C_KV_HEADS) * tq, LANES), F32)] * 3,
        compiler_params=_params("parallel", "arbitrary"))(q, gates, kcmp, vcmp, kvs, kvw, c2st, emat)


def _outproj_ln_kernel(x_ref, a_ref, b_ref, wa_ref, wb_ref, g_ref, bt_ref, o_ref):
    y = ALPHA * x_ref[...]
    y = y + jnp.dot(a_ref[...], wa_ref[...], preferred_element_type=F32)
    y = y + jnp.dot(b_ref[...], wb_ref[...], preferred_element_type=F32)
    o_ref[...] = _layernorm(y, g_ref[...], bt_ref[...])


def _outproj_ln(x2d, a, b, a_blk, b_blk, w_out, g, bt, *, tm):
    m = x2d.shape[0]
    half = w_out.shape[0] // 2
    row = lambda i: (i, 0)
    const = lambda i: (0, 0)
    return pl.pallas_call(
        _outproj_ln_kernel, grid=(m // tm,),
        in_specs=[pl.BlockSpec((tm, D_MODEL), row),
                  pl.BlockSpec((tm, half), lambda i: (i, a_blk)), pl.BlockSpec((tm, half), lambda i: (i, b_blk)),
                  pl.BlockSpec((half, D_MODEL), lambda i: (0, 0)), pl.BlockSpec((half, D_MODEL), lambda i: (1, 0)),
                  pl.BlockSpec((1, D_MODEL), const), pl.BlockSpec((1, D_MODEL), const)],
        out_specs=pl.BlockSpec((tm, D_MODEL), row),
        out_shape=jax.ShapeDtypeStruct((m, D_MODEL), F32),
        compiler_params=_params("parallel"))(x2d, a, b, w_out, w_out, g, bt)


def _ffn_ln_kernel(x_ref, wg_ref, wu_ref, wd_ref, g_ref, bt_ref, o_ref, xb_ref, acc_ref):
    f = pl.program_id(1)

    @pl.when(f == 0)
    def _():
        xb_ref[...] = x_ref[...].astype(BF16)
        acc_ref[...] = jnp.zeros(acc_ref.shape, F32)

    xb = xb_ref[...]
    h = jnp.dot(xb, wg_ref[...], preferred_element_type=F32)
    u = jnp.dot(xb, wu_ref[...], preferred_element_type=F32)
    a = (h / (1.0 + jnp.exp(-h))) * u
    acc_ref[...] += jnp.dot(a.astype(BF16), wd_ref[...], preferred_element_type=F32)

    @pl.when(f == pl.num_programs(1) - 1)
    def _():
        o_ref[...] = _layernorm(ALPHA * x_ref[...] + acc_ref[...], g_ref[...], bt_ref[...])


def _ffn_ln(x2d, wg, wu, wd, g, bt, *, tm, tf):
    m = x2d.shape[0]
    return pl.pallas_call(
        _ffn_ln_kernel, grid=(m // tm, D_FF // tf),
        in_specs=[pl.BlockSpec((tm, D_MODEL), lambda i, f: (i, 0)),
                  pl.BlockSpec((D_MODEL, tf), lambda i, f: (0, f)), pl.BlockSpec((D_MODEL, tf), lambda i, f: (0, f)),
                  pl.BlockSpec((tf, D_MODEL), lambda i, f: (f, 0)),
                  pl.BlockSpec((1, D_MODEL), lambda i, f: (0, 0)), pl.BlockSpec((1, D_MODEL), lambda i, f: (0, 0))],
        out_specs=pl.BlockSpec((tm, D_MODEL), lambda i, f: (i, 0)),
        out_shape=jax.ShapeDtypeStruct((m, D_MODEL), F32),
        scratch_shapes=[pltpu.VMEM((tm, D_MODEL), BF16), pltpu.VMEM((tm, D_MODEL), F32)],
        compiler_params=_params("parallel", "arbitrary"))(x2d, wg, wu, wd, g, bt)


def _rope_tables(pos):
    half = HEAD_DIM // 2
    inv = ROPE_THETA ** (-jnp.arange(half, dtype=F32) / half)
    ang = pos.astype(F32)[:, None] * inv[None, :]
    cos, sin = jnp.cos(ang), jnp.sin(ang)
    c64 = jnp.concatenate([cos, cos], 1)
    s64 = jnp.concatenate([-sin, sin], 1)
    c = jnp.concatenate([c64, c64], 1)
    s = jnp.concatenate([s64, s64], 1)
    cx = jnp.concatenate([c64, jnp.full_like(c64, IDX_HEADS ** -0.5)], 1)
    sx = jnp.concatenate([s64, jnp.zeros_like(s64)], 1)
    return c, s, cx, sx


def _perm_heads(w, perm):
    lead = w.shape[:-1]
    return w.reshape(*lead, len(perm), HEAD_DIM)[..., np.asarray(perm), :].reshape(*lead, len(perm) * HEAD_DIM)


def _ab_w_in(w):
    ab_sizes = (A_WIDTH, A_KV_HEADS * HEAD_DIM, A_KV_HEADS * HEAD_DIM, IDX_HEADS * IDX_DIM, IDX_DIM, IDX_HEADS,
                B_WIDTH, B_WIDTH, B_WIDTH)
    q, k, v, iq, ik, iw, gb, gc, h = jnp.split(w, np.cumsum(ab_sizes)[:-1].tolist(), axis=-1)
    pad = jnp.zeros((w.shape[0], LANES - IDX_DIM - IDX_HEADS), w.dtype)
    return jnp.concatenate([_perm_heads(q, A_PERM), k, iq, ik, iw, pad, v, gb, gc, h], -1).astype(BF16)


def _c_w_in(w):
    kvw = C_KV_HEADS * HEAD_DIM
    c_sizes = (C_WIDTH,) + (kvw,) * 6 + (3 * C_HEADS,)
    q, kc, vc, ks, vs, kw, vw, g = jnp.split(w, np.cumsum(c_sizes)[:-1].tolist(), axis=-1)
    pad = jnp.zeros((w.shape[0], LANES - 3 * C_HEADS), w.dtype)
    return jnp.concatenate([_perm_heads(q, C_PERM), kc, ks, kw, vc, vs, vw, g, pad], -1).astype(BF16)


def _perm_rows(w_out, perm):
    return w_out.reshape(len(perm), HEAD_DIM, w_out.shape[-1])[np.asarray(perm)].reshape(-1, w_out.shape[-1])


def _block_diag(w, n):
    eye = jnp.eye(n, dtype=w.dtype)
    out = jnp.einsum('gh,...ab->...gahb', eye, w)
    return out.reshape(*w.shape[:-2], n * w.shape[-2], n * w.shape[-1])


def _compress_weights(pe, w1, w2):
    g = C_KV_HEADS
    half = CMP_BLOCK // 2
    bd = _block_diag(w1, g)
    wa = bd[:half].reshape(half * g * HEAD_DIM, g * HEAD_DIM).astype(BF16)
    wb = bd[half:].reshape(half * g * HEAD_DIM, g * HEAD_DIM).astype(BF16)
    pet = jnp.tile(pe, (1, g))
    pea = pet[:half].reshape(1, -1)
    peb = pet[half:].reshape(1, -1)
    return pea, peb, wa, wb, _block_diag(w2, g).astype(BF16)


def _slc_from_cmp(ncp, n_cmp, n_sblk):
    n = np.arange(ncp)[None, :]
    mblk = np.arange(LANES)[:, None]
    hit = ((n * CMP_STRIDE < mblk * SLC_BLOCK + SLC_BLOCK) & (n * CMP_STRIDE + CMP_BLOCK > mblk * SLC_BLOCK)
           & (n < n_cmp) & (mblk < n_sblk))
    return jnp.asarray(hit, BF16)


def _expand_matrix(ls, ac):
    k = np.arange(ls)[None, :]
    mblk = np.arange(LANES)[:, None]
    e = (k // SLC_BLOCK == mblk).astype(np.float32)
    return jnp.asarray(e.reshape(LANES, ls // ac, ac).transpose(1, 0, 2), BF16)


def _pad_rows(a, n):
    return jnp.pad(a, ((0, 0), (0, n - a.shape[1]), (0, 0)))


def _stack_heads(q3, slots):
    nb, tq, _ = q3.shape
    qt = q3.reshape(nb, tq, slots, LANES).transpose(0, 2, 1, 3)
    lo = jnp.arange(LANES) < HEAD_DIM
    zero = jnp.zeros((), q3.dtype)
    stacked = jnp.concatenate([jnp.where(lo, qt, zero), jnp.where(lo, zero, qt)], 1)
    return stacked.reshape(nb, 2 * slots * tq, LANES)


def _token_minor(cache):
    nd = cache.ndim
    t = jnp.transpose(cache, (0, 1) + tuple(range(3, nd)) + (2,))
    return t.reshape(t.shape[0], t.shape[1], -1, t.shape[-1])


PROMPT_TM = 512
PROMPT_AC = 512
SAMPLE_TQ = 8
FFN_TF = 1408
DSA_SAMPLE_SEQS = 4
NSA_SAMPLE_SEQS = 2
CMP_SAMPLE_SEQS = 4


def _ab_layer(x2d, nb, tlen, start, past, page_table, w_in, conv_w, tabs, *, sample):
    m = nb * tlen
    conv_w8 = jnp.pad(conv_w, ((0, SUBLANES - CONV_W), (0, 0)))
    if sample:
        kt_cache, vt_cache, ikt_cache, prev, layer = past
        tt = jnp.arange(tlen)
        prevs = []
        for d in range(1, CONV_W):
            idx = jnp.clip(CONV_W - 1 + tt - d, 0, CONV_W - 2)
            prevs.append(prev[:, idx].reshape(m, B_WIDTH))
        outs = _proj_ab(x2d, w_in, tabs, conv_w8, tm=m, seq_tiles=1, sample=True, prevs=prevs, dec_seq=tlen)
    else:
        outs = _proj_ab(x2d, w_in, tabs, conv_w8, tm=PROMPT_TM, seq_tiles=tlen // PROMPT_TM, sample=False)
    q, iq, k, v, ikw, kvb, ikb, b_out, uo = outs
    if sample:
        klen_real = page_table.shape[1] * PAGE_SIZE + tlen
        pad_q = lambda a: _pad_rows(a.reshape(nb, tlen, -1), SAMPLE_TQ)
        pad_k = lambda a: _pad_rows(a.reshape(nb, tlen, -1), LANES)
        kvb3 = kvb.reshape(nb, tlen, 2 * LANES)
        iq_heads = pad_q(iq).reshape(nb, SAMPLE_TQ, IDX_HEADS, IDX_DIM).transpose(0, 2, 1, 3)
        a_out = _dsa_sample(_stack_heads(pad_q(q), A_HEADS // 2), iq_heads.reshape(nb, IDX_HEADS * SAMPLE_TQ, IDX_DIM),
                            pad_q(ikw), pad_k(kvb3[..., :LANES]), pad_k(kvb3[..., LANES:]),
                            pad_k(ikb.reshape(nb, tlen, LANES)[..., :IDX_DIM]),
                            kt_cache, vt_cache, ikt_cache, page_table, layer,
                            seqs=DSA_SAMPLE_SEQS, pos0=start, topk=min(A_TOPK_MAX, klen_real // 4))
        a_out = a_out[:, :tlen].reshape(m, A_WIDTH)
        new_conv = uo.reshape(nb, tlen, B_WIDTH)[:, tlen - (CONV_W - 1):]
    else:
        r3 = lambda a: a.reshape(nb, tlen, -1)
        a_out = _dsa_attention(r3(q), r3(iq), r3(ikw), r3(ikb), r3(kvb), tq=Q_BLOCK,
                               topk=min(A_TOPK_MAX, tlen // 4), sc=512, ac=PROMPT_AC)
        a_out = a_out.reshape(m, A_WIDTH)
        new_conv = uo.reshape(nb, tlen // PROMPT_TM, SUBLANES, B_WIDTH)[:, -1, SUBLANES - (CONV_W - 1):]
    state = (k.reshape(nb, tlen, A_KV_HEADS, HEAD_DIM), v.reshape(nb, tlen, A_KV_HEADS, HEAD_DIM),
             ikw[:, :IDX_DIM].reshape(nb, tlen, IDX_DIM), new_conv)
    return a_out, b_out, state


def _c_layer(x2d, nb, tlen, start, past, w_in, cw_k, cw_v, tabs, *, sample):
    m = nb * tlen
    kvw = C_KV_HEADS * HEAD_DIM
    tm = m if sample else PROMPT_TM
    q, kc, ks, kw, vc, vs, vw, gates, kvb = _proj_c(x2d, w_in, tabs[:2], tm=tm)
    if sample:
        kct, vct, kst, vst, wink_t, winv_t, buf_k, buf_v, page_table, layer = past
        past_len = page_table.shape[1] * PAGE_SIZE
        n_cmp = (past_len + tlen - CMP_BLOCK) // CMP_STRIDE + 1
        assert n_cmp <= past_len // CMP_STRIDE, "compression blocks must lie inside the cached rows"
        assert tlen <= LANES and start == past_len

        def tap_major(cw):
            pea, peb, wa, wb, w2 = cw
            pet = jnp.concatenate([pea.reshape(-1, kvw), peb.reshape(-1, kvw)], 0)
            wab = jnp.concatenate([wa.reshape(-1, kvw, kvw), wb.reshape(-1, kvw, kvw)], 0)
            return pet, wab, w2
        kcmp = _compress_sample(kct, page_table, layer, *tap_major(cw_k), seqs=CMP_SAMPLE_SEQS)
        vcmp = _compress_sample(vct, page_table, layer, *tap_major(cw_v), seqs=CMP_SAMPLE_SEQS)
        ncp = kcmp.shape[1]
        ls = past_len + LANES
        n_sblk = _cdiv(past_len + tlen, SLC_BLOCK)
        kvb3 = kvb.reshape(nb, tlen, 4 * kvw)
        new_rows = [_pad_rows(kvb3[..., n * kvw:(n + 1) * kvw], LANES) for n in range(4)]
        q3 = _pad_rows(q.reshape(nb, tlen, C_WIDTH), SAMPLE_TQ)
        half_w = C_WIDTH // 2
        qs = jnp.concatenate([_stack_heads(q3[..., :half_w], C_HEADS // 4),
                              _stack_heads(q3[..., half_w:], C_HEADS // 4)], 1)
        out = _nsa_sample(qs, _pad_rows(gates.reshape(nb, tlen, LANES), SAMPLE_TQ), kcmp, vcmp, *new_rows,
                          wink_t, winv_t, _slc_from_cmp(ncp, n_cmp, n_sblk), _expand_matrix(ls, ls)[0],
                          kst, vst, page_table, layer, seqs=NSA_SAMPLE_SEQS, pos0=start, n_sblk=n_sblk)
        out = out[:, :tlen].reshape(m, C_WIDTH)
        r4 = lambda a: a.reshape(nb, tlen, C_KV_HEADS, HEAD_DIM)
        win_k = jnp.concatenate([buf_k, r4(kw)], 1)[:, tlen:]
        win_v = jnp.concatenate([buf_v, r4(vw)], 1)[:, tlen:]
    else:
        zk = kc.reshape(nb, tlen // CMP_STRIDE, CMP_STRIDE * kvw)
        zv = vc.reshape(nb, tlen // CMP_STRIDE, CMP_STRIDE * kvw)
        n_cmp = (tlen - CMP_BLOCK) // CMP_STRIDE + 1
        kcmp = _compress(zk, *cw_k, nb_step=1)
        vcmp = _compress(zv, *cw_v, nb_step=1)
        ncp = kcmp.shape[1]
        n_sblk = _cdiv(tlen, SLC_BLOCK)
        kvb3 = kvb.reshape(nb, tlen, 4 * kvw)
        out = _nsa_attention(q.reshape(nb, tlen, C_WIDTH), gates.reshape(nb, tlen, LANES), kcmp, vcmp, kvb3, kvb3,
                             _slc_from_cmp(ncp, n_cmp, n_sblk), _expand_matrix(tlen, PROMPT_AC),
                             tq=Q_BLOCK, ac=PROMPT_AC, n_sblk=n_sblk, s_blk=0, w_blk=1)
        out = out.reshape(m, C_WIDTH)
        r4 = lambda a: a.reshape(nb, tlen, C_KV_HEADS, HEAD_DIM)
        keep = min(WINDOW, tlen)
        win_k = r4(kw)[:, tlen - keep:]
        win_v = r4(vw)[:, tlen - keep:]
    r4 = lambda a: a.reshape(nb, tlen, C_KV_HEADS, HEAD_DIM)
    return out, (r4(kc), r4(vc), r4(ks), r4(vs), win_k, win_v)


def kernel(x_prompt, x_sample, cache_a_k, cache_a_v, cache_a_ik, state_b_conv, cache_c_cmp_k, cache_c_cmp_v, cache_c_slc_k, cache_c_slc_v, state_c_win_k, state_c_win_v, page_table, ab_w_in, ab_conv_w, ab_w_out, c_w_in, c_cmp_pe_k, c_cmp_w1_k, c_cmp_w2_k, c_cmp_pe_v, c_cmp_w1_v, c_cmp_w2_v, c_w_out, ffn_w_gate, ffn_w_up, ffn_w_down, ln1_g, ln1_b, ln2_g, ln2_b):
    bp, tp, _ = x_prompt.shape
    bs, ts, _ = x_sample.shape
    past_len = page_table.shape[1] * PAGE_SIZE
    xp = x_prompt.reshape(bp * tp, D_MODEL)
    xs = x_sample.reshape(bs * ts, D_MODEL)
    tabs_p = _rope_tables(jnp.arange(tp))
    tabs_s = _rope_tables(past_len + jnp.arange(bs * ts) % ts)
    kt_a, vt_a, ikt_a = _token_minor(cache_a_k), _token_minor(cache_a_v), _token_minor(cache_a_ik)
    kct_c, vct_c = _token_minor(cache_c_cmp_k), _token_minor(cache_c_cmp_v)
    kst_c, vst_c = _token_minor(cache_c_slc_k), _token_minor(cache_c_slc_v)
    wink_t, winv_t = _token_minor(state_c_win_k), _token_minor(state_c_win_v)
    ab_p, ab_s, c_p, c_s = [], [], [], []
    for layer in range(DEPTH):
        i = layer // 2
        row1 = lambda a: a[layer].reshape(1, D_MODEL)
        if layer % 2 == 0:
            w_in = _ab_w_in(ab_w_in[i])
            w_out = jnp.concatenate([_perm_rows(ab_w_out[i][:A_WIDTH], A_PERM), ab_w_out[i][A_WIDTH:]], 0).astype(BF16)
            a_p, b_p, st_p = _ab_layer(xp, bp, tp, 0, None, None, w_in, ab_conv_w[i], tabs_p, sample=False)
            past = (kt_a, vt_a, ikt_a, state_b_conv[i], i)
            a_s, b_s, st_s = _ab_layer(xs, bs, ts, past_len, past, page_table, w_in, ab_conv_w[i], tabs_s, sample=True)
            ab_p.append(st_p)
            ab_s.append(st_s)
            xp = _outproj_ln(xp, a_p, b_p, 0, 0, w_out, row1(ln1_g), row1(ln1_b), tm=PROMPT_TM)
            xs = _outproj_ln(xs, a_s, b_s, 0, 0, w_out, row1(ln1_g), row1(ln1_b), tm=bs * ts)
        else:
            w_in = _c_w_in(c_w_in[i])
            w_out = _perm_rows(c_w_out[i], C_PERM).astype(BF16)
            cw_k = _compress_weights(c_cmp_pe_k[i], c_cmp_w1_k[i], c_cmp_w2_k[i])
            cw_v = _compress_weights(c_cmp_pe_v[i], c_cmp_w1_v[i], c_cmp_w2_v[i])
            o_p, st_p = _c_layer(xp, bp, tp, 0, None, w_in, cw_k, cw_v, tabs_p, sample=False)
            past = (kct_c, vct_c, kst_c, vst_c, wink_t, winv_t, state_c_win_k[i], state_c_win_v[i], page_table, i)
            o_s, st_s = _c_layer(xs, bs, ts, past_len, past, w_in, cw_k, cw_v, tabs_s, sample=True)
            c_p.append(st_p)
            c_s.append(st_s)
            xp = _outproj_ln(xp, o_p, o_p, 0, 1, w_out, row1(ln1_g), row1(ln1_b), tm=PROMPT_TM)
            xs = _outproj_ln(xs, o_s, o_s, 0, 1, w_out, row1(ln1_g), row1(ln1_b), tm=bs * ts)
        wg, wu, wd = ffn_w_gate[layer].astype(BF16), ffn_w_up[layer].astype(BF16), ffn_w_down[layer].astype(BF16)
        xp = _ffn_ln(xp, wg, wu, wd, row1(ln2_g), row1(ln2_b), tm=PROMPT_TM, tf=FFN_TF)
        xs = _ffn_ln(xs, wg, wu, wd, row1(ln2_g), row1(ln2_b), tm=bs * ts, tf=FFN_TF)
    stk = lambda lst, j: jnp.stack([e[j] for e in lst], 0)
    return (xp.reshape(bp, tp, D_MODEL), xs.reshape(bs, ts, D_MODEL),
            stk(ab_p, 0), stk(ab_p, 1), stk(ab_p, 2), stk(ab_p, 3),
            stk(c_p, 0), stk(c_p, 1), stk(c_p, 2), stk(c_p, 3), stk(c_p, 4), stk(c_p, 5),
            stk(ab_s, 0), stk(ab_s, 1), stk(ab_s, 2), stk(ab_s, 3),
            stk(c_s, 0), stk(c_s, 1), stk(c_s, 2), stk(c_s, 3), stk(c_s, 4), stk(c_s, 5))
```

```python
import functools

import numpy as np
import jax
import jax.numpy as jnp
from jax import lax
from jax.experimental import pallas as pl
from jax.experimental.pallas import tpu as pltpu

D_MODEL = 1024
DEPTH = 4
PAGE_SIZE = 128
HEAD_DIM = 64
ROPE_THETA = 10000.0
A_HEADS = 8
A_KV_HEADS = 2
IDX_HEADS = 4
IDX_DIM = 64
A_TOPK_MAX = 256
A_WIDTH = A_HEADS * HEAD_DIM
B_WIDTH = D_MODEL // 2
CONV_W = 3
C_HEADS = 16
C_KV_HEADS = 4
C_WIDTH = C_HEADS * HEAD_DIM
CMP_BLOCK = 32
CMP_STRIDE = 16
SLC_BLOCK = 64
SLC_TOPN = 16
WINDOW = 512
D_FF = ((8 * D_MODEL + 3 * 256 - 1) // (3 * 256)) * 256
LN_EPS = 1e-5
ALPHA = (2 * DEPTH) ** 0.25
Q_BLOCK = 128

F32 = jnp.float32
BF16 = jnp.bfloat16
I32 = jnp.int32

LANES = 128
SUBLANES = 8
VMEM_LIMIT_BYTES = 56 * 1024 * 1024
MASKED = -1e30
Q_SCALE = HEAD_DIM ** -0.5 * float(np.log2(np.e))
INT_MIN = -(2 ** 31)
NEG_INF_KEY = int(np.int32(np.uint32(0xFF800000) ^ np.uint32(0x7FFFFFFF)))
NT_DIMS = (((1,), (1,)), ((), ()))

A_PERM = tuple(h for j in range(4) for h in (j, 4 + j))
C_PERM = tuple(h for pr in range(2) for j in range(4) for h in (8 * pr + j, 8 * pr + 4 + j))

AB_COLS = 2688
C_COLS = 2688
C_ROPED = C_WIDTH + 3 * C_KV_HEADS * HEAD_DIM


def _params(*sem):
    return pltpu.CompilerParams(dimension_semantics=sem, vmem_limit_bytes=VMEM_LIMIT_BYTES)


def _cdiv(a, b):
    return (a + b - 1) // b


def _rope128(r, c, s, first_half):
    sw = jnp.where(first_half, pltpu.roll(r, 96, 1), pltpu.roll(r, 32, 1))
    return r * c + sw * s


def _half_masked(q_bf16, mask):
    return jnp.where(mask, q_bf16.astype(F32), 0.0).astype(BF16)


def _flash(qs, heads, k_ref, v_ref, col, c_lo, n_chunks, ac, bias_fn, acc_ref, m_ref, l_ref):
    tq = qs.shape[0] // heads
    reps = ac // LANES
    acc_ref[...] = jnp.zeros(acc_ref.shape, F32)
    m_ref[...] = jnp.full(m_ref.shape, MASKED, F32)
    l_ref[...] = jnp.zeros(l_ref.shape, F32)

    def body(ci, carry):
        c = c_lo + ci
        off = pl.multiple_of(c * ac, ac)
        s = lax.dot_general(qs, k_ref[pl.ds(off, ac), col:col + LANES], NT_DIMS, preferred_element_type=F32)
        head_bias = bias_fn(c)
        ps, alphas = [], []
        for h in range(heads):
            rows = slice(h * tq, (h + 1) * tq)
            sh = s[rows] + head_bias(h)
            m_old = m_ref[rows]
            m_new = jnp.maximum(m_old, jnp.max(sh, axis=1, keepdims=True))
            alpha = jnp.exp2(m_old - m_new)
            p = jnp.exp2(sh - jnp.tile(m_new, (1, reps)))
            l_ref[rows] = alpha * l_ref[rows] + jnp.sum(p, axis=1, keepdims=True)
            m_ref[rows] = m_new
            ps.append(p.astype(BF16))
            alphas.append(alpha)
        pv = jnp.dot(jnp.concatenate(ps, 0), v_ref[pl.ds(off, ac), col:col + LANES], preferred_element_type=F32)
        for h in range(heads):
            rows = slice(h * tq, (h + 1) * tq)
            acc_ref[rows] = alphas[h] * acc_ref[rows] + pv[rows]
        return carry

    lax.fori_loop(0, n_chunks, body, 0)
    acc_ref[...] = jnp.where(m_ref[...] > 0.5 * MASKED, acc_ref[...] / l_ref[...], 0.0)


def _flash_keys_on_sublanes(qs, heads, k_ref, vt_ref, n_chunks, ac, bias_fn, acc_ref):
    cols = qs.shape[0]
    acc_ref[...] = jnp.zeros(acc_ref.shape, F32)

    def body(c, carry):
        m, l = carry
        off = pl.multiple_of(c * ac, ac)
        s = lax.dot_general(k_ref[pl.ds(off, ac), 0:LANES], qs, NT_DIMS, preferred_element_type=F32)
        s = s + jnp.tile(bias_fn(c), (1, heads))
        m_new = jnp.maximum(m, jnp.max(s, axis=0, keepdims=True))
        alpha = jnp.exp2(m - m_new)
        p = jnp.exp2(s - m_new)
        l = alpha * l + jnp.sum(p, axis=0, keepdims=True)
        acc_ref[...] = alpha * acc_ref[...] + jnp.dot(vt_ref[c], p.astype(BF16), preferred_element_type=F32)
        return m_new, l

    m, l = lax.fori_loop(0, n_chunks, body, (jnp.full((1, cols), MASKED, F32), jnp.zeros((1, cols), F32)))
    acc_ref[...] = jnp.where(m > 0.5 * MASKED, acc_ref[...] / l, 0.0)


def _layernorm(y, g, b):
    mu = jnp.mean(y, axis=-1, keepdims=True)
    d = y - mu
    var = jnp.mean(d * d, axis=-1, keepdims=True)
    return d * lax.rsqrt(var + LN_EPS) * g + b


def _proj_ab_kernel(*refs, tm, seq_tiles, sample, dec_seq):
    if sample:
        (x_ref, w_ref, c_ref, s_ref, cx_ref, sx_ref, cw_ref, p1_ref, p2_ref,
         q_ref, iq_ref, k_ref, v_ref, ikw_ref, kvb_ref, ikb_ref, bo_ref, uo_ref, ubuf) = refs
        prev_refs = (None, p1_ref, p2_ref)
    else:
        (x_ref, w_ref, c_ref, s_ref, cx_ref, sx_ref, cw_ref,
         q_ref, iq_ref, k_ref, v_ref, ikw_ref, kvb_ref, ikb_ref, bo_ref, uo_ref, vtb_ref, ubuf) = refs
    i = pl.program_id(0)
    y = jnp.dot(x_ref[...].astype(BF16), w_ref[...], preferred_element_type=F32)
    lane = lax.broadcasted_iota(I32, (tm, LANES), 1)
    first_half = (lane & (HEAD_DIM - 1)) < HEAD_DIM // 2
    c = c_ref[...]
    s = s_ref[...]
    ro = [_rope128(y[:, j * LANES:(j + 1) * LANES], c, s, first_half) for j in range(7)]
    ro.append(_rope128(y[:, 7 * LANES:8 * LANES], cx_ref[...], sx_ref[...], first_half))
    for j in range(4):
        q_ref[:, j * LANES:(j + 1) * LANES] = (ro[j] * Q_SCALE).astype(BF16)
    k = ro[4]
    v = y[:, 1024:1152]
    k_ref[...] = k
    v_ref[...] = v
    kvb_ref[:, 0:LANES] = k.astype(BF16)
    kvb_ref[:, LANES:2 * LANES] = v.astype(BF16)
    if not sample:
        vtb_ref[...] = jnp.concatenate([v[r * LANES:(r + 1) * LANES].T for r in range(tm // LANES)],
                                       1).astype(BF16)
    for j in range(2):
        iq_ref[:, j * LANES:(j + 1) * LANES] = (ro[5 + j] * IDX_DIM ** -0.5).astype(BF16)
    ikw = ro[7]
    ikw_ref[...] = ikw
    ikb_ref[...] = jnp.where(lane < IDX_DIM, ikw, pltpu.roll(ikw, IDX_DIM, 1)).astype(BF16)

    gate_b = y[:, 1152:1664]
    u = y[:, 1664:2176] * y[:, 2176:2688]

    @pl.when(i % seq_tiles == 0)
    def _():
        ubuf[0:SUBLANES, :] = jnp.zeros((SUBLANES, B_WIDTH), F32)

    @pl.when(i % seq_tiles != 0)
    def _():
        ubuf[0:SUBLANES, :] = ubuf[tm:tm + SUBLANES, :]

    ubuf[SUBLANES:tm + SUBLANES, :] = u
    cw = cw_ref[...]
    conv = u * cw[CONV_W - 1:CONV_W, :]
    if sample:
        t = lax.broadcasted_iota(I32, (tm, 1), 0) % dec_seq
    for d in range(1, CONV_W):
        ud = ubuf[SUBLANES - d:tm + SUBLANES - d, :]
        if sample:
            ud = jnp.where(t >= d, ud, prev_refs[d][...])
        conv = conv + ud * cw[CONV_W - 1 - d:CONV_W - d, :]
    bo_ref[...] = (gate_b * conv).astype(BF16)
    if sample:
        uo_ref[...] = u
    else:
        uo_ref[...] = u[tm - SUBLANES:tm, :]


def _proj_ab(x2d, w, tabs, conv_w8, *, tm, seq_tiles, sample, prevs=None, dec_seq=1):
    m = x2d.shape[0]
    nt = tabs[0].shape[0] // tm
    row = lambda i: (i, 0)
    const = lambda i: (0, 0)
    tab = lambda i: (i % nt, 0)
    in_specs = [pl.BlockSpec((tm, D_MODEL), row), pl.BlockSpec((D_MODEL, AB_COLS), const)]
    in_specs += [pl.BlockSpec((tm, LANES), tab)] * 4
    in_specs += [pl.BlockSpec((SUBLANES, B_WIDTH), const)]
    args = [x2d, w, *tabs, conv_w8]
    if sample:
        in_specs += [pl.BlockSpec((tm, B_WIDTH), row)] * 2
        args += list(prevs)
    u_rows = tm if sample else SUBLANES
    out_shape = [
        jax.ShapeDtypeStruct((m, A_WIDTH), BF16),
        jax.ShapeDtypeStruct((m, IDX_HEADS * IDX_DIM), BF16),
        jax.ShapeDtypeStruct((m, LANES), F32),
        jax.ShapeDtypeStruct((m, LANES), F32),
        jax.ShapeDtypeStruct((m, LANES), F32),
        jax.ShapeDtypeStruct((m, 2 * LANES), BF16),
        jax.ShapeDtypeStruct((m, LANES), BF16),
        jax.ShapeDtypeStruct((m, B_WIDTH), BF16),
        jax.ShapeDtypeStruct((m // tm * u_rows, B_WIDTH), F32),
    ]
    out_specs = [
        pl.BlockSpec((tm, A_WIDTH), row), pl.BlockSpec((tm, IDX_HEADS * IDX_DIM), row),
        pl.BlockSpec((tm, LANES), row), pl.BlockSpec((tm, LANES), row), pl.BlockSpec((tm, LANES), row),
        pl.BlockSpec((tm, 2 * LANES), row), pl.BlockSpec((tm, LANES), row),
        pl.BlockSpec((tm, B_WIDTH), row), pl.BlockSpec((u_rows, B_WIDTH), row),
    ]
    if not sample:
        out_shape.append(jax.ShapeDtypeStruct((m // tm, LANES, tm), BF16))
        out_specs.append(pl.BlockSpec((None, LANES, tm), lambda i: (i, 0, 0)))
    kern = functools.partial(_proj_ab_kernel, tm=tm, seq_tiles=seq_tiles, sample=sample, dec_seq=dec_seq)
    return pl.pallas_call(
        kern, grid=(m // tm,), in_specs=in_specs, out_specs=out_specs, out_shape=out_shape,
        scratch_shapes=[pltpu.VMEM((tm + SUBLANES, B_WIDTH), F32)],
        compiler_params=_params("arbitrary"))(*args)


def _count(key_ref, n_chunks, sc, nq, keys_on_sublanes, pred):
    def chunks(body, init):
        if isinstance(n_chunks, int):
            acc = init
            for c in range(n_chunks):
                acc = body(c, acc)
            return acc
        return lax.fori_loop(0, n_chunks, body, init)

    if keys_on_sublanes:
        def body(c, acc):
            part = jnp.where(pred(key_ref[c], c), 1.0, 0.0)
            rows = sc
            while rows > SUBLANES:
                rows //= 2
                part = part[:rows] + part[rows:2 * rows]
            return acc + part
        assert sc % SUBLANES == 0 and (sc // SUBLANES) & (sc // SUBLANES - 1) == 0
        return jnp.sum(chunks(body, jnp.zeros((SUBLANES, nq), F32)), axis=0, keepdims=True)

    def body(c, acc):
        m = jnp.where(pred(key_ref[c], c), 1.0, 0.0)
        part = m[:, 0:LANES]
        for j in range(1, sc // LANES):
            part = part + m[:, j * LANES:(j + 1) * LANES]
        return acc + part
    return jnp.sum(chunks(body, jnp.zeros((nq, LANES), F32)), axis=1, keepdims=True)


def _topk_bias(key_ref, bias_ref, n_sc, sc, ac, nq, topk, klen, keys_on_sublanes=False):
    ratio = sc // ac
    kshape, kaxis, vshape = ((sc, nq), 0, (1, nq)) if keys_on_sublanes else ((nq, sc), 1, (nq, 1))
    kiota = lax.broadcasted_iota(I32, kshape, kaxis)
    count = functools.partial(_count, key_ref, n_sc, sc, nq, keys_on_sublanes)
    kf = jnp.float32(topk)
    n_all = jnp.asarray(n_sc * sc, F32)

    def descend(n_static):
        count_n = functools.partial(_count, key_ref, n_static, sc, nq, keys_on_sublanes)

        def bit_body(b, carry):
            thr, n_ge = carry
            cand = thr + jnp.left_shift(jnp.int32(1), 31 - b)
            cnt = count_n(lambda key, c: key >= cand)
            take = cnt >= kf
            return jnp.where(take, cand, thr), jnp.where(take, cnt, n_ge)
        return lax.fori_loop(0, 32, bit_body, (jnp.full(vshape, INT_MIN, I32), jnp.full(vshape, n_all, F32)))

    if isinstance(n_sc, int):
        thr, n_ge = descend(n_sc)
    else:
        thr, n_ge = lax.switch(n_sc - 1, [functools.partial(descend, n) for n in range(1, klen // sc + 1)])

    def tie_search():
        need = kf - count(lambda key, c: key > thr)

        def tie_body(b, last):
            cand = last + jnp.left_shift(jnp.int32(1), klen.bit_length() - 1 - b)
            cnt = count(lambda key, c: (key == thr) & (c * sc + kiota < cand))
            return jnp.where(cnt < need, cand, last)
        return lax.fori_loop(0, klen.bit_length(), tie_body, jnp.zeros(vshape, I32))

    tied = (n_ge > kf) & (thr > jnp.int32(NEG_INF_KEY))
    last = lax.cond(jnp.max(jnp.where(tied, 1.0, 0.0)) > 0.5, tie_search, lambda: jnp.full(vshape, klen, I32))

    def bias_body(c, carry):
        key = key_ref[c]
        sel = (key > thr) | ((key == thr) & (c * sc + kiota <= last))
        sel = sel & (key > jnp.int32(NEG_INF_KEY))
        bias = jnp.where(sel, 0.0, MASKED)
        if keys_on_sublanes:
            for r in range(ratio):
                bias_ref[c * ratio + r] = bias[r * ac:(r + 1) * ac, :]
        else:
            for r in range(ratio):
                bias_ref[c * ratio + r] = bias[:, r * ac:(r + 1) * ac]
        return carry
    lax.fori_loop(0, n_sc, bias_body, 0)


def _dsa_kernel(q_ref, iq_ref, ikw_ref, ikb_ref, kvb_ref, vt_ref, o_ref, key_ref, bias_ref, acc_ref,
                *, tq, klen, topk, sc, ac):
    i = pl.program_id(1)
    n_sc = _cdiv((i + 1) * tq, sc)
    ratio = sc // ac
    lane = lax.broadcasted_iota(I32, (tq, LANES), 1)
    lo_half = lane < HEAD_DIM
    hi_half = jnp.logical_not(lo_half)

    iqs = jnp.concatenate([_half_masked(iq_ref[:, (h // 2) * LANES:(h // 2 + 1) * LANES],
                                        lo_half if h % 2 == 0 else hi_half) for h in range(IDX_HEADS)], 0)
    iw_t = ikw_ref[...].T
    pos_l = i * tq + lax.broadcasted_iota(I32, (1, tq), 1)
    kidx = lax.broadcasted_iota(I32, (sc, tq), 0)

    def score_body(c, carry):
        off = pl.multiple_of(c * sc, sc)
        logits = lax.dot_general(ikb_ref[pl.ds(off, sc), :], iqs, NT_DIMS, preferred_element_type=F32)
        sco = jnp.zeros((sc, tq), F32)
        for h in range(IDX_HEADS):
            sco = sco + jnp.maximum(logits[:, h * tq:(h + 1) * tq], 0.0) * iw_t[IDX_DIM + h:IDX_DIM + h + 1, :]
        sco = jnp.where(sco == 0.0, 0.0, sco)
        bits = pltpu.bitcast(sco, I32)
        key = jnp.where(bits < 0, bits ^ jnp.int32(0x7FFFFFFF), bits)
        key_ref[c] = jnp.where(c * sc + kidx <= pos_l, key, jnp.int32(NEG_INF_KEY))
        return carry
    lax.fori_loop(0, n_sc, score_body, 0)
    _topk_bias(key_ref, bias_ref, n_sc, sc, ac, tq, topk, klen, keys_on_sublanes=True)

    half_slots = A_HEADS // 2
    qs = jnp.concatenate([_half_masked(q_ref[:, (h % half_slots) * LANES:(h % half_slots + 1) * LANES],
                                       lo_half if h < half_slots else hi_half) for h in range(A_HEADS)], 0)
    _flash_keys_on_sublanes(qs, A_HEADS, kvb_ref, vt_ref, n_sc * ratio, ac, lambda c: bias_ref[c], acc_ref)
    for j in range(half_slots):
        both = jnp.concatenate([acc_ref[0:HEAD_DIM, j * tq:(j + 1) * tq],
                                acc_ref[HEAD_DIM:LANES, (half_slots + j) * tq:(half_slots + j + 1) * tq]], 0)
        o_ref[:, j * LANES:(j + 1) * LANES] = both.T.astype(BF16)


def _dsa_attention(q, iq, ikw, ikb, kvb, vtb, *, tq, topk, sc, ac):
    nb, tlen, _ = q.shape
    klen = ikb.shape[1]
    assert tq == LANES and vtb.shape[1:] == (klen // ac, LANES, ac)
    qspec = lambda w: pl.BlockSpec((None, tq, w), lambda b, i: (b, i, 0))
    kspec = lambda w: pl.BlockSpec((None, klen, w), lambda b, i: (b, 0, 0))
    kern = functools.partial(_dsa_kernel, tq=tq, klen=klen, topk=topk, sc=sc, ac=ac)
    return pl.pallas_call(
        kern, grid=(nb, tlen // tq),
        in_specs=[qspec(A_WIDTH), qspec(IDX_HEADS * IDX_DIM), qspec(LANES), kspec(LANES), kspec(2 * LANES),
                  pl.BlockSpec((None, klen // ac, LANES, ac), lambda b, i: (b, 0, 0, 0))],
        out_specs=qspec(A_WIDTH),
        out_shape=jax.ShapeDtypeStruct((nb, tlen, A_WIDTH), BF16),
        scratch_shapes=[pltpu.VMEM((klen // sc, sc, tq), I32), pltpu.VMEM((klen // ac, ac, tq), F32),
                        pltpu.VMEM((LANES, A_HEADS * tq), F32)],
        compiler_params=_params("parallel", "arbitrary"))(q, iq, ikw, ikb, kvb, vtb)


def _page_specs(shape, layer, seqs, n_pages):
    specs = []
    for s in range(seqs):
        for p in range(n_pages):
            specs.append(pl.BlockSpec((None, None) + shape,
                                      lambda i, pt, s=s, p=p: (layer, pt[i * seqs + s, p], 0, 0)))
    return specs


def _softmax_rows(s):
    m = jnp.max(s, axis=1, keepdims=True)
    p = jnp.exp2(s - m)
    return p, jnp.sum(p, axis=1, keepdims=True)


def _dsa_sample_kernel(pt_ref, *refs, seqs, n_pages, pos0, topk):
    del pt_ref
    np_all = seqs * n_pages
    qs_ref, iq_ref, ikw_ref, knew_ref, vnew_ref, iknew_ref = refs[:6]
    kt_pages = refs[6:6 + np_all]
    vt_pages = refs[6 + np_all:6 + 2 * np_all]
    ikt_pages = refs[6 + 2 * np_all:6 + 3 * np_all]
    o_ref, ktb, vtb, iktb, key_ref, bias_ref = refs[6 + 3 * np_all:]
    tq = SUBLANES
    past = n_pages * PAGE_SIZE
    klen = past + LANES
    for s in range(seqs):
        for p in range(n_pages):
            cols = slice(p * PAGE_SIZE, (p + 1) * PAGE_SIZE)
            ktb[s, :, cols] = kt_pages[s * n_pages + p][...].astype(BF16)
            vtb[s, :, cols] = vt_pages[s * n_pages + p][...].astype(BF16)
            iktb[s, :, cols] = ikt_pages[s * n_pages + p][...].astype(BF16)

    pos = pos0 + lax.broadcasted_iota(I32, (tq, 1), 0)
    kidx = lax.broadcasted_iota(I32, (tq, klen), 1)
    for s in range(seqs):
        iq = iq_ref[s]
        logits = jnp.concatenate(
            [jnp.dot(iq, iktb[s], preferred_element_type=F32),
             lax.dot_general(iq, iknew_ref[s], NT_DIMS, preferred_element_type=F32)], 1)
        sco = jnp.zeros((tq, klen), F32)
        for h in range(IDX_HEADS):
            sco = sco + jnp.maximum(logits[h * tq:(h + 1) * tq], 0.0) * ikw_ref[s, :, IDX_DIM + h:IDX_DIM + h + 1]
        sco = jnp.where(sco == 0.0, 0.0, sco)
        bits = pltpu.bitcast(sco, I32)
        key = jnp.where(bits < 0, bits ^ jnp.int32(0x7FFFFFFF), bits)
        key_ref[0, s * tq:(s + 1) * tq, :] = jnp.where(kidx <= pos, key, jnp.int32(NEG_INF_KEY))
    _topk_bias(key_ref, bias_ref, 1, klen, klen, seqs * tq, topk, klen)

    lane = lax.broadcasted_iota(I32, (tq, LANES), 1)
    lo_half = lane < HEAD_DIM
    half_slots = A_HEADS // 2
    for s in range(seqs):
        qs = qs_ref[s]
        sc_all = jnp.concatenate(
            [jnp.dot(qs, ktb[s], preferred_element_type=F32),
             lax.dot_general(qs, knew_ref[s], NT_DIMS, preferred_element_type=F32)], 1)
        p, l = _softmax_rows(sc_all + jnp.tile(bias_ref[0, s * tq:(s + 1) * tq, :], (A_HEADS, 1)))
        pb = p.astype(BF16)
        o = (lax.dot_general(pb[:, :past], vtb[s], NT_DIMS, preferred_element_type=F32)
             + jnp.dot(pb[:, past:], vnew_ref[s], preferred_element_type=F32)) / l
        for j in range(half_slots):
            o_ref[s, :, j * LANES:(j + 1) * LANES] = jnp.where(
                lo_half, o[j * tq:(j + 1) * tq], o[(half_slots + j) * tq:(half_slots + j + 1) * tq]).astype(BF16)


def _dsa_sample(qs, iq, ikw, knew, vnew, iknew, kt_cache, vt_cache, ikt_cache, page_table, layer, *, seqs, pos0,
                topk):
    nb = qs.shape[0]
    n_pages = page_table.shape[1]
    klen = n_pages * PAGE_SIZE + LANES
    tq = SUBLANES
    seq_spec = lambda a: pl.BlockSpec((seqs,) + a.shape[1:], lambda i, pt: (i, 0, 0))
    in_specs = [seq_spec(a) for a in (qs, iq, ikw, knew, vnew, iknew)]
    in_specs += _page_specs(kt_cache.shape[2:], layer, seqs, n_pages)
    in_specs += _page_specs(vt_cache.shape[2:], layer, seqs, n_pages)
    in_specs += _page_specs(ikt_cache.shape[2:], layer, seqs, n_pages)
    np_all = seqs * n_pages
    kvw = A_KV_HEADS * HEAD_DIM
    grid_spec = pltpu.PrefetchScalarGridSpec(
        num_scalar_prefetch=1, grid=(nb // seqs,), in_specs=in_specs,
        out_specs=pl.BlockSpec((seqs, tq, A_WIDTH), lambda i, pt: (i, 0, 0)),
        scratch_shapes=[pltpu.VMEM((seqs, kvw, n_pages * PAGE_SIZE), BF16),
                        pltpu.VMEM((seqs, kvw, n_pages * PAGE_SIZE), BF16),
                        pltpu.VMEM((seqs, IDX_DIM, n_pages * PAGE_SIZE), BF16),
                        pltpu.VMEM((1, seqs * tq, klen), I32), pltpu.VMEM((1, seqs * tq, klen), F32)])
    kern = functools.partial(_dsa_sample_kernel, seqs=seqs, n_pages=n_pages, pos0=pos0, topk=topk)
    return pl.pallas_call(
        kern, grid_spec=grid_spec, out_shape=jax.ShapeDtypeStruct((nb, tq, A_WIDTH), BF16),
        compiler_params=_params("arbitrary"))(
            page_table, qs, iq, ikw, knew, vnew, iknew,
            *([kt_cache] * np_all), *([vt_cache] * np_all), *([ikt_cache] * np_all))


def _gelu_tanh(x):
    return 0.5 * x * (1.0 + jnp.tanh(np.sqrt(2.0 / np.pi) * (x + 0.044715 * x * x * x)))


def _compress_sample_kernel(pt_ref, pet_ref, wab_ref, w2_ref, *refs, seqs, n_pages):
    del pt_ref
    pages = refs[:seqs * n_pages]
    o_ref, tok_ref = refs[seqs * n_pages:]
    width = C_KV_HEADS * HEAD_DIM
    for s in range(seqs):
        for p in range(n_pages):
            base = (s * n_pages + p) * PAGE_SIZE
            for c in range(width // LANES):
                blk = pages[s * n_pages + p][c * LANES:(c + 1) * LANES, :]
                tok_ref[c, base:base + PAGE_SIZE, :] = blk.T
    rows = seqs * n_pages * PAGE_SIZE // CMP_STRIDE
    half = CMP_BLOCK // 2
    ra = jnp.zeros((rows, width), F32)
    rb = jnp.zeros((rows, width), F32)
    for j in range(half):
        xj = jnp.concatenate([tok_ref[c, pl.ds(j, rows, stride=CMP_STRIDE), :] for c in range(width // LANES)], 1)
        ra = ra + jnp.dot((xj + pet_ref[j:j + 1, :]).astype(BF16), wab_ref[j], preferred_element_type=F32)
        rb = rb + jnp.dot((xj + pet_ref[half + j:half + j + 1, :]).astype(BF16), wab_ref[half + j],
                          preferred_element_type=F32)
    hid = _gelu_tanh(ra + pltpu.roll(rb, rows - 1, 0))
    out = jnp.dot(hid.astype(BF16), w2_ref[...], preferred_element_type=F32)
    o_ref[...] = out.reshape(o_ref.shape).astype(BF16)


def _compress_sample(cache_t, page_table, layer, pet, wab, w2, *, seqs):
    nb, n_pages = page_table.shape
    width = C_KV_HEADS * HEAD_DIM
    r = n_pages * PAGE_SIZE // CMP_STRIDE
    const2 = lambda i, pt: (0, 0)
    grid_spec = pltpu.PrefetchScalarGridSpec(
        num_scalar_prefetch=1, grid=(nb // seqs,),
        in_specs=[pl.BlockSpec(pet.shape, const2), pl.BlockSpec(wab.shape, lambda i, pt: (0, 0, 0)),
                  pl.BlockSpec(w2.shape, const2)] + _page_specs(cache_t.shape[2:], layer, seqs, n_pages),
        out_specs=pl.BlockSpec((seqs, r, width), lambda i, pt: (i, 0, 0)),
        scratch_shapes=[pltpu.VMEM((width // LANES, seqs * n_pages * PAGE_SIZE, LANES), F32)])
    return pl.pallas_call(
        functools.partial(_compress_sample_kernel, seqs=seqs, n_pages=n_pages), grid_spec=grid_spec,
        out_shape=jax.ShapeDtypeStruct((nb, r, width), BF16),
        compiler_params=_params("arbitrary"))(page_table, pet, wab, w2, *([cache_t] * (seqs * n_pages)))


def _nsa_sample_kernel(pt_ref, *refs, seqs, n_pages, pos0, nblk):
    del pt_ref
    np_all = seqs * n_pages
    (qs_ref, g_ref, kc_ref, vc_ref, ksn_ref, vsn_ref, kwn_ref, vwn_ref, wink_ref, winv_ref, c2st_ref,
     e_ref) = refs[:12]
    kst_pages = refs[12:12 + np_all]
    vst_pages = refs[12 + np_all:12 + 2 * np_all]
    o_ref, kst, vst, part_ref, bias_ref = refs[12 + 2 * np_all:]
    tq = SUBLANES
    hpg = C_HEADS // C_KV_HEADS
    nh = 2 * hpg
    npair = C_KV_HEADS // 2
    past = n_pages * PAGE_SIZE
    ls = past + LANES
    nbuf = wink_ref.shape[-1]
    lw = nbuf + LANES
    ncp = kc_ref.shape[1]
    for s in range(seqs):
        for p in range(n_pages):
            cols = slice(p * PAGE_SIZE, (p + 1) * PAGE_SIZE)
            kst[s, :, cols] = kst_pages[s * n_pages + p][...].astype(BF16)
            vst[s, :, cols] = vst_pages[s * n_pages + p][...].astype(BF16)

    rows = nh * tq
    t_row = pos0 + lax.broadcasted_iota(I32, (rows, 1), 0) % tq
    cmp_visible = lax.broadcasted_iota(I32, (rows, ncp), 1) * CMP_STRIDE + (CMP_BLOCK - 1) <= t_row
    lane = lax.broadcasted_iota(I32, (tq, LANES), 1)
    lo_half = lane < HEAD_DIM
    pair_cols = lambda pr: slice(pr * LANES, (pr + 1) * LANES)

    psums = []
    for s in range(seqs):
        for pr in range(npair):
            qp = qs_ref[s, pr * rows:(pr + 1) * rows, :]
            s_c = lax.dot_general(qp, kc_ref[s, :, pair_cols(pr)], NT_DIMS, preferred_element_type=F32)
            s_c = jnp.where(cmp_visible, s_c, -jnp.inf)
            m = jnp.max(s_c, axis=1, keepdims=True)
            m = jnp.where(m > -jnp.inf, m, 0.0)
            p = jnp.exp2(s_c - m)
            den = jnp.sum(p, axis=1, keepdims=True)
            p = p / jnp.where(den > 0.0, den, 1.0)
            part_ref[s, pr] = jnp.dot(p.astype(BF16), vc_ref[s, :, pair_cols(pr)], preferred_element_type=F32)
            for half in range(2):
                acc = p[half * hpg * tq:(half * hpg + 1) * tq]
                for h in range(1, hpg):
                    acc = acc + p[(half * hpg + h) * tq:(half * hpg + h + 1) * tq]
                psums.append(acc)

    ng_rows = seqs * C_KV_HEADS * tq
    sel_rows = _cdiv(ng_rows, LANES) * LANES
    psum = jnp.concatenate(psums + [jnp.zeros((sel_rows - ng_rows, ncp), F32)] * (sel_rows > ng_rows), 0)
    t_lane = pos0 + lax.broadcasted_iota(I32, (1, sel_rows), 1) % tq
    selm = _select_blocks(psum, c2st_ref[...], t_lane, nblk)[:ng_rows]
    tg = pos0 + lax.broadcasted_iota(I32, (ng_rows, 1), 0) % tq
    ex = jnp.dot(selm.astype(BF16), e_ref[...], preferred_element_type=F32)
    ok = (ex > 0.5) & (lax.broadcasted_iota(I32, (ng_rows, ls), 1) <= tg)
    bias_ref[...] = jnp.where(ok, 0.0, MASKED)

    wpos = jnp.concatenate([pos0 - nbuf + lax.broadcasted_iota(I32, (rows, nbuf), 1),
                            pos0 + lax.broadcasted_iota(I32, (rows, LANES), 1)], 1)
    wbias = jnp.where((wpos >= 0) & (wpos <= t_row) & (t_row - wpos < WINDOW), 0.0, MASKED)

    for s in range(seqs):
        for pr in range(npair):
            qp = qs_ref[s, pr * rows:(pr + 1) * rows, :]
            feat = slice(pr * LANES, (pr + 1) * LANES)
            gbase = (s * C_KV_HEADS + 2 * pr) * tq
            bias = jnp.concatenate([jnp.tile(bias_ref[gbase:gbase + tq, :], (hpg, 1)),
                                    jnp.tile(bias_ref[gbase + tq:gbase + 2 * tq, :], (hpg, 1))], 0)
            sc_s = jnp.concatenate(
                [jnp.dot(qp, kst[s, feat, :], preferred_element_type=F32),
                 lax.dot_general(qp, ksn_ref[s, :, pair_cols(pr)], NT_DIMS, preferred_element_type=F32)], 1)
            p, l = _softmax_rows(sc_s + bias)
            pb = p.astype(BF16)
            o_s = (lax.dot_general(pb[:, :past], vst[s, feat, :], NT_DIMS, preferred_element_type=F32)
                   + jnp.dot(pb[:, past:], vsn_ref[s, :, pair_cols(pr)], preferred_element_type=F32)) / l
            sc_w = jnp.concatenate(
                [jnp.dot(qp, wink_ref[s, feat, :].astype(BF16), preferred_element_type=F32),
                 lax.dot_general(qp, kwn_ref[s, :, pair_cols(pr)], NT_DIMS, preferred_element_type=F32)], 1)
            p, l = _softmax_rows(sc_w + wbias)
            pb = p.astype(BF16)
            o_w = (lax.dot_general(pb[:, :nbuf], winv_ref[s, feat, :].astype(BF16), NT_DIMS,
                                   preferred_element_type=F32)
                   + jnp.dot(pb[:, nbuf:], vwn_ref[s, :, pair_cols(pr)], preferred_element_type=F32)) / l

            def gate(branch):
                cols = [3 * (hpg * (2 * pr + h // hpg) + h % hpg) + branch for h in range(nh)]
                return jnp.concatenate([g_ref[s, :, c:c + 1] for c in cols], 0)
            o = gate(0) * part_ref[s, pr] + gate(1) * o_s + gate(2) * o_w
            for j in range(hpg):
                slot = pr * hpg + j
                o_ref[s, :, slot * LANES:(slot + 1) * LANES] = jnp.where(
                    lo_half, o[j * tq:(j + 1) * tq], o[(hpg + j) * tq:(hpg + j + 1) * tq]).astype(BF16)


def _nsa_sample(qs, gates, kcmp, vcmp, ksn, vsn, kwn, vwn, wink_t, winv_t, c2s, emat, kst_cache, vst_cache,
                page_table, layer, *, seqs, pos0, n_sblk):
    nb, n_pages = page_table.shape
    tq = SUBLANES
    width = C_KV_HEADS * HEAD_DIM
    past = n_pages * PAGE_SIZE
    seq_spec = lambda a: pl.BlockSpec((seqs,) + a.shape[1:], lambda i, pt: (i, 0, 0))
    win_spec = pl.BlockSpec((None, seqs) + wink_t.shape[2:], lambda i, pt: (layer, i, 0, 0))
    const2 = lambda i, pt: (0, 0)
    in_specs = [seq_spec(a) for a in (qs, gates, kcmp, vcmp, ksn, vsn, kwn, vwn)]
    in_specs += [win_spec, win_spec, pl.BlockSpec(c2s.shape, const2), pl.BlockSpec(emat.shape, const2)]
    in_specs += _page_specs(kst_cache.shape[2:], layer, seqs, n_pages)
    in_specs += _page_specs(vst_cache.shape[2:], layer, seqs, n_pages)
    np_all = seqs * n_pages
    rows = 2 * (C_HEADS // C_KV_HEADS) * tq
    grid_spec = pltpu.PrefetchScalarGridSpec(
        num_scalar_prefetch=1, grid=(nb // seqs,), in_specs=in_specs,
        out_specs=pl.BlockSpec((seqs, tq, C_WIDTH), lambda i, pt: (i, 0, 0)),
        scratch_shapes=[pltpu.VMEM((seqs, width, past), BF16), pltpu.VMEM((seqs, width, past), BF16),
                        pltpu.VMEM((seqs, C_KV_HEADS // 2, rows, LANES), F32),
                        pltpu.VMEM((seqs * C_KV_HEADS * tq, past + LANES), F32)])
    kern = functools.partial(_nsa_sample_kernel, seqs=seqs, n_pages=n_pages, pos0=pos0,
                             nblk=_cdiv(n_sblk, SUBLANES) * SUBLANES)
    return pl.pallas_call(
        kern, grid_spec=grid_spec, out_shape=jax.ShapeDtypeStruct((nb, tq, C_WIDTH), BF16),
        compiler_params=_params("arbitrary"))(
            page_table, qs, gates, kcmp, vcmp, ksn, vsn, kwn, vwn, wink_t, winv_t, c2s, emat,
            *([kst_cache] * np_all), *([vst_cache] * np_all))


def _proj_c_kernel(x_ref, w_ref, c_ref, s_ref, q_ref, kc_ref, ks_ref, kw_ref, vc_ref, vs_ref, vw_ref,
                   g_ref, kvb_ref, *, tm):
    y = jnp.dot(x_ref[...].astype(BF16), w_ref[...], preferred_element_type=F32)
    lane = lax.broadcasted_iota(I32, (tm, LANES), 1)
    first_half = (lane & (HEAD_DIM - 1)) < HEAD_DIM // 2
    c = c_ref[...]
    s = s_ref[...]
    ro = [_rope128(y[:, j * LANES:(j + 1) * LANES], c, s, first_half) for j in range(C_ROPED // LANES)]
    for j in range(8):
        q_ref[:, j * LANES:(j + 1) * LANES] = (ro[j] * Q_SCALE).astype(BF16)
    for n, ref in enumerate((kc_ref, ks_ref, kw_ref)):
        for j in range(2):
            ref[:, j * LANES:(j + 1) * LANES] = ro[8 + 2 * n + j]
    for n, ref in enumerate((vc_ref, vs_ref, vw_ref)):
        ref[...] = y[:, C_ROPED + 256 * n:C_ROPED + 256 * (n + 1)]
    g = y[:, C_ROPED + 768:C_ROPED + 768 + LANES]
    g_ref[...] = 1.0 / (1.0 + jnp.exp(-g))
    for j in range(2):
        kvb_ref[:, j * LANES:(j + 1) * LANES] = ro[10 + j].astype(BF16)
        kvb_ref[:, 512 + j * LANES:512 + (j + 1) * LANES] = ro[12 + j].astype(BF16)
    kvb_ref[:, 256:512] = y[:, C_ROPED + 256:C_ROPED + 512].astype(BF16)
    kvb_ref[:, 768:1024] = y[:, C_ROPED + 512:C_ROPED + 768].astype(BF16)


def _proj_c(x2d, w, tabs, *, tm):
    m = x2d.shape[0]
    nt = tabs[0].shape[0] // tm
    row = lambda i: (i, 0)
    kvw = C_KV_HEADS * HEAD_DIM
    out_shape = [jax.ShapeDtypeStruct((m, C_WIDTH), BF16)]
    out_shape += [jax.ShapeDtypeStruct((m, kvw), F32)] * 6
    out_shape += [jax.ShapeDtypeStruct((m, LANES), F32), jax.ShapeDtypeStruct((m, 4 * kvw), BF16)]
    out_specs = [pl.BlockSpec((tm, C_WIDTH), row)] + [pl.BlockSpec((tm, kvw), row)] * 6
    out_specs += [pl.BlockSpec((tm, LANES), row), pl.BlockSpec((tm, 4 * kvw), row)]
    return pl.pallas_call(
        functools.partial(_proj_c_kernel, tm=tm), grid=(m // tm,),
        in_specs=[pl.BlockSpec((tm, D_MODEL), row), pl.BlockSpec((D_MODEL, C_COLS), lambda i: (0, 0)),
                  pl.BlockSpec((tm, LANES), lambda i: (i % nt, 0)), pl.BlockSpec((tm, LANES), lambda i: (i % nt, 0))],
        out_specs=out_specs, out_shape=out_shape,
        compiler_params=_params("parallel"))(x2d, w, *tabs)


def _compress_kernel(z_ref, pea_ref, peb_ref, wa_ref, wb_ref, w2_ref, o_ref, *, rows):
    z = z_ref[...].reshape(rows, z_ref.shape[-1])
    ra = jnp.dot((z + pea_ref[...]).astype(BF16), wa_ref[...], preferred_element_type=F32)
    rb = jnp.dot((z + peb_ref[...]).astype(BF16), wb_ref[...], preferred_element_type=F32)
    hid = ra + pltpu.roll(rb, rows - 1, 0)
    hid = 0.5 * hid * (1.0 + jnp.tanh(np.sqrt(2.0 / np.pi) * (hid + 0.044715 * hid * hid * hid)))
    out = jnp.dot(hid.astype(BF16), w2_ref[...], preferred_element_type=F32)
    o_ref[...] = out.reshape(o_ref.shape).astype(BF16)


def _compress(z, pea, peb, wa, wb, w2, *, nb_step):
    nb, r, zw = z.shape
    kvw = C_KV_HEADS * HEAD_DIM
    const = lambda b: (0, 0)
    return pl.pallas_call(
        functools.partial(_compress_kernel, rows=nb_step * r), grid=(nb // nb_step,),
        in_specs=[pl.BlockSpec((nb_step, r, zw), lambda b: (b, 0, 0)),
                  pl.BlockSpec((1, zw), const), pl.BlockSpec((1, zw), const),
                  pl.BlockSpec((zw, kvw), const), pl.BlockSpec((zw, kvw), const), pl.BlockSpec((kvw, kvw), const)],
        out_specs=pl.BlockSpec((nb_step, r, kvw), lambda b: (b, 0, 0)),
        out_shape=jax.ShapeDtypeStruct((nb, r, kvw), BF16),
        compiler_params=_params("parallel"))(z, pea, peb, wa, wb, w2)


def _select_blocks(psum, c2st, t_lane, nblk):
    rows = psum.shape[0]
    p_hi = psum.astype(BF16)
    p_lo = (psum - p_hi.astype(F32)).astype(BF16)
    imp = (lax.dot_general(c2st, p_hi, NT_DIMS, preferred_element_type=F32)
           + lax.dot_general(c2st, p_lo, NT_DIMS, preferred_element_type=F32))[:nblk]
    blk = lax.broadcasted_iota(I32, (nblk, rows), 0)
    blk_f = blk.astype(F32)
    cur = t_lane // SLC_BLOCK
    forced = (blk == 0) | (blk == cur) | (blk == cur - 1)
    imp = jnp.where(forced, jnp.inf, imp)
    imp = jnp.where(blk * SLC_BLOCK <= t_lane, imp, -jnp.inf)

    def top_body(_, carry):
        val, sel = carry
        mx = jnp.max(val, axis=0, keepdims=True)
        first = jnp.min(jnp.where(val == mx, blk_f, float(LANES)), axis=0, keepdims=True)
        pick = blk_f == first
        sel = jnp.where(pick & (mx > -jnp.inf), 1.0, sel)
        return jnp.where(pick, -jnp.inf, val), sel
    _, sel = lax.fori_loop(0, SLC_TOPN, top_body, (imp, jnp.zeros((nblk, rows), F32)))
    sel = jnp.concatenate([sel, jnp.zeros((LANES - nblk, rows), F32)], 0)
    return jnp.concatenate([sel[:, j * LANES:(j + 1) * LANES].T for j in range(rows // LANES)], 0)


def _nsa_kernel(q_ref, g_ref, kc_ref, vc_ref, kvs_ref, kvw_ref, c2st_ref, e_ref, o_ref,
                bias_ref, part_ref, acc_ref, m_ref, l_ref,
                *, tq, ncp, ac, nblk):
    i = pl.program_id(1)
    t = i * tq + lax.broadcasted_iota(I32, (tq, 1), 0)
    lane = lax.broadcasted_iota(I32, (tq, LANES), 1)
    lo_half = lane < HEAD_DIM
    n_s = _cdiv((i + 1) * tq, ac)
    span = WINDOW + tq
    w_start = pl.multiple_of(jnp.maximum(i * tq - WINDOW, 0), tq)
    kvw = C_KV_HEADS * HEAD_DIM
    ks_ref = kvs_ref
    vs_ref = kvs_ref.at[:, kvw:2 * kvw]
    kw_ref = kvw_ref
    vw_ref = kvw_ref.at[:, kvw:2 * kvw]
    hi_half = jnp.logical_not(lo_half)
    cmp_visible = lax.broadcasted_iota(I32, (tq, ncp), 1) * CMP_STRIDE + (CMP_BLOCK - 1) <= t
    ng = C_KV_HEADS
    hpg = C_HEADS // C_KV_HEADS
    nh = 2 * hpg
    head_rows = lambda h: slice(h * tq, (h + 1) * tq)
    tg = jnp.concatenate([t] * ng, 0)
    kiota_sg = lax.broadcasted_iota(I32, (ng * tq, ac), 1)

    def gate(pr, h, branch):
        head = hpg * (2 * pr + h // hpg) + h % hpg
        return g_ref[:, 3 * head + branch:3 * head + branch + 1]

    wpos = w_start + lax.broadcasted_iota(I32, (tq, span), 1)
    wbias = jnp.tile(jnp.where((wpos <= t) & (t - wpos < WINDOW), 0.0, MASKED), (nh, 1))

    def stacked_q(pr):
        return jnp.concatenate(
            [_half_masked(q_ref[:, (pr * hpg + h % hpg) * LANES:(pr * hpg + h % hpg + 1) * LANES],
                          lo_half if h < hpg else hi_half) for h in range(nh)], 0)

    psums = []
    for pr in range(ng // 2):
        col = pr * LANES
        s_c = lax.dot_general(stacked_q(pr), kc_ref[:, col:col + LANES], NT_DIMS, preferred_element_type=F32)
        pair_sums = [jnp.zeros((tq, ncp), F32), jnp.zeros((tq, ncp), F32)]
        ps = []
        for h in range(nh):
            sh = jnp.where(cmp_visible, s_c[head_rows(h)], -jnp.inf)
            m = jnp.max(sh, axis=1, keepdims=True)
            m = jnp.where(m > -jnp.inf, m, 0.0)
            p = jnp.exp2(sh - m)
            den = jnp.sum(p, axis=1, keepdims=True)
            p = p / jnp.where(den > 0.0, den, 1.0)
            pair_sums[h // hpg] = pair_sums[h // hpg] + p
            ps.append(p.astype(BF16))
        part_ref[pr * nh * tq:(pr + 1) * nh * tq] = jnp.dot(jnp.concatenate(ps, 0), vc_ref[:, col:col + LANES],
                                                           preferred_element_type=F32)
        psums += pair_sums

    t_lane = i * tq + lax.broadcasted_iota(I32, (1, ng * tq), 1) % tq
    selb = _select_blocks(jnp.concatenate(psums, 0), c2st_ref[...], t_lane, nblk).astype(BF16)

    def bias_body(c, carry):
        ex = jnp.dot(selb, e_ref[c], preferred_element_type=F32)
        ok = (ex > 0.5) & (c * ac + kiota_sg <= tg)
        bias_ref[c] = jnp.where(ok, 0.0, MASKED)
        return carry
    lax.fori_loop(0, n_s, bias_body, 0)

    for pr in range(ng // 2):
        col = pr * LANES
        qs = stacked_q(pr)
        part = part_ref.at[pr * nh * tq:(pr + 1) * nh * tq]

        def selected_bias(c, pr=pr):
            return lambda h: bias_ref[c, (2 * pr + h // hpg) * tq:(2 * pr + h // hpg + 1) * tq, :]
        _flash(qs, nh, ks_ref, vs_ref, col, 0, n_s, ac, selected_bias, acc_ref, m_ref, l_ref)
        for h in range(nh):
            r = head_rows(h)
            part[r] = gate(pr, h, 0) * part[r] + gate(pr, h, 1) * acc_ref[r]
        s_w = lax.dot_general(qs, kw_ref[pl.ds(w_start, span), col:col + LANES], NT_DIMS,
                              preferred_element_type=F32)
        p_w, l_w = _softmax_rows(s_w + wbias)
        o_w = jnp.dot(p_w.astype(BF16), vw_ref[pl.ds(w_start, span), col:col + LANES],
                      preferred_element_type=F32) / l_w
        for j in range(hpg):
            slot = pr * hpg + j
            lo, hi = head_rows(j), head_rows(hpg + j)
            o_lo = part[lo] + gate(pr, j, 2) * o_w[lo]
            o_hi = part[hi] + gate(pr, hpg + j, 2) * o_w[hi]
            o_ref[:, slot * LANES:(slot + 1) * LANES] = jnp.where(lo_half, o_lo, o_hi).astype(BF16)


def _nsa_attention(q, gates, kcmp, vcmp, kvs, kvw, c2st, emat, *, tq, ac, n_sblk, s_blk=0, w_blk=0):
    nb, tlen, _ = q.shape
    ncp = kcmp.shape[1]
    ls, lw = kvs.shape[1], kvw.shape[1]
    assert WINDOW % tq == 0 and lw >= WINDOW + tq and (C_KV_HEADS * tq) % LANES == 0
    kvwid = C_KV_HEADS * HEAD_DIM
    qspec = lambda w: pl.BlockSpec((None, tq, w), lambda b, i: (b, i, 0))
    kspec = lambda n, w, blk=0: pl.BlockSpec((None, n, w), lambda b, i: (b, 0, blk))
    kern = functools.partial(_nsa_kernel, tq=tq, ncp=ncp, ac=ac, nblk=_cdiv(n_sblk, SUBLANES) * SUBLANES)
    return pl.pallas_call(
        kern, grid=(nb, tlen // tq),
        in_specs=[qspec(C_WIDTH), qspec(LANES), kspec(ncp, kvwid), kspec(ncp, kvwid),
                  kspec(ls, 2 * kvwid, s_blk), kspec(lw, 2 * kvwid, w_blk),
                  pl.BlockSpec((LANES, ncp), lambda b, i: (0, 0)),
                  pl.BlockSpec((ls // ac, LANES, ac), lambda b, i: (0, 0, 0))],
        out_specs=qspec(C_WIDTH),
        out_shape=jax.ShapeDtypeStruct((nb, tlen, C_WIDTH), BF16),
        scratch_shapes=[pltpu.VMEM((ls // ac, C_KV_HEADS * tq, ac), F32), pltpu.VMEM((C_HEADS * tq, LANES), F32)]
        + [pltpu.VMEM((2 * (C_HEADS // C_KV_HEADS) * tq, LANES), F32)] * 3,
        compiler_params=_params("parallel", "arbitrary"))(q, gates, kcmp, vcmp, kvs, kvw, c2st, emat)


def _outproj_ln_kernel(x_ref, a_ref, b_ref, wa_ref, wb_ref, g_ref, bt_ref, o_ref):
    y = ALPHA * x_ref[...]
    y = y + jnp.dot(a_ref[...], wa_ref[...], preferred_element_type=F32)
    y = y + jnp.dot(b_ref[...], wb_ref[...], preferred_element_type=F32)
    o_ref[...] = _layernorm(y, g_ref[...], bt_ref[...])


def _outproj_ln(x2d, a, b, a_blk, b_blk, w_out, g, bt, *, tm):
    m = x2d.shape[0]
    half = w_out.shape[0] // 2
    row = lambda i: (i, 0)
    const = lambda i: (0, 0)
    return pl.pallas_call(
        _outproj_ln_kernel, grid=(m // tm,),
        in_specs=[pl.BlockSpec((tm, D_MODEL), row),
                  pl.BlockSpec((tm, half), lambda i: (i, a_blk)), pl.BlockSpec((tm, half), lambda i: (i, b_blk)),
                  pl.BlockSpec((half, D_MODEL), lambda i: (0, 0)), pl.BlockSpec((half, D_MODEL), lambda i: (1, 0)),
                  pl.BlockSpec((1, D_MODEL), const), pl.BlockSpec((1, D_MODEL), const)],
        out_specs=pl.BlockSpec((tm, D_MODEL), row),
        out_shape=jax.ShapeDtypeStruct((m, D_MODEL), F32),
        compiler_params=_params("parallel"))(x2d, a, b, w_out, w_out, g, bt)


def _ffn_ln_kernel(x_ref, wg_ref, wu_ref, wd_ref, g_ref, bt_ref, o_ref, xb_ref, acc_ref):
    f = pl.program_id(1)

    @pl.when(f == 0)
    def _():
        xb_ref[...] = x_ref[...].astype(BF16)
        acc_ref[...] = jnp.zeros(acc_ref.shape, F32)

    xb = xb_ref[...]
    h = jnp.dot(xb, wg_ref[...], preferred_element_type=F32)
    u = jnp.dot(xb, wu_ref[...], preferred_element_type=F32)
    a = (h / (1.0 + jnp.exp(-h))) * u
    acc_ref[...] += jnp.dot(a.astype(BF16), wd_ref[...], preferred_element_type=F32)

    @pl.when(f == pl.num_programs(1) - 1)
    def _():
        o_ref[...] = _layernorm(ALPHA * x_ref[...] + acc_ref[...], g_ref[...], bt_ref[...])


def _ffn_ln(x2d, wg, wu, wd, g, bt, *, tm, tf):
    m = x2d.shape[0]
    return pl.pallas_call(
        _ffn_ln_kernel, grid=(m // tm, D_FF // tf),
        in_specs=[pl.BlockSpec((tm, D_MODEL), lambda i, f: (i, 0)),
                  pl.BlockSpec((D_MODEL, tf), lambda i, f: (0, f)), pl.BlockSpec((D_MODEL, tf), lambda i, f: (0, f)),
                  pl.BlockSpec((tf, D_MODEL), lambda i, f: (f, 0)),
                  pl.BlockSpec((1, D_MODEL), lambda i, f: (0, 0)), pl.BlockSpec((1, D_MODEL), lambda i, f: (0, 0))],
        out_specs=pl.BlockSpec((tm, D_MODEL), lambda i, f: (i, 0)),
        out_shape=jax.ShapeDtypeStruct((m, D_MODEL), F32),
        scratch_shapes=[pltpu.VMEM((tm, D_MODEL), BF16), pltpu.VMEM((tm, D_MODEL), F32)],
        compiler_params=_params("parallel", "arbitrary"))(x2d, wg, wu, wd, g, bt)


def _rope_tables(pos):
    half = HEAD_DIM // 2
    inv = ROPE_THETA ** (-jnp.arange(half, dtype=F32) / half)
    ang = pos.astype(F32)[:, None] * inv[None, :]
    cos, sin = jnp.cos(ang), jnp.sin(ang)
    c64 = jnp.concatenate([cos, cos], 1)
    s64 = jnp.concatenate([-sin, sin], 1)
    c = jnp.concatenate([c64, c64], 1)
    s = jnp.concatenate([s64, s64], 1)
    cx = jnp.concatenate([c64, jnp.full_like(c64, IDX_HEADS ** -0.5)], 1)
    sx = jnp.concatenate([s64, jnp.zeros_like(s64)], 1)
    return c, s, cx, sx


def _perm_heads(w, perm):
    lead = w.shape[:-1]
    return w.reshape(*lead, len(perm), HEAD_DIM)[..., np.asarray(perm), :].reshape(*lead, len(perm) * HEAD_DIM)


def _ab_w_in(w):
    ab_sizes = (A_WIDTH, A_KV_HEADS * HEAD_DIM, A_KV_HEADS * HEAD_DIM, IDX_HEADS * IDX_DIM, IDX_DIM, IDX_HEADS,
                B_WIDTH, B_WIDTH, B_WIDTH)
    q, k, v, iq, ik, iw, gb, gc, h = jnp.split(w, np.cumsum(ab_sizes)[:-1].tolist(), axis=-1)
    pad = jnp.zeros((w.shape[0], LANES - IDX_DIM - IDX_HEADS), w.dtype)
    return jnp.concatenate([_perm_heads(q, A_PERM), k, iq, ik, iw, pad, v, gb, gc, h], -1).astype(BF16)


def _c_w_in(w):
    kvw = C_KV_HEADS * HEAD_DIM
    c_sizes = (C_WIDTH,) + (kvw,) * 6 + (3 * C_HEADS,)
    q, kc, vc, ks, vs, kw, vw, g = jnp.split(w, np.cumsum(c_sizes)[:-1].tolist(), axis=-1)
    pad = jnp.zeros((w.shape[0], LANES - 3 * C_HEADS), w.dtype)
    return jnp.concatenate([_perm_heads(q, C_PERM), kc, ks, kw, vc, vs, vw, g, pad], -1).astype(BF16)


def _perm_rows(w_out, perm):
    return w_out.reshape(len(perm), HEAD_DIM, w_out.shape[-1])[np.asarray(perm)].reshape(-1, w_out.shape[-1])


def _block_diag(w, n):
    eye = jnp.eye(n, dtype=w.dtype)
    out = jnp.einsum('gh,...ab->...gahb', eye, w)
    return out.reshape(*w.shape[:-2], n * w.shape[-2], n * w.shape[-1])


def _compress_weights(pe, w1, w2):
    g = C_KV_HEADS
    half = CMP_BLOCK // 2
    bd = _block_diag(w1, g)
    wa = bd[:half].reshape(half * g * HEAD_DIM, g * HEAD_DIM).astype(BF16)
    wb = bd[half:].reshape(half * g * HEAD_DIM, g * HEAD_DIM).astype(BF16)
    pet = jnp.tile(pe, (1, g))
    pea = pet[:half].reshape(1, -1)
    peb = pet[half:].reshape(1, -1)
    return pea, peb, wa, wb, _block_diag(w2, g).astype(BF16)


def _slc_from_cmp(ncp, n_cmp, n_sblk):
    n = np.arange(ncp)[None, :]
    mblk = np.arange(LANES)[:, None]
    hit = ((n * CMP_STRIDE < mblk * SLC_BLOCK + SLC_BLOCK) & (n * CMP_STRIDE + CMP_BLOCK > mblk * SLC_BLOCK)
           & (n < n_cmp) & (mblk < n_sblk))
    return jnp.asarray(hit, BF16)


def _expand_matrix(ls, ac):
    k = np.arange(ls)[None, :]
    mblk = np.arange(LANES)[:, None]
    e = (k // SLC_BLOCK == mblk).astype(np.float32)
    return jnp.asarray(e.reshape(LANES, ls // ac, ac).transpose(1, 0, 2), BF16)


def _pad_rows(a, n):
    return jnp.pad(a, ((0, 0), (0, n - a.shape[1]), (0, 0)))


def _stack_heads(q3, slots):
    nb, tq, _ = q3.shape
    qt = q3.reshape(nb, tq, slots, LANES).transpose(0, 2, 1, 3)
    lo = jnp.arange(LANES) < HEAD_DIM
    zero = jnp.zeros((), q3.dtype)
    stacked = jnp.concatenate([jnp.where(lo, qt, zero), jnp.where(lo, zero, qt)], 1)
    return stacked.reshape(nb, 2 * slots * tq, LANES)


def _token_minor(cache):
    nd = cache.ndim
    t = jnp.transpose(cache, (0, 1) + tuple(range(3, nd)) + (2,))
    return t.reshape(t.shape[0], t.shape[1], -1, t.shape[-1])


PROMPT_TM = 512
PROMPT_AC = 512
SAMPLE_TQ = 8
FFN_TF = 1408
DSA_SAMPLE_SEQS = 4
NSA_SAMPLE_SEQS = 2
CMP_SAMPLE_SEQS = 4


def _ab_layer(x2d, nb, tlen, start, past, page_table, w_in, conv_w, tabs, *, sample):
    m = nb * tlen
    conv_w8 = jnp.pad(conv_w, ((0, SUBLANES - CONV_W), (0, 0)))
    if sample:
        kt_cache, vt_cache, ikt_cache, prev, layer = past
        tt = jnp.arange(tlen)
        prevs = []
        for d in range(1, CONV_W):
            idx = jnp.clip(CONV_W - 1 + tt - d, 0, CONV_W - 2)
            prevs.append(prev[:, idx].reshape(m, B_WIDTH))
        outs = _proj_ab(x2d, w_in, tabs, conv_w8, tm=m, seq_tiles=1, sample=True, prevs=prevs, dec_seq=tlen)
    else:
        outs = _proj_ab(x2d, w_in, tabs, conv_w8, tm=PROMPT_TM, seq_tiles=tlen // PROMPT_TM, sample=False)
    q, iq, k, v, ikw, kvb, ikb, b_out, uo = outs[:9]
    if sample:
        klen_real = page_table.shape[1] * PAGE_SIZE + tlen
        pad_q = lambda a: _pad_rows(a.reshape(nb, tlen, -1), SAMPLE_TQ)
        pad_k = lambda a: _pad_rows(a.reshape(nb, tlen, -1), LANES)
        kvb3 = kvb.reshape(nb, tlen, 2 * LANES)
        iq_heads = pad_q(iq).reshape(nb, SAMPLE_TQ, IDX_HEADS, IDX_DIM).transpose(0, 2, 1, 3)
        a_out = _dsa_sample(_stack_heads(pad_q(q), A_HEADS // 2), iq_heads.reshape(nb, IDX_HEADS * SAMPLE_TQ, IDX_DIM),
                            pad_q(ikw), pad_k(kvb3[..., :LANES]), pad_k(kvb3[..., LANES:]),
                            pad_k(ikb.reshape(nb, tlen, LANES)[..., :IDX_DIM]),
                            kt_cache, vt_cache, ikt_cache, page_table, layer,
                            seqs=DSA_SAMPLE_SEQS, pos0=start, topk=min(A_TOPK_MAX, klen_real // 4))
        a_out = a_out[:, :tlen].reshape(m, A_WIDTH)
        new_conv = uo.reshape(nb, tlen, B_WIDTH)[:, tlen - (CONV_W - 1):]
    else:
        r3 = lambda a: a.reshape(nb, tlen, -1)
        assert PROMPT_TM == PROMPT_AC
        a_out = _dsa_attention(r3(q), r3(iq), r3(ikw), r3(ikb), r3(kvb),
                               outs[9].reshape(nb, tlen // PROMPT_AC, LANES, PROMPT_AC), tq=Q_BLOCK,
                               topk=min(A_TOPK_MAX, tlen // 4), sc=PROMPT_AC, ac=PROMPT_AC)
        a_out = a_out.reshape(m, A_WIDTH)
        new_conv = uo.reshape(nb, tlen // PROMPT_TM, SUBLANES, B_WIDTH)[:, -1, SUBLANES - (CONV_W - 1):]
    state = (k.reshape(nb, tlen, A_KV_HEADS, HEAD_DIM), v.reshape(nb, tlen, A_KV_HEADS, HEAD_DIM),
             ikw[:, :IDX_DIM].reshape(nb, tlen, IDX_DIM), new_conv)
    return a_out, b_out, state


def _c_layer(x2d, nb, tlen, start, past, w_in, cw_k, cw_v, tabs, *, sample):
    m = nb * tlen
    kvw = C_KV_HEADS * HEAD_DIM
    tm = m if sample else PROMPT_TM
    q, kc, ks, kw, vc, vs, vw, gates, kvb = _proj_c(x2d, w_in, tabs[:2], tm=tm)
    if sample:
        kct, vct, kst, vst, wink_t, winv_t, buf_k, buf_v, page_table, layer = past
        past_len = page_table.shape[1] * PAGE_SIZE
        n_cmp = (past_len + tlen - CMP_BLOCK) // CMP_STRIDE + 1
        assert n_cmp <= past_len // CMP_STRIDE, "compression blocks must lie inside the cached rows"
        assert tlen <= LANES and start == past_len

        def tap_major(cw):
            pea, peb, wa, wb, w2 = cw
            pet = jnp.concatenate([pea.reshape(-1, kvw), peb.reshape(-1, kvw)], 0)
            wab = jnp.concatenate([wa.reshape(-1, kvw, kvw), wb.reshape(-1, kvw, kvw)], 0)
            return pet, wab, w2
        kcmp = _compress_sample(kct, page_table, layer, *tap_major(cw_k), seqs=CMP_SAMPLE_SEQS)
        vcmp = _compress_sample(vct, page_table, layer, *tap_major(cw_v), seqs=CMP_SAMPLE_SEQS)
        ncp = kcmp.shape[1]
        ls = past_len + LANES
        n_sblk = _cdiv(past_len + tlen, SLC_BLOCK)
        kvb3 = kvb.reshape(nb, tlen, 4 * kvw)
        new_rows = [_pad_rows(kvb3[..., n * kvw:(n + 1) * kvw], LANES) for n in range(4)]
        q3 = _pad_rows(q.reshape(nb, tlen, C_WIDTH), SAMPLE_TQ)
        half_w = C_WIDTH // 2
        qs = jnp.concatenate([_stack_heads(q3[..., :half_w], C_HEADS // 4),
                              _stack_heads(q3[..., half_w:], C_HEADS // 4)], 1)
        out = _nsa_sample(qs, _pad_rows(gates.reshape(nb, tlen, LANES), SAMPLE_TQ), kcmp, vcmp, *new_rows,
                          wink_t, winv_t, _slc_from_cmp(ncp, n_cmp, n_sblk), _expand_matrix(ls, ls)[0],
                          kst, vst, page_table, layer, seqs=NSA_SAMPLE_SEQS, pos0=start, n_sblk=n_sblk)
        out = out[:, :tlen].reshape(m, C_WIDTH)
        r4 = lambda a: a.reshape(nb, tlen, C_KV_HEADS, HEAD_DIM)
        win_k = jnp.concatenate([buf_k, r4(kw)], 1)[:, tlen:]
        win_v = jnp.concatenate([buf_v, r4(vw)], 1)[:, tlen:]
    else:
        zk = kc.reshape(nb, tlen // CMP_STRIDE, CMP_STRIDE * kvw)
        zv = vc.reshape(nb, tlen // CMP_STRIDE, CMP_STRIDE * kvw)
        n_cmp = (tlen - CMP_BLOCK) // CMP_STRIDE + 1
        kcmp = _compress(zk, *cw_k, nb_step=1)
        vcmp = _compress(zv, *cw_v, nb_step=1)
        ncp = kcmp.shape[1]
        n_sblk = _cdiv(tlen, SLC_BLOCK)
        kvb3 = kvb.reshape(nb, tlen, 4 * kvw)
        out = _nsa_attention(q.reshape(nb, tlen, C_WIDTH), gates.reshape(nb, tlen, LANES), kcmp, vcmp, kvb3, kvb3,
                             _slc_from_cmp(ncp, n_cmp, n_sblk), _expand_matrix(tlen, PROMPT_AC),
                             tq=Q_BLOCK, ac=PROMPT_AC, n_sblk=n_sblk, s_blk=0, w_blk=1)
        out = out.reshape(m, C_WIDTH)
        r4 = lambda a: a.reshape(nb, tlen, C_KV_HEADS, HEAD_DIM)
        keep = min(WINDOW, tlen)
        win_k = r4(kw)[:, tlen - keep:]
        win_v = r4(vw)[:, tlen - keep:]
    r4 = lambda a: a.reshape(nb, tlen, C_KV_HEADS, HEAD_DIM)
    return out, (r4(kc), r4(vc), r4(ks), r4(vs), win_k, win_v)


def kernel(x_prompt, x_sample, cache_a_k, cache_a_v, cache_a_ik, state_b_conv, cache_c_cmp_k, cache_c_cmp_v, cache_c_slc_k, cache_c_slc_v, state_c_win_k, state_c_win_v, page_table, ab_w_in, ab_conv_w, ab_w_out, c_w_in, c_cmp_pe_k, c_cmp_w1_k, c_cmp_w2_k, c_cmp_pe_v, c_cmp_w1_v, c_cmp_w2_v, c_w_out, ffn_w_gate, ffn_w_up, ffn_w_down, ln1_g, ln1_b, ln2_g, ln2_b):
    bp, tp, _ = x_prompt.shape
    bs, ts, _ = x_sample.shape
    past_len = page_table.shape[1] * PAGE_SIZE
    xp = x_prompt.reshape(bp * tp, D_MODEL)
    xs = x_sample.reshape(bs * ts, D_MODEL)
    tabs_p = _rope_tables(jnp.arange(tp))
    tabs_s = _rope_tables(past_len + jnp.arange(bs * ts) % ts)
    kt_a, vt_a, ikt_a = _token_minor(cache_a_k), _token_minor(cache_a_v), _token_minor(cache_a_ik)
    kct_c, vct_c = _token_minor(cache_c_cmp_k), _token_minor(cache_c_cmp_v)
    kst_c, vst_c = _token_minor(cache_c_slc_k), _token_minor(cache_c_slc_v)
    wink_t, winv_t = _token_minor(state_c_win_k), _token_minor(state_c_win_v)
    ab_p, ab_s, c_p, c_s = [], [], [], []
    for layer in range(DEPTH):
        i = layer // 2
        row1 = lambda a: a[layer].reshape(1, D_MODEL)
        if layer % 2 == 0:
            w_in = _ab_w_in(ab_w_in[i])
            w_out = jnp.concatenate([_perm_rows(ab_w_out[i][:A_WIDTH], A_PERM), ab_w_out[i][A_WIDTH:]], 0).astype(BF16)
            a_p, b_p, st_p = _ab_layer(xp, bp, tp, 0, None, None, w_in, ab_conv_w[i], tabs_p, sample=False)
            past = (kt_a, vt_a, ikt_a, state_b_conv[i], i)
            a_s, b_s, st_s = _ab_layer(xs, bs, ts, past_len, past, page_table, w_in, ab_conv_w[i], tabs_s, sample=True)
            ab_p.append(st_p)
            ab_s.append(st_s)
            xp = _outproj_ln(xp, a_p, b_p, 0, 0, w_out, row1(ln1_g), row1(ln1_b), tm=PROMPT_TM)
            xs = _outproj_ln(xs, a_s, b_s, 0, 0, w_out, row1(ln1_g), row1(ln1_b), tm=bs * ts)
        else:
            w_in = _c_w_in(c_w_in[i])
            w_out = _perm_rows(c_w_out[i], C_PERM).astype(BF16)
            cw_k = _compress_weights(c_cmp_pe_k[i], c_cmp_w1_k[i], c_cmp_w2_k[i])
            cw_v = _compress_weights(c_cmp_pe_v[i], c_cmp_w1_v[i], c_cmp_w2_v[i])
            o_p, st_p = _c_layer(xp, bp, tp, 0, None, w_in, cw_k, cw_v, tabs_p, sample=False)
            past = (kct_c, vct_c, kst_c, vst_c, wink_t, winv_t, state_c_win_k[i], state_c_win_v[i], page_table, i)
            o_s, st_s = _c_layer(xs, bs, ts, past_len, past, w_in, cw_k, cw_v, tabs_s, sample=True)
            c_p.append(st_p)
            c_s.append(st_s)
            xp = _outproj_ln(xp, o_p, o_p, 0, 1, w_out, row1(ln1_g), row1(ln1_b), tm=PROMPT_TM)
            xs = _outproj_ln(xs, o_s, o_s, 0, 1, w_out, row1(ln1_g), row1(ln1_b), tm=bs * ts)
        wg, wu, wd = ffn_w_gate[layer].astype(BF16), ffn_w_up[layer].astype(BF16), ffn_w_down[layer].astype(BF16)
        xp = _ffn_ln(xp, wg, wu, wd, row1(ln2_g), row1(ln2_b), tm=PROMPT_TM, tf=FFN_TF)
        xs = _ffn_ln(xs, wg, wu, wd, row1(ln2_g), row1(ln2_b), tm=bs * ts, tf=FFN_TF)
    stk = lambda lst, j: jnp.stack([e[j] for e in lst], 0)
    return (xp.reshape(bp, tp, D_MODEL), xs.reshape(bs, ts, D_MODEL),
            stk(ab_p, 0), stk(ab_p, 1), stk(ab_p, 2), stk(ab_p, 3),
            stk(c_p, 0), stk(c_p, 1), stk(c_p, 2), stk(c_p, 3), stk(c_p, 4), stk(c_p, 5),
            stk(ab_s, 0), stk(ab_s, 1), stk(ab_s, 2), stk(ab_s, 3),
            stk(c_s, 0), stk(c_s, 1), stk(c_s, 2), stk(c_s, 3), stk(c_s, 4), stk(c_s, 5))
```

```python
import functools

import numpy as np
import jax
import jax.numpy as jnp
from jax import lax
from jax.experimental import pallas as pl
from jax.experimental.pallas import tpu as pltpu

D_MODEL = 1024
DEPTH = 4
PAGE_SIZE = 128
HEAD_DIM = 64
ROPE_THETA = 10000.0
A_HEADS = 8
A_KV_HEADS = 2
IDX_HEADS = 4
IDX_DIM = 64
A_TOPK_MAX = 256
A_WIDTH = A_HEADS * HEAD_DIM
B_WIDTH = D_MODEL // 2
CONV_W = 3
C_HEADS = 16
C_KV_HEADS = 4
C_WIDTH = C_HEADS * HEAD_DIM
CMP_BLOCK = 32
CMP_STRIDE = 16
SLC_BLOCK = 64
SLC_TOPN = 16
WINDOW = 512
D_FF = ((8 * D_MODEL + 3 * 256 - 1) // (3 * 256)) * 256
LN_EPS = 1e-5
ALPHA = (2 * DEPTH) ** 0.25
Q_BLOCK = 128

F32 = jnp.float32
BF16 = jnp.bfloat16
I32 = jnp.int32

LANES = 128
SUBLANES = 8
VMEM_LIMIT_BYTES = 56 * 1024 * 1024
MASKED = -1e30
Q_SCALE = HEAD_DIM ** -0.5 * float(np.log2(np.e))
INT_MIN = -(2 ** 31)
NEG_INF_KEY = int(np.int32(np.uint32(0xFF800000) ^ np.uint32(0x7FFFFFFF)))
NT_DIMS = (((1,), (1,)), ((), ()))

A_PERM = tuple(h for j in range(4) for h in (j, 4 + j))
C_PERM = tuple(h for pr in range(2) for j in range(4) for h in (8 * pr + j, 8 * pr + 4 + j))

AB_COLS = 2688
C_COLS = 2688
C_ROPED = C_WIDTH + 3 * C_KV_HEADS * HEAD_DIM


def _params(*sem):
    return pltpu.CompilerParams(dimension_semantics=sem, vmem_limit_bytes=VMEM_LIMIT_BYTES)


def _cdiv(a, b):
    return (a + b - 1) // b


def _rope128(r, c, s, first_half):
    sw = jnp.where(first_half, pltpu.roll(r, 96, 1), pltpu.roll(r, 32, 1))
    return r * c + sw * s


def _half_masked(q_bf16, mask):
    return jnp.where(mask, q_bf16.astype(F32), 0.0).astype(BF16)


def _flash(qs, heads, k_ref, v_ref, col, c_lo, n_chunks, ac, bias_fn, acc_ref, m_ref, l_ref):
    tq = qs.shape[0] // heads
    reps = ac // LANES
    acc_ref[...] = jnp.zeros(acc_ref.shape, F32)
    m_ref[...] = jnp.full(m_ref.shape, MASKED, F32)
    l_ref[...] = jnp.zeros(l_ref.shape, F32)

    def body(ci, carry):
        c = c_lo + ci
        off = pl.multiple_of(c * ac, ac)
        s = lax.dot_general(qs, k_ref[pl.ds(off, ac), col:col + LANES], NT_DIMS, preferred_element_type=F32)
        head_bias = bias_fn(c)
        ps, alphas = [], []
        for h in range(heads):
            rows = slice(h * tq, (h + 1) * tq)
            sh = s[rows] + head_bias(h)
            m_old = m_ref[rows]
            m_new = jnp.maximum(m_old, jnp.max(sh, axis=1, keepdims=True))
            alpha = jnp.exp2(m_old - m_new)
            p = jnp.exp2(sh - jnp.tile(m_new, (1, reps)))
            l_ref[rows] = alpha * l_ref[rows] + jnp.sum(p, axis=1, keepdims=True)
            m_ref[rows] = m_new
            ps.append(p.astype(BF16))
            alphas.append(alpha)
        pv = jnp.dot(jnp.concatenate(ps, 0), v_ref[pl.ds(off, ac), col:col + LANES], preferred_element_type=F32)
        for h in range(heads):
            rows = slice(h * tq, (h + 1) * tq)
            acc_ref[rows] = alphas[h] * acc_ref[rows] + pv[rows]
        return carry

    lax.fori_loop(0, n_chunks, body, 0)
    acc_ref[...] = jnp.where(m_ref[...] > 0.5 * MASKED, acc_ref[...] / l_ref[...], 0.0)


def _layernorm(y, g, b):
    mu = jnp.mean(y, axis=-1, keepdims=True)
    d = y - mu
    var = jnp.mean(d * d, axis=-1, keepdims=True)
    return d * lax.rsqrt(var + LN_EPS) * g + b


def _proj_ab_kernel(*refs, tm, seq_tiles, sample, dec_seq):
    if sample:
        (x_ref, w_ref, c_ref, s_ref, cx_ref, sx_ref, cw_ref, p1_ref, p2_ref,
         q_ref, iq_ref, k_ref, v_ref, ikw_ref, kvb_ref, ikb_ref, bo_ref, uo_ref, ubuf) = refs
        prev_refs = (None, p1_ref, p2_ref)
    else:
        (x_ref, w_ref, c_ref, s_ref, cx_ref, sx_ref, cw_ref,
         q_ref, iq_ref, k_ref, v_ref, ikw_ref, kvb_ref, ikb_ref, bo_ref, uo_ref, ubuf) = refs
    i = pl.program_id(0)
    y = jnp.dot(x_ref[...].astype(BF16), w_ref[...], preferred_element_type=F32)
    lane = lax.broadcasted_iota(I32, (tm, LANES), 1)
    first_half = (lane & (HEAD_DIM - 1)) < HEAD_DIM // 2
    c = c_ref[...]
    s = s_ref[...]
    ro = [_rope128(y[:, j * LANES:(j + 1) * LANES], c, s, first_half) for j in range(7)]
    ro.append(_rope128(y[:, 7 * LANES:8 * LANES], cx_ref[...], sx_ref[...], first_half))
    for j in range(4):
        q_ref[:, j * LANES:(j + 1) * LANES] = (ro[j] * Q_SCALE).astype(BF16)
    k = ro[4]
    v = y[:, 1024:1152]
    k_ref[...] = k
    v_ref[...] = v
    kvb_ref[:, 0:LANES] = k.astype(BF16)
    kvb_ref[:, LANES:2 * LANES] = v.astype(BF16)
    for j in range(2):
        iq_ref[:, j * LANES:(j + 1) * LANES] = (ro[5 + j] * IDX_DIM ** -0.5).astype(BF16)
    ikw = ro[7]
    ikw_ref[...] = ikw
    ikb_ref[...] = jnp.where(lane < IDX_DIM, ikw, pltpu.roll(ikw, IDX_DIM, 1)).astype(BF16)

    gate_b = y[:, 1152:1664]
    u = y[:, 1664:2176] * y[:, 2176:2688]

    @pl.when(i % seq_tiles == 0)
    def _():
        ubuf[0:SUBLANES, :] = jnp.zeros((SUBLANES, B_WIDTH), F32)

    @pl.when(i % seq_tiles != 0)
    def _():
        ubuf[0:SUBLANES, :] = ubuf[tm:tm + SUBLANES, :]

    ubuf[SUBLANES:tm + SUBLANES, :] = u
    cw = cw_ref[...]
    conv = u * cw[CONV_W - 1:CONV_W, :]
    if sample:
        t = lax.broadcasted_iota(I32, (tm, 1), 0) % dec_seq
    for d in range(1, CONV_W):
        ud = ubuf[SUBLANES - d:tm + SUBLANES - d, :]
        if sample:
            ud = jnp.where(t >= d, ud, prev_refs[d][...])
        conv = conv + ud * cw[CONV_W - 1 - d:CONV_W - d, :]
    bo_ref[...] = (gate_b * conv).astype(BF16)
    if sample:
        uo_ref[...] = u
    else:
        uo_ref[...] = u[tm - SUBLANES:tm, :]


def _proj_ab(x2d, w, tabs, conv_w8, *, tm, seq_tiles, sample, prevs=None, dec_seq=1):
    m = x2d.shape[0]
    nt = tabs[0].shape[0] // tm
    row = lambda i: (i, 0)
    const = lambda i: (0, 0)
    tab = lambda i: (i % nt, 0)
    in_specs = [pl.BlockSpec((tm, D_MODEL), row), pl.BlockSpec((D_MODEL, AB_COLS), const)]
    in_specs += [pl.BlockSpec((tm, LANES), tab)] * 4
    in_specs += [pl.BlockSpec((SUBLANES, B_WIDTH), const)]
    args = [x2d, w, *tabs, conv_w8]
    if sample:
        in_specs += [pl.BlockSpec((tm, B_WIDTH), row)] * 2
        args += list(prevs)
    u_rows = tm if sample else SUBLANES
    out_shape = [
        jax.ShapeDtypeStruct((m, A_WIDTH), BF16),
        jax.ShapeDtypeStruct((m, IDX_HEADS * IDX_DIM), BF16),
        jax.ShapeDtypeStruct((m, LANES), F32),
        jax.ShapeDtypeStruct((m, LANES), F32),
        jax.ShapeDtypeStruct((m, LANES), F32),
        jax.ShapeDtypeStruct((m, 2 * LANES), BF16),
        jax.ShapeDtypeStruct((m, LANES), BF16),
        jax.ShapeDtypeStruct((m, B_WIDTH), BF16),
        jax.ShapeDtypeStruct((m // tm * u_rows, B_WIDTH), F32),
    ]
    out_specs = [
        pl.BlockSpec((tm, A_WIDTH), row), pl.BlockSpec((tm, IDX_HEADS * IDX_DIM), row),
        pl.BlockSpec((tm, LANES), row), pl.BlockSpec((tm, LANES), row), pl.BlockSpec((tm, LANES), row),
        pl.BlockSpec((tm, 2 * LANES), row), pl.BlockSpec((tm, LANES), row),
        pl.BlockSpec((tm, B_WIDTH), row), pl.BlockSpec((u_rows, B_WIDTH), row),
    ]
    kern = functools.partial(_proj_ab_kernel, tm=tm, seq_tiles=seq_tiles, sample=sample, dec_seq=dec_seq)
    return pl.pallas_call(
        kern, grid=(m // tm,), in_specs=in_specs, out_specs=out_specs, out_shape=out_shape,
        scratch_shapes=[pltpu.VMEM((tm + SUBLANES, B_WIDTH), F32)],
        compiler_params=_params("arbitrary"))(*args)


def _count(key_ref, n_chunks, sc, nq, keys_on_sublanes, pred):
    def chunks(body, init):
        if isinstance(n_chunks, int):
            acc = init
            for c in range(n_chunks):
                acc = body(c, acc)
            return acc
        return lax.fori_loop(0, n_chunks, body, init)

    if keys_on_sublanes:
        def body(c, acc):
            part = jnp.where(pred(key_ref[c], c), 1.0, 0.0)
            rows = sc
            while rows > SUBLANES:
                rows //= 2
                part = part[:rows] + part[rows:2 * rows]
            return acc + part
        assert sc % SUBLANES == 0 and (sc // SUBLANES) & (sc // SUBLANES - 1) == 0
        return jnp.sum(chunks(body, jnp.zeros((SUBLANES, nq), F32)), axis=0, keepdims=True)

    def body(c, acc):
        m = jnp.where(pred(key_ref[c], c), 1.0, 0.0)
        part = m[:, 0:LANES]
        for j in range(1, sc // LANES):
            part = part + m[:, j * LANES:(j + 1) * LANES]
        return acc + part
    return jnp.sum(chunks(body, jnp.zeros((nq, LANES), F32)), axis=1, keepdims=True)


def _topk_bias(key_ref, bias_ref, n_sc, sc, ac, nq, topk, klen, keys_on_sublanes=False):
    ratio = sc // ac
    kshape, kaxis, vshape = ((sc, nq), 0, (1, nq)) if keys_on_sublanes else ((nq, sc), 1, (nq, 1))
    kiota = lax.broadcasted_iota(I32, kshape, kaxis)
    count = functools.partial(_count, key_ref, n_sc, sc, nq, keys_on_sublanes)
    kf = jnp.float32(topk)
    n_all = jnp.asarray(n_sc * sc, F32)

    def descend(n_static):
        count_n = functools.partial(_count, key_ref, n_static, sc, nq, keys_on_sublanes)

        def bit_body(b, carry):
            thr, n_ge = carry
            cand = thr + jnp.left_shift(jnp.int32(1), 31 - b)
            cnt = count_n(lambda key, c: key >= cand)
            take = cnt >= kf
            return jnp.where(take, cand, thr), jnp.where(take, cnt, n_ge)
        return lax.fori_loop(0, 32, bit_body, (jnp.full(vshape, INT_MIN, I32), jnp.full(vshape, n_all, F32)))

    if isinstance(n_sc, int):
        thr, n_ge = descend(n_sc)
    else:
        thr, n_ge = lax.switch(n_sc - 1, [functools.partial(descend, n) for n in range(1, klen // sc + 1)])

    def tie_search():
        need = kf - count(lambda key, c: key > thr)

        def tie_body(b, last):
            cand = last + jnp.left_shift(jnp.int32(1), klen.bit_length() - 1 - b)
            cnt = count(lambda key, c: (key == thr) & (c * sc + kiota < cand))
            return jnp.where(cnt < need, cand, last)
        return lax.fori_loop(0, klen.bit_length(), tie_body, jnp.zeros(vshape, I32))

    tied = (n_ge > kf) & (thr > jnp.int32(NEG_INF_KEY))
    last = lax.cond(jnp.max(jnp.where(tied, 1.0, 0.0)) > 0.5, tie_search, lambda: jnp.full(vshape, klen, I32))

    def bias_body(c, carry):
        key = key_ref[c]
        sel = (key > thr) | ((key == thr) & (c * sc + kiota <= last))
        sel = sel & (key > jnp.int32(NEG_INF_KEY))
        bias = jnp.where(sel, 0.0, MASKED)
        if keys_on_sublanes:
            for r in range(sc // LANES):
                blk = bias[r * LANES:(r + 1) * LANES, :].T
                bias_ref[c * ratio + r * LANES // ac, :, (r * LANES) % ac:(r * LANES) % ac + LANES] = blk
        else:
            for r in range(ratio):
                bias_ref[c * ratio + r] = bias[:, r * ac:(r + 1) * ac]
        return carry
    lax.fori_loop(0, n_sc, bias_body, 0)


def _dsa_kernel(q_ref, iq_ref, ikw_ref, ikb_ref, kvb_ref, o_ref, key_ref, bias_ref, acc_ref, m_ref, l_ref,
                *, tq, klen, topk, sc, ac):
    i = pl.program_id(1)
    n_sc = _cdiv((i + 1) * tq, sc)
    ratio = sc // ac
    lane = lax.broadcasted_iota(I32, (tq, LANES), 1)
    lo_half = lane < HEAD_DIM
    hi_half = jnp.logical_not(lo_half)

    iqs = jnp.concatenate([_half_masked(iq_ref[:, (h // 2) * LANES:(h // 2 + 1) * LANES],
                                        lo_half if h % 2 == 0 else hi_half) for h in range(IDX_HEADS)], 0)
    iw_t = ikw_ref[...].T
    pos_l = i * tq + lax.broadcasted_iota(I32, (1, tq), 1)
    kidx = lax.broadcasted_iota(I32, (sc, tq), 0)

    def score_body(c, carry):
        off = pl.multiple_of(c * sc, sc)
        logits = lax.dot_general(ikb_ref[pl.ds(off, sc), :], iqs, NT_DIMS, preferred_element_type=F32)
        sco = jnp.zeros((sc, tq), F32)
        for h in range(IDX_HEADS):
            sco = sco + jnp.maximum(logits[:, h * tq:(h + 1) * tq], 0.0) * iw_t[IDX_DIM + h:IDX_DIM + h + 1, :]
        sco = jnp.where(sco == 0.0, 0.0, sco)
        bits = pltpu.bitcast(sco, I32)
        key = jnp.where(bits < 0, bits ^ jnp.int32(0x7FFFFFFF), bits)
        key_ref[c] = jnp.where(c * sc + kidx <= pos_l, key, jnp.int32(NEG_INF_KEY))
        return carry
    lax.fori_loop(0, n_sc, score_body, 0)
    _topk_bias(key_ref, bias_ref, n_sc, sc, ac, tq, topk, klen, keys_on_sublanes=True)

    half_slots = A_HEADS // 2
    qs = jnp.concatenate([_half_masked(q_ref[:, (h % half_slots) * LANES:(h % half_slots + 1) * LANES],
                                       lo_half if h < half_slots else hi_half) for h in range(A_HEADS)], 0)
    def shared_bias(c):
        bias = bias_ref[c]
        return lambda h: bias
    _flash(qs, A_HEADS, kvb_ref, kvb_ref.at[:, LANES:2 * LANES], 0, 0, n_sc * ratio, ac, shared_bias,
           acc_ref, m_ref, l_ref)
    for j in range(half_slots):
        o_lo = acc_ref[j * tq:(j + 1) * tq]
        o_hi = acc_ref[(half_slots + j) * tq:(half_slots + j + 1) * tq]
        o_ref[:, j * LANES:(j + 1) * LANES] = jnp.where(lo_half, o_lo, o_hi).astype(BF16)


def _dsa_attention(q, iq, ikw, ikb, kvb, *, tq, topk, sc, ac):
    nb, tlen, _ = q.shape
    klen = ikb.shape[1]
    qspec = lambda w: pl.BlockSpec((None, tq, w), lambda b, i: (b, i, 0))
    kspec = lambda w: pl.BlockSpec((None, klen, w), lambda b, i: (b, 0, 0))
    kern = functools.partial(_dsa_kernel, tq=tq, klen=klen, topk=topk, sc=sc, ac=ac)
    return pl.pallas_call(
        kern, grid=(nb, tlen // tq),
        in_specs=[qspec(A_WIDTH), qspec(IDX_HEADS * IDX_DIM), qspec(LANES), kspec(LANES), kspec(2 * LANES)],
        out_specs=qspec(A_WIDTH),
        out_shape=jax.ShapeDtypeStruct((nb, tlen, A_WIDTH), BF16),
        scratch_shapes=[pltpu.VMEM((klen // sc, sc, tq), I32), pltpu.VMEM((klen // ac, tq, ac), F32)]
        + [pltpu.VMEM((A_HEADS * tq, LANES), F32)] * 3,
        compiler_params=_params("parallel", "arbitrary"))(q, iq, ikw, ikb, kvb)


def _page_specs(shape, layer, seqs, n_pages):
    specs = []
    for s in range(seqs):
        for p in range(n_pages):
            specs.append(pl.BlockSpec((None, None) + shape,
                                      lambda i, pt, s=s, p=p: (layer, pt[i * seqs + s, p], 0, 0)))
    return specs


def _softmax_rows(s):
    m = jnp.max(s, axis=1, keepdims=True)
    p = jnp.exp2(s - m)
    return p, jnp.sum(p, axis=1, keepdims=True)


def _dsa_sample_kernel(pt_ref, *refs, seqs, n_pages, pos0, topk):
    del pt_ref
    np_all = seqs * n_pages
    qs_ref, iq_ref, ikw_ref, knew_ref, vnew_ref, iknew_ref = refs[:6]
    kt_pages = refs[6:6 + np_all]
    vt_pages = refs[6 + np_all:6 + 2 * np_all]
    ikt_pages = refs[6 + 2 * np_all:6 + 3 * np_all]
    o_ref, ktb, vtb, iktb, key_ref, bias_ref = refs[6 + 3 * np_all:]
    tq = SUBLANES
    past = n_pages * PAGE_SIZE
    klen = past + LANES
    for s in range(seqs):
        for p in range(n_pages):
            cols = slice(p * PAGE_SIZE, (p + 1) * PAGE_SIZE)
            ktb[s, :, cols] = kt_pages[s * n_pages + p][...].astype(BF16)
            vtb[s, :, cols] = vt_pages[s * n_pages + p][...].astype(BF16)
            iktb[s, :, cols] = ikt_pages[s * n_pages + p][...].astype(BF16)

    pos = pos0 + lax.broadcasted_iota(I32, (tq, 1), 0)
    kidx = lax.broadcasted_iota(I32, (tq, klen), 1)
    for s in range(seqs):
        iq = iq_ref[s]
        logits = jnp.concatenate(
            [jnp.dot(iq, iktb[s], preferred_element_type=F32),
             lax.dot_general(iq, iknew_ref[s], NT_DIMS, preferred_element_type=F32)], 1)
        sco = jnp.zeros((tq, klen), F32)
        for h in range(IDX_HEADS):
            sco = sco + jnp.maximum(logits[h * tq:(h + 1) * tq], 0.0) * ikw_ref[s, :, IDX_DIM + h:IDX_DIM + h + 1]
        sco = jnp.where(sco == 0.0, 0.0, sco)
        bits = pltpu.bitcast(sco, I32)
        key = jnp.where(bits < 0, bits ^ jnp.int32(0x7FFFFFFF), bits)
        key_ref[0, s * tq:(s + 1) * tq, :] = jnp.where(kidx <= pos, key, jnp.int32(NEG_INF_KEY))
    _topk_bias(key_ref, bias_ref, 1, klen, klen, seqs * tq, topk, klen)

    lane = lax.broadcasted_iota(I32, (tq, LANES), 1)
    lo_half = lane < HEAD_DIM
    half_slots = A_HEADS // 2
    for s in range(seqs):
        qs = qs_ref[s]
        sc_all = jnp.concatenate(
            [jnp.dot(qs, ktb[s], preferred_element_type=F32),
             lax.dot_general(qs, knew_ref[s], NT_DIMS, preferred_element_type=F32)], 1)
        p, l = _softmax_rows(sc_all + jnp.tile(bias_ref[0, s * tq:(s + 1) * tq, :], (A_HEADS, 1)))
        pb = p.astype(BF16)
        o = (lax.dot_general(pb[:, :past], vtb[s], NT_DIMS, preferred_element_type=F32)
             + jnp.dot(pb[:, past:], vnew_ref[s], preferred_element_type=F32)) / l
        for j in range(half_slots):
            o_ref[s, :, j * LANES:(j + 1) * LANES] = jnp.where(
                lo_half, o[j * tq:(j + 1) * tq], o[(half_slots + j) * tq:(half_slots + j + 1) * tq]).astype(BF16)


def _dsa_sample(qs, iq, ikw, knew, vnew, iknew, kt_cache, vt_cache, ikt_cache, page_table, layer, *, seqs, pos0,
                topk):
    nb = qs.shape[0]
    n_pages = page_table.shape[1]
    klen = n_pages * PAGE_SIZE + LANES
    tq = SUBLANES
    seq_spec = lambda a: pl.BlockSpec((seqs,) + a.shape[1:], lambda i, pt: (i, 0, 0))
    in_specs = [seq_spec(a) for a in (qs, iq, ikw, knew, vnew, iknew)]
    in_specs += _page_specs(kt_cache.shape[2:], layer, seqs, n_pages)
    in_specs += _page_specs(vt_cache.shape[2:], layer, seqs, n_pages)
    in_specs += _page_specs(ikt_cache.shape[2:], layer, seqs, n_pages)
    np_all = seqs * n_pages
    kvw = A_KV_HEADS * HEAD_DIM
    grid_spec = pltpu.PrefetchScalarGridSpec(
        num_scalar_prefetch=1, grid=(nb // seqs,), in_specs=in_specs,
        out_specs=pl.BlockSpec((seqs, tq, A_WIDTH), lambda i, pt: (i, 0, 0)),
        scratch_shapes=[pltpu.VMEM((seqs, kvw, n_pages * PAGE_SIZE), BF16),
                        pltpu.VMEM((seqs, kvw, n_pages * PAGE_SIZE), BF16),
                        pltpu.VMEM((seqs, IDX_DIM, n_pages * PAGE_SIZE), BF16),
                        pltpu.VMEM((1, seqs * tq, klen), I32), pltpu.VMEM((1, seqs * tq, klen), F32)])
    kern = functools.partial(_dsa_sample_kernel, seqs=seqs, n_pages=n_pages, pos0=pos0, topk=topk)
    return pl.pallas_call(
        kern, grid_spec=grid_spec, out_shape=jax.ShapeDtypeStruct((nb, tq, A_WIDTH), BF16),
        compiler_params=_params("arbitrary"))(
            page_table, qs, iq, ikw, knew, vnew, iknew,
            *([kt_cache] * np_all), *([vt_cache] * np_all), *([ikt_cache] * np_all))


def _gelu_tanh(x):
    return 0.5 * x * (1.0 + jnp.tanh(np.sqrt(2.0 / np.pi) * (x + 0.044715 * x * x * x)))


def _compress_sample_kernel(pt_ref, pet_ref, wab_ref, w2_ref, *refs, seqs, n_pages):
    del pt_ref
    pages = refs[:seqs * n_pages]
    o_ref, tok_ref = refs[seqs * n_pages:]
    width = C_KV_HEADS * HEAD_DIM
    for s in range(seqs):
        for p in range(n_pages):
            base = (s * n_pages + p) * PAGE_SIZE
            for c in range(width // LANES):
                blk = pages[s * n_pages + p][c * LANES:(c + 1) * LANES, :]
                tok_ref[c, base:base + PAGE_SIZE, :] = blk.T
    rows = seqs * n_pages * PAGE_SIZE // CMP_STRIDE
    half = CMP_BLOCK // 2
    ra = jnp.zeros((rows, width), F32)
    rb = jnp.zeros((rows, width), F32)
    for j in range(half):
        xj = jnp.concatenate([tok_ref[c, pl.ds(j, rows, stride=CMP_STRIDE), :] for c in range(width // LANES)], 1)
        ra = ra + jnp.dot((xj + pet_ref[j:j + 1, :]).astype(BF16), wab_ref[j], preferred_element_type=F32)
        rb = rb + jnp.dot((xj + pet_ref[half + j:half + j + 1, :]).astype(BF16), wab_ref[half + j],
                          preferred_element_type=F32)
    hid = _gelu_tanh(ra + pltpu.roll(rb, rows - 1, 0))
    out = jnp.dot(hid.astype(BF16), w2_ref[...], preferred_element_type=F32)
    o_ref[...] = out.reshape(o_ref.shape).astype(BF16)


def _compress_sample(cache_t, page_table, layer, pet, wab, w2, *, seqs):
    nb, n_pages = page_table.shape
    width = C_KV_HEADS * HEAD_DIM
    r = n_pages * PAGE_SIZE // CMP_STRIDE
    const2 = lambda i, pt: (0, 0)
    grid_spec = pltpu.PrefetchScalarGridSpec(
        num_scalar_prefetch=1, grid=(nb // seqs,),
        in_specs=[pl.BlockSpec(pet.shape, const2), pl.BlockSpec(wab.shape, lambda i, pt: (0, 0, 0)),
                  pl.BlockSpec(w2.shape, const2)] + _page_specs(cache_t.shape[2:], layer, seqs, n_pages),
        out_specs=pl.BlockSpec((seqs, r, width), lambda i, pt: (i, 0, 0)),
        scratch_shapes=[pltpu.VMEM((width // LANES, seqs * n_pages * PAGE_SIZE, LANES), F32)])
    return pl.pallas_call(
        functools.partial(_compress_sample_kernel, seqs=seqs, n_pages=n_pages), grid_spec=grid_spec,
        out_shape=jax.ShapeDtypeStruct((nb, r, width), BF16),
        compiler_params=_params("arbitrary"))(page_table, pet, wab, w2, *([cache_t] * (seqs * n_pages)))


def _nsa_sample_kernel(pt_ref, *refs, seqs, n_pages, pos0, nblk):
    del pt_ref
    np_all = seqs * n_pages
    (qs_ref, g_ref, kc_ref, vc_ref, ksn_ref, vsn_ref, kwn_ref, vwn_ref, wink_ref, winv_ref, c2st_ref,
     e_ref) = refs[:12]
    kst_pages = refs[12:12 + np_all]
    vst_pages = refs[12 + np_all:12 + 2 * np_all]
    o_ref, kst, vst, part_ref, bias_ref = refs[12 + 2 * np_all:]
    tq = SUBLANES
    hpg = C_HEADS // C_KV_HEADS
    nh = 2 * hpg
    npair = C_KV_HEADS // 2
    past = n_pages * PAGE_SIZE
    ls = past + LANES
    nbuf = wink_ref.shape[-1]
    lw = nbuf + LANES
    ncp = kc_ref.shape[1]
    for s in range(seqs):
        for p in range(n_pages):
            cols = slice(p * PAGE_SIZE, (p + 1) * PAGE_SIZE)
            kst[s, :, cols] = kst_pages[s * n_pages + p][...].astype(BF16)
            vst[s, :, cols] = vst_pages[s * n_pages + p][...].astype(BF16)

    rows = nh * tq
    t_row = pos0 + lax.broadcasted_iota(I32, (rows, 1), 0) % tq
    cmp_visible = lax.broadcasted_iota(I32, (rows, ncp), 1) * CMP_STRIDE + (CMP_BLOCK - 1) <= t_row
    lane = lax.broadcasted_iota(I32, (tq, LANES), 1)
    lo_half = lane < HEAD_DIM
    pair_cols = lambda pr: slice(pr * LANES, (pr + 1) * LANES)

    psums = []
    for s in range(seqs):
        for pr in range(npair):
            qp = qs_ref[s, pr * rows:(pr + 1) * rows, :]
            s_c = lax.dot_general(qp, kc_ref[s, :, pair_cols(pr)], NT_DIMS, preferred_element_type=F32)
            s_c = jnp.where(cmp_visible, s_c, -jnp.inf)
            m = jnp.max(s_c, axis=1, keepdims=True)
            m = jnp.where(m > -jnp.inf, m, 0.0)
            p = jnp.exp2(s_c - m)
            den = jnp.sum(p, axis=1, keepdims=True)
            p = p / jnp.where(den > 0.0, den, 1.0)
            part_ref[s, pr] = jnp.dot(p.astype(BF16), vc_ref[s, :, pair_cols(pr)], preferred_element_type=F32)
            for half in range(2):
                acc = p[half * hpg * tq:(half * hpg + 1) * tq]
                for h in range(1, hpg):
                    acc = acc + p[(half * hpg + h) * tq:(half * hpg + h + 1) * tq]
                psums.append(acc)

    ng_rows = seqs * C_KV_HEADS * tq
    sel_rows = _cdiv(ng_rows, LANES) * LANES
    psum = jnp.concatenate(psums + [jnp.zeros((sel_rows - ng_rows, ncp), F32)] * (sel_rows > ng_rows), 0)
    t_lane = pos0 + lax.broadcasted_iota(I32, (1, sel_rows), 1) % tq
    selm = _select_blocks(psum, c2st_ref[...], t_lane, nblk)[:ng_rows]
    tg = pos0 + lax.broadcasted_iota(I32, (ng_rows, 1), 0) % tq
    ex = jnp.dot(selm.astype(BF16), e_ref[...], preferred_element_type=F32)
    ok = (ex > 0.5) & (lax.broadcasted_iota(I32, (ng_rows, ls), 1) <= tg)
    bias_ref[...] = jnp.where(ok, 0.0, MASKED)

    wpos = jnp.concatenate([pos0 - nbuf + lax.broadcasted_iota(I32, (rows, nbuf), 1),
                            pos0 + lax.broadcasted_iota(I32, (rows, LANES), 1)], 1)
    wbias = jnp.where((wpos >= 0) & (wpos <= t_row) & (t_row - wpos < WINDOW), 0.0, MASKED)

    for s in range(seqs):
        for pr in range(npair):
            qp = qs_ref[s, pr * rows:(pr + 1) * rows, :]
            feat = slice(pr * LANES, (pr + 1) * LANES)
            gbase = (s * C_KV_HEADS + 2 * pr) * tq
            bias = jnp.concatenate([jnp.tile(bias_ref[gbase:gbase + tq, :], (hpg, 1)),
                                    jnp.tile(bias_ref[gbase + tq:gbase + 2 * tq, :], (hpg, 1))], 0)
            sc_s = jnp.concatenate(
                [jnp.dot(qp, kst[s, feat, :], preferred_element_type=F32),
                 lax.dot_general(qp, ksn_ref[s, :, pair_cols(pr)], NT_DIMS, preferred_element_type=F32)], 1)
            p, l = _softmax_rows(sc_s + bias)
            pb = p.astype(BF16)
            o_s = (lax.dot_general(pb[:, :past], vst[s, feat, :], NT_DIMS, preferred_element_type=F32)
                   + jnp.dot(pb[:, past:], vsn_ref[s, :, pair_cols(pr)], preferred_element_type=F32)) / l
            sc_w = jnp.concatenate(
                [jnp.dot(qp, wink_ref[s, feat, :].astype(BF16), preferred_element_type=F32),
                 lax.dot_general(qp, kwn_ref[s, :, pair_cols(pr)], NT_DIMS, preferred_element_type=F32)], 1)
            p, l = _softmax_rows(sc_w + wbias)
            pb = p.astype(BF16)
            o_w = (lax.dot_general(pb[:, :nbuf], winv_ref[s, feat, :].astype(BF16), NT_DIMS,
                                   preferred_element_type=F32)
                   + jnp.dot(pb[:, nbuf:], vwn_ref[s, :, pair_cols(pr)], preferred_element_type=F32)) / l

            def gate(branch):
                cols = [3 * (hpg * (2 * pr + h // hpg) + h % hpg) + branch for h in range(nh)]
                return jnp.concatenate([g_ref[s, :, c:c + 1] for c in cols], 0)
            o = gate(0) * part_ref[s, pr] + gate(1) * o_s + gate(2) * o_w
            for j in range(hpg):
                slot = pr * hpg + j
                o_ref[s, :, slot * LANES:(slot + 1) * LANES] = jnp.where(
                    lo_half, o[j * tq:(j + 1) * tq], o[(hpg + j) * tq:(hpg + j + 1) * tq]).astype(BF16)


def _nsa_sample(qs, gates, kcmp, vcmp, ksn, vsn, kwn, vwn, wink_t, winv_t, c2s, emat, kst_cache, vst_cache,
                page_table, layer, *, seqs, pos0, n_sblk):
    nb, n_pages = page_table.shape
    tq = SUBLANES
    width = C_KV_HEADS * HEAD_DIM
    past = n_pages * PAGE_SIZE
    seq_spec = lambda a: pl.BlockSpec((seqs,) + a.shape[1:], lambda i, pt: (i, 0, 0))
    win_spec = pl.BlockSpec((None, seqs) + wink_t.shape[2:], lambda i, pt: (layer, i, 0, 0))
    const2 = lambda i, pt: (0, 0)
    in_specs = [seq_spec(a) for a in (qs, gates, kcmp, vcmp, ksn, vsn, kwn, vwn)]
    in_specs += [win_spec, win_spec, pl.BlockSpec(c2s.shape, const2), pl.BlockSpec(emat.shape, const2)]
    in_specs += _page_specs(kst_cache.shape[2:], layer, seqs, n_pages)
    in_specs += _page_specs(vst_cache.shape[2:], layer, seqs, n_pages)
    np_all = seqs * n_pages
    rows = 2 * (C_HEADS // C_KV_HEADS) * tq
    grid_spec = pltpu.PrefetchScalarGridSpec(
        num_scalar_prefetch=1, grid=(nb // seqs,), in_specs=in_specs,
        out_specs=pl.BlockSpec((seqs, tq, C_WIDTH), lambda i, pt: (i, 0, 0)),
        scratch_shapes=[pltpu.VMEM((seqs, width, past), BF16), pltpu.VMEM((seqs, width, past), BF16),
                        pltpu.VMEM((seqs, C_KV_HEADS // 2, rows, LANES), F32),
                        pltpu.VMEM((seqs * C_KV_HEADS * tq, past + LANES), F32)])
    kern = functools.partial(_nsa_sample_kernel, seqs=seqs, n_pages=n_pages, pos0=pos0,
                             nblk=_cdiv(n_sblk, SUBLANES) * SUBLANES)
    return pl.pallas_call(
        kern, grid_spec=grid_spec, out_shape=jax.ShapeDtypeStruct((nb, tq, C_WIDTH), BF16),
        compiler_params=_params("arbitrary"))(
            page_table, qs, gates, kcmp, vcmp, ksn, vsn, kwn, vwn, wink_t, winv_t, c2s, emat,
            *([kst_cache] * np_all), *([vst_cache] * np_all))


def _proj_c_kernel(x_ref, w_ref, c_ref, s_ref, q_ref, kc_ref, ks_ref, kw_ref, vc_ref, vs_ref, vw_ref,
                   g_ref, kvb_ref, *, tm):
    y = jnp.dot(x_ref[...].astype(BF16), w_ref[...], preferred_element_type=F32)
    lane = lax.broadcasted_iota(I32, (tm, LANES), 1)
    first_half = (lane & (HEAD_DIM - 1)) < HEAD_DIM // 2
    c = c_ref[...]
    s = s_ref[...]
    ro = [_rope128(y[:, j * LANES:(j + 1) * LANES], c, s, first_half) for j in range(C_ROPED // LANES)]
    for j in range(8):
        q_ref[:, j * LANES:(j + 1) * LANES] = (ro[j] * Q_SCALE).astype(BF16)
    for n, ref in enumerate((kc_ref, ks_ref, kw_ref)):
        for j in range(2):
            ref[:, j * LANES:(j + 1) * LANES] = ro[8 + 2 * n + j]
    for n, ref in enumerate((vc_ref, vs_ref, vw_ref)):
        ref[...] = y[:, C_ROPED + 256 * n:C_ROPED + 256 * (n + 1)]
    g = y[:, C_ROPED + 768:C_ROPED + 768 + LANES]
    g_ref[...] = 1.0 / (1.0 + jnp.exp(-g))
    for j in range(2):
        kvb_ref[:, j * LANES:(j + 1) * LANES] = ro[10 + j].astype(BF16)
        kvb_ref[:, 512 + j * LANES:512 + (j + 1) * LANES] = ro[12 + j].astype(BF16)
    kvb_ref[:, 256:512] = y[:, C_ROPED + 256:C_ROPED + 512].astype(BF16)
    kvb_ref[:, 768:1024] = y[:, C_ROPED + 512:C_ROPED + 768].astype(BF16)


def _proj_c(x2d, w, tabs, *, tm):
    m = x2d.shape[0]
    nt = tabs[0].shape[0] // tm
    row = lambda i: (i, 0)
    kvw = C_KV_HEADS * HEAD_DIM
    out_shape = [jax.ShapeDtypeStruct((m, C_WIDTH), BF16)]
    out_shape += [jax.ShapeDtypeStruct((m, kvw), F32)] * 6
    out_shape += [jax.ShapeDtypeStruct((m, LANES), F32), jax.ShapeDtypeStruct((m, 4 * kvw), BF16)]
    out_specs = [pl.BlockSpec((tm, C_WIDTH), row)] + [pl.BlockSpec((tm, kvw), row)] * 6
    out_specs += [pl.BlockSpec((tm, LANES), row), pl.BlockSpec((tm, 4 * kvw), row)]
    return pl.pallas_call(
        functools.partial(_proj_c_kernel, tm=tm), grid=(m // tm,),
        in_specs=[pl.BlockSpec((tm, D_MODEL), row), pl.BlockSpec((D_MODEL, C_COLS), lambda i: (0, 0)),
                  pl.BlockSpec((tm, LANES), lambda i: (i % nt, 0)), pl.BlockSpec((tm, LANES), lambda i: (i % nt, 0))],
        out_specs=out_specs, out_shape=out_shape,
        compiler_params=_params("parallel"))(x2d, w, *tabs)


def _compress_kernel(z_ref, pea_ref, peb_ref, wa_ref, wb_ref, w2_ref, o_ref, *, rows):
    z = z_ref[...].reshape(rows, z_ref.shape[-1])
    ra = jnp.dot((z + pea_ref[...]).astype(BF16), wa_ref[...], preferred_element_type=F32)
    rb = jnp.dot((z + peb_ref[...]).astype(BF16), wb_ref[...], preferred_element_type=F32)
    hid = ra + pltpu.roll(rb, rows - 1, 0)
    hid = 0.5 * hid * (1.0 + jnp.tanh(np.sqrt(2.0 / np.pi) * (hid + 0.044715 * hid * hid * hid)))
    out = jnp.dot(hid.astype(BF16), w2_ref[...], preferred_element_type=F32)
    o_ref[...] = out.reshape(o_ref.shape).astype(BF16)


def _compress(z, pea, peb, wa, wb, w2, *, nb_step):
    nb, r, zw = z.shape
    kvw = C_KV_HEADS * HEAD_DIM
    const = lambda b: (0, 0)
    return pl.pallas_call(
        functools.partial(_compress_kernel, rows=nb_step * r), grid=(nb // nb_step,),
        in_specs=[pl.BlockSpec((nb_step, r, zw), lambda b: (b, 0, 0)),
                  pl.BlockSpec((1, zw), const), pl.BlockSpec((1, zw), const),
                  pl.BlockSpec((zw, kvw), const), pl.BlockSpec((zw, kvw), const), pl.BlockSpec((kvw, kvw), const)],
        out_specs=pl.BlockSpec((nb_step, r, kvw), lambda b: (b, 0, 0)),
        out_shape=jax.ShapeDtypeStruct((nb, r, kvw), BF16),
        compiler_params=_params("parallel"))(z, pea, peb, wa, wb, w2)


def _select_blocks(psum, c2st, t_lane, nblk):
    rows = psum.shape[0]
    p_hi = psum.astype(BF16)
    p_lo = (psum - p_hi.astype(F32)).astype(BF16)
    imp = (lax.dot_general(c2st, p_hi, NT_DIMS, preferred_element_type=F32)
           + lax.dot_general(c2st, p_lo, NT_DIMS, preferred_element_type=F32))[:nblk]
    blk = lax.broadcasted_iota(I32, (nblk, rows), 0)
    blk_f = blk.astype(F32)
    cur = t_lane // SLC_BLOCK
    forced = (blk == 0) | (blk == cur) | (blk == cur - 1)
    imp = jnp.where(forced, jnp.inf, imp)
    imp = jnp.where(blk * SLC_BLOCK <= t_lane, imp, -jnp.inf)

    def top_body(_, carry):
        val, sel = carry
        mx = jnp.max(val, axis=0, keepdims=True)
        first = jnp.min(jnp.where(val == mx, blk_f, float(LANES)), axis=0, keepdims=True)
        pick = blk_f == first
        sel = jnp.where(pick & (mx > -jnp.inf), 1.0, sel)
        return jnp.where(pick, -jnp.inf, val), sel
    _, sel = lax.fori_loop(0, SLC_TOPN, top_body, (imp, jnp.zeros((nblk, rows), F32)))
    sel = jnp.concatenate([sel, jnp.zeros((LANES - nblk, rows), F32)], 0)
    return jnp.concatenate([sel[:, j * LANES:(j + 1) * LANES].T for j in range(rows // LANES)], 0)


def _nsa_kernel(q_ref, g_ref, kc_ref, vc_ref, kvs_ref, kvw_ref, c2st_ref, e_ref, o_ref,
                bias_ref, part_ref, acc_ref, m_ref, l_ref,
                *, tq, ncp, ac, nblk):
    i = pl.program_id(1)
    t = i * tq + lax.broadcasted_iota(I32, (tq, 1), 0)
    lane = lax.broadcasted_iota(I32, (tq, LANES), 1)
    lo_half = lane < HEAD_DIM
    n_s = _cdiv((i + 1) * tq, ac)
    span = WINDOW + tq
    w_start = pl.multiple_of(jnp.maximum(i * tq - WINDOW, 0), tq)
    kvw = C_KV_HEADS * HEAD_DIM
    ks_ref = kvs_ref
    vs_ref = kvs_ref.at[:, kvw:2 * kvw]
    kw_ref = kvw_ref
    vw_ref = kvw_ref.at[:, kvw:2 * kvw]
    hi_half = jnp.logical_not(lo_half)
    cmp_visible = lax.broadcasted_iota(I32, (tq, ncp), 1) * CMP_STRIDE + (CMP_BLOCK - 1) <= t
    ng = C_KV_HEADS
    hpg = C_HEADS // C_KV_HEADS
    nh = 2 * hpg
    head_rows = lambda h: slice(h * tq, (h + 1) * tq)
    tg = jnp.concatenate([t] * ng, 0)
    kiota_sg = lax.broadcasted_iota(I32, (ng * tq, ac), 1)

    def gate(pr, h, branch):
        head = hpg * (2 * pr + h // hpg) + h % hpg
        return g_ref[:, 3 * head + branch:3 * head + branch + 1]

    wpos = w_start + lax.broadcasted_iota(I32, (tq, span), 1)
    wbias = jnp.tile(jnp.where((wpos <= t) & (t - wpos < WINDOW), 0.0, MASKED), (nh, 1))

    def stacked_q(pr):
        return jnp.concatenate(
            [_half_masked(q_ref[:, (pr * hpg + h % hpg) * LANES:(pr * hpg + h % hpg + 1) * LANES],
                          lo_half if h < hpg else hi_half) for h in range(nh)], 0)

    psums = []
    for pr in range(ng // 2):
        col = pr * LANES
        s_c = lax.dot_general(stacked_q(pr), kc_ref[:, col:col + LANES], NT_DIMS, preferred_element_type=F32)
        pair_sums = [jnp.zeros((tq, ncp), F32), jnp.zeros((tq, ncp), F32)]
        ps = []
        for h in range(nh):
            sh = jnp.where(cmp_visible, s_c[head_rows(h)], -jnp.inf)
            m = jnp.max(sh, axis=1, keepdims=True)
            m = jnp.where(m > -jnp.inf, m, 0.0)
            p = jnp.exp2(sh - m)
            den = jnp.sum(p, axis=1, keepdims=True)
            p = p / jnp.where(den > 0.0, den, 1.0)
            pair_sums[h // hpg] = pair_sums[h // hpg] + p
            ps.append(p.astype(BF16))
        part_ref[pr * nh * tq:(pr + 1) * nh * tq] = jnp.dot(jnp.concatenate(ps, 0), vc_ref[:, col:col + LANES],
                                                           preferred_element_type=F32)
        psums += pair_sums

    t_lane = i * tq + lax.broadcasted_iota(I32, (1, ng * tq), 1) % tq
    selb = _select_blocks(jnp.concatenate(psums, 0), c2st_ref[...], t_lane, nblk).astype(BF16)

    def bias_body(c, carry):
        ex = jnp.dot(selb, e_ref[c], preferred_element_type=F32)
        ok = (ex > 0.5) & (c * ac + kiota_sg <= tg)
        bias_ref[c] = jnp.where(ok, 0.0, MASKED)
        return carry
    lax.fori_loop(0, n_s, bias_body, 0)

    for pr in range(ng // 2):
        col = pr * LANES
        qs = stacked_q(pr)
        part = part_ref.at[pr * nh * tq:(pr + 1) * nh * tq]

        def selected_bias(c, pr=pr):
            return lambda h: bias_ref[c, (2 * pr + h // hpg) * tq:(2 * pr + h // hpg + 1) * tq, :]
        _flash(qs, nh, ks_ref, vs_ref, col, 0, n_s, ac, selected_bias, acc_ref, m_ref, l_ref)
        for h in range(nh):
            r = head_rows(h)
            part[r] = gate(pr, h, 0) * part[r] + gate(pr, h, 1) * acc_ref[r]
        s_w = lax.dot_general(qs, kw_ref[pl.ds(w_start, span), col:col + LANES], NT_DIMS,
                              preferred_element_type=F32)
        p_w, l_w = _softmax_rows(s_w + wbias)
        o_w = jnp.dot(p_w.astype(BF16), vw_ref[pl.ds(w_start, span), col:col + LANES],
                      preferred_element_type=F32) / l_w
        for j in range(hpg):
            slot = pr * hpg + j
            lo, hi = head_rows(j), head_rows(hpg + j)
            o_lo = part[lo] + gate(pr, j, 2) * o_w[lo]
            o_hi = part[hi] + gate(pr, hpg + j, 2) * o_w[hi]
            o_ref[:, slot * LANES:(slot + 1) * LANES] = jnp.where(lo_half, o_lo, o_hi).astype(BF16)


def _nsa_attention(q, gates, kcmp, vcmp, kvs, kvw, c2st, emat, *, tq, ac, n_sblk, s_blk=0, w_blk=0):
    nb, tlen, _ = q.shape
    ncp = kcmp.shape[1]
    ls, lw = kvs.shape[1], kvw.shape[1]
    assert WINDOW % tq == 0 and lw >= WINDOW + tq and (C_KV_HEADS * tq) % LANES == 0
    kvwid = C_KV_HEADS * HEAD_DIM
    qspec = lambda w: pl.BlockSpec((None, tq, w), lambda b, i: (b, i, 0))
    kspec = lambda n, w, blk=0: pl.BlockSpec((None, n, w), lambda b, i: (b, 0, blk))
    kern = functools.partial(_nsa_kernel, tq=tq, ncp=ncp, ac=ac, nblk=_cdiv(n_sblk, SUBLANES) * SUBLANES)
    return pl.pallas_call(
        kern, grid=(nb, tlen // tq),
        in_specs=[qspec(C_WIDTH), qspec(LANES), kspec(ncp, kvwid), kspec(ncp, kvwid),
                  kspec(ls, 2 * kvwid, s_blk), kspec(lw, 2 * kvwid, w_blk),
                  pl.BlockSpec((LANES, ncp), lambda b, i: (0, 0)),
                  pl.BlockSpec((ls // ac, LANES, ac), lambda b, i: (0, 0, 0))],
        out_specs=qspec(C_WIDTH),
        out_shape=jax.ShapeDtypeStruct((nb, tlen, C_WIDTH), BF16),
        scratch_shapes=[pltpu.VMEM((ls // ac, C_KV_HEADS * tq, ac), F32), pltpu.VMEM((C_HEADS * tq, LANES), F32)]
        + [pltpu.VMEM((2 * (C_HEADS // C_KV_HEADS) * tq, LANES), F32)] * 3,
        compiler_params=_params("parallel", "arbitrary"))(q, gates, kcmp, vcmp, kvs, kvw, c2st, emat)


def _outproj_ln_kernel(x_ref, a_ref, b_ref, wa_ref, wb_ref, g_ref, bt_ref, o_ref):
    y = ALPHA * x_ref[...]
    y = y + jnp.dot(a_ref[...], wa_ref[...], preferred_element_type=F32)
    y = y + jnp.dot(b_ref[...], wb_ref[...], preferred_element_type=F32)
    o_ref[...] = _layernorm(y, g_ref[...], bt_ref[...])


def _outproj_ln(x2d, a, b, a_blk, b_blk, w_out, g, bt, *, tm):
    m = x2d.shape[0]
    half = w_out.shape[0] // 2
    row = lambda i: (i, 0)
    const = lambda i: (0, 0)
    return pl.pallas_call(
        _outproj_ln_kernel, grid=(m // tm,),
        in_specs=[pl.BlockSpec((tm, D_MODEL), row),
                  pl.BlockSpec((tm, half), lambda i: (i, a_blk)), pl.BlockSpec((tm, half), lambda i: (i, b_blk)),
                  pl.BlockSpec((half, D_MODEL), lambda i: (0, 0)), pl.BlockSpec((half, D_MODEL), lambda i: (1, 0)),
                  pl.BlockSpec((1, D_MODEL), const), pl.BlockSpec((1, D_MODEL), const)],
        out_specs=pl.BlockSpec((tm, D_MODEL), row),
        out_shape=jax.ShapeDtypeStruct((m, D_MODEL), F32),
        compiler_params=_params("parallel"))(x2d, a, b, w_out, w_out, g, bt)


def _ffn_ln_kernel(x_ref, wg_ref, wu_ref, wd_ref, g_ref, bt_ref, o_ref, xb_ref, acc_ref):
    f = pl.program_id(1)

    @pl.when(f == 0)
    def _():
        xb_ref[...] = x_ref[...].astype(BF16)
        acc_ref[...] = jnp.zeros(acc_ref.shape, F32)

    xb = xb_ref[...]
    h = jnp.dot(xb, wg_ref[...], preferred_element_type=F32)
    u = jnp.dot(xb, wu_ref[...], preferred_element_type=F32)
    a = (h / (1.0 + jnp.exp(-h))) * u
    acc_ref[...] += jnp.dot(a.astype(BF16), wd_ref[...], preferred_element_type=F32)

    @pl.when(f == pl.num_programs(1) - 1)
    def _():
        o_ref[...] = _layernorm(ALPHA * x_ref[...] + acc_ref[...], g_ref[...], bt_ref[...])


def _ffn_ln(x2d, wg, wu, wd, g, bt, *, tm, tf):
    m = x2d.shape[0]
    return pl.pallas_call(
        _ffn_ln_kernel, grid=(m // tm, D_FF // tf),
        in_specs=[pl.BlockSpec((tm, D_MODEL), lambda i, f: (i, 0)),
                  pl.BlockSpec((D_MODEL, tf), lambda i, f: (0, f)), pl.BlockSpec((D_MODEL, tf), lambda i, f: (0, f)),
                  pl.BlockSpec((tf, D_MODEL), lambda i, f: (f, 0)),
                  pl.BlockSpec((1, D_MODEL), lambda i, f: (0, 0)), pl.BlockSpec((1, D_MODEL), lambda i, f: (0, 0))],
        out_specs=pl.BlockSpec((tm, D_MODEL), lambda i, f: (i, 0)),
        out_shape=jax.ShapeDtypeStruct((m, D_MODEL), F32),
        scratch_shapes=[pltpu.VMEM((tm, D_MODEL), BF16), pltpu.VMEM((tm, D_MODEL), F32)],
        compiler_params=_params("parallel", "arbitrary"))(x2d, wg, wu, wd, g, bt)


def _rope_tables(pos):
    half = HEAD_DIM // 2
    inv = ROPE_THETA ** (-jnp.arange(half, dtype=F32) / half)
    ang = pos.astype(F32)[:, None] * inv[None, :]
    cos, sin = jnp.cos(ang), jnp.sin(ang)
    c64 = jnp.concatenate([cos, cos], 1)
    s64 = jnp.concatenate([-sin, sin], 1)
    c = jnp.concatenate([c64, c64], 1)
    s = jnp.concatenate([s64, s64], 1)
    cx = jnp.concatenate([c64, jnp.full_like(c64, IDX_HEADS ** -0.5)], 1)
    sx = jnp.concatenate([s64, jnp.zeros_like(s64)], 1)
    return c, s, cx, sx


def _perm_heads(w, perm):
    lead = w.shape[:-1]
    return w.reshape(*lead, len(perm), HEAD_DIM)[..., np.asarray(perm), :].reshape(*lead, len(perm) * HEAD_DIM)


def _ab_w_in(w):
    ab_sizes = (A_WIDTH, A_KV_HEADS * HEAD_DIM, A_KV_HEADS * HEAD_DIM, IDX_HEADS * IDX_DIM, IDX_DIM, IDX_HEADS,
                B_WIDTH, B_WIDTH, B_WIDTH)
    q, k, v, iq, ik, iw, gb, gc, h = jnp.split(w, np.cumsum(ab_sizes)[:-1].tolist(), axis=-1)
    pad = jnp.zeros((w.shape[0], LANES - IDX_DIM - IDX_HEADS), w.dtype)
    return jnp.concatenate([_perm_heads(q, A_PERM), k, iq, ik, iw, pad, v, gb, gc, h], -1).astype(BF16)


def _c_w_in(w):
    kvw = C_KV_HEADS * HEAD_DIM
    c_sizes = (C_WIDTH,) + (kvw,) * 6 + (3 * C_HEADS,)
    q, kc, vc, ks, vs, kw, vw, g = jnp.split(w, np.cumsum(c_sizes)[:-1].tolist(), axis=-1)
    pad = jnp.zeros((w.shape[0], LANES - 3 * C_HEADS), w.dtype)
    return jnp.concatenate([_perm_heads(q, C_PERM), kc, ks, kw, vc, vs, vw, g, pad], -1).astype(BF16)


def _perm_rows(w_out, perm):
    return w_out.reshape(len(perm), HEAD_DIM, w_out.shape[-1])[np.asarray(perm)].reshape(-1, w_out.shape[-1])


def _block_diag(w, n):
    eye = jnp.eye(n, dtype=w.dtype)
    out = jnp.einsum('gh,...ab->...gahb', eye, w)
    return out.reshape(*w.shape[:-2], n * w.shape[-2], n * w.shape[-1])


def _compress_weights(pe, w1, w2):
    g = C_KV_HEADS
    half = CMP_BLOCK // 2
    bd = _block_diag(w1, g)
    wa = bd[:half].reshape(half * g * HEAD_DIM, g * HEAD_DIM).astype(BF16)
    wb = bd[half:].reshape(half * g * HEAD_DIM, g * HEAD_DIM).astype(BF16)
    pet = jnp.tile(pe, (1, g))
    pea = pet[:half].reshape(1, -1)
    peb = pet[half:].reshape(1, -1)
    return pea, peb, wa, wb, _block_diag(w2, g).astype(BF16)


def _slc_from_cmp(ncp, n_cmp, n_sblk):
    n = np.arange(ncp)[None, :]
    mblk = np.arange(LANES)[:, None]
    hit = ((n * CMP_STRIDE < mblk * SLC_BLOCK + SLC_BLOCK) & (n * CMP_STRIDE + CMP_BLOCK > mblk * SLC_BLOCK)
           & (n < n_cmp) & (mblk < n_sblk))
    return jnp.asarray(hit, BF16)


def _expand_matrix(ls, ac):
    k = np.arange(ls)[None, :]
    mblk = np.arange(LANES)[:, None]
    e = (k // SLC_BLOCK == mblk).astype(np.float32)
    return jnp.asarray(e.reshape(LANES, ls // ac, ac).transpose(1, 0, 2), BF16)


def _pad_rows(a, n):
    return jnp.pad(a, ((0, 0), (0, n - a.shape[1]), (0, 0)))


def _stack_heads(q3, slots):
    nb, tq, _ = q3.shape
    qt = q3.reshape(nb, tq, slots, LANES).transpose(0, 2, 1, 3)
    lo = jnp.arange(LANES) < HEAD_DIM
    zero = jnp.zeros((), q3.dtype)
    stacked = jnp.concatenate([jnp.where(lo, qt, zero), jnp.where(lo, zero, qt)], 1)
    return stacked.reshape(nb, 2 * slots * tq, LANES)


def _token_minor(cache):
    nd = cache.ndim
    t = jnp.transpose(cache, (0, 1) + tuple(range(3, nd)) + (2,))
    return t.reshape(t.shape[0], t.shape[1], -1, t.shape[-1])


PROMPT_TM = 512
PROMPT_AC = 512
DSA_TQ = 256
SAMPLE_TQ = 8
FFN_TF = 1408
DSA_SAMPLE_SEQS = 4
NSA_SAMPLE_SEQS = 2
CMP_SAMPLE_SEQS = 4


def _ab_layer(x2d, nb, tlen, start, past, page_table, w_in, conv_w, tabs, *, sample):
    m = nb * tlen
    conv_w8 = jnp.pad(conv_w, ((0, SUBLANES - CONV_W), (0, 0)))
    if sample:
        kt_cache, vt_cache, ikt_cache, prev, layer = past
        tt = jnp.arange(tlen)
        prevs = []
        for d in range(1, CONV_W):
            idx = jnp.clip(CONV_W - 1 + tt - d, 0, CONV_W - 2)
            prevs.append(prev[:, idx].reshape(m, B_WIDTH))
        outs = _proj_ab(x2d, w_in, tabs, conv_w8, tm=m, seq_tiles=1, sample=True, prevs=prevs, dec_seq=tlen)
    else:
        outs = _proj_ab(x2d, w_in, tabs, conv_w8, tm=PROMPT_TM, seq_tiles=tlen // PROMPT_TM, sample=False)
    q, iq, k, v, ikw, kvb, ikb, b_out, uo = outs
    if sample:
        klen_real = page_table.shape[1] * PAGE_SIZE + tlen
        pad_q = lambda a: _pad_rows(a.reshape(nb, tlen, -1), SAMPLE_TQ)
        pad_k = lambda a: _pad_rows(a.reshape(nb, tlen, -1), LANES)
        kvb3 = kvb.reshape(nb, tlen, 2 * LANES)
        iq_heads = pad_q(iq).reshape(nb, SAMPLE_TQ, IDX_HEADS, IDX_DIM).transpose(0, 2, 1, 3)
        a_out = _dsa_sample(_stack_heads(pad_q(q), A_HEADS // 2), iq_heads.reshape(nb, IDX_HEADS * SAMPLE_TQ, IDX_DIM),
                            pad_q(ikw), pad_k(kvb3[..., :LANES]), pad_k(kvb3[..., LANES:]),
                            pad_k(ikb.reshape(nb, tlen, LANES)[..., :IDX_DIM]),
                            kt_cache, vt_cache, ikt_cache, page_table, layer,
                            seqs=DSA_SAMPLE_SEQS, pos0=start, topk=min(A_TOPK_MAX, klen_real // 4))
        a_out = a_out[:, :tlen].reshape(m, A_WIDTH)
        new_conv = uo.reshape(nb, tlen, B_WIDTH)[:, tlen - (CONV_W - 1):]
    else:
        r3 = lambda a: a.reshape(nb, tlen, -1)
        a_out = _dsa_attention(r3(q), r3(iq), r3(ikw), r3(ikb), r3(kvb), tq=DSA_TQ,
                               topk=min(A_TOPK_MAX, tlen // 4), sc=512, ac=PROMPT_AC)
        a_out = a_out.reshape(m, A_WIDTH)
        new_conv = uo.reshape(nb, tlen // PROMPT_TM, SUBLANES, B_WIDTH)[:, -1, SUBLANES - (CONV_W - 1):]
    state = (k.reshape(nb, tlen, A_KV_HEADS, HEAD_DIM), v.reshape(nb, tlen, A_KV_HEADS, HEAD_DIM),
             ikw[:, :IDX_DIM].reshape(nb, tlen, IDX_DIM), new_conv)
    return a_out, b_out, state


def _c_layer(x2d, nb, tlen, start, past, w_in, cw_k, cw_v, tabs, *, sample):
    m = nb * tlen
    kvw = C_KV_HEADS * HEAD_DIM
    tm = m if sample else PROMPT_TM
    q, kc, ks, kw, vc, vs, vw, gates, kvb = _proj_c(x2d, w_in, tabs[:2], tm=tm)
    if sample:
        kct, vct, kst, vst, wink_t, winv_t, buf_k, buf_v, page_table, layer = past
        past_len = page_table.shape[1] * PAGE_SIZE
        n_cmp = (past_len + tlen - CMP_BLOCK) // CMP_STRIDE + 1
        assert n_cmp <= past_len // CMP_STRIDE, "compression blocks must lie inside the cached rows"
        assert tlen <= LANES and start == past_len

        def tap_major(cw):
            pea, peb, wa, wb, w2 = cw
            pet = jnp.concatenate([pea.reshape(-1, kvw), peb.reshape(-1, kvw)], 0)
            wab = jnp.concatenate([wa.reshape(-1, kvw, kvw), wb.reshape(-1, kvw, kvw)], 0)
            return pet, wab, w2
        kcmp = _compress_sample(kct, page_table, layer, *tap_major(cw_k), seqs=CMP_SAMPLE_SEQS)
        vcmp = _compress_sample(vct, page_table, layer, *tap_major(cw_v), seqs=CMP_SAMPLE_SEQS)
        ncp = kcmp.shape[1]
        ls = past_len + LANES
        n_sblk = _cdiv(past_len + tlen, SLC_BLOCK)
        kvb3 = kvb.reshape(nb, tlen, 4 * kvw)
        new_rows = [_pad_rows(kvb3[..., n * kvw:(n + 1) * kvw], LANES) for n in range(4)]
        q3 = _pad_rows(q.reshape(nb, tlen, C_WIDTH), SAMPLE_TQ)
        half_w = C_WIDTH // 2
        qs = jnp.concatenate([_stack_heads(q3[..., :half_w], C_HEADS // 4),
                              _stack_heads(q3[..., half_w:], C_HEADS // 4)], 1)
        out = _nsa_sample(qs, _pad_rows(gates.reshape(nb, tlen, LANES), SAMPLE_TQ), kcmp, vcmp, *new_rows,
                          wink_t, winv_t, _slc_from_cmp(ncp, n_cmp, n_sblk), _expand_matrix(ls, ls)[0],
                          kst, vst, page_table, layer, seqs=NSA_SAMPLE_SEQS, pos0=start, n_sblk=n_sblk)
        out = out[:, :tlen].reshape(m, C_WIDTH)
        r4 = lambda a: a.reshape(nb, tlen, C_KV_HEADS, HEAD_DIM)
        win_k = jnp.concatenate([buf_k, r4(kw)], 1)[:, tlen:]
        win_v = jnp.concatenate([buf_v, r4(vw)], 1)[:, tlen:]
    else:
        zk = kc.reshape(nb, tlen // CMP_STRIDE, CMP_STRIDE * kvw)
        zv = vc.reshape(nb, tlen // CMP_STRIDE, CMP_STRIDE * kvw)
        n_cmp = (tlen - CMP_BLOCK) // CMP_STRIDE + 1
        kcmp = _compress(zk, *cw_k, nb_step=1)
        vcmp = _compress(zv, *cw_v, nb_step=1)
        ncp = kcmp.shape[1]
        n_sblk = _cdiv(tlen, SLC_BLOCK)
        kvb3 = kvb.reshape(nb, tlen, 4 * kvw)
        out = _nsa_attention(q.reshape(nb, tlen, C_WIDTH), gates.reshape(nb, tlen, LANES), kcmp, vcmp, kvb3, kvb3,
                             _slc_from_cmp(ncp, n_cmp, n_sblk), _expand_matrix(tlen, PROMPT_AC),
                             tq=Q_BLOCK, ac=PROMPT_AC, n_sblk=n_sblk, s_blk=0, w_blk=1)
        out = out.reshape(m, C_WIDTH)
        r4 = lambda a: a.reshape(nb, tlen, C_KV_HEADS, HEAD_DIM)
        keep = min(WINDOW, tlen)
        win_k = r4(kw)[:, tlen - keep:]
        win_v = r4(vw)[:, tlen - keep:]
    r4 = lambda a: a.reshape(nb, tlen, C_KV_HEADS, HEAD_DIM)
    return out, (r4(kc), r4(vc), r4(ks), r4(vs), win_k, win_v)


def kernel(x_prompt, x_sample, cache_a_k, cache_a_v, cache_a_ik, state_b_conv, cache_c_cmp_k, cache_c_cmp_v, cache_c_slc_k, cache_c_slc_v, state_c_win_k, state_c_win_v, page_table, ab_w_in, ab_conv_w, ab_w_out, c_w_in, c_cmp_pe_k, c_cmp_w1_k, c_cmp_w2_k, c_cmp_pe_v, c_cmp_w1_v, c_cmp_w2_v, c_w_out, ffn_w_gate, ffn_w_up, ffn_w_down, ln1_g, ln1_b, ln2_g, ln2_b):
    bp, tp, _ = x_prompt.shape
    bs, ts, _ = x_sample.shape
    past_len = page_table.shape[1] * PAGE_SIZE
    xp = x_prompt.reshape(bp * tp, D_MODEL)
    xs = x_sample.reshape(bs * ts, D_MODEL)
    tabs_p = _rope_tables(jnp.arange(tp))
    tabs_s = _rope_tables(past_len + jnp.arange(bs * ts) % ts)
    kt_a, vt_a, ikt_a = _token_minor(cache_a_k), _token_minor(cache_a_v), _token_minor(cache_a_ik)
    kct_c, vct_c = _token_minor(cache_c_cmp_k), _token_minor(cache_c_cmp_v)
    kst_c, vst_c = _token_minor(cache_c_slc_k), _token_minor(cache_c_slc_v)
    wink_t, winv_t = _token_minor(state_c_win_k), _token_minor(state_c_win_v)
    ab_p, ab_s, c_p, c_s = [], [], [], []
    for layer in range(DEPTH):
        i = layer // 2
        row1 = lambda a: a[layer].reshape(1, D_MODEL)
        if layer % 2 == 0:
            w_in = _ab_w_in(ab_w_in[i])
            w_out = jnp.concatenate([_perm_rows(ab_w_out[i][:A_WIDTH], A_PERM), ab_w_out[i][A_WIDTH:]], 0).astype(BF16)
            a_p, b_p, st_p = _ab_layer(xp, bp, tp, 0, None, None, w_in, ab_conv_w[i], tabs_p, sample=False)
            past = (kt_a, vt_a, ikt_a, state_b_conv[i], i)
            a_s, b_s, st_s = _ab_layer(xs, bs, ts, past_len, past, page_table, w_in, ab_conv_w[i], tabs_s, sample=True)
            ab_p.append(st_p)
            ab_s.append(st_s)
            xp = _outproj_ln(xp, a_p, b_p, 0, 0, w_out, row1(ln1_g), row1(ln1_b), tm=PROMPT_TM)
            xs = _outproj_ln(xs, a_s, b_s, 0, 0, w_out, row1(ln1_g), row1(ln1_b), tm=bs * ts)
        else:
            w_in = _c_w_in(c_w_in[i])
            w_out = _perm_rows(c_w_out[i], C_PERM).astype(BF16)
            cw_k = _compress_weights(c_cmp_pe_k[i], c_cmp_w1_k[i], c_cmp_w2_k[i])
            cw_v = _compress_weights(c_cmp_pe_v[i], c_cmp_w1_v[i], c_cmp_w2_v[i])
            o_p, st_p = _c_layer(xp, bp, tp, 0, None, w_in, cw_k, cw_v, tabs_p, sample=False)
            past = (kct_c, vct_c, kst_c, vst_c, wink_t, winv_t, state_c_win_k[i], state_c_win_v[i], page_table, i)
            o_s, st_s = _c_layer(xs, bs, ts, past_len, past, w_in, cw_k, cw_v, tabs_s, sample=True)
            c_p.append(st_p)
            c_s.append(st_s)
            xp = _outproj_ln(xp, o_p, o_p, 0, 1, w_out, row1(ln1_g), row1(ln1_b), tm=PROMPT_TM)
            xs = _outproj_ln(xs, o_s, o_s, 0, 1, w_out, row1(ln1_g), row1(ln1_b), tm=bs * ts)
        wg, wu, wd = ffn_w_gate[layer].astype(BF16), ffn_w_up[layer].astype(BF16), ffn_w_down[layer].astype(BF16)
        xp = _ffn_ln(xp, wg, wu, wd, row1(ln2_g), row1(ln2_b), tm=PROMPT_TM, tf=FFN_TF)
        xs = _ffn_ln(xs, wg, wu, wd, row1(ln2_g), row1(ln2_b), tm=bs * ts, tf=FFN_TF)
    stk = lambda lst, j: jnp.stack([e[j] for e in lst], 0)
    return (xp.reshape(bp, tp, D_MODEL), xs.reshape(bs, ts, D_MODEL),
            stk(ab_p, 0), stk(ab_p, 1), stk(ab_p, 2), stk(ab_p, 3),
            stk(c_p, 0), stk(c_p, 1), stk(c_p, 2), stk(c_p, 3), stk(c_p, 4), stk(c_p, 5),
            stk(ab_s, 0), stk(ab_s, 1), stk(ab_s, 2), stk(ab_s, 3),
            stk(c_s, 0), stk(c_s, 1), stk(c_s, 2), stk(c_s, 3), stk(c_s, 4), stk(c_s, 5))
```

```python
import functools

import numpy as np
import jax
import jax.numpy as jnp
from jax import lax
from jax.experimental import pallas as pl
from jax.experimental.pallas import tpu as pltpu

D_MODEL = 1024
DEPTH = 4
PAGE_SIZE = 128
HEAD_DIM = 64
ROPE_THETA = 10000.0
A_HEADS = 8
A_KV_HEADS = 2
IDX_HEADS = 4
IDX_DIM = 64
A_TOPK_MAX = 256
A_WIDTH = A_HEADS * HEAD_DIM
B_WIDTH = D_MODEL // 2
CONV_W = 3
C_HEADS = 16
C_KV_HEADS = 4
C_WIDTH = C_HEADS * HEAD_DIM
CMP_BLOCK = 32
CMP_STRIDE = 16
SLC_BLOCK = 64
SLC_TOPN = 16
WINDOW = 512
D_FF = ((8 * D_MODEL + 3 * 256 - 1) // (3 * 256)) * 256
LN_EPS = 1e-5
ALPHA = (2 * DEPTH) ** 0.25
Q_BLOCK = 128

F32 = jnp.float32
BF16 = jnp.bfloat16
I32 = jnp.int32

LANES = 128
SUBLANES = 8
VMEM_LIMIT_BYTES = 56 * 1024 * 1024
MASKED = -1e30
Q_SCALE = HEAD_DIM ** -0.5 * float(np.log2(np.e))
INT_MIN = -(2 ** 31)
NEG_INF_KEY = int(np.int32(np.uint32(0xFF800000) ^ np.uint32(0x7FFFFFFF)))
NT_DIMS = (((1,), (1,)), ((), ()))

A_PERM = tuple(h for j in range(4) for h in (j, 4 + j))
C_PERM = tuple(h for pr in range(2) for j in range(4) for h in (8 * pr + j, 8 * pr + 4 + j))

AB_COLS = 2688
C_COLS = 2688
C_ROPED = C_WIDTH + 3 * C_KV_HEADS * HEAD_DIM


def _params(*sem):
    return pltpu.CompilerParams(dimension_semantics=sem, vmem_limit_bytes=VMEM_LIMIT_BYTES)


def _cdiv(a, b):
    return (a + b - 1) // b


def _rope128(r, c, s, first_half):
    sw = jnp.where(first_half, pltpu.roll(r, 96, 1), pltpu.roll(r, 32, 1))
    return r * c + sw * s


def _half_masked(q_bf16, mask):
    return jnp.where(mask, q_bf16.astype(F32), 0.0).astype(BF16)


def _flash(qs, heads, k_ref, v_ref, col, c_lo, n_chunks, ac, bias_fn, acc_ref, m_ref, l_ref):
    tq = qs.shape[0] // heads
    reps = ac // LANES
    acc_ref[...] = jnp.zeros(acc_ref.shape, F32)
    m_ref[...] = jnp.full(m_ref.shape, MASKED, F32)
    l_ref[...] = jnp.zeros(l_ref.shape, F32)

    def body(ci, carry):
        c = c_lo + ci
        off = pl.multiple_of(c * ac, ac)
        s = lax.dot_general(qs, k_ref[pl.ds(off, ac), col:col + LANES], NT_DIMS, preferred_element_type=F32)
        head_bias = bias_fn(c)
        ps, alphas = [], []
        for h in range(heads):
            rows = slice(h * tq, (h + 1) * tq)
            sh = s[rows] + head_bias(h)
            m_old = m_ref[rows]
            m_new = jnp.maximum(m_old, jnp.max(sh, axis=1, keepdims=True))
            alpha = jnp.exp2(m_old - m_new)
            p = jnp.exp2(sh - jnp.tile(m_new, (1, reps)))
            l_ref[rows] = alpha * l_ref[rows] + jnp.sum(p, axis=1, keepdims=True)
            m_ref[rows] = m_new
            ps.append(p.astype(BF16))
            alphas.append(alpha)
        pv = jnp.dot(jnp.concatenate(ps, 0), v_ref[pl.ds(off, ac), col:col + LANES], preferred_element_type=F32)
        for h in range(heads):
            rows = slice(h * tq, (h + 1) * tq)
            acc_ref[rows] = alphas[h] * acc_ref[rows] + pv[rows]
        return carry

    lax.fori_loop(0, n_chunks, body, 0)
    acc_ref[...] = jnp.where(m_ref[...] > 0.5 * MASKED, acc_ref[...] / l_ref[...], 0.0)


def _layernorm(y, g, b):
    mu = jnp.mean(y, axis=-1, keepdims=True)
    d = y - mu
    var = jnp.mean(d * d, axis=-1, keepdims=True)
    return d * lax.rsqrt(var + LN_EPS) * g + b


def _proj_ab_kernel(*refs, tm, seq_tiles, sample, dec_seq):
    if sample:
        (x_ref, w_ref, c_ref, s_ref, cx_ref, sx_ref, cw_ref, p1_ref, p2_ref,
         q_ref, iq_ref, k_ref, v_ref, ikw_ref, kvb_ref, ikb_ref, bo_ref, uo_ref, ubuf) = refs
        prev_refs = (None, p1_ref, p2_ref)
    else:
        (x_ref, w_ref, c_ref, s_ref, cx_ref, sx_ref, cw_ref,
         q_ref, iq_ref, k_ref, v_ref, ikw_ref, kvb_ref, ikb_ref, bo_ref, uo_ref, ubuf) = refs
    i = pl.program_id(0)
    y = jnp.dot(x_ref[...].astype(BF16), w_ref[...], preferred_element_type=F32)
    lane = lax.broadcasted_iota(I32, (tm, LANES), 1)
    first_half = (lane & (HEAD_DIM - 1)) < HEAD_DIM // 2
    c = c_ref[...]
    s = s_ref[...]
    ro = [_rope128(y[:, j * LANES:(j + 1) * LANES], c, s, first_half) for j in range(7)]
    ro.append(_rope128(y[:, 7 * LANES:8 * LANES], cx_ref[...], sx_ref[...], first_half))
    for j in range(4):
        q_ref[:, j * LANES:(j + 1) * LANES] = (ro[j] * Q_SCALE).astype(BF16)
    k = ro[4]
    v = y[:, 1024:1152]
    k_ref[...] = k
    v_ref[...] = v
    kvb_ref[:, 0:LANES] = k.astype(BF16)
    kvb_ref[:, LANES:2 * LANES] = v.astype(BF16)
    for j in range(2):
        iq_ref[:, j * LANES:(j + 1) * LANES] = (ro[5 + j] * IDX_DIM ** -0.5).astype(BF16)
    ikw = ro[7]
    ikw_ref[...] = ikw
    ikb_ref[...] = jnp.where(lane < IDX_DIM, ikw, pltpu.roll(ikw, IDX_DIM, 1)).astype(BF16)

    gate_b = y[:, 1152:1664]
    u = y[:, 1664:2176] * y[:, 2176:2688]

    @pl.when(i % seq_tiles == 0)
    def _():
        ubuf[0:SUBLANES, :] = jnp.zeros((SUBLANES, B_WIDTH), F32)

    @pl.when(i % seq_tiles != 0)
    def _():
        ubuf[0:SUBLANES, :] = ubuf[tm:tm + SUBLANES, :]

    ubuf[SUBLANES:tm + SUBLANES, :] = u
    cw = cw_ref[...]
    conv = u * cw[CONV_W - 1:CONV_W, :]
    if sample:
        t = lax.broadcasted_iota(I32, (tm, 1), 0) % dec_seq
    for d in range(1, CONV_W):
        ud = ubuf[SUBLANES - d:tm + SUBLANES - d, :]
        if sample:
            ud = jnp.where(t >= d, ud, prev_refs[d][...])
        conv = conv + ud * cw[CONV_W - 1 - d:CONV_W - d, :]
    bo_ref[...] = (gate_b * conv).astype(BF16)
    if sample:
        uo_ref[...] = u
    else:
        uo_ref[...] = u[tm - SUBLANES:tm, :]


def _proj_ab(x2d, w, tabs, conv_w8, *, tm, seq_tiles, sample, prevs=None, dec_seq=1):
    m = x2d.shape[0]
    nt = tabs[0].shape[0] // tm
    row = lambda i: (i, 0)
    const = lambda i: (0, 0)
    tab = lambda i: (i % nt, 0)
    in_specs = [pl.BlockSpec((tm, D_MODEL), row), pl.BlockSpec((D_MODEL, AB_COLS), const)]
    in_specs += [pl.BlockSpec((tm, LANES), tab)] * 4
    in_specs += [pl.BlockSpec((SUBLANES, B_WIDTH), const)]
    args = [x2d, w, *tabs, conv_w8]
    if sample:
        in_specs += [pl.BlockSpec((tm, B_WIDTH), row)] * 2
        args += list(prevs)
    u_rows = tm if sample else SUBLANES
    out_shape = [
        jax.ShapeDtypeStruct((m, A_WIDTH), BF16),
        jax.ShapeDtypeStruct((m, IDX_HEADS * IDX_DIM), BF16),
        jax.ShapeDtypeStruct((m, LANES), F32),
        jax.ShapeDtypeStruct((m, LANES), F32),
        jax.ShapeDtypeStruct((m, LANES), F32),
        jax.ShapeDtypeStruct((m, 2 * LANES), BF16),
        jax.ShapeDtypeStruct((m, LANES), BF16),
        jax.ShapeDtypeStruct((m, B_WIDTH), BF16),
        jax.ShapeDtypeStruct((m // tm * u_rows, B_WIDTH), F32),
    ]
    out_specs = [
        pl.BlockSpec((tm, A_WIDTH), row), pl.BlockSpec((tm, IDX_HEADS * IDX_DIM), row),
        pl.BlockSpec((tm, LANES), row), pl.BlockSpec((tm, LANES), row), pl.BlockSpec((tm, LANES), row),
        pl.BlockSpec((tm, 2 * LANES), row), pl.BlockSpec((tm, LANES), row),
        pl.BlockSpec((tm, B_WIDTH), row), pl.BlockSpec((u_rows, B_WIDTH), row),
    ]
    kern = functools.partial(_proj_ab_kernel, tm=tm, seq_tiles=seq_tiles, sample=sample, dec_seq=dec_seq)
    return pl.pallas_call(
        kern, grid=(m // tm,), in_specs=in_specs, out_specs=out_specs, out_shape=out_shape,
        scratch_shapes=[pltpu.VMEM((tm + SUBLANES, B_WIDTH), F32)],
        compiler_params=_params("arbitrary"))(*args)


def _count(key_ref, n_chunks, sc, nq, keys_on_sublanes, pred):
    def chunks(body, init):
        if isinstance(n_chunks, int):
            acc = init
            for c in range(n_chunks):
                acc = body(c, acc)
            return acc
        return lax.fori_loop(0, n_chunks, body, init)

    if keys_on_sublanes:
        def body(c, acc):
            part = jnp.where(pred(key_ref[c], c), 1.0, 0.0)
            rows = sc
            while rows > SUBLANES:
                rows //= 2
                part = part[:rows] + part[rows:2 * rows]
            return acc + part
        assert sc % SUBLANES == 0 and (sc // SUBLANES) & (sc // SUBLANES - 1) == 0
        return jnp.sum(chunks(body, jnp.zeros((SUBLANES, nq), F32)), axis=0, keepdims=True)

    def body(c, acc):
        m = jnp.where(pred(key_ref[c], c), 1.0, 0.0)
        part = m[:, 0:LANES]
        for j in range(1, sc // LANES):
            part = part + m[:, j * LANES:(j + 1) * LANES]
        return acc + part
    return jnp.sum(chunks(body, jnp.zeros((nq, LANES), F32)), axis=1, keepdims=True)


def _topk_bias(key_ref, bias_ref, n_sc, sc, ac, nq, topk, klen, keys_on_sublanes=False):
    ratio = sc // ac
    kshape, kaxis, vshape = ((sc, nq), 0, (1, nq)) if keys_on_sublanes else ((nq, sc), 1, (nq, 1))
    kiota = lax.broadcasted_iota(I32, kshape, kaxis)
    count = functools.partial(_count, key_ref, n_sc, sc, nq, keys_on_sublanes)
    kf = jnp.float32(topk)
    n_all = jnp.asarray(n_sc * sc, F32)

    def descend(n_static):
        count_n = functools.partial(_count, key_ref, n_static, sc, nq, keys_on_sublanes)

        def bit_body(b, carry):
            thr, n_ge = carry
            cand = thr + jnp.left_shift(jnp.int32(1), 31 - b)
            cnt = count_n(lambda key, c: key >= cand)
            take = cnt >= kf
            return jnp.where(take, cand, thr), jnp.where(take, cnt, n_ge)
        return lax.fori_loop(0, 32, bit_body, (jnp.full(vshape, INT_MIN, I32), jnp.full(vshape, n_all, F32)))

    if isinstance(n_sc, int):
        thr, n_ge = descend(n_sc)
    else:
        thr, n_ge = lax.switch(n_sc - 1, [functools.partial(descend, n) for n in range(1, klen // sc + 1)])

    def tie_search():
        need = kf - count(lambda key, c: key > thr)

        def tie_body(b, last):
            cand = last + jnp.left_shift(jnp.int32(1), klen.bit_length() - 1 - b)
            cnt = count(lambda key, c: (key == thr) & (c * sc + kiota < cand))
            return jnp.where(cnt < need, cand, last)
        return lax.fori_loop(0, klen.bit_length(), tie_body, jnp.zeros(vshape, I32))

    tied = (n_ge > kf) & (thr > jnp.int32(NEG_INF_KEY))
    last = lax.cond(jnp.max(jnp.where(tied, 1.0, 0.0)) > 0.5, tie_search, lambda: jnp.full(vshape, klen, I32))

    def bias_body(c, carry):
        key = key_ref[c]
        sel = (key > thr) | ((key == thr) & (c * sc + kiota <= last))
        sel = sel & (key > jnp.int32(NEG_INF_KEY))
        bias = jnp.where(sel, 0.0, MASKED)
        if keys_on_sublanes:
            for r in range(sc // LANES):
                blk = bias[r * LANES:(r + 1) * LANES, :].T
                bias_ref[c * ratio + r * LANES // ac, :, (r * LANES) % ac:(r * LANES) % ac + LANES] = blk
        else:
            for r in range(ratio):
                bias_ref[c * ratio + r] = bias[:, r * ac:(r + 1) * ac]
        return carry
    lax.fori_loop(0, n_sc, bias_body, 0)


def _dsa_kernel(q_ref, iq_ref, ikw_ref, ikb_ref, kvb_ref, o_ref, key_ref, bias_ref, acc_ref, m_ref, l_ref,
                *, tq, klen, topk, sc, ac):
    i = pl.program_id(1)
    n_sc = _cdiv((i + 1) * tq, sc)
    ratio = sc // ac
    lane = lax.broadcasted_iota(I32, (tq, LANES), 1)
    lo_half = lane < HEAD_DIM
    hi_half = jnp.logical_not(lo_half)

    iqs = jnp.concatenate([_half_masked(iq_ref[:, (h // 2) * LANES:(h // 2 + 1) * LANES],
                                        lo_half if h % 2 == 0 else hi_half) for h in range(IDX_HEADS)], 0)
    iw_t = ikw_ref[...].T
    pos_l = i * tq + lax.broadcasted_iota(I32, (1, tq), 1)
    kidx = lax.broadcasted_iota(I32, (sc, tq), 0)

    def score_body(c, carry):
        off = pl.multiple_of(c * sc, sc)
        logits = lax.dot_general(ikb_ref[pl.ds(off, sc), :], iqs, NT_DIMS, preferred_element_type=F32)
        sco = jnp.zeros((sc, tq), F32)
        for h in range(IDX_HEADS):
            sco = sco + jnp.maximum(logits[:, h * tq:(h + 1) * tq], 0.0) * iw_t[IDX_DIM + h:IDX_DIM + h + 1, :]
        sco = jnp.where(sco == 0.0, 0.0, sco)
        bits = pltpu.bitcast(sco, I32)
        key = jnp.where(bits < 0, bits ^ jnp.int32(0x7FFFFFFF), bits)
        key_ref[c] = jnp.where(c * sc + kidx <= pos_l, key, jnp.int32(NEG_INF_KEY))
        return carry
    lax.fori_loop(0, n_sc, score_body, 0)
    _topk_bias(key_ref, bias_ref, n_sc, sc, ac, tq, topk, klen, keys_on_sublanes=True)

    half_slots = A_HEADS // 2
    qs = jnp.concatenate([_half_masked(q_ref[:, (h % half_slots) * LANES:(h % half_slots + 1) * LANES],
                                       lo_half if h < half_slots else hi_half) for h in range(A_HEADS)], 0)
    def shared_bias(c):
        bias = bias_ref[c]
        return lambda h: bias
    _flash(qs, A_HEADS, kvb_ref, kvb_ref.at[:, LANES:2 * LANES], 0, 0, n_sc * ratio, ac, shared_bias,
           acc_ref, m_ref, l_ref)
    for j in range(half_slots):
        o_lo = acc_ref[j * tq:(j + 1) * tq]
        o_hi = acc_ref[(half_slots + j) * tq:(half_slots + j + 1) * tq]
        o_ref[:, j * LANES:(j + 1) * LANES] = jnp.where(lo_half, o_lo, o_hi).astype(BF16)


def _dsa_attention(q, iq, ikw, ikb, kvb, *, tq, topk, sc, ac):
    nb, tlen, _ = q.shape
    klen = ikb.shape[1]
    qspec = lambda w: pl.BlockSpec((None, tq, w), lambda b, i: (b, i, 0))
    kspec = lambda w: pl.BlockSpec((None, klen, w), lambda b, i: (b, 0, 0))
    kern = functools.partial(_dsa_kernel, tq=tq, klen=klen, topk=topk, sc=sc, ac=ac)
    return pl.pallas_call(
        kern, grid=(nb, tlen // tq),
        in_specs=[qspec(A_WIDTH), qspec(IDX_HEADS * IDX_DIM), qspec(LANES), kspec(LANES), kspec(2 * LANES)],
        out_specs=qspec(A_WIDTH),
        out_shape=jax.ShapeDtypeStruct((nb, tlen, A_WIDTH), BF16),
        scratch_shapes=[pltpu.VMEM((klen // sc, sc, tq), I32), pltpu.VMEM((klen // ac, tq, ac), F32)]
        + [pltpu.VMEM((A_HEADS * tq, LANES), F32)] * 3,
        compiler_params=_params("parallel", "arbitrary"))(q, iq, ikw, ikb, kvb)


def _page_specs(shape, layer, seqs, n_pages):
    specs = []
    for s in range(seqs):
        for p in range(n_pages):
            specs.append(pl.BlockSpec((None, None) + shape,
                                      lambda i, pt, s=s, p=p: (layer, pt[i * seqs + s, p], 0, 0)))
    return specs


def _softmax_rows(s):
    m = jnp.max(s, axis=1, keepdims=True)
    p = jnp.exp2(s - m)
    return p, jnp.sum(p, axis=1, keepdims=True)


def _dsa_sample_kernel(pt_ref, *refs, seqs, n_pages, pos0, topk):
    del pt_ref
    np_all = seqs * n_pages
    qs_ref, iq_ref, ikw_ref, knew_ref, vnew_ref, iknew_ref = refs[:6]
    kt_pages = refs[6:6 + np_all]
    vt_pages = refs[6 + np_all:6 + 2 * np_all]
    ikt_pages = refs[6 + 2 * np_all:6 + 3 * np_all]
    o_ref, ktb, vtb, iktb, key_ref, bias_ref = refs[6 + 3 * np_all:]
    tq = SUBLANES
    past = n_pages * PAGE_SIZE
    klen = past + LANES
    for s in range(seqs):
        for p in range(n_pages):
            cols = slice(p * PAGE_SIZE, (p + 1) * PAGE_SIZE)
            ktb[s, :, cols] = kt_pages[s * n_pages + p][...].astype(BF16)
            vtb[s, :, cols] = vt_pages[s * n_pages + p][...].astype(BF16)
            iktb[s, :, cols] = ikt_pages[s * n_pages + p][...].astype(BF16)

    pos = pos0 + lax.broadcasted_iota(I32, (tq, 1), 0)
    kidx = lax.broadcasted_iota(I32, (tq, klen), 1)
    for s in range(seqs):
        iq = iq_ref[s]
        logits = jnp.concatenate(
            [jnp.dot(iq, iktb[s], preferred_element_type=F32),
             lax.dot_general(iq, iknew_ref[s], NT_DIMS, preferred_element_type=F32)], 1)
        sco = jnp.zeros((tq, klen), F32)
        for h in range(IDX_HEADS):
            sco = sco + jnp.maximum(logits[h * tq:(h + 1) * tq], 0.0) * ikw_ref[s, :, IDX_DIM + h:IDX_DIM + h + 1]
        sco = jnp.where(sco == 0.0, 0.0, sco)
        bits = pltpu.bitcast(sco, I32)
        key = jnp.where(bits < 0, bits ^ jnp.int32(0x7FFFFFFF), bits)
        key_ref[0, s * tq:(s + 1) * tq, :] = jnp.where(kidx <= pos, key, jnp.int32(NEG_INF_KEY))
    _topk_bias(key_ref, bias_ref, 1, klen, klen, seqs * tq, topk, klen)

    lane = lax.broadcasted_iota(I32, (tq, LANES), 1)
    lo_half = lane < HEAD_DIM
    half_slots = A_HEADS // 2
    for s in range(seqs):
        qs = qs_ref[s]
        sc_all = jnp.concatenate(
            [jnp.dot(qs, ktb[s], preferred_element_type=F32),
             lax.dot_general(qs, knew_ref[s], NT_DIMS, preferred_element_type=F32)], 1)
        p, l = _softmax_rows(sc_all + jnp.tile(bias_ref[0, s * tq:(s + 1) * tq, :], (A_HEADS, 1)))
        pb = p.astype(BF16)
        o = (lax.dot_general(pb[:, :past], vtb[s], NT_DIMS, preferred_element_type=F32)
             + jnp.dot(pb[:, past:], vnew_ref[s], preferred_element_type=F32)) / l
        for j in range(half_slots):
            o_ref[s, :, j * LANES:(j + 1) * LANES] = jnp.where(
                lo_half, o[j * tq:(j + 1) * tq], o[(half_slots + j) * tq:(half_slots + j + 1) * tq]).astype(BF16)


def _dsa_sample(qs, iq, ikw, knew, vnew, iknew, kt_cache, vt_cache, ikt_cache, page_table, layer, *, seqs, pos0,
                topk):
    nb = qs.shape[0]
    n_pages = page_table.shape[1]
    klen = n_pages * PAGE_SIZE + LANES
    tq = SUBLANES
    seq_spec = lambda a: pl.BlockSpec((seqs,) + a.shape[1:], lambda i, pt: (i, 0, 0))
    in_specs = [seq_spec(a) for a in (qs, iq, ikw, knew, vnew, iknew)]
    in_specs += _page_specs(kt_cache.shape[2:], layer, seqs, n_pages)
    in_specs += _page_specs(vt_cache.shape[2:], layer, seqs, n_pages)
    in_specs += _page_specs(ikt_cache.shape[2:], layer, seqs, n_pages)
    np_all = seqs * n_pages
    kvw = A_KV_HEADS * HEAD_DIM
    grid_spec = pltpu.PrefetchScalarGridSpec(
        num_scalar_prefetch=1, grid=(nb // seqs,), in_specs=in_specs,
        out_specs=pl.BlockSpec((seqs, tq, A_WIDTH), lambda i, pt: (i, 0, 0)),
        scratch_shapes=[pltpu.VMEM((seqs, kvw, n_pages * PAGE_SIZE), BF16),
                        pltpu.VMEM((seqs, kvw, n_pages * PAGE_SIZE), BF16),
                        pltpu.VMEM((seqs, IDX_DIM, n_pages * PAGE_SIZE), BF16),
                        pltpu.VMEM((1, seqs * tq, klen), I32), pltpu.VMEM((1, seqs * tq, klen), F32)])
    kern = functools.partial(_dsa_sample_kernel, seqs=seqs, n_pages=n_pages, pos0=pos0, topk=topk)
    return pl.pallas_call(
        kern, grid_spec=grid_spec, out_shape=jax.ShapeDtypeStruct((nb, tq, A_WIDTH), BF16),
        compiler_params=_params("arbitrary"))(
            page_table, qs, iq, ikw, knew, vnew, iknew,
            *([kt_cache] * np_all), *([vt_cache] * np_all), *([ikt_cache] * np_all))


def _gelu_tanh(x):
    return 0.5 * x * (1.0 + jnp.tanh(np.sqrt(2.0 / np.pi) * (x + 0.044715 * x * x * x)))


def _compress_sample_kernel(pt_ref, pet_ref, wab_ref, w2_ref, *refs, seqs, n_pages):
    del pt_ref
    pages = refs[:seqs * n_pages]
    o_ref, tok_ref = refs[seqs * n_pages:]
    width = C_KV_HEADS * HEAD_DIM
    for s in range(seqs):
        for p in range(n_pages):
            base = (s * n_pages + p) * PAGE_SIZE
            for c in range(width // LANES):
                blk = pages[s * n_pages + p][c * LANES:(c + 1) * LANES, :]
                tok_ref[c, base:base + PAGE_SIZE, :] = blk.T
    rows = seqs * n_pages * PAGE_SIZE // CMP_STRIDE
    half = CMP_BLOCK // 2
    ra = jnp.zeros((rows, width), F32)
    rb = jnp.zeros((rows, width), F32)
    for j in range(half):
        xj = jnp.concatenate([tok_ref[c, pl.ds(j, rows, stride=CMP_STRIDE), :] for c in range(width // LANES)], 1)
        ra = ra + jnp.dot((xj + pet_ref[j:j + 1, :]).astype(BF16), wab_ref[j], preferred_element_type=F32)
        rb = rb + jnp.dot((xj + pet_ref[half + j:half + j + 1, :]).astype(BF16), wab_ref[half + j],
                          preferred_element_type=F32)
    hid = _gelu_tanh(ra + pltpu.roll(rb, rows - 1, 0))
    out = jnp.dot(hid.astype(BF16), w2_ref[...], preferred_element_type=F32)
    o_ref[...] = out.reshape(o_ref.shape).astype(BF16)


def _compress_sample(cache_t, page_table, layer, pet, wab, w2, *, seqs):
    nb, n_pages = page_table.shape
    width = C_KV_HEADS * HEAD_DIM
    r = n_pages * PAGE_SIZE // CMP_STRIDE
    const2 = lambda i, pt: (0, 0)
    grid_spec = pltpu.PrefetchScalarGridSpec(
        num_scalar_prefetch=1, grid=(nb // seqs,),
        in_specs=[pl.BlockSpec(pet.shape, const2), pl.BlockSpec(wab.shape, lambda i, pt: (0, 0, 0)),
                  pl.BlockSpec(w2.shape, const2)] + _page_specs(cache_t.shape[2:], layer, seqs, n_pages),
        out_specs=pl.BlockSpec((seqs, r, width), lambda i, pt: (i, 0, 0)),
        scratch_shapes=[pltpu.VMEM((width // LANES, seqs * n_pages * PAGE_SIZE, LANES), F32)])
    return pl.pallas_call(
        functools.partial(_compress_sample_kernel, seqs=seqs, n_pages=n_pages), grid_spec=grid_spec,
        out_shape=jax.ShapeDtypeStruct((nb, r, width), BF16),
        compiler_params=_params("arbitrary"))(page_table, pet, wab, w2, *([cache_t] * (seqs * n_pages)))


def _nsa_sample_kernel(pt_ref, *refs, seqs, n_pages, pos0, nblk):
    del pt_ref
    np_all = seqs * n_pages
    (qs_ref, g_ref, kc_ref, vc_ref, ksn_ref, vsn_ref, kwn_ref, vwn_ref, wink_ref, winv_ref, c2st_ref,
     e_ref) = refs[:12]
    kst_pages = refs[12:12 + np_all]
    vst_pages = refs[12 + np_all:12 + 2 * np_all]
    o_ref, kst, vst, part_ref, bias_ref = refs[12 + 2 * np_all:]
    tq = SUBLANES
    hpg = C_HEADS // C_KV_HEADS
    nh = 2 * hpg
    npair = C_KV_HEADS // 2
    past = n_pages * PAGE_SIZE
    ls = past + LANES
    nbuf = wink_ref.shape[-1]
    lw = nbuf + LANES
    ncp = kc_ref.shape[1]
    for s in range(seqs):
        for p in range(n_pages):
            cols = slice(p * PAGE_SIZE, (p + 1) * PAGE_SIZE)
            kst[s, :, cols] = kst_pages[s * n_pages + p][...].astype(BF16)
            vst[s, :, cols] = vst_pages[s * n_pages + p][...].astype(BF16)

    rows = nh * tq
    t_row = pos0 + lax.broadcasted_iota(I32, (rows, 1), 0) % tq
    cmp_visible = lax.broadcasted_iota(I32, (rows, ncp), 1) * CMP_STRIDE + (CMP_BLOCK - 1) <= t_row
    lane = lax.broadcasted_iota(I32, (tq, LANES), 1)
    lo_half = lane < HEAD_DIM
    pair_cols = lambda pr: slice(pr * LANES, (pr + 1) * LANES)

    psums = []
    for s in range(seqs):
        for pr in range(npair):
            qp = qs_ref[s, pr * rows:(pr + 1) * rows, :]
            s_c = lax.dot_general(qp, kc_ref[s, :, pair_cols(pr)], NT_DIMS, preferred_element_type=F32)
            s_c = jnp.where(cmp_visible, s_c, -jnp.inf)
            m = jnp.max(s_c, axis=1, keepdims=True)
            m = jnp.where(m > -jnp.inf, m, 0.0)
            p = jnp.exp2(s_c - m)
            den = jnp.sum(p, axis=1, keepdims=True)
            p = p / jnp.where(den > 0.0, den, 1.0)
            part_ref[s, pr] = jnp.dot(p.astype(BF16), vc_ref[s, :, pair_cols(pr)], preferred_element_type=F32)
            for half in range(2):
                acc = p[half * hpg * tq:(half * hpg + 1) * tq]
                for h in range(1, hpg):
                    acc = acc + p[(half * hpg + h) * tq:(half * hpg + h + 1) * tq]
                psums.append(acc)

    ng_rows = seqs * C_KV_HEADS * tq
    sel_rows = _cdiv(ng_rows, LANES) * LANES
    psum = jnp.concatenate(psums + [jnp.zeros((sel_rows - ng_rows, ncp), F32)] * (sel_rows > ng_rows), 0)
    t_lane = pos0 + lax.broadcasted_iota(I32, (1, sel_rows), 1) % tq
    selm = _select_blocks(psum, c2st_ref[...], t_lane, nblk)[:ng_rows]
    tg = pos0 + lax.broadcasted_iota(I32, (ng_rows, 1), 0) % tq
    ex = jnp.dot(selm.astype(BF16), e_ref[...], preferred_element_type=F32)
    ok = (ex > 0.5) & (lax.broadcasted_iota(I32, (ng_rows, ls), 1) <= tg)
    bias_ref[...] = jnp.where(ok, 0.0, MASKED)

    wpos = jnp.concatenate([pos0 - nbuf + lax.broadcasted_iota(I32, (rows, nbuf), 1),
                            pos0 + lax.broadcasted_iota(I32, (rows, LANES), 1)], 1)
    wbias = jnp.where((wpos >= 0) & (wpos <= t_row) & (t_row - wpos < WINDOW), 0.0, MASKED)

    for s in range(seqs):
        for pr in range(npair):
            qp = qs_ref[s, pr * rows:(pr + 1) * rows, :]
            feat = slice(pr * LANES, (pr + 1) * LANES)
            gbase = (s * C_KV_HEADS + 2 * pr) * tq
            bias = jnp.concatenate([jnp.tile(bias_ref[gbase:gbase + tq, :], (hpg, 1)),
                                    jnp.tile(bias_ref[gbase + tq:gbase + 2 * tq, :], (hpg, 1))], 0)
            sc_s = jnp.concatenate(
                [jnp.dot(qp, kst[s, feat, :], preferred_element_type=F32),
                 lax.dot_general(qp, ksn_ref[s, :, pair_cols(pr)], NT_DIMS, preferred_element_type=F32)], 1)
            p, l = _softmax_rows(sc_s + bias)
            pb = p.astype(BF16)
            o_s = (lax.dot_general(pb[:, :past], vst[s, feat, :], NT_DIMS, preferred_element_type=F32)
                   + jnp.dot(pb[:, past:], vsn_ref[s, :, pair_cols(pr)], preferred_element_type=F32)) / l
            sc_w = jnp.concatenate(
                [jnp.dot(qp, wink_ref[s, feat, :].astype(BF16), preferred_element_type=F32),
                 lax.dot_general(qp, kwn_ref[s, :, pair_cols(pr)], NT_DIMS, preferred_element_type=F32)], 1)
            p, l = _softmax_rows(sc_w + wbias)
            pb = p.astype(BF16)
            o_w = (lax.dot_general(pb[:, :nbuf], winv_ref[s, feat, :].astype(BF16), NT_DIMS,
                                   preferred_element_type=F32)
                   + jnp.dot(pb[:, nbuf:], vwn_ref[s, :, pair_cols(pr)], preferred_element_type=F32)) / l

            def gate(branch):
                cols = [3 * (hpg * (2 * pr + h // hpg) + h % hpg) + branch for h in range(nh)]
                return jnp.concatenate([g_ref[s, :, c:c + 1] for c in cols], 0)
            o = gate(0) * part_ref[s, pr] + gate(1) * o_s + gate(2) * o_w
            for j in range(hpg):
                slot = pr * hpg + j
                o_ref[s, :, slot * LANES:(slot + 1) * LANES] = jnp.where(
                    lo_half, o[j * tq:(j + 1) * tq], o[(hpg + j) * tq:(hpg + j + 1) * tq]).astype(BF16)


def _nsa_sample(qs, gates, kcmp, vcmp, ksn, vsn, kwn, vwn, wink_t, winv_t, c2s, emat, kst_cache, vst_cache,
                page_table, layer, *, seqs, pos0, n_sblk):
    nb, n_pages = page_table.shape
    tq = SUBLANES
    width = C_KV_HEADS * HEAD_DIM
    past = n_pages * PAGE_SIZE
    seq_spec = lambda a: pl.BlockSpec((seqs,) + a.shape[1:], lambda i, pt: (i, 0, 0))
    win_spec = pl.BlockSpec((None, seqs) + wink_t.shape[2:], lambda i, pt: (layer, i, 0, 0))
    const2 = lambda i, pt: (0, 0)
    in_specs = [seq_spec(a) for a in (qs, gates, kcmp, vcmp, ksn, vsn, kwn, vwn)]
    in_specs += [win_spec, win_spec, pl.BlockSpec(c2s.shape, const2), pl.BlockSpec(emat.shape, const2)]
    in_specs += _page_specs(kst_cache.shape[2:], layer, seqs, n_pages)
    in_specs += _page_specs(vst_cache.shape[2:], layer, seqs, n_pages)
    np_all = seqs * n_pages
    rows = 2 * (C_HEADS // C_KV_HEADS) * tq
    grid_spec = pltpu.PrefetchScalarGridSpec(
        num_scalar_prefetch=1, grid=(nb // seqs,), in_specs=in_specs,
        out_specs=pl.BlockSpec((seqs, tq, C_WIDTH), lambda i, pt: (i, 0, 0)),
        scratch_shapes=[pltpu.VMEM((seqs, width, past), BF16), pltpu.VMEM((seqs, width, past), BF16),
                        pltpu.VMEM((seqs, C_KV_HEADS // 2, rows, LANES), F32),
                        pltpu.VMEM((seqs * C_KV_HEADS * tq, past + LANES), F32)])
    kern = functools.partial(_nsa_sample_kernel, seqs=seqs, n_pages=n_pages, pos0=pos0,
                             nblk=_cdiv(n_sblk, SUBLANES) * SUBLANES)
    return pl.pallas_call(
        kern, grid_spec=grid_spec, out_shape=jax.ShapeDtypeStruct((nb, tq, C_WIDTH), BF16),
        compiler_params=_params("arbitrary"))(
            page_table, qs, gates, kcmp, vcmp, ksn, vsn, kwn, vwn, wink_t, winv_t, c2s, emat,
            *([kst_cache] * np_all), *([vst_cache] * np_all))


def _proj_c_kernel(x_ref, w_ref, c_ref, s_ref, q_ref, kc_ref, ks_ref, kw_ref, vc_ref, vs_ref, vw_ref,
                   g_ref, kvb_ref, *, tm):
    y = jnp.dot(x_ref[...].astype(BF16), w_ref[...], preferred_element_type=F32)
    lane = lax.broadcasted_iota(I32, (tm, LANES), 1)
    first_half = (lane & (HEAD_DIM - 1)) < HEAD_DIM // 2
    c = c_ref[...]
    s = s_ref[...]
    ro = [_rope128(y[:, j * LANES:(j + 1) * LANES], c, s, first_half) for j in range(C_ROPED // LANES)]
    for j in range(8):
        q_ref[:, j * LANES:(j + 1) * LANES] = (ro[j] * Q_SCALE).astype(BF16)
    for n, ref in enumerate((kc_ref, ks_ref, kw_ref)):
        for j in range(2):
            ref[:, j * LANES:(j + 1) * LANES] = ro[8 + 2 * n + j]
    for n, ref in enumerate((vc_ref, vs_ref, vw_ref)):
        ref[...] = y[:, C_ROPED + 256 * n:C_ROPED + 256 * (n + 1)]
    g = y[:, C_ROPED + 768:C_ROPED + 768 + LANES]
    g_ref[...] = 1.0 / (1.0 + jnp.exp(-g))
    for j in range(2):
        kvb_ref[:, j * LANES:(j + 1) * LANES] = ro[10 + j].astype(BF16)
        kvb_ref[:, 512 + j * LANES:512 + (j + 1) * LANES] = ro[12 + j].astype(BF16)
    kvb_ref[:, 256:512] = y[:, C_ROPED + 256:C_ROPED + 512].astype(BF16)
    kvb_ref[:, 768:1024] = y[:, C_ROPED + 512:C_ROPED + 768].astype(BF16)


def _proj_c(x2d, w, tabs, *, tm):
    m = x2d.shape[0]
    nt = tabs[0].shape[0] // tm
    row = lambda i: (i, 0)
    kvw = C_KV_HEADS * HEAD_DIM
    out_shape = [jax.ShapeDtypeStruct((m, C_WIDTH), BF16)]
    out_shape += [jax.ShapeDtypeStruct((m, kvw), F32)] * 6
    out_shape += [jax.ShapeDtypeStruct((m, LANES), F32), jax.ShapeDtypeStruct((m, 4 * kvw), BF16)]
    out_specs = [pl.BlockSpec((tm, C_WIDTH), row)] + [pl.BlockSpec((tm, kvw), row)] * 6
    out_specs += [pl.BlockSpec((tm, LANES), row), pl.BlockSpec((tm, 4 * kvw), row)]
    return pl.pallas_call(
        functools.partial(_proj_c_kernel, tm=tm), grid=(m // tm,),
        in_specs=[pl.BlockSpec((tm, D_MODEL), row), pl.BlockSpec((D_MODEL, C_COLS), lambda i: (0, 0)),
                  pl.BlockSpec((tm, LANES), lambda i: (i % nt, 0)), pl.BlockSpec((tm, LANES), lambda i: (i % nt, 0))],
        out_specs=out_specs, out_shape=out_shape,
        compiler_params=_params("parallel"))(x2d, w, *tabs)


def _compress_kernel(z_ref, pea_ref, peb_ref, wa_ref, wb_ref, w2_ref, o_ref, *, rows):
    z = z_ref[...].reshape(rows, z_ref.shape[-1])
    ra = jnp.dot((z + pea_ref[...]).astype(BF16), wa_ref[...], preferred_element_type=F32)
    rb = jnp.dot((z + peb_ref[...]).astype(BF16), wb_ref[...], preferred_element_type=F32)
    hid = ra + pltpu.roll(rb, rows - 1, 0)
    hid = 0.5 * hid * (1.0 + jnp.tanh(np.sqrt(2.0 / np.pi) * (hid + 0.044715 * hid * hid * hid)))
    out = jnp.dot(hid.astype(BF16), w2_ref[...], preferred_element_type=F32)
    o_ref[...] = out.reshape(o_ref.shape).astype(BF16)


def _compress(z, pea, peb, wa, wb, w2, *, nb_step):
    nb, r, zw = z.shape
    kvw = C_KV_HEADS * HEAD_DIM
    const = lambda b: (0, 0)
    return pl.pallas_call(
        functools.partial(_compress_kernel, rows=nb_step * r), grid=(nb // nb_step,),
        in_specs=[pl.BlockSpec((nb_step, r, zw), lambda b: (b, 0, 0)),
                  pl.BlockSpec((1, zw), const), pl.BlockSpec((1, zw), const),
                  pl.BlockSpec((zw, kvw), const), pl.BlockSpec((zw, kvw), const), pl.BlockSpec((kvw, kvw), const)],
        out_specs=pl.BlockSpec((nb_step, r, kvw), lambda b: (b, 0, 0)),
        out_shape=jax.ShapeDtypeStruct((nb, r, kvw), BF16),
        compiler_params=_params("parallel"))(z, pea, peb, wa, wb, w2)


def _select_blocks(psum, c2st, t_lane, nblk):
    rows = psum.shape[0]
    p_hi = psum.astype(BF16)
    p_lo = (psum - p_hi.astype(F32)).astype(BF16)
    imp = (lax.dot_general(c2st, p_hi, NT_DIMS, preferred_element_type=F32)
           + lax.dot_general(c2st, p_lo, NT_DIMS, preferred_element_type=F32))[:nblk]
    blk = lax.broadcasted_iota(I32, (nblk, rows), 0)
    blk_f = blk.astype(F32)
    cur = t_lane // SLC_BLOCK
    forced = (blk == 0) | (blk == cur) | (blk == cur - 1)
    imp = jnp.where(forced, jnp.inf, imp)
    imp = jnp.where(blk * SLC_BLOCK <= t_lane, imp, -jnp.inf)

    def top_body(_, carry):
        val, sel = carry
        mx = jnp.max(val, axis=0, keepdims=True)
        first = jnp.min(jnp.where(val == mx, blk_f, float(LANES)), axis=0, keepdims=True)
        pick = blk_f == first
        sel = jnp.where(pick & (mx > -jnp.inf), 1.0, sel)
        return jnp.where(pick, -jnp.inf, val), sel
    _, sel = lax.fori_loop(0, SLC_TOPN, top_body, (imp, jnp.zeros((nblk, rows), F32)))
    sel = jnp.concatenate([sel, jnp.zeros((LANES - nblk, rows), F32)], 0)
    return jnp.concatenate([sel[:, j * LANES:(j + 1) * LANES].T for j in range(rows // LANES)], 0)


def _nsa_kernel(q_ref, g_ref, kc_ref, vc_ref, kvs_ref, kvw_ref, c2st_ref, e_ref, o_ref,
                bias_ref, part_ref, acc_ref, m_ref, l_ref,
                *, tq, ncp, ac, nblk):
    i = pl.program_id(1)
    t = i * tq + lax.broadcasted_iota(I32, (tq, 1), 0)
    lane = lax.broadcasted_iota(I32, (tq, LANES), 1)
    lo_half = lane < HEAD_DIM
    n_s = _cdiv((i + 1) * tq, ac)
    span = WINDOW + tq
    w_start = pl.multiple_of(jnp.maximum(i * tq - WINDOW, 0), tq)
    kvw = C_KV_HEADS * HEAD_DIM
    ks_ref = kvs_ref
    vs_ref = kvs_ref.at[:, kvw:2 * kvw]
    kw_ref = kvw_ref
    vw_ref = kvw_ref.at[:, kvw:2 * kvw]
    hi_half = jnp.logical_not(lo_half)
    cmp_visible = lax.broadcasted_iota(I32, (tq, ncp), 1) * CMP_STRIDE + (CMP_BLOCK - 1) <= t
    ng = C_KV_HEADS
    hpg = C_HEADS // C_KV_HEADS
    nh = 2 * hpg
    head_rows = lambda h: slice(h * tq, (h + 1) * tq)
    tg = jnp.concatenate([t] * ng, 0)
    kiota_sg = lax.broadcasted_iota(I32, (ng * tq, ac), 1)

    def gate(pr, h, branch):
        head = hpg * (2 * pr + h // hpg) + h % hpg
        return g_ref[:, 3 * head + branch:3 * head + branch + 1]

    wpos = w_start + lax.broadcasted_iota(I32, (tq, span), 1)
    wbias = jnp.tile(jnp.where((wpos <= t) & (t - wpos < WINDOW), 0.0, MASKED), (nh, 1))

    def stacked_q(pr):
        return jnp.concatenate(
            [_half_masked(q_ref[:, (pr * hpg + h % hpg) * LANES:(pr * hpg + h % hpg + 1) * LANES],
                          lo_half if h < hpg else hi_half) for h in range(nh)], 0)

    psums = []
    for pr in range(ng // 2):
        col = pr * LANES
        s_c = lax.dot_general(stacked_q(pr), kc_ref[:, col:col + LANES], NT_DIMS, preferred_element_type=F32)
        pair_sums = [jnp.zeros((tq, ncp), F32), jnp.zeros((tq, ncp), F32)]
        ps = []
        for h in range(nh):
            sh = jnp.where(cmp_visible, s_c[head_rows(h)], -jnp.inf)
            m = jnp.max(sh, axis=1, keepdims=True)
            m = jnp.where(m > -jnp.inf, m, 0.0)
            p = jnp.exp2(sh - m)
            den = jnp.sum(p, axis=1, keepdims=True)
            p = p / jnp.where(den > 0.0, den, 1.0)
            pair_sums[h // hpg] = pair_sums[h // hpg] + p
            ps.append(p.astype(BF16))
        part_ref[pr * nh * tq:(pr + 1) * nh * tq] = jnp.dot(jnp.concatenate(ps, 0), vc_ref[:, col:col + LANES],
                                                           preferred_element_type=F32)
        psums += pair_sums

    t_lane = i * tq + lax.broadcasted_iota(I32, (1, ng * tq), 1) % tq
    selb = _select_blocks(jnp.concatenate(psums, 0), c2st_ref[...], t_lane, nblk).astype(BF16)

    def bias_body(c, carry):
        ex = jnp.dot(selb, e_ref[c], preferred_element_type=F32)
        ok = (ex > 0.5) & (c * ac + kiota_sg <= tg)
        bias_ref[c] = jnp.where(ok, 0.0, MASKED)
        return carry
    lax.fori_loop(0, n_s, bias_body, 0)

    for pr in range(ng // 2):
        col = pr * LANES
        qs = stacked_q(pr)
        part = part_ref.at[pr * nh * tq:(pr + 1) * nh * tq]

        def selected_bias(c, pr=pr):
            return lambda h: bias_ref[c, (2 * pr + h // hpg) * tq:(2 * pr + h // hpg + 1) * tq, :]
        _flash(qs, nh, ks_ref, vs_ref, col, 0, n_s, ac, selected_bias, acc_ref, m_ref, l_ref)
        for h in range(nh):
            r = head_rows(h)
            part[r] = gate(pr, h, 0) * part[r] + gate(pr, h, 1) * acc_ref[r]
        s_w = lax.dot_general(qs, kw_ref[pl.ds(w_start, span), col:col + LANES], NT_DIMS,
                              preferred_element_type=F32)
        p_w, l_w = _softmax_rows(s_w + wbias)
        o_w = jnp.dot(p_w.astype(BF16), vw_ref[pl.ds(w_start, span), col:col + LANES],
                      preferred_element_type=F32) / l_w
        for j in range(hpg):
            slot = pr * hpg + j
            lo, hi = head_rows(j), head_rows(hpg + j)
            o_lo = part[lo] + gate(pr, j, 2) * o_w[lo]
            o_hi = part[hi] + gate(pr, hpg + j, 2) * o_w[hi]
            o_ref[:, slot * LANES:(slot + 1) * LANES] = jnp.where(lo_half, o_lo, o_hi).astype(BF16)


def _nsa_attention(q, gates, kcmp, vcmp, kvs, kvw, c2st, emat, *, tq, ac, n_sblk, s_blk=0, w_blk=0):
    nb, tlen, _ = q.shape
    ncp = kcmp.shape[1]
    ls, lw = kvs.shape[1], kvw.shape[1]
    assert WINDOW % tq == 0 and lw >= WINDOW + tq and (C_KV_HEADS * tq) % LANES == 0
    kvwid = C_KV_HEADS * HEAD_DIM
    qspec = lambda w: pl.BlockSpec((None, tq, w), lambda b, i: (b, i, 0))
    kspec = lambda n, w, blk=0: pl.BlockSpec((None, n, w), lambda b, i: (b, 0, blk), pipeline_mode=pl.Buffered(1))
    kern = functools.partial(_nsa_kernel, tq=tq, ncp=ncp, ac=ac, nblk=_cdiv(n_sblk, SUBLANES) * SUBLANES)
    return pl.pallas_call(
        kern, grid=(nb, tlen // tq),
        in_specs=[qspec(C_WIDTH), qspec(LANES), kspec(ncp, kvwid), kspec(ncp, kvwid),
                  kspec(ls, 2 * kvwid, s_blk), kspec(lw, 2 * kvwid, w_blk),
                  pl.BlockSpec((LANES, ncp), lambda b, i: (0, 0)),
                  pl.BlockSpec((ls // ac, LANES, ac), lambda b, i: (0, 0, 0))],
        out_specs=qspec(C_WIDTH),
        out_shape=jax.ShapeDtypeStruct((nb, tlen, C_WIDTH), BF16),
        scratch_shapes=[pltpu.VMEM((ls // ac, C_KV_HEADS * tq, ac), F32), pltpu.VMEM((C_HEADS * tq, LANES), F32)]
        + [pltpu.VMEM((2 * (C_HEADS // C_KV_HEADS) * tq, LANES), F32)] * 3,
        compiler_params=_params("parallel", "arbitrary"))(q, gates, kcmp, vcmp, kvs, kvw, c2st, emat)


def _outproj_ln_kernel(x_ref, a_ref, b_ref, wa_ref, wb_ref, g_ref, bt_ref, o_ref):
    y = ALPHA * x_ref[...]
    y = y + jnp.dot(a_ref[...], wa_ref[...], preferred_element_type=F32)
    y = y + jnp.dot(b_ref[...], wb_ref[...], preferred_element_type=F32)
    o_ref[...] = _layernorm(y, g_ref[...], bt_ref[...])


def _outproj_ln(x2d, a, b, a_blk, b_blk, w_out, g, bt, *, tm):
    m = x2d.shape[0]
    half = w_out.shape[0] // 2
    row = lambda i: (i, 0)
    const = lambda i: (0, 0)
    return pl.pallas_call(
        _outproj_ln_kernel, grid=(m // tm,),
        in_specs=[pl.BlockSpec((tm, D_MODEL), row),
                  pl.BlockSpec((tm, half), lambda i: (i, a_blk)), pl.BlockSpec((tm, half), lambda i: (i, b_blk)),
                  pl.BlockSpec((half, D_MODEL), lambda i: (0, 0)), pl.BlockSpec((half, D_MODEL), lambda i: (1, 0)),
                  pl.BlockSpec((1, D_MODEL), const), pl.BlockSpec((1, D_MODEL), const)],
        out_specs=pl.BlockSpec((tm, D_MODEL), row),
        out_shape=jax.ShapeDtypeStruct((m, D_MODEL), F32),
        compiler_params=_params("parallel"))(x2d, a, b, w_out, w_out, g, bt)


def _ffn_ln_kernel(x_ref, wg_ref, wu_ref, wd_ref, g_ref, bt_ref, o_ref, xb_ref, acc_ref):
    f = pl.program_id(1)

    @pl.when(f == 0)
    def _():
        xb_ref[...] = x_ref[...].astype(BF16)
        acc_ref[...] = jnp.zeros(acc_ref.shape, F32)

    xb = xb_ref[...]
    h = jnp.dot(xb, wg_ref[...], preferred_element_type=F32)
    u = jnp.dot(xb, wu_ref[...], preferred_element_type=F32)
    a = (h / (1.0 + jnp.exp(-h))) * u
    acc_ref[...] += jnp.dot(a.astype(BF16), wd_ref[...], preferred_element_type=F32)

    @pl.when(f == pl.num_programs(1) - 1)
    def _():
        o_ref[...] = _layernorm(ALPHA * x_ref[...] + acc_ref[...], g_ref[...], bt_ref[...])


def _ffn_ln(x2d, wg, wu, wd, g, bt, *, tm, tf):
    m = x2d.shape[0]
    return pl.pallas_call(
        _ffn_ln_kernel, grid=(m // tm, D_FF // tf),
        in_specs=[pl.BlockSpec((tm, D_MODEL), lambda i, f: (i, 0)),
                  pl.BlockSpec((D_MODEL, tf), lambda i, f: (0, f)), pl.BlockSpec((D_MODEL, tf), lambda i, f: (0, f)),
                  pl.BlockSpec((tf, D_MODEL), lambda i, f: (f, 0)),
                  pl.BlockSpec((1, D_MODEL), lambda i, f: (0, 0)), pl.BlockSpec((1, D_MODEL), lambda i, f: (0, 0))],
        out_specs=pl.BlockSpec((tm, D_MODEL), lambda i, f: (i, 0)),
        out_shape=jax.ShapeDtypeStruct((m, D_MODEL), F32),
        scratch_shapes=[pltpu.VMEM((tm, D_MODEL), BF16), pltpu.VMEM((tm, D_MODEL), F32)],
        compiler_params=_params("parallel", "arbitrary"))(x2d, wg, wu, wd, g, bt)


def _rope_tables(pos):
    half = HEAD_DIM // 2
    inv = ROPE_THETA ** (-jnp.arange(half, dtype=F32) / half)
    ang = pos.astype(F32)[:, None] * inv[None, :]
    cos, sin = jnp.cos(ang), jnp.sin(ang)
    c64 = jnp.concatenate([cos, cos], 1)
    s64 = jnp.concatenate([-sin, sin], 1)
    c = jnp.concatenate([c64, c64], 1)
    s = jnp.concatenate([s64, s64], 1)
    cx = jnp.concatenate([c64, jnp.full_like(c64, IDX_HEADS ** -0.5)], 1)
    sx = jnp.concatenate([s64, jnp.zeros_like(s64)], 1)
    return c, s, cx, sx


def _perm_heads(w, perm):
    lead = w.shape[:-1]
    return w.reshape(*lead, len(perm), HEAD_DIM)[..., np.asarray(perm), :].reshape(*lead, len(perm) * HEAD_DIM)


def _ab_w_in(w):
    ab_sizes = (A_WIDTH, A_KV_HEADS * HEAD_DIM, A_KV_HEADS * HEAD_DIM, IDX_HEADS * IDX_DIM, IDX_DIM, IDX_HEADS,
                B_WIDTH, B_WIDTH, B_WIDTH)
    q, k, v, iq, ik, iw, gb, gc, h = jnp.split(w, np.cumsum(ab_sizes)[:-1].tolist(), axis=-1)
    pad = jnp.zeros((w.shape[0], LANES - IDX_DIM - IDX_HEADS), w.dtype)
    return jnp.concatenate([_perm_heads(q, A_PERM), k, iq, ik, iw, pad, v, gb, gc, h], -1).astype(BF16)


def _c_w_in(w):
    kvw = C_KV_HEADS * HEAD_DIM
    c_sizes = (C_WIDTH,) + (kvw,) * 6 + (3 * C_HEADS,)
    q, kc, vc, ks, vs, kw, vw, g = jnp.split(w, np.cumsum(c_sizes)[:-1].tolist(), axis=-1)
    pad = jnp.zeros((w.shape[0], LANES - 3 * C_HEADS), w.dtype)
    return jnp.concatenate([_perm_heads(q, C_PERM), kc, ks, kw, vc, vs, vw, g, pad], -1).astype(BF16)


def _perm_rows(w_out, perm):
    return w_out.reshape(len(perm), HEAD_DIM, w_out.shape[-1])[np.asarray(perm)].reshape(-1, w_out.shape[-1])


def _block_diag(w, n):
    eye = jnp.eye(n, dtype=w.dtype)
    out = jnp.einsum('gh,...ab->...gahb', eye, w)
    return out.reshape(*w.shape[:-2], n * w.shape[-2], n * w.shape[-1])


def _compress_weights(pe, w1, w2):
    g = C_KV_HEADS
    half = CMP_BLOCK // 2
    bd = _block_diag(w1, g)
    wa = bd[:half].reshape(half * g * HEAD_DIM, g * HEAD_DIM).astype(BF16)
    wb = bd[half:].reshape(half * g * HEAD_DIM, g * HEAD_DIM).astype(BF16)
    pet = jnp.tile(pe, (1, g))
    pea = pet[:half].reshape(1, -1)
    peb = pet[half:].reshape(1, -1)
    return pea, peb, wa, wb, _block_diag(w2, g).astype(BF16)


def _slc_from_cmp(ncp, n_cmp, n_sblk):
    n = np.arange(ncp)[None, :]
    mblk = np.arange(LANES)[:, None]
    hit = ((n * CMP_STRIDE < mblk * SLC_BLOCK + SLC_BLOCK) & (n * CMP_STRIDE + CMP_BLOCK > mblk * SLC_BLOCK)
           & (n < n_cmp) & (mblk < n_sblk))
    return jnp.asarray(hit, BF16)


def _expand_matrix(ls, ac):
    k = np.arange(ls)[None, :]
    mblk = np.arange(LANES)[:, None]
    e = (k // SLC_BLOCK == mblk).astype(np.float32)
    return jnp.asarray(e.reshape(LANES, ls // ac, ac).transpose(1, 0, 2), BF16)


def _pad_rows(a, n):
    return jnp.pad(a, ((0, 0), (0, n - a.shape[1]), (0, 0)))


def _stack_heads(q3, slots):
    nb, tq, _ = q3.shape
    qt = q3.reshape(nb, tq, slots, LANES).transpose(0, 2, 1, 3)
    lo = jnp.arange(LANES) < HEAD_DIM
    zero = jnp.zeros((), q3.dtype)
    stacked = jnp.concatenate([jnp.where(lo, qt, zero), jnp.where(lo, zero, qt)], 1)
    return stacked.reshape(nb, 2 * slots * tq, LANES)


def _token_minor(cache):
    nd = cache.ndim
    t = jnp.transpose(cache, (0, 1) + tuple(range(3, nd)) + (2,))
    return t.reshape(t.shape[0], t.shape[1], -1, t.shape[-1])


PROMPT_TM = 512
PROMPT_AC = 512
DSA_TQ = 256
NSA_TQ = 256
SAMPLE_TQ = 8
FFN_TF = 1408
DSA_SAMPLE_SEQS = 4
NSA_SAMPLE_SEQS = 2
CMP_SAMPLE_SEQS = 4


def _ab_layer(x2d, nb, tlen, start, past, page_table, w_in, conv_w, tabs, *, sample):
    m = nb * tlen
    conv_w8 = jnp.pad(conv_w, ((0, SUBLANES - CONV_W), (0, 0)))
    if sample:
        kt_cache, vt_cache, ikt_cache, prev, layer = past
        tt = jnp.arange(tlen)
        prevs = []
        for d in range(1, CONV_W):
            idx = jnp.clip(CONV_W - 1 + tt - d, 0, CONV_W - 2)
            prevs.append(prev[:, idx].reshape(m, B_WIDTH))
        outs = _proj_ab(x2d, w_in, tabs, conv_w8, tm=m, seq_tiles=1, sample=True, prevs=prevs, dec_seq=tlen)
    else:
        outs = _proj_ab(x2d, w_in, tabs, conv_w8, tm=PROMPT_TM, seq_tiles=tlen // PROMPT_TM, sample=False)
    q, iq, k, v, ikw, kvb, ikb, b_out, uo = outs
    if sample:
        klen_real = page_table.shape[1] * PAGE_SIZE + tlen
        pad_q = lambda a: _pad_rows(a.reshape(nb, tlen, -1), SAMPLE_TQ)
        pad_k = lambda a: _pad_rows(a.reshape(nb, tlen, -1), LANES)
        kvb3 = kvb.reshape(nb, tlen, 2 * LANES)
        iq_heads = pad_q(iq).reshape(nb, SAMPLE_TQ, IDX_HEADS, IDX_DIM).transpose(0, 2, 1, 3)
        a_out = _dsa_sample(_stack_heads(pad_q(q), A_HEADS // 2), iq_heads.reshape(nb, IDX_HEADS * SAMPLE_TQ, IDX_DIM),
                            pad_q(ikw), pad_k(kvb3[..., :LANES]), pad_k(kvb3[..., LANES:]),
                            pad_k(ikb.reshape(nb, tlen, LANES)[..., :IDX_DIM]),
                            kt_cache, vt_cache, ikt_cache, page_table, layer,
                            seqs=DSA_SAMPLE_SEQS, pos0=start, topk=min(A_TOPK_MAX, klen_real // 4))
        a_out = a_out[:, :tlen].reshape(m, A_WIDTH)
        new_conv = uo.reshape(nb, tlen, B_WIDTH)[:, tlen - (CONV_W - 1):]
    else:
        r3 = lambda a: a.reshape(nb, tlen, -1)
        a_out = _dsa_attention(r3(q), r3(iq), r3(ikw), r3(ikb), r3(kvb), tq=DSA_TQ,
                               topk=min(A_TOPK_MAX, tlen // 4), sc=512, ac=PROMPT_AC)
        a_out = a_out.reshape(m, A_WIDTH)
        new_conv = uo.reshape(nb, tlen // PROMPT_TM, SUBLANES, B_WIDTH)[:, -1, SUBLANES - (CONV_W - 1):]
    state = (k.reshape(nb, tlen, A_KV_HEADS, HEAD_DIM), v.reshape(nb, tlen, A_KV_HEADS, HEAD_DIM),
             ikw[:, :IDX_DIM].reshape(nb, tlen, IDX_DIM), new_conv)
    return a_out, b_out, state


def _c_layer(x2d, nb, tlen, start, past, w_in, cw_k, cw_v, tabs, *, sample):
    m = nb * tlen
    kvw = C_KV_HEADS * HEAD_DIM
    tm = m if sample else PROMPT_TM
    q, kc, ks, kw, vc, vs, vw, gates, kvb = _proj_c(x2d, w_in, tabs[:2], tm=tm)
    if sample:
        kct, vct, kst, vst, wink_t, winv_t, buf_k, buf_v, page_table, layer = past
        past_len = page_table.shape[1] * PAGE_SIZE
        n_cmp = (past_len + tlen - CMP_BLOCK) // CMP_STRIDE + 1
        assert n_cmp <= past_len // CMP_STRIDE, "compression blocks must lie inside the cached rows"
        assert tlen <= LANES and start == past_len

        def tap_major(cw):
            pea, peb, wa, wb, w2 = cw
            pet = jnp.concatenate([pea.reshape(-1, kvw), peb.reshape(-1, kvw)], 0)
            wab = jnp.concatenate([wa.reshape(-1, kvw, kvw), wb.reshape(-1, kvw, kvw)], 0)
            return pet, wab, w2
        kcmp = _compress_sample(kct, page_table, layer, *tap_major(cw_k), seqs=CMP_SAMPLE_SEQS)
        vcmp = _compress_sample(vct, page_table, layer, *tap_major(cw_v), seqs=CMP_SAMPLE_SEQS)
        ncp = kcmp.shape[1]
        ls = past_len + LANES
        n_sblk = _cdiv(past_len + tlen, SLC_BLOCK)
        kvb3 = kvb.reshape(nb, tlen, 4 * kvw)
        new_rows = [_pad_rows(kvb3[..., n * kvw:(n + 1) * kvw], LANES) for n in range(4)]
        q3 = _pad_rows(q.reshape(nb, tlen, C_WIDTH), SAMPLE_TQ)
        half_w = C_WIDTH // 2
        qs = jnp.concatenate([_stack_heads(q3[..., :half_w], C_HEADS // 4),
                              _stack_heads(q3[..., half_w:], C_HEADS // 4)], 1)
        out = _nsa_sample(qs, _pad_rows(gates.reshape(nb, tlen, LANES), SAMPLE_TQ), kcmp, vcmp, *new_rows,
                          wink_t, winv_t, _slc_from_cmp(ncp, n_cmp, n_sblk), _expand_matrix(ls, ls)[0],
                          kst, vst, page_table, layer, seqs=NSA_SAMPLE_SEQS, pos0=start, n_sblk=n_sblk)
        out = out[:, :tlen].reshape(m, C_WIDTH)
        r4 = lambda a: a.reshape(nb, tlen, C_KV_HEADS, HEAD_DIM)
        win_k = jnp.concatenate([buf_k, r4(kw)], 1)[:, tlen:]
        win_v = jnp.concatenate([buf_v, r4(vw)], 1)[:, tlen:]
    else:
        zk = kc.reshape(nb, tlen // CMP_STRIDE, CMP_STRIDE * kvw)
        zv = vc.reshape(nb, tlen // CMP_STRIDE, CMP_STRIDE * kvw)
        n_cmp = (tlen - CMP_BLOCK) // CMP_STRIDE + 1
        kcmp = _compress(zk, *cw_k, nb_step=1)
        vcmp = _compress(zv, *cw_v, nb_step=1)
        ncp = kcmp.shape[1]
        n_sblk = _cdiv(tlen, SLC_BLOCK)
        kvb3 = kvb.reshape(nb, tlen, 4 * kvw)
        out = _nsa_attention(q.reshape(nb, tlen, C_WIDTH), gates.reshape(nb, tlen, LANES), kcmp, vcmp, kvb3, kvb3,
                             _slc_from_cmp(ncp, n_cmp, n_sblk), _expand_matrix(tlen, PROMPT_AC),
                             tq=NSA_TQ, ac=PROMPT_AC, n_sblk=n_sblk, s_blk=0, w_blk=1)
        out = out.reshape(m, C_WIDTH)
        r4 = lambda a: a.reshape(nb, tlen, C_KV_HEADS, HEAD_DIM)
        keep = min(WINDOW, tlen)
        win_k = r4(kw)[:, tlen - keep:]
        win_v = r4(vw)[:, tlen - keep:]
    r4 = lambda a: a.reshape(nb, tlen, C_KV_HEADS, HEAD_DIM)
    return out, (r4(kc), r4(vc), r4(ks), r4(vs), win_k, win_v)


def kernel(x_prompt, x_sample, cache_a_k, cache_a_v, cache_a_ik, state_b_conv, cache_c_cmp_k, cache_c_cmp_v, cache_c_slc_k, cache_c_slc_v, state_c_win_k, state_c_win_v, page_table, ab_w_in, ab_conv_w, ab_w_out, c_w_in, c_cmp_pe_k, c_cmp_w1_k, c_cmp_w2_k, c_cmp_pe_v, c_cmp_w1_v, c_cmp_w2_v, c_w_out, ffn_w_gate, ffn_w_up, ffn_w_down, ln1_g, ln1_b, ln2_g, ln2_b):
    bp, tp, _ = x_prompt.shape
    bs, ts, _ = x_sample.shape
    past_len = page_table.shape[1] * PAGE_SIZE
    xp = x_prompt.reshape(bp * tp, D_MODEL)
    xs = x_sample.reshape(bs * ts, D_MODEL)
    tabs_p = _rope_tables(jnp.arange(tp))
    tabs_s = _rope_tables(past_len + jnp.arange(bs * ts) % ts)
    kt_a, vt_a, ikt_a = _token_minor(cache_a_k), _token_minor(cache_a_v), _token_minor(cache_a_ik)
    kct_c, vct_c = _token_minor(cache_c_cmp_k), _token_minor(cache_c_cmp_v)
    kst_c, vst_c = _token_minor(cache_c_slc_k), _token_minor(cache_c_slc_v)
    wink_t, winv_t = _token_minor(state_c_win_k), _token_minor(state_c_win_v)
    ab_p, ab_s, c_p, c_s = [], [], [], []
    for layer in range(DEPTH):
        i = layer // 2
        row1 = lambda a: a[layer].reshape(1, D_MODEL)
        if layer % 2 == 0:
            w_in = _ab_w_in(ab_w_in[i])
            w_out = jnp.concatenate([_perm_rows(ab_w_out[i][:A_WIDTH], A_PERM), ab_w_out[i][A_WIDTH:]], 0).astype(BF16)
            a_p, b_p, st_p = _ab_layer(xp, bp, tp, 0, None, None, w_in, ab_conv_w[i], tabs_p, sample=False)
            past = (kt_a, vt_a, ikt_a, state_b_conv[i], i)
            a_s, b_s, st_s = _ab_layer(xs, bs, ts, past_len, past, page_table, w_in, ab_conv_w[i], tabs_s, sample=True)
            ab_p.append(st_p)
            ab_s.append(st_s)
            xp = _outproj_ln(xp, a_p, b_p, 0, 0, w_out, row1(ln1_g), row1(ln1_b), tm=PROMPT_TM)
            xs = _outproj_ln(xs, a_s, b_s, 0, 0, w_out, row1(ln1_g), row1(ln1_b), tm=bs * ts)
        else:
            w_in = _c_w_in(c_w_in[i])
            w_out = _perm_rows(c_w_out[i], C_PERM).astype(BF16)
            cw_k = _compress_weights(c_cmp_pe_k[i], c_cmp_w1_k[i], c_cmp_w2_k[i])
            cw_v = _compress_weights(c_cmp_pe_v[i], c_cmp_w1_v[i], c_cmp_w2_v[i])
            o_p, st_p = _c_layer(xp, bp, tp, 0, None, w_in, cw_k, cw_v, tabs_p, sample=False)
            past = (kct_c, vct_c, kst_c, vst_c, wink_t, winv_t, state_c_win_k[i], state_c_win_v[i], page_table, i)
            o_s, st_s = _c_layer(xs, bs, ts, past_len, past, w_in, cw_k, cw_v, tabs_s, sample=True)
            c_p.append(st_p)
            c_s.append(st_s)
            xp = _outproj_ln(xp, o_p, o_p, 0, 1, w_out, row1(ln1_g), row1(ln1_b), tm=PROMPT_TM)
            xs = _outproj_ln(xs, o_s, o_s, 0, 1, w_out, row1(ln1_g), row1(ln1_b), tm=bs * ts)
        wg, wu, wd = ffn_w_gate[layer].astype(BF16), ffn_w_up[layer].astype(BF16), ffn_w_down[layer].astype(BF16)
        xp = _ffn_ln(xp, wg, wu, wd, row1(ln2_g), row1(ln2_b), tm=PROMPT_TM, tf=FFN_TF)
        xs = _ffn_ln(xs, wg, wu, wd, row1(ln2_g), row1(ln2_b), tm=bs * ts, tf=FFN_TF)
    stk = lambda lst, j: jnp.stack([e[j] for e in lst], 0)
    return (xp.reshape(bp, tp, D_MODEL), xs.reshape(bs, ts, D_MODEL),
            stk(ab_p, 0), stk(ab_p, 1), stk(ab_p, 2), stk(ab_p, 3),
            stk(c_p, 0), stk(c_p, 1), stk(c_p, 2), stk(c_p, 3), stk(c_p, 4), stk(c_p, 5),
            stk(ab_s, 0), stk(ab_s, 1), stk(ab_s, 2), stk(ab_s, 3),
            stk(c_s, 0), stk(c_s, 1), stk(c_s, 2), stk(c_s, 3), stk(c_s, 4), stk(c_s, 5))
```
